```python
import jax, jax.numpy as jnp
from jax import lax
import numpy as np

D_MODEL = 1024
BATCH = 8
SEQ = 2048
DEPTH = 1

D_MIX = D_MODEL
D_RG = D_MIX // 2
D_ML = D_MIX - D_RG
RG_BLOCKS = 8
RG_BLOCK_DIM = D_RG // RG_BLOCKS
RG_C = 8.0
ML_HEADS = 4
ML_HEAD_DIM = D_ML // ML_HEADS
ML_CHUNK = 64
CONV_WIDTH = 4
D_FF = 4 * D_MODEL
EPS = 1e-6
SPLIT_SIZES = (D_RG, D_RG, D_ML, D_ML, D_ML, D_ML, ML_HEADS, ML_HEADS)
D_IN = sum(SPLIT_SIZES)

kernel_name = "hymba_rglru_mlstm_sandwich_layer"


def rmsnorm(x, g):
    xf = x.astype(jnp.float32)
    y = xf * lax.rsqrt(jnp.mean(xf * xf, axis=-1, keepdims=True) + EPS) * g.astype(jnp.float32)
    return y.astype(x.dtype)


def causal_depthwise_conv(x, w, b):
    c = x.shape[-1]
    y = lax.conv_general_dilated(
        x, w[:, None, :].astype(x.dtype), window_strides=(1,),
        padding=[(CONV_WIDTH - 1, 0)], dimension_numbers=("NWC", "WIO", "NWC"),
        feature_group_count=c)
    return y + b.astype(x.dtype)


def rg_lru(x, gate_r_w, gate_r_b, gate_i_w, gate_i_b, lam):
    bsz, s, _ = x.shape
    xf = x.astype(jnp.float32)
    xb = xf.reshape(bsz, s, RG_BLOCKS, RG_BLOCK_DIM)
    r = jax.nn.sigmoid(jnp.einsum("bsgi,gij->bsgj", xb, gate_r_w.astype(jnp.float32)).reshape(bsz, s, D_RG)
                       + gate_r_b.astype(jnp.float32))
    i = jax.nn.sigmoid(jnp.einsum("bsgi,gij->bsgj", xb, gate_i_w.astype(jnp.float32)).reshape(bsz, s, D_RG)
                       + gate_i_b.astype(jnp.float32))
    log_a = -RG_C * r * jax.nn.softplus(-lam.astype(jnp.float32))
    a = jnp.exp(log_a)
    b_in = jnp.sqrt(-jnp.expm1(2.0 * log_a)) * (i * xf)

    def combine(c1, c2):
        a1, b1 = c1
        a2, b2 = c2
        return a1 * a2, a2 * b1 + b2

    _, h = lax.associative_scan(combine, (a, b_in), axis=1)
    return h


def mlstm_chunkwise(q, k, v, log_i, log_f):
    bsz, s, nh, dh = q.shape
    L = ML_CHUNK
    nc = s // L
    to_chunks = lambda t: t.reshape(bsz, nc, L, nh, dh).transpose(0, 3, 1, 2, 4)
    q = to_chunks(q) * (dh ** -0.5)
    k = to_chunks(k)
    v = to_chunks(v)
    log_i = log_i.reshape(bsz, nc, L, nh).transpose(0, 3, 1, 2)
    log_f = log_f.reshape(bsz, nc, L, nh).transpose(0, 3, 1, 2)
    bcum = jnp.cumsum(log_f, axis=-1)
    b_last = bcum[..., -1]

    w = b_last[..., None] - bcum + log_i
    m_loc = jnp.max(w, axis=-1)
    kw = jnp.exp(w - m_loc[..., None])
    c_loc = jnp.einsum("bhcl,bhcld,bhcle->bhcde", kw, k, v)
    n_loc = jnp.einsum("bhcl,bhcld->bhcd", kw, k)

    def step(carry, inp):
        c_st, n_st, m_st = carry
        c_l, n_l, m_l, bl = inp
        m_new = jnp.maximum(bl + m_st, m_l)
        s_old = jnp.exp(bl + m_st - m_new)
        s_new = jnp.exp(m_l - m_new)
        c_new = s_old[..., None, None] * c_st + s_new[..., None, None] * c_l
        n_new = s_old[..., None] * n_st + s_new[..., None] * n_l
        return (c_new, n_new, m_new), (c_st, n_st, m_st)

    init = (jnp.zeros((bsz, nh, dh, dh), jnp.float32),
            jnp.zeros((bsz, nh, dh), jnp.float32),
            jnp.zeros((bsz, nh), jnp.float32))
    xs = (c_loc.transpose(2, 0, 1, 3, 4), n_loc.transpose(2, 0, 1, 3),
          m_loc.transpose(2, 0, 1), b_last.transpose(2, 0, 1))
    _, (c_prev, n_prev, m_prev) = lax.scan(step, init, xs)
    c_prev = c_prev.transpose(1, 2, 0, 3, 4)
    n_prev = n_prev.transpose(1, 2, 0, 3)
    m_prev = m_prev.transpose(1, 2, 0)

    d_mat = bcum[..., :, None] - bcum[..., None, :] + log_i[..., None, :]
    causal = jnp.tril(jnp.ones((L, L), dtype=bool))
    d_mat = jnp.where(causal, d_mat, -jnp.inf)
    inter_log = bcum + m_prev[..., None]
    m_s = jnp.maximum(inter_log, jnp.max(d_mat, axis=-1))
    p = jnp.exp(d_mat - m_s[..., None])
    scores = jnp.einsum("bhcsd,bhcjd->bhcsj", q, k) * p
    inter_w = jnp.exp(inter_log - m_s)
    num = (jnp.einsum("bhcsj,bhcje->bhcse", scores, v)
           + inter_w[..., None] * jnp.einsum("bhcsd,bhcde->bhcse", q, c_prev))
    den = jnp.sum(scores, axis=-1) + inter_w * jnp.einsum("bhcsd,bhcd->bhcs", q, n_prev)
    h = num / jnp.maximum(jnp.abs(den), jnp.exp(-m_s))[..., None]
    return h.transpose(0, 2, 3, 1, 4).reshape(bsz, s, nh, dh)


def hybrid_mixer(u, w_in, rg_conv_w, rg_conv_b, rg_gate_r_w, rg_gate_r_b, rg_gate_i_w, rg_gate_i_b,
                 rg_lambda, ml_conv_w, ml_conv_b, ml_igate_b, ml_fgate_b, rg_out_gain, ml_out_gain, w_out):
    bsz, s, _ = u.shape
    proj = jnp.einsum("bsd,de->bse", u, w_in.astype(u.dtype))
    idx = np.cumsum(SPLIT_SIZES)[:-1].tolist()
    rg_x, rg_gate, ml_q, ml_k, ml_v, ml_o, ml_i, ml_f = jnp.split(proj, idx, axis=-1)

    rg_xc = causal_depthwise_conv(rg_x, rg_conv_w, rg_conv_b)
    h_rg = rg_lru(rg_xc, rg_gate_r_w, rg_gate_r_b, rg_gate_i_w, rg_gate_i_b, rg_lambda)
    y_rg = rmsnorm(h_rg * jax.nn.gelu(rg_gate.astype(jnp.float32)), rg_out_gain)

    qk = jax.nn.silu(causal_depthwise_conv(jnp.concatenate([ml_q, ml_k], axis=-1), ml_conv_w, ml_conv_b))
    q, k = jnp.split(qk.astype(jnp.float32), 2, axis=-1)
    hd = (bsz, s, ML_HEADS, ML_HEAD_DIM)
    log_i = ml_i.astype(jnp.float32) + ml_igate_b.astype(jnp.float32)
    log_f = jax.nn.log_sigmoid(ml_f.astype(jnp.float32) + ml_fgate_b.astype(jnp.float32))
    h_ml = mlstm_chunkwise(q.reshape(hd), k.reshape(hd), ml_v.astype(jnp.float32).reshape(hd), log_i, log_f)
    h_ml = jax.nn.sigmoid(ml_o.astype(jnp.float32)).reshape(hd) * h_ml
    y_ml = rmsnorm(h_ml, ml_out_gain.reshape(ML_HEADS, ML_HEAD_DIM)).reshape(bsz, s, D_ML)

    y = jnp.concatenate([y_rg, y_ml], axis=-1).astype(u.dtype)
    return jnp.einsum("bse,ed->bsd", y, w_out.astype(u.dtype))


def setup_inputs(seed: int = 0) -> dict:
    key = jax.random.key(seed)
    ks = jax.random.split(key, 24)
    f32 = jnp.float32
    nrm = lambda k, shape, scale: jax.random.normal(k, shape, f32) * scale
    gain = lambda k, shape: 1.0 + 0.02 * jax.random.normal(k, shape, f32)
    u = jax.random.uniform(ks[9], (DEPTH, D_RG), f32, 0.9, 0.999)
    p = u ** (1.0 / RG_C)
    rg_lambda = jnp.log(p) - jnp.log1p(-p)
    ml_fgate_b = jnp.linspace(3.0, 6.0, ML_HEADS, dtype=f32)[None, :] + nrm(ks[13], (DEPTH, ML_HEADS), 0.1)
    return {
        "x": jax.random.normal(ks[0], (BATCH, SEQ, D_MODEL), f32),
        "pre_mix_gain": gain(ks[1], (DEPTH, D_MODEL)),
        "w_in": nrm(ks[2], (DEPTH, D_MODEL, D_IN), D_MODEL ** -0.5),
        "rg_conv_w": nrm(ks[3], (DEPTH, CONV_WIDTH, D_RG), CONV_WIDTH ** -0.5),
        "rg_conv_b": nrm(ks[4], (DEPTH, D_RG), 0.02),
        "rg_gate_r_w": nrm(ks[5], (DEPTH, RG_BLOCKS, RG_BLOCK_DIM, RG_BLOCK_DIM), RG_BLOCK_DIM ** -0.5),
        "rg_gate_r_b": nrm(ks[6], (DEPTH, D_RG), 0.02),
        "rg_gate_i_w": nrm(ks[7], (DEPTH, RG_BLOCKS, RG_BLOCK_DIM, RG_BLOCK_DIM), RG_BLOCK_DIM ** -0.5),
        "rg_gate_i_b": nrm(ks[8], (DEPTH, D_RG), 0.02),
        "rg_lambda": rg_lambda,
        "ml_conv_w": nrm(ks[10], (DEPTH, CONV_WIDTH, 2 * D_ML), CONV_WIDTH ** -0.5),
        "ml_conv_b": nrm(ks[11], (DEPTH, 2 * D_ML), 0.02),
        "ml_igate_b": nrm(ks[12], (DEPTH, ML_HEADS), 0.1),
        "ml_fgate_b": ml_fgate_b,
        "rg_out_gain": gain(ks[14], (DEPTH, D_RG)),
        "ml_out_gain": gain(ks[15], (DEPTH, D_ML)),
        "w_out": nrm(ks[16], (DEPTH, D_MIX, D_MODEL), D_MIX ** -0.5),
        "post_mix_gain": gain(ks[17], (DEPTH, D_MODEL)),
        "pre_mlp_gain": gain(ks[18], (DEPTH, D_MODEL)),
        "mlp_w_up": nrm(ks[19], (DEPTH, D_MODEL, D_FF), D_MODEL ** -0.5),
        "mlp_w_down": nrm(ks[20], (DEPTH, D_FF, D_MODEL), D_FF ** -0.5),
        "post_mlp_gain": gain(ks[21], (DEPTH, D_MODEL)),
    }


def reference(x, pre_mix_gain, w_in, rg_conv_w, rg_conv_b, rg_gate_r_w, rg_gate_r_b, rg_gate_i_w,
              rg_gate_i_b, rg_lambda, ml_conv_w, ml_conv_b, ml_igate_b, ml_fgate_b, rg_out_gain,
              ml_out_gain, w_out, post_mix_gain, pre_mlp_gain, mlp_w_up, mlp_w_down, post_mlp_gain):
    h = x
    for l in range(DEPTH):
        u = rmsnorm(h, pre_mix_gain[l])
        mix = hybrid_mixer(u, w_in[l], rg_conv_w[l], rg_conv_b[l], rg_gate_r_w[l], rg_gate_r_b[l],
                           rg_gate_i_w[l], rg_gate_i_b[l], rg_lambda[l], ml_conv_w[l], ml_conv_b[l],
                           ml_igate_b[l], ml_fgate_b[l], rg_out_gain[l], ml_out_gain[l], w_out[l])
        h = h + rmsnorm(mix, post_mix_gain[l])
        v = rmsnorm(h, pre_mlp_gain[l])
        ff = jnp.square(jax.nn.relu(jnp.einsum("bsd,df->bsf", v, mlp_w_up[l].astype(v.dtype))))
        ff = jnp.einsum("bsf,fd->bsd", ff, mlp_w_down[l].astype(v.dtype))
        h = h + rmsnorm(ff, post_mlp_gain[l])
    return h
```

```python
import functools

import jax
import jax.numpy as jnp
from jax.experimental import pallas as pl
from jax.experimental.pallas import tpu as pltpu

F32 = jnp.float32
BF16 = jnp.bfloat16

RG_BLOCKS = 8
RG_C = 8.0
ML_HEADS = 4
CONV_WIDTH = 4
EPS = 1e-6

SUBLANES = 8
LANES = 128
MXU_WIDTH = 256

SEQ_TILE = 256
ML_CHUNK = LANES
MLP_TILE = 512
MLP_FF_CHUNK = 1024
VMEM_LIMIT_BYTES = 56 * 1024 * 1024


def _rms(x, gain):
    return x * jax.lax.rsqrt(jnp.mean(x * x, axis=-1, keepdims=True) + EPS) * gain


def _sigmoid(x):
    return 1.0 / (1.0 + jnp.exp(-x))


def _softplus(x):
    return jnp.maximum(x, 0.0) + jnp.log1p(jnp.exp(-jnp.abs(x)))


def _gelu_tanh(x):
    c = 0.7978845608028654
    return 0.5 * x * (1.0 + jnp.tanh(c * (x + 0.044715 * (x * x * x))))


def _lane_scan(x, op, fill):
    n = x.shape[1]
    lane = jax.lax.broadcasted_iota(jnp.int32, x.shape, 1)
    d = 1
    while d < n:
        shifted = jnp.where(lane < d, fill, pltpu.roll(x, d, axis=1))
        x = op(x, shifted)
        d *= 2
    return x


def _mixer_kernel(x_ref, pre_g_ref, w_main_ref, w_if_ref, b_if_ref, conv_w_ref, conv_b_ref, w_gate_ref,
                  b_r_ref, b_i_ref, lam_ref, rg_gain_ref, ml_gain_ref, w_out_ref, post_g_ref,
                  out_ref,
                  tail_ref, h_ref, c_ref, m_ref, row_ref, y_ref,
                  *, d_rg, d_ml, head_dim):
    ts = x_ref.shape[1]
    d_conv = d_rg + 2 * d_ml
    n_chunks = ts // ML_CHUNK
    L = ML_CHUNK

    @pl.when(pl.program_id(1) == 0)
    def _():
        tail_ref[...] = jnp.zeros_like(tail_ref)
        h_ref[...] = jnp.zeros_like(h_ref)
        c_ref[...] = jnp.zeros_like(c_ref)
        m_ref[...] = jnp.zeros_like(m_ref)

    x = x_ref[0]
    ub = _rms(x, pre_g_ref[...]).astype(BF16)
    proj = jnp.dot(ub, w_main_ref[...], preferred_element_type=F32)
    gates = jnp.dot(ub, w_if_ref[...], preferred_element_type=F32) + b_if_ref[...]

    pc = proj[:, :d_conv]
    prev = tail_ref[...]
    row8 = jax.lax.broadcasted_iota(jnp.int32, (SUBLANES, d_conv), 0)
    conv = pc * conv_w_ref[CONV_WIDTH - 1:CONV_WIDTH, :] + conv_b_ref[...]
    for d in range(1, CONV_WIDTH):
        rolled = pltpu.roll(pc, d, axis=0)
        top = jnp.where(row8 < d, pltpu.roll(prev, d, axis=0), rolled[:SUBLANES])
        shifted = jnp.concatenate([top, rolled[SUBLANES:]], axis=0)
        conv = conv + shifted * conv_w_ref[CONV_WIDTH - 1 - d:CONV_WIDTH - d, :]
    tail_ref[...] = pc[ts - SUBLANES:]

    xc = conv[:, :d_rg]
    r_parts, i_parts = [], []
    for g in range(d_rg // MXU_WIDTH):
        gg = jnp.dot(xc[:, g * MXU_WIDTH:(g + 1) * MXU_WIDTH].astype(BF16), w_gate_ref[g],
                     preferred_element_type=F32)
        r_parts.append(gg[:, :MXU_WIDTH])
        i_parts.append(gg[:, MXU_WIDTH:])
    r = _sigmoid(jnp.concatenate(r_parts, axis=1) + b_r_ref[...])
    i_gate = _sigmoid(jnp.concatenate(i_parts, axis=1) + b_i_ref[...])
    log_a = r * (-RG_C * _softplus(-lam_ref[...]))
    a = jnp.exp(log_a)
    b_in = jnp.sqrt(-jnp.tanh(log_a) * (a * a + 1.0)) * (i_gate * xc)

    row8r = jax.lax.broadcasted_iota(jnp.int32, (SUBLANES, d_rg), 0)
    sa, sb = a, b_in
    d = 1
    while d < ts:
        if d < SUBLANES:
            ra = pltpu.roll(sa, d, axis=0)
            rb = pltpu.roll(sb, d, axis=0)
            a_sh = jnp.concatenate([jnp.where(row8r < d, 1.0, ra[:SUBLANES]), ra[SUBLANES:]], axis=0)
            b_sh = jnp.concatenate([jnp.where(row8r < d, 0.0, rb[:SUBLANES]), rb[SUBLANES:]], axis=0)
            sb = sb + sa * b_sh
            sa = sa * a_sh
        else:
            sb = jnp.concatenate([sb[:d], sb[d:] + sa[d:] * sb[:ts - d]], axis=0)
            sa = jnp.concatenate([sa[:d], sa[d:] * sa[:ts - d]], axis=0)
        d *= 2
    h_rg = sa * h_ref[SUBLANES - 1:SUBLANES, :] + sb
    h_ref[...] = h_rg[ts - SUBLANES:]

    rg_gate = proj[:, d_conv:d_conv + d_rg]
    y_ref[:, :d_rg] = _rms(h_rg * _gelu_tanh(rg_gate), rg_gain_ref[...]).astype(BF16)

    qk = conv[:, d_rg:]
    qk = qk * _sigmoid(qk)
    q_all = qk[:, :d_ml] * (head_dim ** -0.5)
    k_all = qk[:, d_ml:]
    v_all = proj[:, d_conv + d_rg:d_conv + d_rg + d_ml]
    o_all = proj[:, d_conv + d_rg + d_ml:]

    lane_g = jax.lax.broadcasted_iota(jnp.int32, gates.shape, 1)
    log_sig = jnp.minimum(gates, 0.0) - jnp.log1p(jnp.exp(-jnp.abs(gates)))
    gates_t = jnp.where(lane_g < ML_HEADS, gates, log_sig).T[:SUBLANES]

    causal = (jax.lax.broadcasted_iota(jnp.int32, (L, L), 1)
              <= jax.lax.broadcasted_iota(jnp.int32, (L, L), 0))
    ones_ext = jnp.ones((L, head_dim), BF16)

    for c in range(n_chunks):
        rows = slice(c * L, (c + 1) * L)
        li = gates_t[:, rows]
        lf = pltpu.roll(li, ML_HEADS, axis=0)
        bcum = _lane_scan(lf, jnp.add, 0.0)
        b_last = jnp.sum(lf, axis=1, keepdims=True)
        row_b = li - bcum
        cmax = _lane_scan(row_b, jnp.maximum, -jnp.inf)
        w_loc = b_last + row_b
        m_loc = jnp.max(w_loc, axis=1, keepdims=True)
        m_prev_b = m_ref[...]
        m_prev = jnp.max(m_prev_b, axis=1, keepdims=True)
        m_s = jnp.maximum(bcum + m_prev, bcum + cmax)
        m_new = jnp.maximum(b_last + m_prev, m_loc)
        row_ref[0] = bcum - m_s
        row_ref[1] = m_s
        row_ref[2] = row_b
        row_ref[3] = jnp.exp(w_loc - m_loc)
        row_ref[4] = jnp.broadcast_to(jnp.exp(b_last + m_prev - m_new), (SUBLANES, L))
        row_ref[5] = jnp.broadcast_to(jnp.exp(m_loc - m_new), (SUBLANES, L))
        row_ref[6] = m_prev_b
        m_ref[...] = jnp.broadcast_to(m_new, (SUBLANES, L))

        for h in range(ML_HEADS):
            cols = slice(h * head_dim, (h + 1) * head_dim)
            ca_b = jnp.broadcast_to(row_ref[0, h:h + 1, :], (head_dim, L)).T
            ms_b = jnp.broadcast_to(row_ref[1, h:h + 1, :], (head_dim, L)).T
            p = jnp.exp(jnp.where(causal, ca_b + row_ref[2, h:h + 1, :], -jnp.inf))
            qh = q_all[rows, cols].astype(BF16)
            k_t = k_all[rows, cols].T
            s = jnp.dot(qh, k_t.astype(BF16), preferred_element_type=F32)
            v_ext = jnp.concatenate([v_all[rows, cols].astype(BF16), ones_ext], axis=1)
            sv = jnp.dot((s * p).astype(BF16), v_ext, preferred_element_type=F32)
            c_prev = c_ref[h]
            qc = jnp.dot(qh, c_prev.astype(BF16), preferred_element_type=F32)
            inter_w = jnp.exp(ca_b + row_ref[6, h:h + 1, :])
            num = sv[:, :head_dim] + inter_w * qc[:, :head_dim]
            den = sv[:, head_dim:] + inter_w * qc[:, head_dim:]
            hh = num / jnp.maximum(jnp.abs(den), jnp.exp(-ms_b))
            hh = _sigmoid(o_all[rows, cols]) * hh
            y_ref[rows, d_rg + h * head_dim:d_rg + (h + 1) * head_dim] = _rms(
                hh, ml_gain_ref[:, cols]).astype(BF16)
            upd = jnp.dot((k_t * row_ref[3, h:h + 1, :]).astype(BF16), v_ext, preferred_element_type=F32)
            s_old = row_ref[4, h:h + 1, :]
            s_new = row_ref[5, h:h + 1, :]
            c_ref[h] = (jnp.concatenate([s_old, s_old], axis=1) * c_prev
                        + jnp.concatenate([s_new, s_new], axis=1) * upd)

    mix = jnp.dot(y_ref[...], w_out_ref[...], preferred_element_type=F32)
    out_ref[0] = x + _rms(mix, post_g_ref[...])


def _mlp_kernel(h_ref, pre_g_ref, w_up_ref, w_down_ref, post_g_ref, out_ref):
    h = h_ref[...]
    vb = _rms(h, pre_g_ref[...]).astype(BF16)
    d_ff = w_up_ref.shape[1]
    acc = jnp.zeros(h.shape, F32)
    for c in range(d_ff // MLP_FF_CHUNK):
        cols = slice(c * MLP_FF_CHUNK, (c + 1) * MLP_FF_CHUNK)
        f = jnp.maximum(jnp.dot(vb, w_up_ref[:, cols], preferred_element_type=F32), 0.0)
        acc = acc + jnp.dot((f * f).astype(BF16), w_down_ref[cols, :], preferred_element_type=F32)
    out_ref[...] = h + _rms(acc, post_g_ref[...])


def _const_spec(shape):
    zeros = (0,) * len(shape)
    return pl.BlockSpec(shape, lambda *_: zeros)


def _block_diag_gate(w):
    nb, bd, _ = w.shape
    per = MXU_WIDTH // bd
    w = w.reshape(nb // per, per, bd, bd)
    eye = jnp.eye(per, dtype=w.dtype)
    return jnp.einsum("gpij,pq->gpiqj", w, eye).reshape(nb // per, MXU_WIDTH, MXU_WIDTH)


def _mixer(x, pre_gain, w_in, rg_conv_w, rg_conv_b, gate_r_w, gate_r_b, gate_i_w, gate_i_b, lam,
           ml_conv_w, ml_conv_b, igate_b, fgate_b, rg_gain, ml_gain, w_out, post_gain):
    bsz, seq, d_model = x.shape
    d_rg = lam.shape[0]
    d_ml = ml_gain.shape[0]
    head_dim = d_ml // ML_HEADS
    assert head_dim == LANES and ML_CHUNK == LANES and 2 * ML_HEADS == SUBLANES
    assert seq % SEQ_TILE == 0 and SEQ_TILE % ML_CHUNK == 0 and d_rg % MXU_WIDTH == 0
    d_conv = d_rg + 2 * d_ml

    o0 = 0
    rg_x_w = w_in[:, o0:o0 + d_rg]; o0 += d_rg
    rg_gate_w = w_in[:, o0:o0 + d_rg]; o0 += d_rg
    q_w = w_in[:, o0:o0 + d_ml]; o0 += d_ml
    k_w = w_in[:, o0:o0 + d_ml]; o0 += d_ml
    v_w = w_in[:, o0:o0 + d_ml]; o0 += d_ml
    o_w = w_in[:, o0:o0 + d_ml]; o0 += d_ml
    if_w = w_in[:, o0:]
    w_main = jnp.concatenate([rg_x_w, q_w, k_w, rg_gate_w, v_w, o_w], axis=1).astype(BF16)
    w_if = jnp.pad(if_w, ((0, 0), (0, LANES - 2 * ML_HEADS))).astype(BF16)
    b_if = jnp.pad(jnp.concatenate([igate_b, fgate_b]), (0, LANES - 2 * ML_HEADS)).reshape(1, LANES)
    conv_w = jnp.concatenate([rg_conv_w, ml_conv_w], axis=1)
    conv_b = jnp.concatenate([rg_conv_b, ml_conv_b]).reshape(1, d_conv)
    w_gate = jnp.concatenate([_block_diag_gate(gate_r_w), _block_diag_gate(gate_i_w)], axis=2).astype(BF16)

    row = lambda v: v.reshape(1, -1)
    operands = (x, row(pre_gain), w_main, w_if, b_if, conv_w, conv_b, w_gate, row(gate_r_b), row(gate_i_b),
                row(lam), row(rg_gain), row(ml_gain), w_out.astype(BF16), row(post_gain))
    x_spec = pl.BlockSpec((1, SEQ_TILE, d_model), lambda b, j: (b, j, 0))
    in_specs = [x_spec] + [_const_spec(op.shape) for op in operands[1:]]
    kern = functools.partial(_mixer_kernel, d_rg=d_rg, d_ml=d_ml, head_dim=head_dim)
    return pl.pallas_call(
        kern,
        grid=(bsz, seq // SEQ_TILE),
        in_specs=in_specs,
        out_specs=x_spec,
        out_shape=jax.ShapeDtypeStruct(x.shape, x.dtype),
        scratch_shapes=[
            pltpu.VMEM((SUBLANES, d_conv), F32),
            pltpu.VMEM((SUBLANES, d_rg), F32),
            pltpu.VMEM((ML_HEADS, head_dim, 2 * head_dim), F32),
            pltpu.VMEM((SUBLANES, ML_CHUNK), F32),
            pltpu.VMEM((7, SUBLANES, ML_CHUNK), F32),
            pltpu.VMEM((SEQ_TILE, d_rg + d_ml), BF16),
        ],
        compiler_params=pltpu.CompilerParams(
            dimension_semantics=("arbitrary", "arbitrary"), vmem_limit_bytes=VMEM_LIMIT_BYTES),
        name="mixer",
    )(*operands)


def _mlp(h, pre_gain, w_up, w_down, post_gain):
    bsz, seq, d_model = h.shape
    tokens = bsz * seq
    assert tokens % MLP_TILE == 0 and w_up.shape[1] % MLP_FF_CHUNK == 0
    h2 = h.reshape(tokens, d_model)
    operands = (h2, pre_gain.reshape(1, -1), w_up.astype(BF16), w_down.astype(BF16), post_gain.reshape(1, -1))
    h_spec = pl.BlockSpec((MLP_TILE, d_model), lambda t: (t, 0))
    out = pl.pallas_call(
        _mlp_kernel,
        grid=(tokens // MLP_TILE,),
        in_specs=[h_spec] + [_const_spec(op.shape) for op in operands[1:]],
        out_specs=h_spec,
        out_shape=jax.ShapeDtypeStruct(h2.shape, h2.dtype),
        compiler_params=pltpu.CompilerParams(
            dimension_semantics=("arbitrary",), vmem_limit_bytes=VMEM_LIMIT_BYTES),
        name="mlp",
    )(*operands)
    return out.reshape(h.shape)


def kernel(x, pre_mix_gain, w_in, rg_conv_w, rg_conv_b, rg_gate_r_w, rg_gate_r_b, rg_gate_i_w, rg_gate_i_b, rg_lambda, ml_conv_w, ml_conv_b, ml_igate_b, ml_fgate_b, rg_out_gain, ml_out_gain, w_out, post_mix_gain, pre_mlp_gain, mlp_w_up, mlp_w_down, post_mlp_gain):
    h = x
    for l in range(w_in.shape[0]):
        h = _mixer(h, pre_mix_gain[l], w_in[l], rg_conv_w[l], rg_conv_b[l], rg_gate_r_w[l], rg_gate_r_b[l],
                   rg_gate_i_w[l], rg_gate_i_b[l], rg_lambda[l], ml_conv_w[l], ml_conv_b[l], ml_igate_b[l],
                   ml_fgate_b[l], rg_out_gain[l], ml_out_gain[l], w_out[l], post_mix_gain[l])
        h = _mlp(h, pre_mlp_gain[l], mlp_w_up[l], mlp_w_down[l], post_mlp_gain[l])
    return h
```

```python
import functools

import jax
import jax.numpy as jnp
from jax.experimental import pallas as pl
from jax.experimental.pallas import tpu as pltpu

F32 = jnp.float32
BF16 = jnp.bfloat16

RG_BLOCKS = 8
RG_C = 8.0
ML_HEADS = 4
CONV_WIDTH = 4
EPS = 1e-6

SUBLANES = 8
LANES = 128
MXU_WIDTH = 256

SEQ_TILE = 256
ML_CHUNK = LANES
MLP_FF_CHUNK = 512
MLP_PHASES_AFTER = {"w_in": 1, "conv": 1, "rg_gates": 1, "rg_scan": 1, "ml_head": (1, 1, 1, 1)}
VMEM_LIMIT_BYTES = 56 * 1024 * 1024


def _rms(x, gain):
    return x * jax.lax.rsqrt(jnp.mean(x * x, axis=-1, keepdims=True) + EPS) * gain


def _sigmoid(x):
    return 1.0 / (1.0 + jnp.exp(-x))


def _softplus(x):
    return jnp.maximum(x, 0.0) + jnp.log1p(jnp.exp(-jnp.abs(x)))


def _gelu_tanh(x):
    c = 0.7978845608028654
    return 0.5 * x * (1.0 + jnp.tanh(c * (x + 0.044715 * (x * x * x))))


def _lane_scan(x, op, fill):
    n = x.shape[1]
    lane = jax.lax.broadcasted_iota(jnp.int32, x.shape, 1)
    d = 1
    while d < n:
        shifted = jnp.where(lane < d, fill, pltpu.roll(x, d, axis=1))
        x = op(x, shifted)
        d *= 2
    return x


def _mixer_tile(x, seq_start, pre_g_ref, w_main_ref, w_if_ref, b_if_ref, conv_w_ref, conv_b_ref, w_gate_ref,
                b_r_ref, b_i_ref, lam_ref, rg_gain_ref, ml_gain_ref, w_out_ref, post_g_ref,
                tail_ref, h_ref, c_ref, m_ref, row_ref, y_ref,
                *, d_rg, d_ml, head_dim):
    ts = x.shape[0]
    d_conv = d_rg + 2 * d_ml
    n_chunks = ts // ML_CHUNK
    L = ML_CHUNK

    @pl.when(seq_start)
    def _():
        tail_ref[...] = jnp.zeros_like(tail_ref)
        h_ref[...] = jnp.zeros_like(h_ref)
        c_ref[...] = jnp.zeros_like(c_ref)
        m_ref[...] = jnp.zeros_like(m_ref)

    ub = _rms(x, pre_g_ref[...]).astype(BF16)
    proj = jnp.dot(ub, w_main_ref[...], preferred_element_type=F32)
    gates = jnp.dot(ub, w_if_ref[...], preferred_element_type=F32) + b_if_ref[...]
    yield MLP_PHASES_AFTER["w_in"]

    pc = proj[:, :d_conv]
    prev = tail_ref[...]
    row8 = jax.lax.broadcasted_iota(jnp.int32, (SUBLANES, d_conv), 0)
    conv = pc * conv_w_ref[CONV_WIDTH - 1:CONV_WIDTH, :] + conv_b_ref[...]
    for d in range(1, CONV_WIDTH):
        rolled = pltpu.roll(pc, d, axis=0)
        top = jnp.where(row8 < d, pltpu.roll(prev, d, axis=0), rolled[:SUBLANES])
        shifted = jnp.concatenate([top, rolled[SUBLANES:]], axis=0)
        conv = conv + shifted * conv_w_ref[CONV_WIDTH - 1 - d:CONV_WIDTH - d, :]
    tail_ref[...] = pc[ts - SUBLANES:]
    yield MLP_PHASES_AFTER["conv"]

    xc = conv[:, :d_rg]
    r_parts, i_parts = [], []
    for g in range(d_rg // MXU_WIDTH):
        gg = jnp.dot(xc[:, g * MXU_WIDTH:(g + 1) * MXU_WIDTH].astype(BF16), w_gate_ref[g],
                     preferred_element_type=F32)
        r_parts.append(gg[:, :MXU_WIDTH])
        i_parts.append(gg[:, MXU_WIDTH:])
    r = _sigmoid(jnp.concatenate(r_parts, axis=1) + b_r_ref[...])
    i_gate = _sigmoid(jnp.concatenate(i_parts, axis=1) + b_i_ref[...])
    log_a = r * (-RG_C * _softplus(-lam_ref[...]))
    a = jnp.exp(log_a)
    b_in = jnp.sqrt(-jnp.tanh(log_a) * (a * a + 1.0)) * (i_gate * xc)
    yield MLP_PHASES_AFTER["rg_gates"]

    row8r = jax.lax.broadcasted_iota(jnp.int32, (SUBLANES, d_rg), 0)
    sa, sb = a, b_in
    d = 1
    while d < ts:
        if d < SUBLANES:
            ra = pltpu.roll(sa, d, axis=0)
            rb = pltpu.roll(sb, d, axis=0)
            a_sh = jnp.concatenate([jnp.where(row8r < d, 1.0, ra[:SUBLANES]), ra[SUBLANES:]], axis=0)
            b_sh = jnp.concatenate([jnp.where(row8r < d, 0.0, rb[:SUBLANES]), rb[SUBLANES:]], axis=0)
            sb = sb + sa * b_sh
            sa = sa * a_sh
        else:
            sb = jnp.concatenate([sb[:d], sb[d:] + sa[d:] * sb[:ts - d]], axis=0)
            sa = jnp.concatenate([sa[:d], sa[d:] * sa[:ts - d]], axis=0)
        d *= 2
    h_rg = sa * h_ref[SUBLANES - 1:SUBLANES, :] + sb
    h_ref[...] = h_rg[ts - SUBLANES:]

    rg_gate = proj[:, d_conv:d_conv + d_rg]
    y_ref[:, :d_rg] = _rms(h_rg * _gelu_tanh(rg_gate), rg_gain_ref[...]).astype(BF16)
    yield MLP_PHASES_AFTER["rg_scan"]

    qk = conv[:, d_rg:]
    qk = qk * _sigmoid(qk)
    q_all = qk[:, :d_ml] * (head_dim ** -0.5)
    k_all = qk[:, d_ml:]
    v_all = proj[:, d_conv + d_rg:d_conv + d_rg + d_ml]
    o_all = proj[:, d_conv + d_rg + d_ml:]

    lane_g = jax.lax.broadcasted_iota(jnp.int32, gates.shape, 1)
    log_sig = jnp.minimum(gates, 0.0) - jnp.log1p(jnp.exp(-jnp.abs(gates)))
    gates_t = jnp.where(lane_g < ML_HEADS, gates, log_sig).T[:SUBLANES]

    causal = (jax.lax.broadcasted_iota(jnp.int32, (L, L), 1)
              <= jax.lax.broadcasted_iota(jnp.int32, (L, L), 0))
    ones_ext = jnp.ones((L, head_dim), BF16)

    for c in range(n_chunks):
        rows = slice(c * L, (c + 1) * L)
        li = gates_t[:, rows]
        lf = pltpu.roll(li, ML_HEADS, axis=0)
        bcum = _lane_scan(lf, jnp.add, 0.0)
        b_last = jnp.sum(lf, axis=1, keepdims=True)
        row_b = li - bcum
        cmax = _lane_scan(row_b, jnp.maximum, -jnp.inf)
        w_loc = b_last + row_b
        m_loc = jnp.max(w_loc, axis=1, keepdims=True)
        m_prev_b = m_ref[...]
        m_prev = jnp.max(m_prev_b, axis=1, keepdims=True)
        m_s = jnp.maximum(bcum + m_prev, bcum + cmax)
        m_new = jnp.maximum(b_last + m_prev, m_loc)
        row_ref[0] = bcum - m_s
        row_ref[1] = m_s
        row_ref[2] = row_b
        row_ref[3] = jnp.exp(w_loc - m_loc)
        row_ref[4] = jnp.broadcast_to(jnp.exp(b_last + m_prev - m_new), (SUBLANES, L))
        row_ref[5] = jnp.broadcast_to(jnp.exp(m_loc - m_new), (SUBLANES, L))
        row_ref[6] = m_prev_b
        m_ref[...] = jnp.broadcast_to(m_new, (SUBLANES, L))

        for h in range(ML_HEADS):
            cols = slice(h * head_dim, (h + 1) * head_dim)
            ca_b = jnp.broadcast_to(row_ref[0, h:h + 1, :], (head_dim, L)).T
            ms_b = jnp.broadcast_to(row_ref[1, h:h + 1, :], (head_dim, L)).T
            p = jnp.exp(jnp.where(causal, ca_b + row_ref[2, h:h + 1, :], -jnp.inf))
            qh = q_all[rows, cols].astype(BF16)
            k_t = k_all[rows, cols].T
            s = jnp.dot(qh, k_t.astype(BF16), preferred_element_type=F32)
            v_ext = jnp.concatenate([v_all[rows, cols].astype(BF16), ones_ext], axis=1)
            sv = jnp.dot((s * p).astype(BF16), v_ext, preferred_element_type=F32)
            c_prev = c_ref[h]
            qc = jnp.dot(qh, c_prev.astype(BF16), preferred_element_type=F32)
            inter_w = jnp.exp(ca_b + row_ref[6, h:h + 1, :])
            num = sv[:, :head_dim] + inter_w * qc[:, :head_dim]
            den = sv[:, head_dim:] + inter_w * qc[:, head_dim:]
            hh = num / jnp.maximum(jnp.abs(den), jnp.exp(-ms_b))
            hh = _sigmoid(o_all[rows, cols]) * hh
            y_ref[rows, d_rg + h * head_dim:d_rg + (h + 1) * head_dim] = _rms(
                hh, ml_gain_ref[:, cols]).astype(BF16)
            upd = jnp.dot((k_t * row_ref[3, h:h + 1, :]).astype(BF16), v_ext, preferred_element_type=F32)
            s_old = row_ref[4, h:h + 1, :]
            s_new = row_ref[5, h:h + 1, :]
            c_ref[h] = (jnp.concatenate([s_old, s_old], axis=1) * c_prev
                        + jnp.concatenate([s_new, s_new], axis=1) * upd)
            yield MLP_PHASES_AFTER["ml_head"][h]

    mix = jnp.dot(y_ref[...], w_out_ref[...], preferred_element_type=F32)
    return x + _rms(mix, post_g_ref[...])


def _mlp_tile(h, pre_g_ref, w_up_ref, w_down_ref, post_g_ref):
    vb = _rms(h, pre_g_ref[...]).astype(BF16)
    d_ff = w_up_ref.shape[1]
    acc = jnp.zeros(h.shape, F32)
    for c in range(d_ff // MLP_FF_CHUNK):
        cols = slice(c * MLP_FF_CHUNK, (c + 1) * MLP_FF_CHUNK)
        f = jnp.maximum(jnp.dot(vb, w_up_ref[:, cols], preferred_element_type=F32), 0.0)
        acc = acc + jnp.dot((f * f).astype(BF16), w_down_ref[cols, :], preferred_element_type=F32)
        yield
    return h + _rms(acc, post_g_ref[...])


def _interleave(primary, secondary):
    results = [None, None]

    def advance(idx, gen):
        if results[idx] is None:
            try:
                return next(gen)
            except StopIteration as stop:
                results[idx] = (stop.value,)
        return 0

    while results[0] is None:
        for _ in range(advance(0, primary)):
            advance(1, secondary)
    while results[1] is None:
        advance(1, secondary)
    return results[0][0], results[1][0]


def _layer_kernel(x_ref, *refs, tiles_per_seq, n_mixer_in, **dims):
    mixer_in = refs[:n_mixer_in]
    mlp_in = refs[n_mixer_in:n_mixer_in + 4]
    out_ref = refs[n_mixer_in + 4]
    h1_ref = refs[n_mixer_in + 5]
    state = refs[n_mixer_in + 6:]
    s = pl.program_id(0)

    @pl.when(s == 0)
    def _():
        h1_ref[...] = jnp.zeros_like(h1_ref)

    h1_new, out = _interleave(
        _mixer_tile(x_ref[0], s % tiles_per_seq == 0, *mixer_in, *state, **dims),
        _mlp_tile(h1_ref[...], *mlp_in))
    out_ref[0] = out
    h1_ref[...] = h1_new


def _const_spec(shape):
    zeros = (0,) * len(shape)
    return pl.BlockSpec(shape, lambda *_: zeros, pipeline_mode=pl.Buffered(1))


def _block_diag_gate(w):
    nb, bd, _ = w.shape
    per = MXU_WIDTH // bd
    w = w.reshape(nb // per, per, bd, bd)
    eye = jnp.eye(per, dtype=w.dtype)
    return jnp.einsum("gpij,pq->gpiqj", w, eye).reshape(nb // per, MXU_WIDTH, MXU_WIDTH)


def _layer(x, pre_gain, w_in, rg_conv_w, rg_conv_b, gate_r_w, gate_r_b, gate_i_w, gate_i_b, lam,
           ml_conv_w, ml_conv_b, igate_b, fgate_b, rg_gain, ml_gain, w_out, post_gain,
           pre_mlp_gain, w_up, w_down, post_mlp_gain):
    bsz, seq, d_model = x.shape
    assert w_up.shape[1] % MLP_FF_CHUNK == 0
    d_rg = lam.shape[0]
    d_ml = ml_gain.shape[0]
    head_dim = d_ml // ML_HEADS
    assert head_dim == LANES and ML_CHUNK == LANES and 2 * ML_HEADS == SUBLANES
    assert seq % SEQ_TILE == 0 and SEQ_TILE % ML_CHUNK == 0 and d_rg % MXU_WIDTH == 0
    d_conv = d_rg + 2 * d_ml

    o0 = 0
    rg_x_w = w_in[:, o0:o0 + d_rg]; o0 += d_rg
    rg_gate_w = w_in[:, o0:o0 + d_rg]; o0 += d_rg
    q_w = w_in[:, o0:o0 + d_ml]; o0 += d_ml
    k_w = w_in[:, o0:o0 + d_ml]; o0 += d_ml
    v_w = w_in[:, o0:o0 + d_ml]; o0 += d_ml
    o_w = w_in[:, o0:o0 + d_ml]; o0 += d_ml
    if_w = w_in[:, o0:]
    w_main = jnp.concatenate([rg_x_w, q_w, k_w, rg_gate_w, v_w, o_w], axis=1).astype(BF16)
    w_if = jnp.pad(if_w, ((0, 0), (0, LANES - 2 * ML_HEADS))).astype(BF16)
    b_if = jnp.pad(jnp.concatenate([igate_b, fgate_b]), (0, LANES - 2 * ML_HEADS)).reshape(1, LANES)
    conv_w = jnp.concatenate([rg_conv_w, ml_conv_w], axis=1)
    conv_b = jnp.concatenate([rg_conv_b, ml_conv_b]).reshape(1, d_conv)
    w_gate = jnp.concatenate([_block_diag_gate(gate_r_w), _block_diag_gate(gate_i_w)], axis=2).astype(BF16)

    row = lambda v: v.reshape(1, -1)
    mixer_in = (row(pre_gain), w_main, w_if, b_if, conv_w, conv_b, w_gate, row(gate_r_b), row(gate_i_b),
                row(lam), row(rg_gain), row(ml_gain), w_out.astype(BF16), row(post_gain))
    mlp_in = (row(pre_mlp_gain), w_up.astype(BF16), w_down.astype(BF16), row(post_mlp_gain))

    tiles_per_seq = seq // SEQ_TILE
    n_tiles = bsz * tiles_per_seq

    def tile_block(t):
        return (t // tiles_per_seq, t % tiles_per_seq, 0)

    x_spec = pl.BlockSpec((1, SEQ_TILE, d_model), lambda s: tile_block(jnp.minimum(s, n_tiles - 1)))
    out_spec = pl.BlockSpec((1, SEQ_TILE, d_model), lambda s: tile_block(jnp.maximum(s - 1, 0)))
    in_specs = [x_spec] + [_const_spec(op.shape) for op in mixer_in + mlp_in]
    kern = functools.partial(_layer_kernel, tiles_per_seq=tiles_per_seq, n_mixer_in=len(mixer_in),
                             d_rg=d_rg, d_ml=d_ml, head_dim=head_dim)
    return pl.pallas_call(
        kern,
        grid=(n_tiles + 1,),
        in_specs=in_specs,
        out_specs=out_spec,
        out_shape=jax.ShapeDtypeStruct(x.shape, x.dtype),
        scratch_shapes=[
            pltpu.VMEM((SEQ_TILE, d_model), F32),
            pltpu.VMEM((SUBLANES, d_conv), F32),
            pltpu.VMEM((SUBLANES, d_rg), F32),
            pltpu.VMEM((ML_HEADS, head_dim, 2 * head_dim), F32),
            pltpu.VMEM((SUBLANES, ML_CHUNK), F32),
            pltpu.VMEM((7, SUBLANES, ML_CHUNK), F32),
            pltpu.VMEM((SEQ_TILE, d_rg + d_ml), BF16),
        ],
        compiler_params=pltpu.CompilerParams(
            dimension_semantics=("arbitrary",), vmem_limit_bytes=VMEM_LIMIT_BYTES),
        name="layer",
    )(x, *mixer_in, *mlp_in)


def kernel(x, pre_mix_gain, w_in, rg_conv_w, rg_conv_b, rg_gate_r_w, rg_gate_r_b, rg_gate_i_w, rg_gate_i_b, rg_lambda, ml_conv_w, ml_conv_b, ml_igate_b, ml_fgate_b, rg_out_gain, ml_out_gain, w_out, post_mix_gain, pre_mlp_gain, mlp_w_up, mlp_w_down, post_mlp_gain):
    h = x
    for l in range(w_in.shape[0]):
        h = _layer(h, pre_mix_gain[l], w_in[l], rg_conv_w[l], rg_conv_b[l], rg_gate_r_w[l], rg_gate_r_b[l],
                   rg_gate_i_w[l], rg_gate_i_b[l], rg_lambda[l], ml_conv_w[l], ml_conv_b[l], ml_igate_b[l],
                   ml_fgate_b[l], rg_out_gain[l], ml_out_gain[l], w_out[l], post_mix_gain[l],
                   pre_mlp_gain[l], mlp_w_up[l], mlp_w_down[l], post_mlp_gain[l])
    return h
```

```python
import functools

import jax
import jax.numpy as jnp
from jax.experimental import pallas as pl
from jax.experimental.pallas import tpu as pltpu

F32 = jnp.float32
BF16 = jnp.bfloat16

RG_BLOCKS = 8
RG_C = 8.0
ML_HEADS = 4
CONV_WIDTH = 4
EPS = 1e-6

SUBLANES = 8
LANES = 128
MXU_WIDTH = 256

SEQ_TILE = 256
ML_CHUNK = LANES
PHASES = 4
MLP_FF_CHUNK = 512
MLP_PHASES_AFTER = {"start": 0, "w_in": 1, "conv": 1, "rg_gates": 1, "rg_scan": 1, "ml_stage": (1, 0, 1, 0),
                    "w_out": 0}
VMEM_LIMIT_BYTES = 56 * 1024 * 1024


def _rms(x, gain):
    return x * jax.lax.rsqrt(jnp.mean(x * x, axis=-1, keepdims=True) + EPS) * gain


def _sigmoid(x):
    return 1.0 / (1.0 + jnp.exp(-x))


def _softplus(x):
    return jnp.maximum(x, 0.0) + jnp.log1p(jnp.exp(-jnp.abs(x)))


def _gelu_tanh(x):
    c = 0.7978845608028654
    return 0.5 * x * (1.0 + jnp.tanh(c * (x + 0.044715 * (x * x * x))))


def _lane_scan(x, op, fill):
    n = x.shape[1]
    lane = jax.lax.broadcasted_iota(jnp.int32, x.shape, 1)
    d = 1
    while d < n:
        shifted = jnp.where(lane < d, fill, pltpu.roll(x, d, axis=1))
        x = op(x, shifted)
        d *= 2
    return x


def _affine_row_scan(sa, sb):
    n, width = sa.shape
    row8 = jax.lax.broadcasted_iota(jnp.int32, (SUBLANES, width), 0)
    d = 1
    while d < n:
        if d < SUBLANES:
            ra = pltpu.roll(sa, d, axis=0)
            rb = pltpu.roll(sb, d, axis=0)
            a_sh = jnp.concatenate([jnp.where(row8 < d, 1.0, ra[:SUBLANES]), ra[SUBLANES:]], axis=0)
            b_sh = jnp.concatenate([jnp.where(row8 < d, 0.0, rb[:SUBLANES]), rb[SUBLANES:]], axis=0)
            sb = sb + sa * b_sh
            sa = sa * a_sh
        else:
            sb = jnp.concatenate([sb[:d], sb[d:] + sa[d:] * sb[:n - d]], axis=0)
            sa = jnp.concatenate([sa[:d], sa[d:] * sa[:n - d]], axis=0)
        d *= 2
    return sa, sb


def _mixer_tile(x, seq_start, pre_g_ref, w_main_ref, w_if_ref, b_if_ref, conv_w_ref, conv_b_ref, w_gate_ref,
                b_r_ref, b_i_ref, lam_ref, rg_gain_ref, ml_gain_ref, w_out_ref, post_g_ref,
                cbuf_ref, qk_ref, yrg_ref, h_ref, c_ref, m_ref, row_ref, y_ref,
                *, d_rg, d_ml, head_dim):
    ts = x.shape[0]
    d_conv = d_rg + 2 * d_ml
    n_chunks = ts // ML_CHUNK
    L = ML_CHUNK

    @pl.when(seq_start)
    def _():
        cbuf_ref[:, ts:ts + SUBLANES, :] = jnp.zeros((cbuf_ref.shape[0], SUBLANES, LANES), F32)
        h_ref[...] = jnp.zeros_like(h_ref)
        c_ref[...] = jnp.zeros_like(c_ref)
        m_ref[...] = jnp.zeros_like(m_ref)

    ub = _rms(x, pre_g_ref[...]).astype(BF16)
    proj = jnp.dot(ub, w_main_ref[...], preferred_element_type=F32)
    gates = jnp.dot(ub, w_if_ref[...], preferred_element_type=F32) + b_if_ref[...]
    yield MLP_PHASES_AFTER["w_in"]

    n_cs = d_conv // LANES
    n_rs = d_rg // LANES
    G = ts // PHASES
    cbuf_ref[:n_cs, :SUBLANES, :] = cbuf_ref[:n_cs, ts:ts + SUBLANES, :]
    for j in range(n_cs + n_rs):
        cbuf_ref[j, SUBLANES:, :] = proj[:, j * LANES:(j + 1) * LANES]

    def phase(j, e):
        return cbuf_ref[j, pl.ds(SUBLANES + e, G, stride=PHASES), :]

    conv_ph = [[None] * n_cs for _ in range(PHASES)]
    for j in range(n_cs):
        cols = slice(j * LANES, (j + 1) * LANES)
        taps = {e: phase(j, e) for e in range(1 - CONV_WIDTH, PHASES)}
        for r in range(PHASES):
            acc = taps[r] * conv_w_ref[CONV_WIDTH - 1:CONV_WIDTH, cols] + conv_b_ref[:, cols]
            for k in range(1, CONV_WIDTH):
                acc = acc + taps[r - k] * conv_w_ref[CONV_WIDTH - 1 - k:CONV_WIDTH - k, cols]
            conv_ph[r][j] = acc
    for j in range(n_rs, n_cs):
        scale = head_dim ** -0.5 if j < n_rs + d_ml // LANES else 1.0
        for r in range(PHASES):
            val = conv_ph[r][j]
            val = val * _sigmoid(val)
            qk_ref[j - n_rs, pl.ds(r, G, stride=PHASES), :] = val * scale if scale != 1.0 else val
    yield MLP_PHASES_AFTER["conv"]

    xc = jnp.concatenate([jnp.concatenate(conv_ph[r][:n_rs], axis=1) for r in range(PHASES)], axis=0)
    r_parts, i_parts = [], []
    for g in range(d_rg // MXU_WIDTH):
        gg = jnp.dot(xc[:, g * MXU_WIDTH:(g + 1) * MXU_WIDTH].astype(BF16), w_gate_ref[g],
                     preferred_element_type=F32)
        r_parts.append(gg[:, :MXU_WIDTH])
        i_parts.append(gg[:, MXU_WIDTH:])
    r = _sigmoid(jnp.concatenate(r_parts, axis=1) + b_r_ref[...])
    i_gate = _sigmoid(jnp.concatenate(i_parts, axis=1) + b_i_ref[...])
    log_a = r * (-RG_C * _softplus(-lam_ref[...]))
    a = jnp.exp(log_a)
    b_in = jnp.sqrt(-jnp.tanh(log_a) * (a * a + 1.0)) * (i_gate * xc)
    yield MLP_PHASES_AFTER["rg_gates"]

    comp_a, comp_b = [a[:G]], [b_in[:G]]
    for r in range(1, PHASES):
        a_r, b_r = a[r * G:(r + 1) * G], b_in[r * G:(r + 1) * G]
        comp_b.append(a_r * comp_b[-1] + b_r)
        comp_a.append(a_r * comp_a[-1])
    ga, gb = _affine_row_scan(comp_a[-1], comp_b[-1])
    carry = h_ref[SUBLANES - 1:SUBLANES, :]
    h_end = ga * carry + gb
    h_ref[...] = h_end[G - SUBLANES:]
    rolled = pltpu.roll(h_end, 1, axis=0)
    row8r = jax.lax.broadcasted_iota(jnp.int32, (SUBLANES, d_rg), 0)
    h_prev = jnp.concatenate([jnp.where(row8r < 1, carry, rolled[:SUBLANES]), rolled[SUBLANES:]], axis=0)
    for r in range(PHASES):
        h_r = comp_a[r] * h_prev + comp_b[r]
        gate_r = jnp.concatenate([phase(n_cs + j, r) for j in range(n_rs)], axis=1)
        y_r = _rms(h_r * _gelu_tanh(gate_r), rg_gain_ref[...])
        for j in range(n_rs):
            yrg_ref[j, pl.ds(r, G, stride=PHASES), :] = y_r[:, j * LANES:(j + 1) * LANES]
    for j in range(n_rs):
        y_ref[:, j * LANES:(j + 1) * LANES] = yrg_ref[j].astype(BF16)
    yield MLP_PHASES_AFTER["rg_scan"]

    v_all = proj[:, d_conv + d_rg:d_conv + d_rg + d_ml]
    o_all = proj[:, d_conv + d_rg + d_ml:]

    lane_g = jax.lax.broadcasted_iota(jnp.int32, gates.shape, 1)
    log_sig = jnp.minimum(gates, 0.0) - jnp.log1p(jnp.exp(-jnp.abs(gates)))
    gates_t = jnp.where(lane_g < ML_HEADS, gates, log_sig).T[:SUBLANES]

    causal = (jax.lax.broadcasted_iota(jnp.int32, (L, L), 1)
              <= jax.lax.broadcasted_iota(jnp.int32, (L, L), 0))
    ones_ext = jnp.ones((L, head_dim), BF16)

    for c in range(n_chunks):
        rows = slice(c * L, (c + 1) * L)
        li = gates_t[:, rows]
        lf = pltpu.roll(li, ML_HEADS, axis=0)
        bcum = _lane_scan(lf, jnp.add, 0.0)
        b_last = jnp.sum(lf, axis=1, keepdims=True)
        row_b = li - bcum
        cmax = _lane_scan(row_b, jnp.maximum, -jnp.inf)
        w_loc = b_last + row_b
        m_loc = jnp.max(w_loc, axis=1, keepdims=True)
        m_prev_b = m_ref[...]
        m_prev = jnp.max(m_prev_b, axis=1, keepdims=True)
        m_s = jnp.maximum(bcum + m_prev, bcum + cmax)
        m_new = jnp.maximum(b_last + m_prev, m_loc)
        row_ref[0] = bcum - m_s
        row_ref[1] = m_s
        row_ref[2] = row_b
        row_ref[3] = jnp.exp(w_loc - m_loc)
        row_ref[4] = jnp.broadcast_to(jnp.exp(b_last + m_prev - m_new), (SUBLANES, L))
        row_ref[5] = jnp.broadcast_to(jnp.exp(m_loc - m_new), (SUBLANES, L))
        row_ref[6] = m_prev_b
        m_ref[...] = jnp.broadcast_to(m_new, (SUBLANES, L))

        heads = range(ML_HEADS)
        q_f = [qk_ref[h, rows, :] for h in heads]
        k_t = [qk_ref[ML_HEADS + h, rows, :].T for h in heads]
        s = [jnp.dot(q_f[h].astype(BF16), k_t[h].astype(BF16), preferred_element_type=F32) for h in heads]
        yield MLP_PHASES_AFTER["ml_stage"][0]

        lhs, rhs, v_ext, ms_b = [], [], [], []
        for h in heads:
            cols = slice(h * head_dim, (h + 1) * head_dim)
            ca_b = jnp.broadcast_to(row_ref[0, h:h + 1, :], (head_dim, L)).T
            ms_b.append(jnp.broadcast_to(row_ref[1, h:h + 1, :], (head_dim, L)).T)
            p = jnp.exp(jnp.where(causal, ca_b + row_ref[2, h:h + 1, :], -jnp.inf))
            inter_w = jnp.exp(ca_b + row_ref[6, h:h + 1, :])
            v_ext.append(jnp.concatenate([v_all[rows, cols].astype(BF16), ones_ext], axis=1))
            lhs.append(jnp.concatenate([(s[h] * p).astype(BF16), (q_f[h] * inter_w).astype(BF16)], axis=1))
            rhs.append(jnp.concatenate([v_ext[h], c_ref[h].astype(BF16)], axis=0))
        yield MLP_PHASES_AFTER["ml_stage"][1]

        nd = [jnp.dot(lhs[h], rhs[h], preferred_element_type=F32) for h in heads]
        upd = [jnp.dot((k_t[h] * row_ref[3, h:h + 1, :]).astype(BF16), v_ext[h], preferred_element_type=F32)
               for h in heads]
        yield MLP_PHASES_AFTER["ml_stage"][2]

        for h in heads:
            cols = slice(h * head_dim, (h + 1) * head_dim)
            hh = nd[h][:, :head_dim] / jnp.maximum(jnp.abs(nd[h][:, head_dim:]), jnp.exp(-ms_b[h]))
            hh = _sigmoid(o_all[rows, cols]) * hh
            y_ref[rows, d_rg + h * head_dim:d_rg + (h + 1) * head_dim] = _rms(
                hh, ml_gain_ref[:, cols]).astype(BF16)
            s_old = row_ref[4, h:h + 1, :]
            s_new = row_ref[5, h:h + 1, :]
            c_ref[h] = (jnp.concatenate([s_old, s_old], axis=1) * c_ref[h]
                        + jnp.concatenate([s_new, s_new], axis=1) * upd[h])
        yield MLP_PHASES_AFTER["ml_stage"][3]

    mix = jnp.dot(y_ref[...], w_out_ref[...], preferred_element_type=F32)
    yield MLP_PHASES_AFTER["w_out"]
    return x + _rms(mix, post_g_ref[...])


def _mlp_tile(h_ref, pre_g_ref, w_up_ref, w_down_ref, post_g_ref):
    vb = _rms(h_ref[...], pre_g_ref[...]).astype(BF16)
    d_ff = w_up_ref.shape[1]
    acc = jnp.zeros(h_ref.shape, F32)
    for c in range(d_ff // MLP_FF_CHUNK):
        cols = slice(c * MLP_FF_CHUNK, (c + 1) * MLP_FF_CHUNK)
        f = jnp.maximum(jnp.dot(vb, w_up_ref[:, cols], preferred_element_type=F32), 0.0)
        acc = acc + jnp.dot((f * f).astype(BF16), w_down_ref[cols, :], preferred_element_type=F32)
        yield
    return h_ref[...] + _rms(acc, post_g_ref[...])


def _interleave(primary, secondary, lead):
    results = [None, None]

    def advance(idx, gen):
        if results[idx] is None:
            try:
                return next(gen)
            except StopIteration as stop:
                results[idx] = (stop.value,)
        return 0

    for _ in range(lead):
        advance(1, secondary)
    while results[0] is None:
        for _ in range(advance(0, primary)):
            advance(1, secondary)
    while results[1] is None:
        advance(1, secondary)
    return results[0][0], results[1][0]


def _layer_kernel(x_ref, *refs, tiles_per_seq, n_mixer_in, **dims):
    mixer_in = refs[:n_mixer_in]
    mlp_in = refs[n_mixer_in:n_mixer_in + 4]
    out_ref, h1_ref = refs[n_mixer_in + 4:n_mixer_in + 6]
    state = refs[n_mixer_in + 6:]
    s = pl.program_id(0)

    @pl.when(s == 0)
    def _():
        h1_ref[...] = jnp.zeros_like(h1_ref)

    h1_new, out = _interleave(
        _mixer_tile(x_ref[0], s % tiles_per_seq == 0, *mixer_in, *state, **dims),
        _mlp_tile(h1_ref, *mlp_in),
        MLP_PHASES_AFTER["start"])
    out_ref[0] = out
    h1_ref[...] = h1_new


def _const_spec(shape):
    zeros = (0,) * len(shape)
    return pl.BlockSpec(shape, lambda *_: zeros, pipeline_mode=pl.Buffered(1))


def _block_diag_gate(w):
    nb, bd, _ = w.shape
    per = MXU_WIDTH // bd
    w = w.reshape(nb // per, per, bd, bd)
    eye = jnp.eye(per, dtype=w.dtype)
    return jnp.einsum("gpij,pq->gpiqj", w, eye).reshape(nb // per, MXU_WIDTH, MXU_WIDTH)


def _layer(x, pre_gain, w_in, rg_conv_w, rg_conv_b, gate_r_w, gate_r_b, gate_i_w, gate_i_b, lam,
           ml_conv_w, ml_conv_b, igate_b, fgate_b, rg_gain, ml_gain, w_out, post_gain,
           pre_mlp_gain, w_up, w_down, post_mlp_gain):
    bsz, seq, d_model = x.shape
    assert w_up.shape[1] % MLP_FF_CHUNK == 0
    d_rg = lam.shape[0]
    d_ml = ml_gain.shape[0]
    head_dim = d_ml // ML_HEADS
    assert head_dim == LANES and ML_CHUNK == LANES and 2 * ML_HEADS == SUBLANES
    assert seq % SEQ_TILE == 0 and SEQ_TILE % ML_CHUNK == 0 and d_rg % MXU_WIDTH == 0
    d_conv = d_rg + 2 * d_ml

    o0 = 0
    rg_x_w = w_in[:, o0:o0 + d_rg]; o0 += d_rg
    rg_gate_w = w_in[:, o0:o0 + d_rg]; o0 += d_rg
    q_w = w_in[:, o0:o0 + d_ml]; o0 += d_ml
    k_w = w_in[:, o0:o0 + d_ml]; o0 += d_ml
    v_w = w_in[:, o0:o0 + d_ml]; o0 += d_ml
    o_w = w_in[:, o0:o0 + d_ml]; o0 += d_ml
    if_w = w_in[:, o0:]
    w_main = jnp.concatenate([rg_x_w, q_w, k_w, rg_gate_w, v_w, o_w], axis=1).astype(BF16)
    w_if = jnp.pad(if_w, ((0, 0), (0, LANES - 2 * ML_HEADS))).astype(BF16)
    b_if = jnp.pad(jnp.concatenate([igate_b, fgate_b]), (0, LANES - 2 * ML_HEADS)).reshape(1, LANES)
    conv_w = jnp.concatenate([rg_conv_w, ml_conv_w], axis=1)
    conv_b = jnp.concatenate([rg_conv_b, ml_conv_b]).reshape(1, d_conv)
    w_gate = jnp.concatenate([_block_diag_gate(gate_r_w), _block_diag_gate(gate_i_w)], axis=2).astype(BF16)

    row = lambda v: v.reshape(1, -1)
    mixer_in = (row(pre_gain), w_main, w_if, b_if, conv_w, conv_b, w_gate, row(gate_r_b), row(gate_i_b),
                row(lam), row(rg_gain), row(ml_gain), w_out.astype(BF16), row(post_gain))
    mlp_in = (row(pre_mlp_gain), w_up.astype(BF16), w_down.astype(BF16), row(post_mlp_gain))

    tiles_per_seq = seq // SEQ_TILE
    n_tiles = bsz * tiles_per_seq

    def tile_block(t):
        return (t // tiles_per_seq, t % tiles_per_seq, 0)

    x_spec = pl.BlockSpec((1, SEQ_TILE, d_model), lambda s: tile_block(jnp.minimum(s, n_tiles - 1)))
    out_spec = pl.BlockSpec((1, SEQ_TILE, d_model), lambda s: tile_block(jnp.maximum(s - 1, 0)))
    in_specs = [x_spec] + [_const_spec(op.shape) for op in mixer_in + mlp_in]
    kern = functools.partial(_layer_kernel, tiles_per_seq=tiles_per_seq, n_mixer_in=len(mixer_in),
                             d_rg=d_rg, d_ml=d_ml, head_dim=head_dim)
    return pl.pallas_call(
        kern,
        grid=(n_tiles + 1,),
        in_specs=in_specs,
        out_specs=out_spec,
        out_shape=jax.ShapeDtypeStruct(x.shape, x.dtype),
        scratch_shapes=[
            pltpu.VMEM((SEQ_TILE, d_model), F32),
            pltpu.VMEM(((d_conv + d_rg) // LANES, SUBLANES + SEQ_TILE, LANES), F32),
            pltpu.VMEM((2 * d_ml // LANES, SEQ_TILE, LANES), F32),
            pltpu.VMEM((d_rg // LANES, SEQ_TILE, LANES), F32),
            pltpu.VMEM((SUBLANES, d_rg), F32),
            pltpu.VMEM((ML_HEADS, head_dim, 2 * head_dim), F32),
            pltpu.VMEM((SUBLANES, ML_CHUNK), F32),
            pltpu.VMEM((7, SUBLANES, ML_CHUNK), F32),
            pltpu.VMEM((SEQ_TILE, d_rg + d_ml), BF16),
        ],
        compiler_params=pltpu.CompilerParams(
            dimension_semantics=("arbitrary",), vmem_limit_bytes=VMEM_LIMIT_BYTES),
        name="layer",
    )(x, *mixer_in, *mlp_in)


def kernel(x, pre_mix_gain, w_in, rg_conv_w, rg_conv_b, rg_gate_r_w, rg_gate_r_b, rg_gate_i_w, rg_gate_i_b, rg_lambda, ml_conv_w, ml_conv_b, ml_igate_b, ml_fgate_b, rg_out_gain, ml_out_gain, w_out, post_mix_gain, pre_mlp_gain, mlp_w_up, mlp_w_down, post_mlp_gain):
    h = x
    for l in range(w_in.shape[0]):
        h = _layer(h, pre_mix_gain[l], w_in[l], rg_conv_w[l], rg_conv_b[l], rg_gate_r_w[l], rg_gate_r_b[l],
                   rg_gate_i_w[l], rg_gate_i_b[l], rg_lambda[l], ml_conv_w[l], ml_conv_b[l], ml_igate_b[l],
                   ml_fgate_b[l], rg_out_gain[l], ml_out_gain[l], w_out[l], post_mix_gain[l],
                   pre_mlp_gain[l], mlp_w_up[l], mlp_w_down[l], post_mlp_gain[l])
    return h
```

```python
import functools

import jax
import jax.numpy as jnp
from jax.experimental import pallas as pl
from jax.experimental.pallas import tpu as pltpu

F32 = jnp.float32
BF16 = jnp.bfloat16

RG_BLOCKS = 8
RG_C = 8.0
ML_HEADS = 4
CONV_WIDTH = 4
EPS = 1e-6

SUBLANES = 8
LANES = 128
MXU_WIDTH = 256

SEQ_TILE = 256
ML_CHUNK = LANES
PHASES = 4
MLP_FF_CHUNK = 512
WEIGHT_BLOCK = 512
MLP_PHASES_AFTER = {"start": 0, "w_in": 1, "conv": 1, "rg_gates": 1, "rg_scan": 1, "ml_stage": (1, 0, 1, 0),
                    "w_out": 1}
VMEM_LIMIT_BYTES = 56 * 1024 * 1024


def _rms(x, gain):
    return x * jax.lax.rsqrt(jnp.mean(x * x, axis=-1, keepdims=True) + EPS) * gain


def _sigmoid(x):
    return 1.0 / (1.0 + jnp.exp(-x))


def _softplus(x):
    return jnp.maximum(x, 0.0) + jnp.log1p(jnp.exp(-jnp.abs(x)))


def _gelu_tanh(x):
    c = 0.7978845608028654
    return 0.5 * x * (1.0 + jnp.tanh(c * (x + 0.044715 * (x * x * x))))


def _lane_scan(x, op, fill):
    n = x.shape[1]
    lane = jax.lax.broadcasted_iota(jnp.int32, x.shape, 1)
    d = 1
    while d < n:
        shifted = jnp.where(lane < d, fill, pltpu.roll(x, d, axis=1))
        x = op(x, shifted)
        d *= 2
    return x


def _affine_row_scan(sa, sb):
    n, width = sa.shape
    row8 = jax.lax.broadcasted_iota(jnp.int32, (SUBLANES, width), 0)
    d = 1
    while d < n:
        if d < SUBLANES:
            ra = pltpu.roll(sa, d, axis=0)
            rb = pltpu.roll(sb, d, axis=0)
            a_sh = jnp.concatenate([jnp.where(row8 < d, 1.0, ra[:SUBLANES]), ra[SUBLANES:]], axis=0)
            b_sh = jnp.concatenate([jnp.where(row8 < d, 0.0, rb[:SUBLANES]), rb[SUBLANES:]], axis=0)
            sb = sb + sa * b_sh
            sa = sa * a_sh
        else:
            sb = jnp.concatenate([sb[:d], sb[d:] + sa[d:] * sb[:n - d]], axis=0)
            sa = jnp.concatenate([sa[:d], sa[d:] * sa[:n - d]], axis=0)
        d *= 2
    return sa, sb


def _mixer_tile(x, seq_start, pre_g_ref, w_main_ref, w_if_ref, b_if_ref, conv_w_ref, conv_b_ref, w_gate_ref,
                b_r_ref, b_i_ref, lam_ref, rg_gain_ref, ml_gain_ref, w_out_ref, post_g_ref,
                cbuf_ref, qk_ref, yrg_ref, h_ref, c_ref, m_ref, row_ref, y_ref,
                *, d_rg, d_ml, head_dim):
    ts = x.shape[0]
    d_conv = d_rg + 2 * d_ml
    n_chunks = ts // ML_CHUNK
    L = ML_CHUNK

    @pl.when(seq_start)
    def _():
        cbuf_ref[:, ts:ts + SUBLANES, :] = jnp.zeros((cbuf_ref.shape[0], SUBLANES, LANES), F32)
        h_ref[...] = jnp.zeros_like(h_ref)
        c_ref[...] = jnp.zeros_like(c_ref)
        m_ref[...] = jnp.zeros_like(m_ref)

    ub = _rms(x, pre_g_ref[...]).astype(BF16)
    proj = jnp.dot(ub, w_main_ref[...], preferred_element_type=F32)
    gates = jnp.dot(ub, w_if_ref[...], preferred_element_type=F32) + b_if_ref[...]
    yield MLP_PHASES_AFTER["w_in"]

    n_cs = d_conv // LANES
    n_rs = d_rg // LANES
    G = ts // PHASES
    cbuf_ref[:n_cs, :SUBLANES, :] = cbuf_ref[:n_cs, ts:ts + SUBLANES, :]
    for j in range(n_cs + n_rs):
        cbuf_ref[j, SUBLANES:, :] = proj[:, j * LANES:(j + 1) * LANES]

    def phase(j, e):
        return cbuf_ref[j, pl.ds(SUBLANES + e, G, stride=PHASES), :]

    conv_ph = [[None] * n_cs for _ in range(PHASES)]
    for j in range(n_cs):
        cols = slice(j * LANES, (j + 1) * LANES)
        taps = {e: phase(j, e) for e in range(1 - CONV_WIDTH, PHASES)}
        for r in range(PHASES):
            acc = taps[r] * conv_w_ref[CONV_WIDTH - 1:CONV_WIDTH, cols] + conv_b_ref[:, cols]
            for k in range(1, CONV_WIDTH):
                acc = acc + taps[r - k] * conv_w_ref[CONV_WIDTH - 1 - k:CONV_WIDTH - k, cols]
            conv_ph[r][j] = acc
        if j >= n_rs:
            scale = head_dim ** -0.5 if j < n_rs + d_ml // LANES else 1.0
            for r in range(PHASES):
                val = conv_ph[r][j]
                val = val * _sigmoid(val)
                qk_ref[j - n_rs, pl.ds(r, G, stride=PHASES), :] = val * scale if scale != 1.0 else val
        if j % 2 == 1:
            yield MLP_PHASES_AFTER["conv"]

    xc = jnp.concatenate([jnp.concatenate(conv_ph[r][:n_rs], axis=1) for r in range(PHASES)], axis=0)
    r_parts, i_parts = [], []
    for g in range(d_rg // MXU_WIDTH):
        gg = jnp.dot(xc[:, g * MXU_WIDTH:(g + 1) * MXU_WIDTH].astype(BF16), w_gate_ref[g],
                     preferred_element_type=F32)
        r_parts.append(gg[:, :MXU_WIDTH])
        i_parts.append(gg[:, MXU_WIDTH:])
    r = _sigmoid(jnp.concatenate(r_parts, axis=1) + b_r_ref[...])
    i_gate = _sigmoid(jnp.concatenate(i_parts, axis=1) + b_i_ref[...])
    log_a = r * (-RG_C * _softplus(-lam_ref[...]))
    a = jnp.exp(log_a)
    b_in = jnp.sqrt(-jnp.tanh(log_a) * (a * a + 1.0)) * (i_gate * xc)
    yield MLP_PHASES_AFTER["rg_gates"]

    comp_a, comp_b = [a[:G]], [b_in[:G]]
    for r in range(1, PHASES):
        a_r, b_r = a[r * G:(r + 1) * G], b_in[r * G:(r + 1) * G]
        comp_b.append(a_r * comp_b[-1] + b_r)
        comp_a.append(a_r * comp_a[-1])
    ga, gb = _affine_row_scan(comp_a[-1], comp_b[-1])
    carry = h_ref[SUBLANES - 1:SUBLANES, :]
    h_end = ga * carry + gb
    h_ref[...] = h_end[G - SUBLANES:]
    rolled = pltpu.roll(h_end, 1, axis=0)
    row8r = jax.lax.broadcasted_iota(jnp.int32, (SUBLANES, d_rg), 0)
    h_prev = jnp.concatenate([jnp.where(row8r < 1, carry, rolled[:SUBLANES]), rolled[SUBLANES:]], axis=0)
    yield MLP_PHASES_AFTER["rg_scan"]
    for r in range(PHASES):
        h_r = comp_a[r] * h_prev + comp_b[r]
        gate_r = jnp.concatenate([phase(n_cs + j, r) for j in range(n_rs)], axis=1)
        y_r = _rms(h_r * _gelu_tanh(gate_r), rg_gain_ref[...])
        for j in range(n_rs):
            yrg_ref[j, pl.ds(r, G, stride=PHASES), :] = y_r[:, j * LANES:(j + 1) * LANES]
        if r % 2 == 1:
            yield MLP_PHASES_AFTER["rg_scan"]
    for j in range(n_rs):
        y_ref[:, j * LANES:(j + 1) * LANES] = yrg_ref[j].astype(BF16)

    v_all = proj[:, d_conv + d_rg:d_conv + d_rg + d_ml]
    o_all = proj[:, d_conv + d_rg + d_ml:]

    lane_g = jax.lax.broadcasted_iota(jnp.int32, gates.shape, 1)
    log_sig = jnp.minimum(gates, 0.0) - jnp.log1p(jnp.exp(-jnp.abs(gates)))
    gates_t = jnp.where(lane_g < ML_HEADS, gates, log_sig).T[:SUBLANES]

    causal = (jax.lax.broadcasted_iota(jnp.int32, (L, L), 1)
              <= jax.lax.broadcasted_iota(jnp.int32, (L, L), 0))
    ones_ext = jnp.ones((L, head_dim), BF16)

    for c in range(n_chunks):
        rows = slice(c * L, (c + 1) * L)
        li = gates_t[:, rows]
        lf = pltpu.roll(li, ML_HEADS, axis=0)
        bcum = _lane_scan(lf, jnp.add, 0.0)
        b_last = jnp.sum(lf, axis=1, keepdims=True)
        row_b = li - bcum
        cmax = _lane_scan(row_b, jnp.maximum, -jnp.inf)
        w_loc = b_last + row_b
        m_loc = jnp.max(w_loc, axis=1, keepdims=True)
        m_prev_b = m_ref[...]
        m_prev = jnp.max(m_prev_b, axis=1, keepdims=True)
        m_s = jnp.maximum(bcum + m_prev, bcum + cmax)
        m_new = jnp.maximum(b_last + m_prev, m_loc)
        row_ref[0] = bcum - m_s
        row_ref[1] = m_s
        row_ref[2] = row_b
        row_ref[3] = jnp.exp(w_loc - m_loc)
        row_ref[4] = jnp.broadcast_to(jnp.exp(b_last + m_prev - m_new), (SUBLANES, L))
        row_ref[5] = jnp.broadcast_to(jnp.exp(m_loc - m_new), (SUBLANES, L))
        row_ref[6] = m_prev_b
        m_ref[...] = jnp.broadcast_to(m_new, (SUBLANES, L))

        heads = range(ML_HEADS)
        q_f = [qk_ref[h, rows, :] for h in heads]
        k_t = [qk_ref[ML_HEADS + h, rows, :].T for h in heads]
        s = [jnp.dot(q_f[h].astype(BF16), k_t[h].astype(BF16), preferred_element_type=F32) for h in heads]
        yield MLP_PHASES_AFTER["ml_stage"][0]

        lhs, rhs, v_ext, ms_b = [], [], [], []
        for h in heads:
            cols = slice(h * head_dim, (h + 1) * head_dim)
            ca_b = jnp.broadcast_to(row_ref[0, h:h + 1, :], (head_dim, L)).T
            ms_b.append(jnp.broadcast_to(row_ref[1, h:h + 1, :], (head_dim, L)).T)
            p = jnp.exp(jnp.where(causal, ca_b + row_ref[2, h:h + 1, :], -jnp.inf))
            inter_w = jnp.exp(ca_b + row_ref[6, h:h + 1, :])
            v_ext.append(jnp.concatenate([v_all[rows, cols].astype(BF16), ones_ext], axis=1))
            lhs.append(jnp.concatenate([(s[h] * p).astype(BF16), (q_f[h] * inter_w).astype(BF16)], axis=1))
            rhs.append(jnp.concatenate([v_ext[h], c_ref[h].astype(BF16)], axis=0))
        yield MLP_PHASES_AFTER["ml_stage"][1]

        nd = [jnp.dot(lhs[h], rhs[h], preferred_element_type=F32) for h in heads]
        upd = [jnp.dot((k_t[h] * row_ref[3, h:h + 1, :]).astype(BF16), v_ext[h], preferred_element_type=F32)
               for h in heads]
        yield MLP_PHASES_AFTER["ml_stage"][2]

        for h in heads:
            cols = slice(h * head_dim, (h + 1) * head_dim)
            hh = nd[h][:, :head_dim] / jnp.maximum(jnp.abs(nd[h][:, head_dim:]), jnp.exp(-ms_b[h]))
            hh = _sigmoid(o_all[rows, cols]) * hh
            y_ref[rows, d_rg + h * head_dim:d_rg + (h + 1) * head_dim] = _rms(
                hh, ml_gain_ref[:, cols]).astype(BF16)
            s_old = row_ref[4, h:h + 1, :]
            s_new = row_ref[5, h:h + 1, :]
            c_ref[h] = (jnp.concatenate([s_old, s_old], axis=1) * c_ref[h]
                        + jnp.concatenate([s_new, s_new], axis=1) * upd[h])
        yield MLP_PHASES_AFTER["ml_stage"][3]

    mix = jnp.dot(y_ref[...], w_out_ref[...], preferred_element_type=F32)
    yield MLP_PHASES_AFTER["w_out"]
    return x + _rms(mix, post_g_ref[...])


def _mlp_tile(h_ref, pre_g_ref, w_up_ref, w_down_ref, post_g_ref):
    vb = _rms(h_ref[...], pre_g_ref[...]).astype(BF16)
    d_ff = w_up_ref.shape[1]
    acc = jnp.zeros(h_ref.shape, F32)
    for c in range(d_ff // MLP_FF_CHUNK):
        cols = slice(c * MLP_FF_CHUNK, (c + 1) * MLP_FF_CHUNK)
        f = jnp.maximum(jnp.dot(vb, w_up_ref[:, cols], preferred_element_type=F32), 0.0)
        yield
        acc = acc + jnp.dot((f * f).astype(BF16), w_down_ref[cols, :], preferred_element_type=F32)
        yield
    return h_ref[...] + _rms(acc, post_g_ref[...])


def _interleave(primary, secondary, lead):
    results = [None, None]

    def advance(idx, gen):
        if results[idx] is None:
            try:
                return next(gen)
            except StopIteration as stop:
                results[idx] = (stop.value,)
        return 0

    for _ in range(lead):
        advance(1, secondary)
    while results[0] is None:
        for _ in range(advance(0, primary)):
            advance(1, secondary)
    while results[1] is None:
        advance(1, secondary)
    return results[0][0], results[1][0]


def _weight_copy_jobs(hbm, vmem, src_col_starts):
    rows, cols = vmem.shape
    jobs = []
    for cb in range(cols // WEIGHT_BLOCK):
        src_c0 = cb * WEIGHT_BLOCK if src_col_starts is None else src_col_starts[cb]
        for rb in range(rows // WEIGHT_BLOCK):
            jobs.append((hbm, rb * WEIGHT_BLOCK, src_c0, vmem, rb * WEIGHT_BLOCK, cb * WEIGHT_BLOCK))
    return jobs


def _load_weights(jobs, stage_ref, sem_ref):
    def copy(i):
        src, r0, c0 = jobs[i][:3]
        slot = i % 2
        return pltpu.make_async_copy(
            src.at[pl.ds(r0, WEIGHT_BLOCK), pl.ds(c0, WEIGHT_BLOCK)], stage_ref.at[slot], sem_ref.at[slot])

    copy(0).start()
    for i in range(len(jobs)):
        if i + 1 < len(jobs):
            copy(i + 1).start()
        copy(i).wait()
        dst, r0, c0 = jobs[i][3:]
        dst[r0:r0 + WEIGHT_BLOCK, c0:c0 + WEIGHT_BLOCK] = stage_ref[i % 2].astype(BF16)


def _layer_kernel(x_ref, *refs, tiles_per_seq, n_small, w_in_col_starts, **dims):
    (pre_g, w_if, b_if, conv_w, conv_b, w_gate, b_r, b_i, lam, rg_gain, ml_gain, post_g,
     mlp_pre_g, mlp_post_g) = refs[:n_small]
    w_in_hbm, w_out_hbm, w_up_hbm, w_down_hbm = refs[n_small:n_small + 4]
    out_ref = refs[n_small + 4]
    w_main, w_out, w_up, w_down, stage_ref, sem_ref, h1_ref = refs[n_small + 5:n_small + 12]
    state = refs[n_small + 12:]
    s = pl.program_id(0)

    @pl.when(s == 0)
    def _():
        h1_ref[...] = jnp.zeros_like(h1_ref)
        _load_weights(_weight_copy_jobs(w_in_hbm, w_main, w_in_col_starts)
                      + _weight_copy_jobs(w_up_hbm, w_up, None)
                      + _weight_copy_jobs(w_down_hbm, w_down, None)
                      + _weight_copy_jobs(w_out_hbm, w_out, None), stage_ref, sem_ref)

    h1_new, out = _interleave(
        _mixer_tile(x_ref[0], s % tiles_per_seq == 0, pre_g, w_main, w_if, b_if, conv_w, conv_b, w_gate,
                    b_r, b_i, lam, rg_gain, ml_gain, w_out, post_g, *state, **dims),
        _mlp_tile(h1_ref, mlp_pre_g, w_up, w_down, mlp_post_g),
        MLP_PHASES_AFTER["start"])
    out_ref[0] = out
    h1_ref[...] = h1_new


def _const_spec(shape):
    zeros = (0,) * len(shape)
    return pl.BlockSpec(shape, lambda *_: zeros, pipeline_mode=pl.Buffered(1))


def _block_diag_gate(w):
    nb, bd, _ = w.shape
    per = MXU_WIDTH // bd
    w = w.reshape(nb // per, per, bd, bd)
    eye = jnp.eye(per, dtype=w.dtype)
    return jnp.einsum("gpij,pq->gpiqj", w, eye).reshape(nb // per, MXU_WIDTH, MXU_WIDTH)


def _layer(x, pre_gain, w_in, rg_conv_w, rg_conv_b, gate_r_w, gate_r_b, gate_i_w, gate_i_b, lam,
           ml_conv_w, ml_conv_b, igate_b, fgate_b, rg_gain, ml_gain, w_out, post_gain,
           pre_mlp_gain, w_up, w_down, post_mlp_gain):
    bsz, seq, d_model = x.shape
    assert w_up.shape[1] % MLP_FF_CHUNK == 0
    d_rg = lam.shape[0]
    d_ml = ml_gain.shape[0]
    head_dim = d_ml // ML_HEADS
    assert head_dim == LANES and ML_CHUNK == LANES and 2 * ML_HEADS == SUBLANES
    assert seq % SEQ_TILE == 0 and SEQ_TILE % ML_CHUNK == 0 and d_rg % MXU_WIDTH == 0
    d_conv = d_rg + 2 * d_ml

    assert d_rg == WEIGHT_BLOCK and d_ml == WEIGHT_BLOCK
    starts = dict(rg_x=0, rg_gate=d_rg, q=2 * d_rg, k=2 * d_rg + d_ml, v=2 * d_rg + 2 * d_ml, o=2 * d_rg + 3 * d_ml)
    w_in_col_starts = tuple(starts[n] for n in ("rg_x", "q", "k", "rg_gate", "v", "o"))
    d_main = 2 * d_rg + 4 * d_ml
    if_w = w_in[:, d_main:]
    w_if = jnp.pad(if_w, ((0, 0), (0, LANES - 2 * ML_HEADS))).astype(BF16)
    b_if = jnp.pad(jnp.concatenate([igate_b, fgate_b]), (0, LANES - 2 * ML_HEADS)).reshape(1, LANES)
    conv_w = jnp.concatenate([rg_conv_w, ml_conv_w], axis=1)
    conv_b = jnp.concatenate([rg_conv_b, ml_conv_b]).reshape(1, d_conv)
    w_gate = jnp.concatenate([_block_diag_gate(gate_r_w), _block_diag_gate(gate_i_w)], axis=2).astype(BF16)

    row = lambda v: v.reshape(1, -1)
    small = (row(pre_gain), w_if, b_if, conv_w, conv_b, w_gate, row(gate_r_b), row(gate_i_b), row(lam),
             row(rg_gain), row(ml_gain), row(post_gain), row(pre_mlp_gain), row(post_mlp_gain))
    big = (w_in, w_out, w_up, w_down)
    d_ff = w_up.shape[1]
    for w in (w_out, w_up, w_down):
        assert w.shape[0] % WEIGHT_BLOCK == 0 and w.shape[1] % WEIGHT_BLOCK == 0

    tiles_per_seq = seq // SEQ_TILE
    n_tiles = bsz * tiles_per_seq

    def tile_block(t):
        return (t // tiles_per_seq, t % tiles_per_seq, 0)

    x_spec = pl.BlockSpec((1, SEQ_TILE, d_model), lambda s: tile_block(jnp.minimum(s, n_tiles - 1)))
    out_spec = pl.BlockSpec((1, SEQ_TILE, d_model), lambda s: tile_block(jnp.maximum(s - 1, 0)))
    in_specs = ([x_spec] + [_const_spec(op.shape) for op in small]
                + [pl.BlockSpec(memory_space=pl.ANY)] * len(big))
    kern = functools.partial(_layer_kernel, tiles_per_seq=tiles_per_seq, n_small=len(small),
                             w_in_col_starts=w_in_col_starts, d_rg=d_rg, d_ml=d_ml, head_dim=head_dim)
    return pl.pallas_call(
        kern,
        grid=(n_tiles + 1,),
        in_specs=in_specs,
        out_specs=out_spec,
        out_shape=jax.ShapeDtypeStruct(x.shape, x.dtype),
        scratch_shapes=[
            pltpu.VMEM((d_model, d_main), BF16),
            pltpu.VMEM((d_rg + d_ml, d_model), BF16),
            pltpu.VMEM((d_model, d_ff), BF16),
            pltpu.VMEM((d_ff, d_model), BF16),
            pltpu.VMEM((2, WEIGHT_BLOCK, WEIGHT_BLOCK), F32),
            pltpu.SemaphoreType.DMA((2,)),
            pltpu.VMEM((SEQ_TILE, d_model), F32),
            pltpu.VMEM(((d_conv + d_rg) // LANES, SUBLANES + SEQ_TILE, LANES), F32),
            pltpu.VMEM((2 * d_ml // LANES, SEQ_TILE, LANES), F32),
            pltpu.VMEM((d_rg // LANES, SEQ_TILE, LANES), F32),
            pltpu.VMEM((SUBLANES, d_rg), F32),
            pltpu.VMEM((ML_HEADS, head_dim, 2 * head_dim), F32),
            pltpu.VMEM((SUBLANES, ML_CHUNK), F32),
            pltpu.VMEM((7, SUBLANES, ML_CHUNK), F32),
            pltpu.VMEM((SEQ_TILE, d_rg + d_ml), BF16),
        ],
        compiler_params=pltpu.CompilerParams(
            dimension_semantics=("arbitrary",), vmem_limit_bytes=VMEM_LIMIT_BYTES),
        name="layer",
    )(x, *small, *big)


def kernel(x, pre_mix_gain, w_in, rg_conv_w, rg_conv_b, rg_gate_r_w, rg_gate_r_b, rg_gate_i_w, rg_gate_i_b, rg_lambda, ml_conv_w, ml_conv_b, ml_igate_b, ml_fgate_b, rg_out_gain, ml_out_gain, w_out, post_mix_gain, pre_mlp_gain, mlp_w_up, mlp_w_down, post_mlp_gain):
    h = x
    for l in range(w_in.shape[0]):
        h = _layer(h, pre_mix_gain[l], w_in[l], rg_conv_w[l], rg_conv_b[l], rg_gate_r_w[l], rg_gate_r_b[l],
                   rg_gate_i_w[l], rg_gate_i_b[l], rg_lambda[l], ml_conv_w[l], ml_conv_b[l], ml_igate_b[l],
                   ml_fgate_b[l], rg_out_gain[l], ml_out_gain[l], w_out[l], post_mix_gain[l],
                   pre_mlp_gain[l], mlp_w_up[l], mlp_w_down[l], post_mlp_gain[l])
    return h
```

```python
import functools

import jax
import jax.numpy as jnp
from jax.experimental import pallas as pl
from jax.experimental.pallas import tpu as pltpu

F32 = jnp.float32
BF16 = jnp.bfloat16

RG_BLOCKS = 8
RG_C = 8.0
ML_HEADS = 4
CONV_WIDTH = 4
EPS = 1e-6

SUBLANES = 8
LANES = 128
MXU_WIDTH = 256

SEQ_TILE = 512
ML_CHUNK = LANES
PHASES = 4
MLP_FF_CHUNK = 512
WEIGHT_BLOCK = 512
MLP_PHASES_AFTER = {"start": 0, "w_in": 1, "conv": 1, "rg_gates": 1, "rg_scan": 1, "ml_stage": (1, 0, 1, 0),
                    "w_out": 1}
VMEM_LIMIT_BYTES = 60 * 1024 * 1024


def _rms(x, gain):
    return x * jax.lax.rsqrt(jnp.mean(x * x, axis=-1, keepdims=True) + EPS) * gain


def _sigmoid(x):
    return 1.0 / (1.0 + jnp.exp(-x))


def _softplus(x):
    return jnp.maximum(x, 0.0) + jnp.log1p(jnp.exp(-jnp.abs(x)))


def _gelu_tanh(x):
    c = 0.7978845608028654
    return 0.5 * x * (1.0 + jnp.tanh(c * (x + 0.044715 * (x * x * x))))


def _lane_scan(x, op, fill):
    n = x.shape[1]
    lane = jax.lax.broadcasted_iota(jnp.int32, x.shape, 1)
    d = 1
    while d < n:
        shifted = jnp.where(lane < d, fill, pltpu.roll(x, d, axis=1))
        x = op(x, shifted)
        d *= 2
    return x


def _affine_row_scan(sa, sb):
    n, width = sa.shape
    row8 = jax.lax.broadcasted_iota(jnp.int32, (SUBLANES, width), 0)
    d = 1
    while d < n:
        if d < SUBLANES:
            ra = pltpu.roll(sa, d, axis=0)
            rb = pltpu.roll(sb, d, axis=0)
            a_sh = jnp.concatenate([jnp.where(row8 < d, 1.0, ra[:SUBLANES]), ra[SUBLANES:]], axis=0)
            b_sh = jnp.concatenate([jnp.where(row8 < d, 0.0, rb[:SUBLANES]), rb[SUBLANES:]], axis=0)
            sb = sb + sa * b_sh
            sa = sa * a_sh
        else:
            sb = jnp.concatenate([sb[:d], sb[d:] + sa[d:] * sb[:n - d]], axis=0)
            sa = jnp.concatenate([sa[:d], sa[d:] * sa[:n - d]], axis=0)
        d *= 2
    return sa, sb


def _mixer_tile(x, seq_start, pre_g_ref, w_main_ref, w_if_ref, b_if_ref, conv_w_ref, conv_b_ref, w_gate_ref,
                b_r_ref, b_i_ref, lam_ref, rg_gain_ref, ml_gain_ref, w_out_ref, post_g_ref,
                cbuf_ref, qk_ref, yrg_ref, h_ref, c_ref, m_ref, row_ref, y_ref,
                *, d_rg, d_ml, head_dim):
    ts = x.shape[0]
    d_conv = d_rg + 2 * d_ml
    n_chunks = ts // ML_CHUNK
    L = ML_CHUNK

    @pl.when(seq_start)
    def _():
        cbuf_ref[:, ts:ts + SUBLANES, :] = jnp.zeros((cbuf_ref.shape[0], SUBLANES, LANES), F32)
        h_ref[...] = jnp.zeros_like(h_ref)
        c_ref[...] = jnp.zeros_like(c_ref)
        m_ref[...] = jnp.zeros_like(m_ref)

    ub = _rms(x, pre_g_ref[...]).astype(BF16)
    proj = jnp.dot(ub, w_main_ref[...], preferred_element_type=F32)
    gates = jnp.dot(ub, w_if_ref[...], preferred_element_type=F32) + b_if_ref[...]
    yield MLP_PHASES_AFTER["w_in"]

    n_cs = d_conv // LANES
    n_rs = d_rg // LANES
    G = ts // PHASES
    cbuf_ref[:n_cs, :SUBLANES, :] = cbuf_ref[:n_cs, ts:ts + SUBLANES, :]
    for j in range(n_cs + n_rs):
        cbuf_ref[j, SUBLANES:, :] = proj[:, j * LANES:(j + 1) * LANES]

    def phase(j, e):
        return cbuf_ref[j, pl.ds(SUBLANES + e, G, stride=PHASES), :]

    conv_ph = [[None] * n_cs for _ in range(PHASES)]
    for j in range(n_cs):
        cols = slice(j * LANES, (j + 1) * LANES)
        taps = {e: phase(j, e) for e in range(1 - CONV_WIDTH, PHASES)}
        for r in range(PHASES):
            acc = taps[r] * conv_w_ref[CONV_WIDTH - 1:CONV_WIDTH, cols] + conv_b_ref[:, cols]
            for k in range(1, CONV_WIDTH):
                acc = acc + taps[r - k] * conv_w_ref[CONV_WIDTH - 1 - k:CONV_WIDTH - k, cols]
            conv_ph[r][j] = acc
        if j >= n_rs:
            scale = head_dim ** -0.5 if j < n_rs + d_ml // LANES else 1.0
            for r in range(PHASES):
                val = conv_ph[r][j]
                val = val * _sigmoid(val)
                qk_ref[j - n_rs, pl.ds(r, G, stride=PHASES), :] = val * scale if scale != 1.0 else val
        if j % 2 == 1:
            yield MLP_PHASES_AFTER["conv"]

    xc = jnp.concatenate([jnp.concatenate(conv_ph[r][:n_rs], axis=1) for r in range(PHASES)], axis=0)
    r_parts, i_parts = [], []
    for g in range(d_rg // MXU_WIDTH):
        gg = jnp.dot(xc[:, g * MXU_WIDTH:(g + 1) * MXU_WIDTH].astype(BF16), w_gate_ref[g],
                     preferred_element_type=F32)
        r_parts.append(gg[:, :MXU_WIDTH])
        i_parts.append(gg[:, MXU_WIDTH:])
    r = _sigmoid(jnp.concatenate(r_parts, axis=1) + b_r_ref[...])
    i_gate = _sigmoid(jnp.concatenate(i_parts, axis=1) + b_i_ref[...])
    log_a = r * (-RG_C * _softplus(-lam_ref[...]))
    a = jnp.exp(log_a)
    b_in = jnp.sqrt(-jnp.tanh(log_a) * (a * a + 1.0)) * (i_gate * xc)
    yield MLP_PHASES_AFTER["rg_gates"]

    comp_a, comp_b = [a[:G]], [b_in[:G]]
    for r in range(1, PHASES):
        a_r, b_r = a[r * G:(r + 1) * G], b_in[r * G:(r + 1) * G]
        comp_b.append(a_r * comp_b[-1] + b_r)
        comp_a.append(a_r * comp_a[-1])
    ga, gb = _affine_row_scan(comp_a[-1], comp_b[-1])
    carry = h_ref[SUBLANES - 1:SUBLANES, :]
    h_end = ga * carry + gb
    h_ref[...] = h_end[G - SUBLANES:]
    rolled = pltpu.roll(h_end, 1, axis=0)
    row8r = jax.lax.broadcasted_iota(jnp.int32, (SUBLANES, d_rg), 0)
    h_prev = jnp.concatenate([jnp.where(row8r < 1, carry, rolled[:SUBLANES]), rolled[SUBLANES:]], axis=0)
    yield MLP_PHASES_AFTER["rg_scan"]
    for r in range(PHASES):
        h_r = comp_a[r] * h_prev + comp_b[r]
        gate_r = jnp.concatenate([phase(n_cs + j, r) for j in range(n_rs)], axis=1)
        y_r = _rms(h_r * _gelu_tanh(gate_r), rg_gain_ref[...])
        for j in range(n_rs):
            yrg_ref[j, pl.ds(r, G, stride=PHASES), :] = y_r[:, j * LANES:(j + 1) * LANES]
        if r % 2 == 1:
            yield MLP_PHASES_AFTER["rg_scan"]
    for j in range(n_rs):
        y_ref[:, j * LANES:(j + 1) * LANES] = yrg_ref[j].astype(BF16)

    v_all = proj[:, d_conv + d_rg:d_conv + d_rg + d_ml]
    o_all = proj[:, d_conv + d_rg + d_ml:]

    lane_g = jax.lax.broadcasted_iota(jnp.int32, gates.shape, 1)
    log_sig = jnp.minimum(gates, 0.0) - jnp.log1p(jnp.exp(-jnp.abs(gates)))
    gates_t = jnp.where(lane_g < ML_HEADS, gates, log_sig).T[:SUBLANES]

    causal = (jax.lax.broadcasted_iota(jnp.int32, (L, L), 1)
              <= jax.lax.broadcasted_iota(jnp.int32, (L, L), 0))
    ones_ext = jnp.ones((L, head_dim), BF16)

    for c in range(n_chunks):
        rows = slice(c * L, (c + 1) * L)
        li = gates_t[:, rows]
        lf = pltpu.roll(li, ML_HEADS, axis=0)
        bcum = _lane_scan(lf, jnp.add, 0.0)
        b_last = jnp.sum(lf, axis=1, keepdims=True)
        row_b = li - bcum
        cmax = _lane_scan(row_b, jnp.maximum, -jnp.inf)
        w_loc = b_last + row_b
        m_loc = jnp.max(w_loc, axis=1, keepdims=True)
        m_prev_b = m_ref[...]
        m_prev = jnp.max(m_prev_b, axis=1, keepdims=True)
        m_s = jnp.maximum(bcum + m_prev, bcum + cmax)
        m_new = jnp.maximum(b_last + m_prev, m_loc)
        row_ref[0] = bcum - m_s
        row_ref[1] = m_s
        row_ref[2] = row_b
        row_ref[3] = jnp.exp(w_loc - m_loc)
        row_ref[4] = jnp.broadcast_to(jnp.exp(b_last + m_prev - m_new), (SUBLANES, L))
        row_ref[5] = jnp.broadcast_to(jnp.exp(m_loc - m_new), (SUBLANES, L))
        row_ref[6] = m_prev_b
        m_ref[...] = jnp.broadcast_to(m_new, (SUBLANES, L))

        heads = range(ML_HEADS)
        q_f = [qk_ref[h, rows, :] for h in heads]
        k_t = [qk_ref[ML_HEADS + h, rows, :].T for h in heads]
        s = [jnp.dot(q_f[h].astype(BF16), k_t[h].astype(BF16), preferred_element_type=F32) for h in heads]
        yield MLP_PHASES_AFTER["ml_stage"][0]

        lhs, rhs, v_ext, ms_b = [], [], [], []
        for h in heads:
            cols = slice(h * head_dim, (h + 1) * head_dim)
            ca_b = jnp.broadcast_to(row_ref[0, h:h + 1, :], (head_dim, L)).T
            ms_b.append(jnp.broadcast_to(row_ref[1, h:h + 1, :], (head_dim, L)).T)
            p = jnp.exp(jnp.where(causal, ca_b + row_ref[2, h:h + 1, :], -jnp.inf))
            inter_w = jnp.exp(ca_b + row_ref[6, h:h + 1, :])
            v_ext.append(jnp.concatenate([v_all[rows, cols].astype(BF16), ones_ext], axis=1))
            lhs.append(jnp.concatenate([(s[h] * p).astype(BF16), (q_f[h] * inter_w).astype(BF16)], axis=1))
            rhs.append(jnp.concatenate([v_ext[h], c_ref[h].astype(BF16)], axis=0))
        yield MLP_PHASES_AFTER["ml_stage"][1]

        nd = [jnp.dot(lhs[h], rhs[h], preferred_element_type=F32) for h in heads]
        upd = [jnp.dot((k_t[h] * row_ref[3, h:h + 1, :]).astype(BF16), v_ext[h], preferred_element_type=F32)
               for h in heads]
        yield MLP_PHASES_AFTER["ml_stage"][2]

        for h in heads:
            cols = slice(h * head_dim, (h + 1) * head_dim)
            hh = nd[h][:, :head_dim] / jnp.maximum(jnp.abs(nd[h][:, head_dim:]), jnp.exp(-ms_b[h]))
            hh = _sigmoid(o_all[rows, cols]) * hh
            y_ref[rows, d_rg + h * head_dim:d_rg + (h + 1) * head_dim] = _rms(
                hh, ml_gain_ref[:, cols]).astype(BF16)
            s_old = row_ref[4, h:h + 1, :]
            s_new = row_ref[5, h:h + 1, :]
            c_ref[h] = (jnp.concatenate([s_old, s_old], axis=1) * c_ref[h]
                        + jnp.concatenate([s_new, s_new], axis=1) * upd[h])
        yield MLP_PHASES_AFTER["ml_stage"][3]

    mix = jnp.dot(y_ref[...], w_out_ref[...], preferred_element_type=F32)
    yield MLP_PHASES_AFTER["w_out"]
    return x + _rms(mix, post_g_ref[...])


def _mlp_tile(h_ref, pre_g_ref, w_up_ref, w_down_ref, post_g_ref):
    vb = _rms(h_ref[...], pre_g_ref[...]).astype(BF16)
    d_ff = w_up_ref.shape[1]
    acc = jnp.zeros(h_ref.shape, F32)
    for c in range(d_ff // MLP_FF_CHUNK):
        cols = slice(c * MLP_FF_CHUNK, (c + 1) * MLP_FF_CHUNK)
        f = jnp.maximum(jnp.dot(vb, w_up_ref[:, cols], preferred_element_type=F32), 0.0)
        yield
        acc = acc + jnp.dot((f * f).astype(BF16), w_down_ref[cols, :], preferred_element_type=F32)
        yield
    return h_ref[...] + _rms(acc, post_g_ref[...])


def _interleave(primary, secondary, lead):
    results = [None, None]

    def advance(idx, gen):
        if results[idx] is None:
            try:
                return next(gen)
            except StopIteration as stop:
                results[idx] = (stop.value,)
        return 0

    for _ in range(lead):
        advance(1, secondary)
    while results[0] is None:
        for _ in range(advance(0, primary)):
            advance(1, secondary)
    while results[1] is None:
        advance(1, secondary)
    return results[0][0], results[1][0]


def _weight_copy_jobs(hbm, vmem, src_col_starts):
    rows, cols = vmem.shape
    jobs = []
    for cb in range(cols // WEIGHT_BLOCK):
        src_c0 = cb * WEIGHT_BLOCK if src_col_starts is None else src_col_starts[cb]
        for rb in range(rows // WEIGHT_BLOCK):
            jobs.append((hbm, rb * WEIGHT_BLOCK, src_c0, vmem, rb * WEIGHT_BLOCK, cb * WEIGHT_BLOCK))
    return jobs


def _load_weights(jobs, stage_ref, sem_ref):
    def copy(i):
        src, r0, c0 = jobs[i][:3]
        slot = i % 2
        return pltpu.make_async_copy(
            src.at[pl.ds(r0, WEIGHT_BLOCK), pl.ds(c0, WEIGHT_BLOCK)], stage_ref.at[slot], sem_ref.at[slot])

    copy(0).start()
    for i in range(len(jobs)):
        if i + 1 < len(jobs):
            copy(i + 1).start()
        copy(i).wait()
        dst, r0, c0 = jobs[i][3:]
        dst[r0:r0 + WEIGHT_BLOCK, c0:c0 + WEIGHT_BLOCK] = stage_ref[i % 2].astype(BF16)


def _layer_kernel(x_ref, *refs, tiles_per_seq, n_small, w_in_col_starts, **dims):
    (pre_g, w_if, b_if, conv_w, conv_b, w_gate, b_r, b_i, lam, rg_gain, ml_gain, post_g,
     mlp_pre_g, mlp_post_g) = refs[:n_small]
    w_in_hbm, w_out_hbm, w_up_hbm, w_down_hbm = refs[n_small:n_small + 4]
    out_ref = refs[n_small + 4]
    w_main, w_out, w_up, w_down, stage_ref, sem_ref, h1_ref = refs[n_small + 5:n_small + 12]
    state = refs[n_small + 12:]
    s = pl.program_id(0)

    @pl.when(s == 0)
    def _():
        h1_ref[...] = jnp.zeros_like(h1_ref)
        _load_weights(_weight_copy_jobs(w_in_hbm, w_main, w_in_col_starts)
                      + _weight_copy_jobs(w_up_hbm, w_up, None)
                      + _weight_copy_jobs(w_down_hbm, w_down, None)
                      + _weight_copy_jobs(w_out_hbm, w_out, None), stage_ref, sem_ref)

    h1_new, out = _interleave(
        _mixer_tile(x_ref[0], s % tiles_per_seq == 0, pre_g, w_main, w_if, b_if, conv_w, conv_b, w_gate,
                    b_r, b_i, lam, rg_gain, ml_gain, w_out, post_g, *state, **dims),
        _mlp_tile(h1_ref, mlp_pre_g, w_up, w_down, mlp_post_g),
        MLP_PHASES_AFTER["start"])
    out_ref[0] = out
    h1_ref[...] = h1_new


def _const_spec(shape):
    zeros = (0,) * len(shape)
    return pl.BlockSpec(shape, lambda *_: zeros, pipeline_mode=pl.Buffered(1))


def _block_diag_gate(w):
    nb, bd, _ = w.shape
    per = MXU_WIDTH // bd
    w = w.reshape(nb // per, per, bd, bd)
    eye = jnp.eye(per, dtype=w.dtype)
    return jnp.einsum("gpij,pq->gpiqj", w, eye).reshape(nb // per, MXU_WIDTH, MXU_WIDTH)


def _layer(x, pre_gain, w_in, rg_conv_w, rg_conv_b, gate_r_w, gate_r_b, gate_i_w, gate_i_b, lam,
           ml_conv_w, ml_conv_b, igate_b, fgate_b, rg_gain, ml_gain, w_out, post_gain,
           pre_mlp_gain, w_up, w_down, post_mlp_gain):
    bsz, seq, d_model = x.shape
    assert w_up.shape[1] % MLP_FF_CHUNK == 0
    d_rg = lam.shape[0]
    d_ml = ml_gain.shape[0]
    head_dim = d_ml // ML_HEADS
    assert head_dim == LANES and ML_CHUNK == LANES and 2 * ML_HEADS == SUBLANES
    assert seq % SEQ_TILE == 0 and SEQ_TILE % ML_CHUNK == 0 and d_rg % MXU_WIDTH == 0
    d_conv = d_rg + 2 * d_ml

    assert d_rg == WEIGHT_BLOCK and d_ml == WEIGHT_BLOCK
    starts = dict(rg_x=0, rg_gate=d_rg, q=2 * d_rg, k=2 * d_rg + d_ml, v=2 * d_rg + 2 * d_ml, o=2 * d_rg + 3 * d_ml)
    w_in_col_starts = tuple(starts[n] for n in ("rg_x", "q", "k", "rg_gate", "v", "o"))
    d_main = 2 * d_rg + 4 * d_ml
    if_w = w_in[:, d_main:]
    w_if = jnp.pad(if_w, ((0, 0), (0, LANES - 2 * ML_HEADS))).astype(BF16)
    b_if = jnp.pad(jnp.concatenate([igate_b, fgate_b]), (0, LANES - 2 * ML_HEADS)).reshape(1, LANES)
    conv_w = jnp.concatenate([rg_conv_w, ml_conv_w], axis=1)
    conv_b = jnp.concatenate([rg_conv_b, ml_conv_b]).reshape(1, d_conv)
    w_gate = jnp.concatenate([_block_diag_gate(gate_r_w), _block_diag_gate(gate_i_w)], axis=2).astype(BF16)

    row = lambda v: v.reshape(1, -1)
    small = (row(pre_gain), w_if, b_if, conv_w, conv_b, w_gate, row(gate_r_b), row(gate_i_b), row(lam),
             row(rg_gain), row(ml_gain), row(post_gain), row(pre_mlp_gain), row(post_mlp_gain))
    big = (w_in, w_out, w_up, w_down)
    d_ff = w_up.shape[1]
    for w in (w_out, w_up, w_down):
        assert w.shape[0] % WEIGHT_BLOCK == 0 and w.shape[1] % WEIGHT_BLOCK == 0

    tiles_per_seq = seq // SEQ_TILE
    n_tiles = bsz * tiles_per_seq

    def tile_block(t):
        return (t // tiles_per_seq, t % tiles_per_seq, 0)

    x_spec = pl.BlockSpec((1, SEQ_TILE, d_model), lambda s: tile_block(jnp.minimum(s, n_tiles - 1)))
    out_spec = pl.BlockSpec((1, SEQ_TILE, d_model), lambda s: tile_block(jnp.maximum(s - 1, 0)))
    in_specs = ([x_spec] + [_const_spec(op.shape) for op in small]
                + [pl.BlockSpec(memory_space=pl.ANY)] * len(big))
    kern = functools.partial(_layer_kernel, tiles_per_seq=tiles_per_seq, n_small=len(small),
                             w_in_col_starts=w_in_col_starts, d_rg=d_rg, d_ml=d_ml, head_dim=head_dim)
    return pl.pallas_call(
        kern,
        grid=(n_tiles + 1,),
        in_specs=in_specs,
        out_specs=out_spec,
        out_shape=jax.ShapeDtypeStruct(x.shape, x.dtype),
        scratch_shapes=[
            pltpu.VMEM((d_model, d_main), BF16),
            pltpu.VMEM((d_rg + d_ml, d_model), BF16),
            pltpu.VMEM((d_model, d_ff), BF16),
            pltpu.VMEM((d_ff, d_model), BF16),
            pltpu.VMEM((2, WEIGHT_BLOCK, WEIGHT_BLOCK), F32),
            pltpu.SemaphoreType.DMA((2,)),
            pltpu.VMEM((SEQ_TILE, d_model), F32),
            pltpu.VMEM(((d_conv + d_rg) // LANES, SUBLANES + SEQ_TILE, LANES), F32),
            pltpu.VMEM((2 * d_ml // LANES, SEQ_TILE, LANES), F32),
            pltpu.VMEM((d_rg // LANES, SEQ_TILE, LANES), F32),
            pltpu.VMEM((SUBLANES, d_rg), F32),
            pltpu.VMEM((ML_HEADS, head_dim, 2 * head_dim), F32),
            pltpu.VMEM((SUBLANES, ML_CHUNK), F32),
            pltpu.VMEM((7, SUBLANES, ML_CHUNK), F32),
            pltpu.VMEM((SEQ_TILE, d_rg + d_ml), BF16),
        ],
        compiler_params=pltpu.CompilerParams(
            dimension_semantics=("arbitrary",), vmem_limit_bytes=VMEM_LIMIT_BYTES),
        name="layer",
    )(x, *small, *big)


def kernel(x, pre_mix_gain, w_in, rg_conv_w, rg_conv_b, rg_gate_r_w, rg_gate_r_b, rg_gate_i_w, rg_gate_i_b, rg_lambda, ml_conv_w, ml_conv_b, ml_igate_b, ml_fgate_b, rg_out_gain, ml_out_gain, w_out, post_mix_gain, pre_mlp_gain, mlp_w_up, mlp_w_down, post_mlp_gain):
    h = x
    for l in range(w_in.shape[0]):
        h = _layer(h, pre_mix_gain[l], w_in[l], rg_conv_w[l], rg_conv_b[l], rg_gate_r_w[l], rg_gate_r_b[l],
                   rg_gate_i_w[l], rg_gate_i_b[l], rg_lambda[l], ml_conv_w[l], ml_conv_b[l], ml_igate_b[l],
                   ml_fgate_b[l], rg_out_gain[l], ml_out_gain[l], w_out[l], post_mix_gain[l],
                   pre_mlp_gain[l], mlp_w_up[l], mlp_w_down[l], post_mlp_gain[l])
    return h
```

```python
import functools

import jax
import jax.numpy as jnp
from jax.experimental import pallas as pl
from jax.experimental.pallas import tpu as pltpu

F32 = jnp.float32
BF16 = jnp.bfloat16

RG_BLOCKS = 8
RG_C = 8.0
ML_HEADS = 4
CONV_WIDTH = 4
EPS = 1e-6

SUBLANES = 8
LANES = 128
MXU_WIDTH = 256

SEQ_TILE = 512
ML_CHUNK = LANES
PHASES = 4
MLP_FF_CHUNK = 512
WEIGHT_BLOCK = 512
MLP_PHASES_AFTER = {"start": 0, "w_in": 1, "conv": 1, "rg_gates": 1, "rg_scan": 1, "ml_stage": (1, 0, 1, 0),
                    "w_out": 1}
VMEM_LIMIT_BYTES = 60 * 1024 * 1024


def _rms(x, gain):
    return x * jax.lax.rsqrt(jnp.mean(x * x, axis=-1, keepdims=True) + EPS) * gain


def _sigmoid(x):
    return 1.0 / (1.0 + jnp.exp(-x))


def _softplus(x):
    return jnp.maximum(x, 0.0) + jnp.log1p(jnp.exp(-jnp.abs(x)))


def _gelu_tanh(x):
    c = 0.7978845608028654
    return 0.5 * x * (1.0 + jnp.tanh(c * (x + 0.044715 * (x * x * x))))


def _lane_scan(x, op, fill):
    n = x.shape[1]
    lane = jax.lax.broadcasted_iota(jnp.int32, x.shape, 1)
    d = 1
    while d < n:
        shifted = jnp.where(lane < d, fill, pltpu.roll(x, d, axis=1))
        x = op(x, shifted)
        d *= 2
    return x


def _affine_row_scan(sa, sb):
    n, width = sa.shape
    row8 = jax.lax.broadcasted_iota(jnp.int32, (SUBLANES, width), 0)
    d = 1
    while d < n:
        if d < SUBLANES:
            ra = pltpu.roll(sa, d, axis=0)
            rb = pltpu.roll(sb, d, axis=0)
            a_sh = jnp.concatenate([jnp.where(row8 < d, 1.0, ra[:SUBLANES]), ra[SUBLANES:]], axis=0)
            b_sh = jnp.concatenate([jnp.where(row8 < d, 0.0, rb[:SUBLANES]), rb[SUBLANES:]], axis=0)
            sb = sb + sa * b_sh
            sa = sa * a_sh
        else:
            sb = jnp.concatenate([sb[:d], sb[d:] + sa[d:] * sb[:n - d]], axis=0)
            sa = jnp.concatenate([sa[:d], sa[d:] * sa[:n - d]], axis=0)
        d *= 2
    return sa, sb


def _mixer_tile(x, seq_start, pre_g_ref, w_main_ref, w_if_ref, b_if_ref, conv_w_ref, conv_b_ref, w_gate_ref,
                b_r_ref, b_i_ref, lam_ref, rg_gain_ref, ml_gain_ref, w_out_ref, post_g_ref,
                cbuf_ref, qk_ref, yrg_ref, h_ref, c_ref, m_ref, row_ref, y_ref,
                *, d_rg, d_ml, head_dim):
    ts = x.shape[0]
    d_conv = d_rg + 2 * d_ml
    n_chunks = ts // ML_CHUNK
    L = ML_CHUNK

    @pl.when(seq_start)
    def _():
        cbuf_ref[:, ts:ts + SUBLANES, :] = jnp.zeros((cbuf_ref.shape[0], SUBLANES, LANES), F32)
        h_ref[...] = jnp.zeros_like(h_ref)
        c_ref[...] = jnp.zeros_like(c_ref)
        m_ref[...] = jnp.zeros_like(m_ref)

    ub = _rms(x, pre_g_ref[...]).astype(BF16)
    proj = jnp.dot(ub, w_main_ref[...], preferred_element_type=F32)
    gates = jnp.dot(ub, w_if_ref[...], preferred_element_type=F32) + b_if_ref[...]
    yield MLP_PHASES_AFTER["w_in"]

    n_cs = d_conv // LANES
    n_rs = d_rg // LANES
    G = ts // PHASES
    cbuf_ref[:n_cs, :SUBLANES, :] = cbuf_ref[:n_cs, ts:ts + SUBLANES, :]
    for j in range(n_cs + n_rs):
        cbuf_ref[j, SUBLANES:, :] = proj[:, j * LANES:(j + 1) * LANES]

    def phase(j, e):
        return cbuf_ref[j, pl.ds(SUBLANES + e, G, stride=PHASES), :]

    conv_ph = [[None] * n_cs for _ in range(PHASES)]
    for j in range(n_cs):
        cols = slice(j * LANES, (j + 1) * LANES)
        taps = {e: phase(j, e) for e in range(1 - CONV_WIDTH, PHASES)}
        for r in range(PHASES):
            acc = taps[r] * conv_w_ref[CONV_WIDTH - 1:CONV_WIDTH, cols] + conv_b_ref[:, cols]
            for k in range(1, CONV_WIDTH):
                acc = acc + taps[r - k] * conv_w_ref[CONV_WIDTH - 1 - k:CONV_WIDTH - k, cols]
            conv_ph[r][j] = acc
        if j >= n_rs:
            scale = head_dim ** -0.5 if j < n_rs + d_ml // LANES else 1.0
            for r in range(PHASES):
                val = conv_ph[r][j]
                val = val * _sigmoid(val)
                qk_ref[j - n_rs, pl.ds(r, G, stride=PHASES), :] = val * scale if scale != 1.0 else val
        if j % 2 == 1:
            yield MLP_PHASES_AFTER["conv"]

    xc = jnp.concatenate([jnp.concatenate(conv_ph[r][:n_rs], axis=1) for r in range(PHASES)], axis=0)
    r_parts, i_parts = [], []
    for g in range(d_rg // MXU_WIDTH):
        gg = jnp.dot(xc[:, g * MXU_WIDTH:(g + 1) * MXU_WIDTH].astype(BF16), w_gate_ref[g],
                     preferred_element_type=F32)
        r_parts.append(gg[:, :MXU_WIDTH])
        i_parts.append(gg[:, MXU_WIDTH:])
    r = _sigmoid(jnp.concatenate(r_parts, axis=1) + b_r_ref[...])
    i_gate = _sigmoid(jnp.concatenate(i_parts, axis=1) + b_i_ref[...])
    log_a = r * (-RG_C * _softplus(-lam_ref[...]))
    a = jnp.exp(log_a)
    b_in = jnp.sqrt(-jnp.tanh(log_a) * (a * a + 1.0)) * (i_gate * xc)
    yield MLP_PHASES_AFTER["rg_gates"]

    comp_a, comp_b = [a[:G]], [b_in[:G]]
    for r in range(1, PHASES):
        a_r, b_r = a[r * G:(r + 1) * G], b_in[r * G:(r + 1) * G]
        comp_b.append(a_r * comp_b[-1] + b_r)
        comp_a.append(a_r * comp_a[-1])
    ga, gb = _affine_row_scan(comp_a[-1], comp_b[-1])
    carry = h_ref[SUBLANES - 1:SUBLANES, :]
    h_end = ga * carry + gb
    h_ref[...] = h_end[G - SUBLANES:]
    rolled = pltpu.roll(h_end, 1, axis=0)
    row8r = jax.lax.broadcasted_iota(jnp.int32, (SUBLANES, d_rg), 0)
    h_prev = jnp.concatenate([jnp.where(row8r < 1, carry, rolled[:SUBLANES]), rolled[SUBLANES:]], axis=0)
    yield MLP_PHASES_AFTER["rg_scan"]
    for r in range(PHASES):
        h_r = comp_a[r] * h_prev + comp_b[r]
        gate_r = jnp.concatenate([phase(n_cs + j, r) for j in range(n_rs)], axis=1)
        y_r = _rms(h_r * _gelu_tanh(gate_r), rg_gain_ref[...])
        for j in range(n_rs):
            yrg_ref[j, pl.ds(r, G, stride=PHASES), :] = y_r[:, j * LANES:(j + 1) * LANES]
        if r % 2 == 1:
            yield MLP_PHASES_AFTER["rg_scan"]
    for j in range(n_rs):
        y_ref[:, j * LANES:(j + 1) * LANES] = yrg_ref[j].astype(BF16)

    v_all = proj[:, d_conv + d_rg:d_conv + d_rg + d_ml]
    o_all = proj[:, d_conv + d_rg + d_ml:]

    lane_g = jax.lax.broadcasted_iota(jnp.int32, gates.shape, 1)
    log_sig = jnp.minimum(gates, 0.0) - jnp.log1p(jnp.exp(-jnp.abs(gates)))
    gates_t = jnp.where(lane_g < ML_HEADS, gates, log_sig).T[:SUBLANES]

    causal = (jax.lax.broadcasted_iota(jnp.int32, (L, L), 1)
              <= jax.lax.broadcasted_iota(jnp.int32, (L, L), 0))
    ones_ext = jnp.ones((L, head_dim), BF16)

    for c in range(n_chunks):
        rows = slice(c * L, (c + 1) * L)
        li = gates_t[:, rows]
        lf = pltpu.roll(li, ML_HEADS, axis=0)
        bcum = _lane_scan(lf, jnp.add, 0.0)
        b_last = jnp.sum(lf, axis=1, keepdims=True)
        row_b = li - bcum
        cmax = _lane_scan(row_b, jnp.maximum, -jnp.inf)
        w_loc = b_last + row_b
        m_loc = jnp.max(w_loc, axis=1, keepdims=True)
        m_prev_b = m_ref[...]
        m_prev = jnp.max(m_prev_b, axis=1, keepdims=True)
        m_s = jnp.maximum(bcum + m_prev, bcum + cmax)
        m_new = jnp.maximum(b_last + m_prev, m_loc)
        row_ref[0] = bcum - m_s
        row_ref[1] = m_s
        row_ref[2] = row_b
        row_ref[3] = jnp.exp(w_loc - m_loc)
        row_ref[4] = jnp.broadcast_to(jnp.exp(b_last + m_prev - m_new), (SUBLANES, L))
        row_ref[5] = jnp.broadcast_to(jnp.exp(m_loc - m_new), (SUBLANES, L))
        row_ref[6] = m_prev_b
        m_ref[...] = jnp.broadcast_to(m_new, (SUBLANES, L))

        heads = range(ML_HEADS)
        q_f = [qk_ref[h, rows, :] for h in heads]
        k_t = [qk_ref[ML_HEADS + h, rows, :].T for h in heads]
        s = [jnp.dot(q_f[h].astype(BF16), k_t[h].astype(BF16), preferred_element_type=F32) for h in heads]
        yield MLP_PHASES_AFTER["ml_stage"][0]

        lhs, rhs, v_ext, ms_b = [], [], [], []
        for h in heads:
            cols = slice(h * head_dim, (h + 1) * head_dim)
            ca_b = jnp.broadcast_to(row_ref[0, h:h + 1, :], (head_dim, L)).T
            ms_b.append(jnp.broadcast_to(row_ref[1, h:h + 1, :], (head_dim, L)).T)
            p = jnp.exp(jnp.where(causal, ca_b + row_ref[2, h:h + 1, :], -jnp.inf))
            inter_w = jnp.exp(ca_b + row_ref[6, h:h + 1, :])
            v_ext.append(jnp.concatenate([v_all[rows, cols].astype(BF16), ones_ext], axis=1))
            lhs.append(jnp.concatenate([(s[h] * p).astype(BF16), (q_f[h] * inter_w).astype(BF16)], axis=1))
            rhs.append(jnp.concatenate([v_ext[h], c_ref[h].astype(BF16)], axis=0))
        yield MLP_PHASES_AFTER["ml_stage"][1]

        nd = [jnp.dot(lhs[h], rhs[h], preferred_element_type=F32) for h in heads]
        upd = [jnp.dot((k_t[h] * row_ref[3, h:h + 1, :]).astype(BF16), v_ext[h], preferred_element_type=F32)
               for h in heads]
        yield MLP_PHASES_AFTER["ml_stage"][2]

        for h in heads:
            cols = slice(h * head_dim, (h + 1) * head_dim)
            hh = nd[h][:, :head_dim] / jnp.maximum(jnp.abs(nd[h][:, head_dim:]), jnp.exp(-ms_b[h]))
            hh = _sigmoid(o_all[rows, cols]) * hh
            y_ref[rows, d_rg + h * head_dim:d_rg + (h + 1) * head_dim] = _rms(
                hh, ml_gain_ref[:, cols]).astype(BF16)
            s_old = row_ref[4, h:h + 1, :]
            s_new = row_ref[5, h:h + 1, :]
            c_ref[h] = (jnp.concatenate([s_old, s_old], axis=1) * c_ref[h]
                        + jnp.concatenate([s_new, s_new], axis=1) * upd[h])
        yield MLP_PHASES_AFTER["ml_stage"][3]

    mix = jnp.dot(y_ref[...], w_out_ref[...], preferred_element_type=F32)
    yield MLP_PHASES_AFTER["w_out"]
    return x + _rms(mix, post_g_ref[...])


def _mlp_tile(h_ref, pre_g_ref, w_up_ref, w_down_ref, post_g_ref):
    vb = _rms(h_ref[...], pre_g_ref[...]).astype(BF16)
    d_ff = w_up_ref.shape[1]
    acc = jnp.zeros(h_ref.shape, F32)
    for c in range(d_ff // MLP_FF_CHUNK):
        cols = slice(c * MLP_FF_CHUNK, (c + 1) * MLP_FF_CHUNK)
        f = jnp.maximum(jnp.dot(vb, w_up_ref[:, cols], preferred_element_type=F32), 0.0)
        yield
        acc = acc + jnp.dot((f * f).astype(BF16), w_down_ref[cols, :], preferred_element_type=F32)
        yield
    return h_ref[...] + _rms(acc, post_g_ref[...])


def _run(gen):
    while True:
        try:
            next(gen)
        except StopIteration as stop:
            return stop.value


def _interleave(primary, secondary, lead=0, per=1):
    results = [None, None]

    def advance(idx, gen):
        if results[idx] is None:
            try:
                return next(gen)
            except StopIteration as stop:
                results[idx] = (stop.value,)
        return 0

    for _ in range(lead):
        advance(1, secondary)
    while results[0] is None:
        for _ in range(advance(0, primary) * per):
            advance(1, secondary)
    while results[1] is None:
        advance(1, secondary)
    return results[0][0], results[1][0]


def _weight_copy_jobs(hbm, vmem, src_row_starts=None):
    rows, cols = vmem.shape
    jobs = []
    for cb in range(cols // WEIGHT_BLOCK):
        for rb in range(rows // WEIGHT_BLOCK):
            dst = (vmem, rb * WEIGHT_BLOCK, cb * WEIGHT_BLOCK)
            if src_row_starts is None:
                jobs.append((hbm, rb * WEIGHT_BLOCK, cb * WEIGHT_BLOCK) + dst + (False,))
            else:
                jobs.append((hbm, src_row_starts[cb], rb * WEIGHT_BLOCK) + dst + (True,))
    return jobs


def _load_weights(jobs, stage_ref, sem_ref):
    def copy(i):
        src, r0, c0 = jobs[i][:3]
        slot = i % 2
        return pltpu.make_async_copy(
            src.at[pl.ds(r0, WEIGHT_BLOCK), pl.ds(c0, WEIGHT_BLOCK)], stage_ref.at[slot], sem_ref.at[slot])

    copy(0).start()
    for i in range(len(jobs)):
        if i + 1 < len(jobs):
            copy(i + 1).start()
        copy(i).wait()
        dst, r0, c0, transposed = jobs[i][3:]
        block = stage_ref[i % 2]
        dst[r0:r0 + WEIGHT_BLOCK, c0:c0 + WEIGHT_BLOCK] = (block.T if transposed else block).astype(BF16)
        yield


def _layer_kernel(x_ref, *refs, tiles_per_seq, n_tiles, n_small, w_in_row_starts, **dims):
    (pre_g, w_if, b_if, conv_w, conv_b, w_gate, b_r, b_i, lam, rg_gain, ml_gain, post_g,
     mlp_pre_g, mlp_post_g) = refs[:n_small]
    w_in_t_hbm, w_out_hbm, w_up_hbm, w_down_hbm = refs[n_small:n_small + 4]
    out_ref = refs[n_small + 4]
    w_main, w_out, w_up, w_down, stage_ref, sem_ref, h1_ref = refs[n_small + 5:n_small + 12]
    state = refs[n_small + 12:]
    s = pl.program_id(0)

    def mixer():
        return _mixer_tile(x_ref[0], s % tiles_per_seq == 0, pre_g, w_main, w_if, b_if, conv_w, conv_b,
                           w_gate, b_r, b_i, lam, rg_gain, ml_gain, w_out, post_g, *state, **dims)

    def mlp():
        return _mlp_tile(h1_ref, mlp_pre_g, w_up, w_down, mlp_post_g)

    @pl.when(s == 0)
    def _():
        _run(_load_weights(_weight_copy_jobs(w_in_t_hbm, w_main, w_in_row_starts)
                           + _weight_copy_jobs(w_out_hbm, w_out), stage_ref, sem_ref))
        later = _weight_copy_jobs(w_up_hbm, w_up) + _weight_copy_jobs(w_down_hbm, w_down)
        h1_ref[...], _ = _interleave(mixer(), _load_weights(later, stage_ref, sem_ref), per=2)

    @pl.when(jnp.logical_and(s > 0, s < n_tiles))
    def _():
        h1_new, out = _interleave(mixer(), mlp(), MLP_PHASES_AFTER["start"])
        out_ref[0] = out
        h1_ref[...] = h1_new

    @pl.when(s == n_tiles)
    def _():
        out_ref[0] = _run(mlp())


def _const_spec(shape):
    zeros = (0,) * len(shape)
    return pl.BlockSpec(shape, lambda *_: zeros, pipeline_mode=pl.Buffered(1))


def _block_diag_gate(w):
    nb, bd, _ = w.shape
    per = MXU_WIDTH // bd
    w = w.reshape(nb // per, per, bd, bd)
    eye = jnp.eye(per, dtype=w.dtype)
    return jnp.einsum("gpij,pq->gpiqj", w, eye).reshape(nb // per, MXU_WIDTH, MXU_WIDTH)


def _layer(x, pre_gain, w_in, rg_conv_w, rg_conv_b, gate_r_w, gate_r_b, gate_i_w, gate_i_b, lam,
           ml_conv_w, ml_conv_b, igate_b, fgate_b, rg_gain, ml_gain, w_out, post_gain,
           pre_mlp_gain, w_up, w_down, post_mlp_gain):
    bsz, seq, d_model = x.shape
    assert w_up.shape[1] % MLP_FF_CHUNK == 0
    d_rg = lam.shape[0]
    d_ml = ml_gain.shape[0]
    head_dim = d_ml // ML_HEADS
    assert head_dim == LANES and ML_CHUNK == LANES and 2 * ML_HEADS == SUBLANES
    assert seq % SEQ_TILE == 0 and SEQ_TILE % ML_CHUNK == 0 and d_rg % MXU_WIDTH == 0
    d_conv = d_rg + 2 * d_ml

    assert d_rg == WEIGHT_BLOCK and d_ml == WEIGHT_BLOCK
    starts = dict(rg_x=0, rg_gate=d_rg, q=2 * d_rg, k=2 * d_rg + d_ml, v=2 * d_rg + 2 * d_ml, o=2 * d_rg + 3 * d_ml)
    w_in_row_starts = tuple(starts[n] for n in ("rg_x", "q", "k", "rg_gate", "v", "o"))
    d_main = 2 * d_rg + 4 * d_ml
    if_w = w_in[:, d_main:]
    w_if = jnp.pad(if_w, ((0, 0), (0, LANES - 2 * ML_HEADS))).astype(BF16)
    b_if = jnp.pad(jnp.concatenate([igate_b, fgate_b]), (0, LANES - 2 * ML_HEADS)).reshape(1, LANES)
    conv_w = jnp.concatenate([rg_conv_w, ml_conv_w], axis=1)
    conv_b = jnp.concatenate([rg_conv_b, ml_conv_b]).reshape(1, d_conv)
    w_gate = jnp.concatenate([_block_diag_gate(gate_r_w), _block_diag_gate(gate_i_w)], axis=2).astype(BF16)

    row = lambda v: v.reshape(1, -1)
    small = (row(pre_gain), w_if, b_if, conv_w, conv_b, w_gate, row(gate_r_b), row(gate_i_b), row(lam),
             row(rg_gain), row(ml_gain), row(post_gain), row(pre_mlp_gain), row(post_mlp_gain))
    big = (jnp.swapaxes(w_in, 0, 1), w_out, w_up, w_down)
    d_ff = w_up.shape[1]
    for w in (w_out, w_up, w_down):
        assert w.shape[0] % WEIGHT_BLOCK == 0 and w.shape[1] % WEIGHT_BLOCK == 0

    tiles_per_seq = seq // SEQ_TILE
    n_tiles = bsz * tiles_per_seq

    def tile_block(t):
        return (t // tiles_per_seq, t % tiles_per_seq, 0)

    x_spec = pl.BlockSpec((1, SEQ_TILE, d_model), lambda s: tile_block(jnp.minimum(s, n_tiles - 1)))
    out_spec = pl.BlockSpec((1, SEQ_TILE, d_model), lambda s: tile_block(jnp.maximum(s - 1, 0)))
    in_specs = ([x_spec] + [_const_spec(op.shape) for op in small]
                + [pl.BlockSpec(memory_space=pl.ANY)] * len(big))
    kern = functools.partial(_layer_kernel, tiles_per_seq=tiles_per_seq, n_tiles=n_tiles, n_small=len(small),
                             w_in_row_starts=w_in_row_starts, d_rg=d_rg, d_ml=d_ml, head_dim=head_dim)
    return pl.pallas_call(
        kern,
        grid=(n_tiles + 1,),
        in_specs=in_specs,
        out_specs=out_spec,
        out_shape=jax.ShapeDtypeStruct(x.shape, x.dtype),
        scratch_shapes=[
            pltpu.VMEM((d_model, d_main), BF16),
            pltpu.VMEM((d_rg + d_ml, d_model), BF16),
            pltpu.VMEM((d_model, d_ff), BF16),
            pltpu.VMEM((d_ff, d_model), BF16),
            pltpu.VMEM((2, WEIGHT_BLOCK, WEIGHT_BLOCK), F32),
            pltpu.SemaphoreType.DMA((2,)),
            pltpu.VMEM((SEQ_TILE, d_model), F32),
            pltpu.VMEM(((d_conv + d_rg) // LANES, SUBLANES + SEQ_TILE, LANES), F32),
            pltpu.VMEM((2 * d_ml // LANES, SEQ_TILE, LANES), F32),
            pltpu.VMEM((d_rg // LANES, SEQ_TILE, LANES), F32),
            pltpu.VMEM((SUBLANES, d_rg), F32),
            pltpu.VMEM((ML_HEADS, head_dim, 2 * head_dim), F32),
            pltpu.VMEM((SUBLANES, ML_CHUNK), F32),
            pltpu.VMEM((7, SUBLANES, ML_CHUNK), F32),
            pltpu.VMEM((SEQ_TILE, d_rg + d_ml), BF16),
        ],
        compiler_params=pltpu.CompilerParams(
            dimension_semantics=("arbitrary",), vmem_limit_bytes=VMEM_LIMIT_BYTES),
        name="layer",
    )(x, *small, *big)


def kernel(x, pre_mix_gain, w_in, rg_conv_w, rg_conv_b, rg_gate_r_w, rg_gate_r_b, rg_gate_i_w, rg_gate_i_b, rg_lambda, ml_conv_w, ml_conv_b, ml_igate_b, ml_fgate_b, rg_out_gain, ml_out_gain, w_out, post_mix_gain, pre_mlp_gain, mlp_w_up, mlp_w_down, post_mlp_gain):
    h = x
    for l in range(w_in.shape[0]):
        h = _layer(h, pre_mix_gain[l], w_in[l], rg_conv_w[l], rg_conv_b[l], rg_gate_r_w[l], rg_gate_r_b[l],
                   rg_gate_i_w[l], rg_gate_i_b[l], rg_lambda[l], ml_conv_w[l], ml_conv_b[l], ml_igate_b[l],
                   ml_fgate_b[l], rg_out_gain[l], ml_out_gain[l], w_out[l], post_mix_gain[l],
                   pre_mlp_gain[l], mlp_w_up[l], mlp_w_down[l], post_mlp_gain[l])
    return h
```

```python
import functools

import jax
import jax.numpy as jnp
from jax.experimental import pallas as pl
from jax.experimental.pallas import tpu as pltpu

F32 = jnp.float32
BF16 = jnp.bfloat16

RG_BLOCKS = 8
RG_C = 8.0
ML_HEADS = 4
CONV_WIDTH = 4
EPS = 1e-6

SUBLANES = 8
LANES = 128
MXU_WIDTH = 256

SEQ_TILE = 512
ML_CHUNK = LANES
PHASES = 4
MLP_FF_CHUNK = 512
WEIGHT_BLOCK = 512
MLP_PHASES_AFTER = {"start": 0, "w_in": 1, "conv": 1, "rg_gates": 1, "rg_scan": 1, "ml_stage": (1, 0, 1, 0),
                    "w_out": 1}
VMEM_LIMIT_BYTES = 60 * 1024 * 1024


def _rms(x, gain):
    return x * jax.lax.rsqrt(jnp.mean(x * x, axis=-1, keepdims=True) + EPS) * gain


def _sigmoid(x):
    return 1.0 / (1.0 + jnp.exp(-x))


def _softplus(x):
    return jnp.maximum(x, 0.0) + jnp.log1p(jnp.exp(-jnp.abs(x)))


def _gelu_tanh(x):
    c = 0.7978845608028654
    return 0.5 * x * (1.0 + jnp.tanh(c * (x + 0.044715 * (x * x * x))))


def _lane_scan(x, op, fill):
    n = x.shape[1]
    lane = jax.lax.broadcasted_iota(jnp.int32, x.shape, 1)
    d = 1
    while d < n:
        shifted = jnp.where(lane < d, fill, pltpu.roll(x, d, axis=1))
        x = op(x, shifted)
        d *= 2
    return x


def _affine_row_scan(sa, sb):
    n, width = sa.shape
    row8 = jax.lax.broadcasted_iota(jnp.int32, (SUBLANES, width), 0)
    d = 1
    while d < n:
        if d < SUBLANES:
            ra = pltpu.roll(sa, d, axis=0)
            rb = pltpu.roll(sb, d, axis=0)
            a_sh = jnp.concatenate([jnp.where(row8 < d, 1.0, ra[:SUBLANES]), ra[SUBLANES:]], axis=0)
            b_sh = jnp.concatenate([jnp.where(row8 < d, 0.0, rb[:SUBLANES]), rb[SUBLANES:]], axis=0)
            sb = sb + sa * b_sh
            sa = sa * a_sh
        else:
            sb = jnp.concatenate([sb[:d], sb[d:] + sa[d:] * sb[:n - d]], axis=0)
            sa = jnp.concatenate([sa[:d], sa[d:] * sa[:n - d]], axis=0)
        d *= 2
    return sa, sb


def _mixer_tile(x, seq_start, pre_g_ref, w_main_ref, w_if_ref, b_if_ref, conv_w_ref, conv_b_ref, w_gate_ref,
                b_r_ref, b_i_ref, lam_ref, rg_gain_ref, ml_gain_ref, w_out_ref, post_g_ref,
                cbuf_ref, qk_ref, yrg_ref, h_ref, c_ref, m_ref, row_ref, y_ref,
                *, d_rg, d_ml, head_dim):
    ts = x.shape[0]
    d_conv = d_rg + 2 * d_ml
    n_chunks = ts // ML_CHUNK
    L = ML_CHUNK

    @pl.when(seq_start)
    def _():
        cbuf_ref[:, ts:ts + SUBLANES, :] = jnp.zeros((cbuf_ref.shape[0], SUBLANES, LANES), F32)
        h_ref[...] = jnp.zeros_like(h_ref)
        c_ref[...] = jnp.zeros_like(c_ref)
        m_ref[...] = jnp.zeros_like(m_ref)

    ub = _rms(x, pre_g_ref[...]).astype(BF16)
    proj = jnp.dot(ub, w_main_ref[...], preferred_element_type=F32)
    gates = jnp.dot(ub, w_if_ref[...], preferred_element_type=F32) + b_if_ref[...]
    yield MLP_PHASES_AFTER["w_in"]

    n_cs = d_conv // LANES
    n_rs = d_rg // LANES
    G = ts // PHASES
    cbuf_ref[:n_cs, :SUBLANES, :] = cbuf_ref[:n_cs, ts:ts + SUBLANES, :]
    for j in range(n_cs + n_rs):
        cbuf_ref[j, SUBLANES:, :] = proj[:, j * LANES:(j + 1) * LANES]

    def phase(j, e):
        return cbuf_ref[j, pl.ds(SUBLANES + e, G, stride=PHASES), :]

    conv_ph = [[None] * n_cs for _ in range(PHASES)]
    for j in range(n_cs):
        cols = slice(j * LANES, (j + 1) * LANES)
        taps = {e: phase(j, e) for e in range(1 - CONV_WIDTH, PHASES)}
        for r in range(PHASES):
            acc = taps[r] * conv_w_ref[CONV_WIDTH - 1:CONV_WIDTH, cols] + conv_b_ref[:, cols]
            for k in range(1, CONV_WIDTH):
                acc = acc + taps[r - k] * conv_w_ref[CONV_WIDTH - 1 - k:CONV_WIDTH - k, cols]
            conv_ph[r][j] = acc
        if j >= n_rs:
            scale = head_dim ** -0.5 if j < n_rs + d_ml // LANES else 1.0
            for r in range(PHASES):
                val = conv_ph[r][j]
                val = val * _sigmoid(val)
                qk_ref[j - n_rs, pl.ds(r, G, stride=PHASES), :] = val * scale if scale != 1.0 else val
        if j % 2 == 1:
            yield MLP_PHASES_AFTER["conv"]

    xc = jnp.concatenate([jnp.concatenate(conv_ph[r][:n_rs], axis=1) for r in range(PHASES)], axis=0)
    r_parts, i_parts = [], []
    for g in range(d_rg // MXU_WIDTH):
        gg = jnp.dot(xc[:, g * MXU_WIDTH:(g + 1) * MXU_WIDTH].astype(BF16), w_gate_ref[g],
                     preferred_element_type=F32)
        r_parts.append(gg[:, :MXU_WIDTH])
        i_parts.append(gg[:, MXU_WIDTH:])
    r = _sigmoid(jnp.concatenate(r_parts, axis=1) + b_r_ref[...])
    i_gate = _sigmoid(jnp.concatenate(i_parts, axis=1) + b_i_ref[...])
    log_a = r * (-RG_C * _softplus(-lam_ref[...]))
    a = jnp.exp(log_a)
    b_in = jnp.sqrt(-jnp.tanh(log_a) * (a * a + 1.0)) * (i_gate * xc)
    yield MLP_PHASES_AFTER["rg_gates"]

    comp_a, comp_b = [a[:G]], [b_in[:G]]
    for r in range(1, PHASES):
        a_r, b_r = a[r * G:(r + 1) * G], b_in[r * G:(r + 1) * G]
        comp_b.append(a_r * comp_b[-1] + b_r)
        comp_a.append(a_r * comp_a[-1])
    ga, gb = _affine_row_scan(comp_a[-1], comp_b[-1])
    carry = h_ref[SUBLANES - 1:SUBLANES, :]
    h_end = ga * carry + gb
    h_ref[...] = h_end[G - SUBLANES:]
    rolled = pltpu.roll(h_end, 1, axis=0)
    row8r = jax.lax.broadcasted_iota(jnp.int32, (SUBLANES, d_rg), 0)
    h_prev = jnp.concatenate([jnp.where(row8r < 1, carry, rolled[:SUBLANES]), rolled[SUBLANES:]], axis=0)
    yield MLP_PHASES_AFTER["rg_scan"]
    for r in range(PHASES):
        h_r = comp_a[r] * h_prev + comp_b[r]
        gate_r = jnp.concatenate([phase(n_cs + j, r) for j in range(n_rs)], axis=1)
        y_r = _rms(h_r * _gelu_tanh(gate_r), rg_gain_ref[...])
        for j in range(n_rs):
            yrg_ref[j, pl.ds(r, G, stride=PHASES), :] = y_r[:, j * LANES:(j + 1) * LANES]
        if r % 2 == 1:
            yield MLP_PHASES_AFTER["rg_scan"]
    for j in range(n_rs):
        y_ref[:, j * LANES:(j + 1) * LANES] = yrg_ref[j].astype(BF16)

    v_all = proj[:, d_conv + d_rg:d_conv + d_rg + d_ml]
    o_all = proj[:, d_conv + d_rg + d_ml:]

    lane_g = jax.lax.broadcasted_iota(jnp.int32, gates.shape, 1)
    log_sig = jnp.minimum(gates, 0.0) - jnp.log1p(jnp.exp(-jnp.abs(gates)))
    gates_t = jnp.where(lane_g < ML_HEADS, gates, log_sig).T[:SUBLANES]

    causal = (jax.lax.broadcasted_iota(jnp.int32, (L, L), 1)
              <= jax.lax.broadcasted_iota(jnp.int32, (L, L), 0))
    ones_ext = jnp.ones((L, head_dim), BF16)

    for c in range(n_chunks):
        rows = slice(c * L, (c + 1) * L)
        li = gates_t[:, rows]
        lf = pltpu.roll(li, ML_HEADS, axis=0)
        bcum = _lane_scan(lf, jnp.add, 0.0)
        b_last = jnp.sum(lf, axis=1, keepdims=True)
        row_b = li - bcum
        cmax = _lane_scan(row_b, jnp.maximum, -jnp.inf)
        w_loc = b_last + row_b
        m_loc = jnp.max(w_loc, axis=1, keepdims=True)
        m_prev_b = m_ref[...]
        m_prev = jnp.max(m_prev_b, axis=1, keepdims=True)
        m_s = jnp.maximum(bcum + m_prev, bcum + cmax)
        m_new = jnp.maximum(b_last + m_prev, m_loc)
        row_ref[0] = bcum - m_s
        row_ref[1] = m_s
        row_ref[2] = row_b
        row_ref[3] = jnp.exp(w_loc - m_loc)
        row_ref[4] = jnp.broadcast_to(jnp.exp(b_last + m_prev - m_new), (SUBLANES, L))
        row_ref[5] = jnp.broadcast_to(jnp.exp(m_loc - m_new), (SUBLANES, L))
        row_ref[6] = m_prev_b
        m_ref[...] = jnp.broadcast_to(m_new, (SUBLANES, L))

        heads = range(ML_HEADS)
        q_f = [qk_ref[h, rows, :] for h in heads]
        k_t = [qk_ref[ML_HEADS + h, rows, :].T for h in heads]
        s = [jnp.dot(q_f[h].astype(BF16), k_t[h].astype(BF16), preferred_element_type=F32) for h in heads]
        yield MLP_PHASES_AFTER["ml_stage"][0]

        lhs, rhs, v_ext, ms_b = [], [], [], []
        for h in heads:
            cols = slice(h * head_dim, (h + 1) * head_dim)
            ca_b = jnp.broadcast_to(row_ref[0, h:h + 1, :], (head_dim, L)).T
            ms_b.append(jnp.broadcast_to(row_ref[1, h:h + 1, :], (head_dim, L)).T)
            p = jnp.exp(jnp.where(causal, ca_b + row_ref[2, h:h + 1, :], -jnp.inf))
            inter_w = jnp.exp(ca_b + row_ref[6, h:h + 1, :])
            v_ext.append(jnp.concatenate([v_all[rows, cols].astype(BF16), ones_ext], axis=1))
            lhs.append(jnp.concatenate([(s[h] * p).astype(BF16), (q_f[h] * inter_w).astype(BF16)], axis=1))
            rhs.append(jnp.concatenate([v_ext[h], c_ref[h].astype(BF16)], axis=0))
        yield MLP_PHASES_AFTER["ml_stage"][1]

        nd = [jnp.dot(lhs[h], rhs[h], preferred_element_type=F32) for h in heads]
        upd = [jnp.dot((k_t[h] * row_ref[3, h:h + 1, :]).astype(BF16), v_ext[h], preferred_element_type=F32)
               for h in heads]
        yield MLP_PHASES_AFTER["ml_stage"][2]

        for h in heads:
            cols = slice(h * head_dim, (h + 1) * head_dim)
            hh = nd[h][:, :head_dim] / jnp.maximum(jnp.abs(nd[h][:, head_dim:]), jnp.exp(-ms_b[h]))
            hh = _sigmoid(o_all[rows, cols]) * hh
            y_ref[rows, d_rg + h * head_dim:d_rg + (h + 1) * head_dim] = _rms(
                hh, ml_gain_ref[:, cols]).astype(BF16)
            s_old = row_ref[4, h:h + 1, :]
            s_new = row_ref[5, h:h + 1, :]
            c_ref[h] = (jnp.concatenate([s_old, s_old], axis=1) * c_ref[h]
                        + jnp.concatenate([s_new, s_new], axis=1) * upd[h])
        yield MLP_PHASES_AFTER["ml_stage"][3]

    mix = jnp.dot(y_ref[...], w_out_ref[...], preferred_element_type=F32)
    yield MLP_PHASES_AFTER["w_out"]
    return x + _rms(mix, post_g_ref[...])


def _mlp_tile(h_ref, pre_g_ref, w_up_ref, w_down_ref, post_g_ref):
    vb = _rms(h_ref[...], pre_g_ref[...]).astype(BF16)
    d_ff = w_up_ref.shape[1]
    acc = jnp.zeros(h_ref.shape, F32)
    for c in range(d_ff // MLP_FF_CHUNK):
        cols = slice(c * MLP_FF_CHUNK, (c + 1) * MLP_FF_CHUNK)
        f = jnp.maximum(jnp.dot(vb, w_up_ref[:, cols], preferred_element_type=F32), 0.0)
        yield
        acc = acc + jnp.dot((f * f).astype(BF16), w_down_ref[cols, :], preferred_element_type=F32)
        yield
    return h_ref[...] + _rms(acc, post_g_ref[...])


def _run(gen):
    while True:
        try:
            next(gen)
        except StopIteration as stop:
            return stop.value


def _interleave(primary, secondary, lead=0, per=1):
    results = [None, None]

    def advance(idx, gen):
        if results[idx] is None:
            try:
                return next(gen)
            except StopIteration as stop:
                results[idx] = (stop.value,)
        return 0

    for _ in range(lead):
        advance(1, secondary)
    while results[0] is None:
        for _ in range(advance(0, primary) * per):
            advance(1, secondary)
    while results[1] is None:
        advance(1, secondary)
    return results[0][0], results[1][0]


def _weight_copy_jobs(hbm, vmem, src_row_starts=None):
    rows, cols = vmem.shape
    jobs = []
    for cb in range(cols // WEIGHT_BLOCK):
        for rb in range(rows // WEIGHT_BLOCK):
            dst = (vmem, rb * WEIGHT_BLOCK, cb * WEIGHT_BLOCK)
            if src_row_starts is None:
                jobs.append((hbm, rb * WEIGHT_BLOCK, cb * WEIGHT_BLOCK) + dst + (False,))
            else:
                jobs.append((hbm, src_row_starts[cb], rb * WEIGHT_BLOCK) + dst + (True,))
    return jobs


def _load_weights(jobs, stage_ref, sem_ref):
    def copy(i):
        src, r0, c0 = jobs[i][:3]
        slot = i % 2
        return pltpu.make_async_copy(
            src.at[pl.ds(r0, WEIGHT_BLOCK), pl.ds(c0, WEIGHT_BLOCK)], stage_ref.at[slot], sem_ref.at[slot])

    copy(0).start()
    for i in range(len(jobs)):
        if i + 1 < len(jobs):
            copy(i + 1).start()
        copy(i).wait()
        dst, r0, c0, transposed = jobs[i][3:]
        block = stage_ref[i % 2]
        dst[r0:r0 + WEIGHT_BLOCK, c0:c0 + WEIGHT_BLOCK] = (block.T if transposed else block).astype(BF16)
        yield


def _layer_kernel(x_ref, *refs, tiles_per_seq, n_tiles, n_small, w_in_row_starts, **dims):
    (pre_g, w_if, b_if, conv_w, conv_b, w_gate, b_r, b_i, lam, rg_gain, ml_gain, post_g,
     mlp_pre_g, mlp_post_g) = refs[:n_small]
    w_in_t_hbm, w_out_hbm, w_up_hbm, w_down_hbm = refs[n_small:n_small + 4]
    out_ref = refs[n_small + 4]
    w_main, w_out, w_up, w_down, stage_ref, sem_ref, h1_ref = refs[n_small + 5:n_small + 12]
    state = refs[n_small + 12:]
    s = pl.program_id(0)

    @pl.when(s == 0)
    def _():
        h1_ref[...] = jnp.zeros_like(h1_ref)
        _run(_load_weights(_weight_copy_jobs(w_in_t_hbm, w_main, w_in_row_starts)
                           + _weight_copy_jobs(w_out_hbm, w_out) + _weight_copy_jobs(w_up_hbm, w_up)
                           + _weight_copy_jobs(w_down_hbm, w_down), stage_ref, sem_ref))

    h1_new, out = _interleave(
        _mixer_tile(x_ref[0], s % tiles_per_seq == 0, pre_g, w_main, w_if, b_if, conv_w, conv_b,
                    w_gate, b_r, b_i, lam, rg_gain, ml_gain, w_out, post_g, *state, **dims),
        _mlp_tile(h1_ref, mlp_pre_g, w_up, w_down, mlp_post_g),
        MLP_PHASES_AFTER["start"])
    out_ref[0] = out
    h1_ref[...] = h1_new


def _const_spec(shape):
    zeros = (0,) * len(shape)
    return pl.BlockSpec(shape, lambda *_: zeros, pipeline_mode=pl.Buffered(1))


def _block_diag_gate(w):
    nb, bd, _ = w.shape
    per = MXU_WIDTH // bd
    w = w.reshape(nb // per, per, bd, bd)
    eye = jnp.eye(per, dtype=w.dtype)
    return jnp.einsum("gpij,pq->gpiqj", w, eye).reshape(nb // per, MXU_WIDTH, MXU_WIDTH)


def _layer(x, pre_gain, w_in, rg_conv_w, rg_conv_b, gate_r_w, gate_r_b, gate_i_w, gate_i_b, lam,
           ml_conv_w, ml_conv_b, igate_b, fgate_b, rg_gain, ml_gain, w_out, post_gain,
           pre_mlp_gain, w_up, w_down, post_mlp_gain):
    bsz, seq, d_model = x.shape
    assert w_up.shape[1] % MLP_FF_CHUNK == 0
    d_rg = lam.shape[0]
    d_ml = ml_gain.shape[0]
    head_dim = d_ml // ML_HEADS
    assert head_dim == LANES and ML_CHUNK == LANES and 2 * ML_HEADS == SUBLANES
    assert seq % SEQ_TILE == 0 and SEQ_TILE % ML_CHUNK == 0 and d_rg % MXU_WIDTH == 0
    d_conv = d_rg + 2 * d_ml

    assert d_rg == WEIGHT_BLOCK and d_ml == WEIGHT_BLOCK
    starts = dict(rg_x=0, rg_gate=d_rg, q=2 * d_rg, k=2 * d_rg + d_ml, v=2 * d_rg + 2 * d_ml, o=2 * d_rg + 3 * d_ml)
    w_in_row_starts = tuple(starts[n] for n in ("rg_x", "q", "k", "rg_gate", "v", "o"))
    d_main = 2 * d_rg + 4 * d_ml
    if_w = w_in[:, d_main:]
    w_if = jnp.pad(if_w, ((0, 0), (0, LANES - 2 * ML_HEADS))).astype(BF16)
    b_if = jnp.pad(jnp.concatenate([igate_b, fgate_b]), (0, LANES - 2 * ML_HEADS)).reshape(1, LANES)
    conv_w = jnp.concatenate([rg_conv_w, ml_conv_w], axis=1)
    conv_b = jnp.concatenate([rg_conv_b, ml_conv_b]).reshape(1, d_conv)
    w_gate = jnp.concatenate([_block_diag_gate(gate_r_w), _block_diag_gate(gate_i_w)], axis=2).astype(BF16)

    row = lambda v: v.reshape(1, -1)
    small = (row(pre_gain), w_if, b_if, conv_w, conv_b, w_gate, row(gate_r_b), row(gate_i_b), row(lam),
             row(rg_gain), row(ml_gain), row(post_gain), row(pre_mlp_gain), row(post_mlp_gain))
    big = (jnp.swapaxes(w_in, 0, 1), w_out, w_up, w_down)
    d_ff = w_up.shape[1]
    for w in (w_out, w_up, w_down):
        assert w.shape[0] % WEIGHT_BLOCK == 0 and w.shape[1] % WEIGHT_BLOCK == 0

    tiles_per_seq = seq // SEQ_TILE
    n_tiles = bsz * tiles_per_seq

    def tile_block(t):
        return (t // tiles_per_seq, t % tiles_per_seq, 0)

    x_spec = pl.BlockSpec((1, SEQ_TILE, d_model), lambda s: tile_block(jnp.minimum(s, n_tiles - 1)))
    out_spec = pl.BlockSpec((1, SEQ_TILE, d_model), lambda s: tile_block(jnp.maximum(s - 1, 0)))
    in_specs = ([x_spec] + [_const_spec(op.shape) for op in small]
                + [pl.BlockSpec(memory_space=pl.ANY)] * len(big))
    kern = functools.partial(_layer_kernel, tiles_per_seq=tiles_per_seq, n_tiles=n_tiles, n_small=len(small),
                             w_in_row_starts=w_in_row_starts, d_rg=d_rg, d_ml=d_ml, head_dim=head_dim)
    return pl.pallas_call(
        kern,
        grid=(n_tiles + 1,),
        in_specs=in_specs,
        out_specs=out_spec,
        out_shape=jax.ShapeDtypeStruct(x.shape, x.dtype),
        scratch_shapes=[
            pltpu.VMEM((d_model, d_main), BF16),
            pltpu.VMEM((d_rg + d_ml, d_model), BF16),
            pltpu.VMEM((d_model, d_ff), BF16),
            pltpu.VMEM((d_ff, d_model), BF16),
            pltpu.VMEM((2, WEIGHT_BLOCK, WEIGHT_BLOCK), F32),
            pltpu.SemaphoreType.DMA((2,)),
            pltpu.VMEM((SEQ_TILE, d_model), F32),
            pltpu.VMEM(((d_conv + d_rg) // LANES, SUBLANES + SEQ_TILE, LANES), F32),
            pltpu.VMEM((2 * d_ml // LANES, SEQ_TILE, LANES), F32),
            pltpu.VMEM((d_rg // LANES, SEQ_TILE, LANES), F32),
            pltpu.VMEM((SUBLANES, d_rg), F32),
            pltpu.VMEM((ML_HEADS, head_dim, 2 * head_dim), F32),
            pltpu.VMEM((SUBLANES, ML_CHUNK), F32),
            pltpu.VMEM((7, SUBLANES, ML_CHUNK), F32),
            pltpu.VMEM((SEQ_TILE, d_rg + d_ml), BF16),
        ],
        compiler_params=pltpu.CompilerParams(
            dimension_semantics=("arbitrary",), vmem_limit_bytes=VMEM_LIMIT_BYTES),
        name="layer",
    )(x, *small, *big)


def kernel(x, pre_mix_gain, w_in, rg_conv_w, rg_conv_b, rg_gate_r_w, rg_gate_r_b, rg_gate_i_w, rg_gate_i_b, rg_lambda, ml_conv_w, ml_conv_b, ml_igate_b, ml_fgate_b, rg_out_gain, ml_out_gain, w_out, post_mix_gain, pre_mlp_gain, mlp_w_up, mlp_w_down, post_mlp_gain):
    h = x
    for l in range(w_in.shape[0]):
        h = _layer(h, pre_mix_gain[l], w_in[l], rg_conv_w[l], rg_conv_b[l], rg_gate_r_w[l], rg_gate_r_b[l],
                   rg_gate_i_w[l], rg_gate_i_b[l], rg_lambda[l], ml_conv_w[l], ml_conv_b[l], ml_igate_b[l],
                   ml_fgate_b[l], rg_out_gain[l], ml_out_gain[l], w_out[l], post_mix_gain[l],
                   pre_mlp_gain[l], mlp_w_up[l], mlp_w_down[l], post_mlp_gain[l])
    return h
```

```python
import functools

import jax
import jax.numpy as jnp
from jax.experimental import pallas as pl
from jax.experimental.pallas import tpu as pltpu

F32 = jnp.float32
BF16 = jnp.bfloat16

RG_BLOCKS = 8
RG_C = 8.0
ML_HEADS = 4
CONV_WIDTH = 4
EPS = 1e-6

SUBLANES = 8
LANES = 128
MXU_WIDTH = 256

SEQ_TILE = 512
ML_CHUNK = LANES
PHASES = 4
MLP_FF_CHUNK = 512
WEIGHT_BLOCK = 512
MLP_PHASES_AFTER = {"start": 0, "w_in": 1, "conv": (1, 1, 1, 1, 1, 1), "rg_gates": 1, "rg_scan": (1, 1, 1),
                    "ml_stage": (1, 0, 1, 0), "w_out": 1}
VMEM_LIMIT_BYTES = 60 * 1024 * 1024


def _rms(x, gain):
    return x * jax.lax.rsqrt(jnp.mean(x * x, axis=-1, keepdims=True) + EPS) * gain


LOG2E = 1.4426950408889634


def _sigmoid(x):
    return 1.0 / (1.0 + jnp.exp2(x * (-LOG2E)))


def _softplus(x):
    return jnp.maximum(x, 0.0) + jnp.log1p(jnp.exp(-jnp.abs(x)))


def _gelu_tanh(x):
    c = 0.7978845608028654
    return 0.5 * x * (1.0 + jnp.tanh(c * (x + 0.044715 * (x * x * x))))


def _lane_scan(x, op, fill):
    n = x.shape[1]
    lane = jax.lax.broadcasted_iota(jnp.int32, x.shape, 1)
    d = 1
    while d < n:
        shifted = jnp.where(lane < d, fill, pltpu.roll(x, d, axis=1))
        x = op(x, shifted)
        d *= 2
    return x


def _affine_row_scan(sa, sb):
    n, width = sa.shape
    row8 = jax.lax.broadcasted_iota(jnp.int32, (SUBLANES, width), 0)
    d = 1
    while d < n:
        if d < SUBLANES:
            ra = pltpu.roll(sa, d, axis=0)
            rb = pltpu.roll(sb, d, axis=0)
            a_sh = jnp.concatenate([jnp.where(row8 < d, 1.0, ra[:SUBLANES]), ra[SUBLANES:]], axis=0)
            b_sh = jnp.concatenate([jnp.where(row8 < d, 0.0, rb[:SUBLANES]), rb[SUBLANES:]], axis=0)
            sb = sb + sa * b_sh
            sa = sa * a_sh
        else:
            sb = jnp.concatenate([sb[:d], sb[d:] + sa[d:] * sb[:n - d]], axis=0)
            sa = jnp.concatenate([sa[:d], sa[d:] * sa[:n - d]], axis=0)
        d *= 2
    return sa, sb


def _mixer_tile(x, seq_start, pre_g_ref, w_main_ref, w_if_ref, b_if_ref, conv_w_ref, conv_b_ref, w_gate_ref,
                b_r_ref, b_i_ref, lam_ref, rg_gain_ref, ml_gain_ref, w_out_ref, post_g_ref,
                cbuf_ref, qk_ref, yrg_ref, h_ref, c_ref, m_ref, row_ref, y_ref,
                *, d_rg, d_ml, head_dim):
    ts = x.shape[0]
    d_conv = d_rg + 2 * d_ml
    n_chunks = ts // ML_CHUNK
    L = ML_CHUNK

    @pl.when(seq_start)
    def _():
        cbuf_ref[:, ts:ts + SUBLANES, :] = jnp.zeros((cbuf_ref.shape[0], SUBLANES, LANES), F32)
        h_ref[...] = jnp.zeros_like(h_ref)
        c_ref[...] = jnp.zeros_like(c_ref)
        m_ref[...] = jnp.zeros_like(m_ref)

    ub = _rms(x, pre_g_ref[...]).astype(BF16)
    proj = jnp.dot(ub, w_main_ref[...], preferred_element_type=F32)
    gates = jnp.dot(ub, w_if_ref[...], preferred_element_type=F32) + b_if_ref[...]
    yield MLP_PHASES_AFTER["w_in"]

    n_cs = d_conv // LANES
    n_rs = d_rg // LANES
    G = ts // PHASES
    cbuf_ref[:n_cs, :SUBLANES, :] = cbuf_ref[:n_cs, ts:ts + SUBLANES, :]
    for j in range(n_cs + n_rs):
        cbuf_ref[j, SUBLANES:, :] = proj[:, j * LANES:(j + 1) * LANES]

    def phase(j, e):
        return cbuf_ref[j, pl.ds(SUBLANES + e, G, stride=PHASES), :]

    conv_ph = [[None] * n_cs for _ in range(PHASES)]
    for j in range(n_cs):
        cols = slice(j * LANES, (j + 1) * LANES)
        taps = {e: phase(j, e) for e in range(1 - CONV_WIDTH, PHASES)}
        for r in range(PHASES):
            acc = taps[r] * conv_w_ref[CONV_WIDTH - 1:CONV_WIDTH, cols] + conv_b_ref[:, cols]
            for k in range(1, CONV_WIDTH):
                acc = acc + taps[r - k] * conv_w_ref[CONV_WIDTH - 1 - k:CONV_WIDTH - k, cols]
            conv_ph[r][j] = acc
        if j >= n_rs:
            scale = head_dim ** -0.5 if j < n_rs + d_ml // LANES else 1.0
            for r in range(PHASES):
                val = conv_ph[r][j]
                val = val * _sigmoid(val)
                qk_ref[j - n_rs, pl.ds(r, G, stride=PHASES), :] = val * scale if scale != 1.0 else val
        if j % 2 == 1:
            yield MLP_PHASES_AFTER["conv"][j // 2]

    xc = jnp.concatenate([jnp.concatenate(conv_ph[r][:n_rs], axis=1) for r in range(PHASES)], axis=0)
    r_parts, i_parts = [], []
    for g in range(d_rg // MXU_WIDTH):
        gg = jnp.dot(xc[:, g * MXU_WIDTH:(g + 1) * MXU_WIDTH].astype(BF16), w_gate_ref[g],
                     preferred_element_type=F32)
        r_parts.append(gg[:, :MXU_WIDTH])
        i_parts.append(gg[:, MXU_WIDTH:])
    r = _sigmoid(jnp.concatenate(r_parts, axis=1) + b_r_ref[...])
    i_gate = _sigmoid(jnp.concatenate(i_parts, axis=1) + b_i_ref[...])
    neg_log_a_unit = RG_C * _softplus(-lam_ref[...])
    a = jnp.exp2(r * (neg_log_a_unit * (-LOG2E)))
    z = jnp.tanh(r * neg_log_a_unit) * (a * a + 1.0)
    b_in = jnp.where(z > 0.0, z * jax.lax.rsqrt(z), 0.0) * (i_gate * xc)
    yield MLP_PHASES_AFTER["rg_gates"]

    comp_a, comp_b = [a[:G]], [b_in[:G]]
    for r in range(1, PHASES):
        a_r, b_r = a[r * G:(r + 1) * G], b_in[r * G:(r + 1) * G]
        comp_b.append(a_r * comp_b[-1] + b_r)
        comp_a.append(a_r * comp_a[-1])
    ga, gb = _affine_row_scan(comp_a[-1], comp_b[-1])
    carry = h_ref[SUBLANES - 1:SUBLANES, :]
    h_end = ga * carry + gb
    h_ref[...] = h_end[G - SUBLANES:]
    rolled = pltpu.roll(h_end, 1, axis=0)
    row8r = jax.lax.broadcasted_iota(jnp.int32, (SUBLANES, d_rg), 0)
    h_prev = jnp.concatenate([jnp.where(row8r < 1, carry, rolled[:SUBLANES]), rolled[SUBLANES:]], axis=0)
    yield MLP_PHASES_AFTER["rg_scan"][0]
    for r in range(PHASES):
        h_r = comp_a[r] * h_prev + comp_b[r]
        gate_r = jnp.concatenate([phase(n_cs + j, r) for j in range(n_rs)], axis=1)
        y_r = _rms(h_r * _gelu_tanh(gate_r), rg_gain_ref[...])
        for j in range(n_rs):
            yrg_ref[j, pl.ds(r, G, stride=PHASES), :] = y_r[:, j * LANES:(j + 1) * LANES]
        if r % 2 == 1:
            yield MLP_PHASES_AFTER["rg_scan"][1 + r // 2]
    for j in range(n_rs):
        y_ref[:, j * LANES:(j + 1) * LANES] = yrg_ref[j].astype(BF16)

    v_all = proj[:, d_conv + d_rg:d_conv + d_rg + d_ml]
    o_all = proj[:, d_conv + d_rg + d_ml:]

    lane_g = jax.lax.broadcasted_iota(jnp.int32, gates.shape, 1)
    log_sig = jnp.minimum(gates, 0.0) - jnp.log1p(jnp.exp(-jnp.abs(gates)))
    gates_t = jnp.where(lane_g < ML_HEADS, gates, log_sig).T[:SUBLANES]

    causal = (jax.lax.broadcasted_iota(jnp.int32, (L, L), 1)
              <= jax.lax.broadcasted_iota(jnp.int32, (L, L), 0))
    ones_ext = jnp.ones((L, head_dim), BF16)

    for c in range(n_chunks):
        rows = slice(c * L, (c + 1) * L)
        li = gates_t[:, rows]
        lf = pltpu.roll(li, ML_HEADS, axis=0)
        bcum = _lane_scan(lf, jnp.add, 0.0)
        b_last = jnp.sum(lf, axis=1, keepdims=True)
        row_b = li - bcum
        cmax = _lane_scan(row_b, jnp.maximum, -jnp.inf)
        w_loc = b_last + row_b
        m_loc = jnp.max(w_loc, axis=1, keepdims=True)
        m_prev_b = m_ref[...]
        m_prev = jnp.max(m_prev_b, axis=1, keepdims=True)
        m_s = jnp.maximum(bcum + m_prev, bcum + cmax)
        m_new = jnp.maximum(b_last + m_prev, m_loc)
        row_ref[0] = bcum - m_s
        row_ref[1] = m_s
        row_ref[2] = row_b
        row_ref[3] = jnp.exp(w_loc - m_loc)
        row_ref[4] = jnp.broadcast_to(jnp.exp(b_last + m_prev - m_new), (SUBLANES, L))
        row_ref[5] = jnp.broadcast_to(jnp.exp(m_loc - m_new), (SUBLANES, L))
        row_ref[6] = m_prev_b
        m_ref[...] = jnp.broadcast_to(m_new, (SUBLANES, L))

        heads = range(ML_HEADS)
        q_f = [qk_ref[h, rows, :] for h in heads]
        k_t = [qk_ref[ML_HEADS + h, rows, :].T for h in heads]
        s = [jnp.dot(q_f[h].astype(BF16), k_t[h].astype(BF16), preferred_element_type=F32) for h in heads]
        yield MLP_PHASES_AFTER["ml_stage"][0]

        lhs, rhs, v_ext, ms_b = [], [], [], []
        for h in heads:
            cols = slice(h * head_dim, (h + 1) * head_dim)
            ca_b = jnp.broadcast_to(row_ref[0, h:h + 1, :], (head_dim, L)).T
            ms_b.append(jnp.broadcast_to(row_ref[1, h:h + 1, :], (head_dim, L)).T)
            p = jnp.exp(jnp.where(causal, ca_b + row_ref[2, h:h + 1, :], -jnp.inf))
            inter_w = jnp.exp(ca_b + row_ref[6, h:h + 1, :])
            v_ext.append(jnp.concatenate([v_all[rows, cols].astype(BF16), ones_ext], axis=1))
            lhs.append(jnp.concatenate([(s[h] * p).astype(BF16), (q_f[h] * inter_w).astype(BF16)], axis=1))
            rhs.append(jnp.concatenate([v_ext[h], c_ref[h].astype(BF16)], axis=0))
        yield MLP_PHASES_AFTER["ml_stage"][1]

        nd = [jnp.dot(lhs[h], rhs[h], preferred_element_type=F32) for h in heads]
        upd = [jnp.dot((k_t[h] * row_ref[3, h:h + 1, :]).astype(BF16), v_ext[h], preferred_element_type=F32)
               for h in heads]
        yield MLP_PHASES_AFTER["ml_stage"][2]

        for h in heads:
            cols = slice(h * head_dim, (h + 1) * head_dim)
            hh = nd[h][:, :head_dim] / jnp.maximum(jnp.abs(nd[h][:, head_dim:]), jnp.exp(-ms_b[h]))
            hh = _sigmoid(o_all[rows, cols]) * hh
            y_ref[rows, d_rg + h * head_dim:d_rg + (h + 1) * head_dim] = _rms(
                hh, ml_gain_ref[:, cols]).astype(BF16)
            s_old = row_ref[4, h:h + 1, :]
            s_new = row_ref[5, h:h + 1, :]
            c_ref[h] = (jnp.concatenate([s_old, s_old], axis=1) * c_ref[h]
                        + jnp.concatenate([s_new, s_new], axis=1) * upd[h])
        yield MLP_PHASES_AFTER["ml_stage"][3]

    mix = jnp.dot(y_ref[...], w_out_ref[...], preferred_element_type=F32)
    yield MLP_PHASES_AFTER["w_out"]
    return x + _rms(mix, post_g_ref[...])


def _mlp_tile(h_ref, pre_g_ref, w_up_ref, w_down_ref, post_g_ref):
    vb = _rms(h_ref[...], pre_g_ref[...]).astype(BF16)
    d_ff = w_up_ref.shape[1]
    acc = jnp.zeros(h_ref.shape, F32)
    for c in range(d_ff // MLP_FF_CHUNK):
        cols = slice(c * MLP_FF_CHUNK, (c + 1) * MLP_FF_CHUNK)
        f = jnp.maximum(jnp.dot(vb, w_up_ref[:, cols], preferred_element_type=F32), 0.0)
        yield
        acc = acc + jnp.dot((f * f).astype(BF16), w_down_ref[cols, :], preferred_element_type=F32)
        yield
    return h_ref[...] + _rms(acc, post_g_ref[...])


def _run(gen):
    while True:
        try:
            next(gen)
        except StopIteration as stop:
            return stop.value


def _interleave(primary, secondary, lead=0, per=1):
    results = [None, None]

    def advance(idx, gen):
        if results[idx] is None:
            try:
                return next(gen)
            except StopIteration as stop:
                results[idx] = (stop.value,)
        return 0

    for _ in range(lead):
        advance(1, secondary)
    while results[0] is None:
        for _ in range(advance(0, primary) * per):
            advance(1, secondary)
    while results[1] is None:
        advance(1, secondary)
    return results[0][0], results[1][0]


def _weight_copy_jobs(hbm, vmem, src_row_starts=None):
    rows, cols = vmem.shape
    jobs = []
    for cb in range(cols // WEIGHT_BLOCK):
        for rb in range(rows // WEIGHT_BLOCK):
            dst = (vmem, rb * WEIGHT_BLOCK, cb * WEIGHT_BLOCK)
            if src_row_starts is None:
                jobs.append((hbm, rb * WEIGHT_BLOCK, cb * WEIGHT_BLOCK) + dst + (False,))
            else:
                jobs.append((hbm, src_row_starts[cb], rb * WEIGHT_BLOCK) + dst + (True,))
    return jobs


def _load_weights(jobs, stage_ref, sem_ref):
    def copy(i):
        src, r0, c0 = jobs[i][:3]
        slot = i % 2
        return pltpu.make_async_copy(
            src.at[pl.ds(r0, WEIGHT_BLOCK), pl.ds(c0, WEIGHT_BLOCK)], stage_ref.at[slot], sem_ref.at[slot])

    copy(0).start()
    for i in range(len(jobs)):
        if i + 1 < len(jobs):
            copy(i + 1).start()
        copy(i).wait()
        dst, r0, c0, transposed = jobs[i][3:]
        block = stage_ref[i % 2]
        dst[r0:r0 + WEIGHT_BLOCK, c0:c0 + WEIGHT_BLOCK] = (block.T if transposed else block).astype(BF16)
        yield


def _layer_kernel(x_ref, *refs, tiles_per_seq, n_tiles, n_small, w_in_row_starts, **dims):
    (pre_g, w_if, b_if, conv_w, conv_b, w_gate, b_r, b_i, lam, rg_gain, ml_gain, post_g,
     mlp_pre_g, mlp_post_g) = refs[:n_small]
    w_in_t_hbm, w_out_hbm, w_up_hbm, w_down_hbm = refs[n_small:n_small + 4]
    out_ref = refs[n_small + 4]
    w_main, w_out, w_up, w_down, stage_ref, sem_ref, h1_ref = refs[n_small + 5:n_small + 12]
    state = refs[n_small + 12:]
    s = pl.program_id(0)

    @pl.when(s == 0)
    def _():
        h1_ref[...] = jnp.zeros_like(h1_ref)
        _run(_load_weights(_weight_copy_jobs(w_in_t_hbm, w_main, w_in_row_starts)
                           + _weight_copy_jobs(w_out_hbm, w_out) + _weight_copy_jobs(w_up_hbm, w_up)
                           + _weight_copy_jobs(w_down_hbm, w_down), stage_ref, sem_ref))

    h1_new, out = _interleave(
        _mixer_tile(x_ref[0], s % tiles_per_seq == 0, pre_g, w_main, w_if, b_if, conv_w, conv_b,
                    w_gate, b_r, b_i, lam, rg_gain, ml_gain, w_out, post_g, *state, **dims),
        _mlp_tile(h1_ref, mlp_pre_g, w_up, w_down, mlp_post_g),
        MLP_PHASES_AFTER["start"])
    out_ref[0] = out
    h1_ref[...] = h1_new


def _const_spec(shape):
    zeros = (0,) * len(shape)
    return pl.BlockSpec(shape, lambda *_: zeros, pipeline_mode=pl.Buffered(1))


def _block_diag_gate(w):
    nb, bd, _ = w.shape
    per = MXU_WIDTH // bd
    w = w.reshape(nb // per, per, bd, bd)
    eye = jnp.eye(per, dtype=w.dtype)
    return jnp.einsum("gpij,pq->gpiqj", w, eye).reshape(nb // per, MXU_WIDTH, MXU_WIDTH)


def _layer(x, pre_gain, w_in, rg_conv_w, rg_conv_b, gate_r_w, gate_r_b, gate_i_w, gate_i_b, lam,
           ml_conv_w, ml_conv_b, igate_b, fgate_b, rg_gain, ml_gain, w_out, post_gain,
           pre_mlp_gain, w_up, w_down, post_mlp_gain):
    bsz, seq, d_model = x.shape
    assert w_up.shape[1] % MLP_FF_CHUNK == 0
    d_rg = lam.shape[0]
    d_ml = ml_gain.shape[0]
    head_dim = d_ml // ML_HEADS
    assert head_dim == LANES and ML_CHUNK == LANES and 2 * ML_HEADS == SUBLANES
    assert seq % SEQ_TILE == 0 and SEQ_TILE % ML_CHUNK == 0 and d_rg % MXU_WIDTH == 0
    d_conv = d_rg + 2 * d_ml

    assert d_rg == WEIGHT_BLOCK and d_ml == WEIGHT_BLOCK
    starts = dict(rg_x=0, rg_gate=d_rg, q=2 * d_rg, k=2 * d_rg + d_ml, v=2 * d_rg + 2 * d_ml, o=2 * d_rg + 3 * d_ml)
    w_in_row_starts = tuple(starts[n] for n in ("rg_x", "q", "k", "rg_gate", "v", "o"))
    d_main = 2 * d_rg + 4 * d_ml
    if_w = w_in[:, d_main:]
    w_if = jnp.pad(if_w, ((0, 0), (0, LANES - 2 * ML_HEADS))).astype(BF16)
    b_if = jnp.pad(jnp.concatenate([igate_b, fgate_b]), (0, LANES - 2 * ML_HEADS)).reshape(1, LANES)
    conv_w = jnp.concatenate([rg_conv_w, ml_conv_w], axis=1)
    conv_b = jnp.concatenate([rg_conv_b, ml_conv_b]).reshape(1, d_conv)
    w_gate = jnp.concatenate([_block_diag_gate(gate_r_w), _block_diag_gate(gate_i_w)], axis=2).astype(BF16)

    row = lambda v: v.reshape(1, -1)
    small = (row(pre_gain), w_if, b_if, conv_w, conv_b, w_gate, row(gate_r_b), row(gate_i_b), row(lam),
             row(rg_gain), row(ml_gain), row(post_gain), row(pre_mlp_gain), row(post_mlp_gain))
    big = (jnp.swapaxes(w_in, 0, 1), w_out, w_up, w_down)
    d_ff = w_up.shape[1]
    for w in (w_out, w_up, w_down):
        assert w.shape[0] % WEIGHT_BLOCK == 0 and w.shape[1] % WEIGHT_BLOCK == 0

    tiles_per_seq = seq // SEQ_TILE
    n_tiles = bsz * tiles_per_seq

    def tile_block(t):
        return (t // tiles_per_seq, t % tiles_per_seq, 0)

    x_spec = pl.BlockSpec((1, SEQ_TILE, d_model), lambda s: tile_block(jnp.minimum(s, n_tiles - 1)))
    out_spec = pl.BlockSpec((1, SEQ_TILE, d_model), lambda s: tile_block(jnp.maximum(s - 1, 0)))
    in_specs = ([x_spec] + [_const_spec(op.shape) for op in small]
                + [pl.BlockSpec(memory_space=pl.ANY)] * len(big))
    kern = functools.partial(_layer_kernel, tiles_per_seq=tiles_per_seq, n_tiles=n_tiles, n_small=len(small),
                             w_in_row_starts=w_in_row_starts, d_rg=d_rg, d_ml=d_ml, head_dim=head_dim)
    return pl.pallas_call(
        kern,
        grid=(n_tiles + 1,),
        in_specs=in_specs,
        out_specs=out_spec,
        out_shape=jax.ShapeDtypeStruct(x.shape, x.dtype),
        scratch_shapes=[
            pltpu.VMEM((d_model, d_main), BF16),
            pltpu.VMEM((d_rg + d_ml, d_model), BF16),
            pltpu.VMEM((d_model, d_ff), BF16),
            pltpu.VMEM((d_ff, d_model), BF16),
            pltpu.VMEM((2, WEIGHT_BLOCK, WEIGHT_BLOCK), F32),
            pltpu.SemaphoreType.DMA((2,)),
            pltpu.VMEM((SEQ_TILE, d_model), F32),
            pltpu.VMEM(((d_conv + d_rg) // LANES, SUBLANES + SEQ_TILE, LANES), F32),
            pltpu.VMEM((2 * d_ml // LANES, SEQ_TILE, LANES), F32),
            pltpu.VMEM((d_rg // LANES, SEQ_TILE, LANES), F32),
            pltpu.VMEM((SUBLANES, d_rg), F32),
            pltpu.VMEM((ML_HEADS, head_dim, 2 * head_dim), F32),
            pltpu.VMEM((SUBLANES, ML_CHUNK), F32),
            pltpu.VMEM((7, SUBLANES, ML_CHUNK), F32),
            pltpu.VMEM((SEQ_TILE, d_rg + d_ml), BF16),
        ],
        compiler_params=pltpu.CompilerParams(
            dimension_semantics=("arbitrary",), vmem_limit_bytes=VMEM_LIMIT_BYTES),
        name="layer",
    )(x, *small, *big)


def kernel(x, pre_mix_gain, w_in, rg_conv_w, rg_conv_b, rg_gate_r_w, rg_gate_r_b, rg_gate_i_w, rg_gate_i_b, rg_lambda, ml_conv_w, ml_conv_b, ml_igate_b, ml_fgate_b, rg_out_gain, ml_out_gain, w_out, post_mix_gain, pre_mlp_gain, mlp_w_up, mlp_w_down, post_mlp_gain):
    h = x
    for l in range(w_in.shape[0]):
        h = _layer(h, pre_mix_gain[l], w_in[l], rg_conv_w[l], rg_conv_b[l], rg_gate_r_w[l], rg_gate_r_b[l],
                   rg_gate_i_w[l], rg_gate_i_b[l], rg_lambda[l], ml_conv_w[l], ml_conv_b[l], ml_igate_b[l],
                   ml_fgate_b[l], rg_out_gain[l], ml_out_gain[l], w_out[l], post_mix_gain[l],
                   pre_mlp_gain[l], mlp_w_up[l], mlp_w_down[l], post_mlp_gain[l])
    return h
```

```python
import functools

import jax
import jax.numpy as jnp
from jax.experimental import pallas as pl
from jax.experimental.pallas import tpu as pltpu

F32 = jnp.float32
BF16 = jnp.bfloat16

RG_BLOCKS = 8
RG_C = 8.0
ML_HEADS = 4
CONV_WIDTH = 4
EPS = 1e-6

SUBLANES = 8
LANES = 128
MXU_WIDTH = 256

SEQ_TILE = 512
ML_CHUNK = LANES
PHASES = 4
MLP_FF_CHUNK = 1024
WEIGHT_BLOCK = 512
MLP_PHASES_AFTER = {"start": 0, "w_in": 1, "conv": (1, 0, 1, 0, 1, 0), "rg_gates": 1, "rg_scan": (1, 0, 1),
                    "ml_stage": (1, 0, 0, 0), "w_out": 0}
VMEM_LIMIT_BYTES = 60 * 1024 * 1024


def _rms(x, gain):
    return x * jax.lax.rsqrt(jnp.mean(x * x, axis=-1, keepdims=True) + EPS) * gain


LOG2E = 1.4426950408889634


def _sigmoid(x):
    return 1.0 / (1.0 + jnp.exp2(x * (-LOG2E)))


def _softplus(x):
    return jnp.maximum(x, 0.0) + jnp.log1p(jnp.exp(-jnp.abs(x)))


def _gelu_tanh(x):
    c = 0.7978845608028654
    return 0.5 * x * (1.0 + jnp.tanh(c * (x + 0.044715 * (x * x * x))))


def _lane_scan(x, op, fill):
    n = x.shape[1]
    lane = jax.lax.broadcasted_iota(jnp.int32, x.shape, 1)
    d = 1
    while d < n:
        shifted = jnp.where(lane < d, fill, pltpu.roll(x, d, axis=1))
        x = op(x, shifted)
        d *= 2
    return x


def _affine_row_scan(sa, sb):
    n, width = sa.shape
    row8 = jax.lax.broadcasted_iota(jnp.int32, (SUBLANES, width), 0)
    d = 1
    while d < n:
        if d < SUBLANES:
            ra = pltpu.roll(sa, d, axis=0)
            rb = pltpu.roll(sb, d, axis=0)
            a_sh = jnp.concatenate([jnp.where(row8 < d, 1.0, ra[:SUBLANES]), ra[SUBLANES:]], axis=0)
            b_sh = jnp.concatenate([jnp.where(row8 < d, 0.0, rb[:SUBLANES]), rb[SUBLANES:]], axis=0)
            sb = sb + sa * b_sh
            sa = sa * a_sh
        else:
            sb = jnp.concatenate([sb[:d], sb[d:] + sa[d:] * sb[:n - d]], axis=0)
            sa = jnp.concatenate([sa[:d], sa[d:] * sa[:n - d]], axis=0)
        d *= 2
    return sa, sb


def _mixer_tile(x, seq_start, pre_g_ref, w_main_ref, w_if_ref, b_if_ref, conv_w_ref, conv_b_ref, w_gate_ref,
                b_r_ref, b_i_ref, lam_ref, rg_gain_ref, ml_gain_ref, w_out_ref, post_g_ref,
                cbuf_ref, qk_ref, yrg_ref, h_ref, c_ref, m_ref, row_ref, y_ref,
                *, d_rg, d_ml, head_dim):
    ts = x.shape[0]
    d_conv = d_rg + 2 * d_ml
    n_chunks = ts // ML_CHUNK
    L = ML_CHUNK

    @pl.when(seq_start)
    def _():
        cbuf_ref[:, ts:ts + SUBLANES, :] = jnp.zeros((cbuf_ref.shape[0], SUBLANES, LANES), F32)
        h_ref[...] = jnp.zeros_like(h_ref)
        c_ref[...] = jnp.zeros_like(c_ref)
        m_ref[...] = jnp.zeros_like(m_ref)

    ub = _rms(x, pre_g_ref[...]).astype(BF16)
    proj = jnp.dot(ub, w_main_ref[...], preferred_element_type=F32)
    gates = jnp.dot(ub, w_if_ref[...], preferred_element_type=F32) + b_if_ref[...]
    yield MLP_PHASES_AFTER["w_in"]

    n_cs = d_conv // LANES
    n_rs = d_rg // LANES
    G = ts // PHASES
    cbuf_ref[:n_cs, :SUBLANES, :] = cbuf_ref[:n_cs, ts:ts + SUBLANES, :]
    for j in range(n_cs + n_rs):
        cbuf_ref[j, SUBLANES:, :] = proj[:, j * LANES:(j + 1) * LANES]

    def phase(j, e):
        return cbuf_ref[j, pl.ds(SUBLANES + e, G, stride=PHASES), :]

    conv_ph = [[None] * n_cs for _ in range(PHASES)]
    for j in range(n_cs):
        cols = slice(j * LANES, (j + 1) * LANES)
        taps = {e: phase(j, e) for e in range(1 - CONV_WIDTH, PHASES)}
        for r in range(PHASES):
            acc = taps[r] * conv_w_ref[CONV_WIDTH - 1:CONV_WIDTH, cols] + conv_b_ref[:, cols]
            for k in range(1, CONV_WIDTH):
                acc = acc + taps[r - k] * conv_w_ref[CONV_WIDTH - 1 - k:CONV_WIDTH - k, cols]
            conv_ph[r][j] = acc
        if j >= n_rs:
            scale = head_dim ** -0.5 if j < n_rs + d_ml // LANES else 1.0
            for r in range(PHASES):
                val = conv_ph[r][j]
                val = val * _sigmoid(val)
                qk_ref[j - n_rs, pl.ds(r, G, stride=PHASES), :] = val * scale if scale != 1.0 else val
        if j % 2 == 1:
            yield MLP_PHASES_AFTER["conv"][j // 2]

    xc = jnp.concatenate([jnp.concatenate(conv_ph[r][:n_rs], axis=1) for r in range(PHASES)], axis=0)
    r_parts, i_parts = [], []
    for g in range(d_rg // MXU_WIDTH):
        gg = jnp.dot(xc[:, g * MXU_WIDTH:(g + 1) * MXU_WIDTH].astype(BF16), w_gate_ref[g],
                     preferred_element_type=F32)
        r_parts.append(gg[:, :MXU_WIDTH])
        i_parts.append(gg[:, MXU_WIDTH:])
    r = _sigmoid(jnp.concatenate(r_parts, axis=1) + b_r_ref[...])
    i_gate = _sigmoid(jnp.concatenate(i_parts, axis=1) + b_i_ref[...])
    neg_log_a_unit = RG_C * _softplus(-lam_ref[...])
    a = jnp.exp2(r * (neg_log_a_unit * (-LOG2E)))
    z = jnp.tanh(r * neg_log_a_unit) * (a * a + 1.0)
    b_in = jnp.where(z > 0.0, z * jax.lax.rsqrt(z), 0.0) * (i_gate * xc)
    yield MLP_PHASES_AFTER["rg_gates"]

    comp_a, comp_b = [a[:G]], [b_in[:G]]
    for r in range(1, PHASES):
        a_r, b_r = a[r * G:(r + 1) * G], b_in[r * G:(r + 1) * G]
        comp_b.append(a_r * comp_b[-1] + b_r)
        comp_a.append(a_r * comp_a[-1])
    ga, gb = _affine_row_scan(comp_a[-1], comp_b[-1])
    carry = h_ref[SUBLANES - 1:SUBLANES, :]
    h_end = ga * carry + gb
    h_ref[...] = h_end[G - SUBLANES:]
    rolled = pltpu.roll(h_end, 1, axis=0)
    row8r = jax.lax.broadcasted_iota(jnp.int32, (SUBLANES, d_rg), 0)
    h_prev = jnp.concatenate([jnp.where(row8r < 1, carry, rolled[:SUBLANES]), rolled[SUBLANES:]], axis=0)
    yield MLP_PHASES_AFTER["rg_scan"][0]
    for r in range(PHASES):
        h_r = comp_a[r] * h_prev + comp_b[r]
        gate_r = jnp.concatenate([phase(n_cs + j, r) for j in range(n_rs)], axis=1)
        y_r = _rms(h_r * _gelu_tanh(gate_r), rg_gain_ref[...])
        for j in range(n_rs):
            yrg_ref[j, pl.ds(r, G, stride=PHASES), :] = y_r[:, j * LANES:(j + 1) * LANES]
        if r % 2 == 1:
            yield MLP_PHASES_AFTER["rg_scan"][1 + r // 2]
    for j in range(n_rs):
        y_ref[:, j * LANES:(j + 1) * LANES] = yrg_ref[j].astype(BF16)

    v_all = proj[:, d_conv + d_rg:d_conv + d_rg + d_ml]
    o_all = proj[:, d_conv + d_rg + d_ml:]

    lane_g = jax.lax.broadcasted_iota(jnp.int32, gates.shape, 1)
    log_sig = jnp.minimum(gates, 0.0) - jnp.log1p(jnp.exp(-jnp.abs(gates)))
    gates_t = jnp.where(lane_g < ML_HEADS, gates, log_sig).T[:SUBLANES]

    causal = (jax.lax.broadcasted_iota(jnp.int32, (L, L), 1)
              <= jax.lax.broadcasted_iota(jnp.int32, (L, L), 0))
    ones_ext = jnp.ones((L, head_dim), BF16)

    for c in range(n_chunks):
        rows = slice(c * L, (c + 1) * L)
        li = gates_t[:, rows]
        lf = pltpu.roll(li, ML_HEADS, axis=0)
        bcum = _lane_scan(lf, jnp.add, 0.0)
        b_last = jnp.sum(lf, axis=1, keepdims=True)
        row_b = li - bcum
        cmax = _lane_scan(row_b, jnp.maximum, -jnp.inf)
        w_loc = b_last + row_b
        m_loc = jnp.max(w_loc, axis=1, keepdims=True)
        m_prev_b = m_ref[...]
        m_prev = jnp.max(m_prev_b, axis=1, keepdims=True)
        m_s = jnp.maximum(bcum + m_prev, bcum + cmax)
        m_new = jnp.maximum(b_last + m_prev, m_loc)
        row_ref[0] = bcum - m_s
        row_ref[1] = m_s
        row_ref[2] = row_b
        row_ref[3] = jnp.exp(w_loc - m_loc)
        row_ref[4] = jnp.broadcast_to(jnp.exp(b_last + m_prev - m_new), (SUBLANES, L))
        row_ref[5] = jnp.broadcast_to(jnp.exp(m_loc - m_new), (SUBLANES, L))
        row_ref[6] = m_prev_b
        m_ref[...] = jnp.broadcast_to(m_new, (SUBLANES, L))

        heads = range(ML_HEADS)
        q_f = [qk_ref[h, rows, :] for h in heads]
        k_t = [qk_ref[ML_HEADS + h, rows, :].T for h in heads]
        s = [jnp.dot(q_f[h].astype(BF16), k_t[h].astype(BF16), preferred_element_type=F32) for h in heads]
        yield MLP_PHASES_AFTER["ml_stage"][0]

        lhs, rhs, v_ext, ms_b = [], [], [], []
        for h in heads:
            cols = slice(h * head_dim, (h + 1) * head_dim)
            ca_b = jnp.broadcast_to(row_ref[0, h:h + 1, :], (head_dim, L)).T
            ms_b.append(jnp.broadcast_to(row_ref[1, h:h + 1, :], (head_dim, L)).T)
            p = jnp.exp(jnp.where(causal, ca_b + row_ref[2, h:h + 1, :], -jnp.inf))
            inter_w = jnp.exp(ca_b + row_ref[6, h:h + 1, :])
            v_ext.append(jnp.concatenate([v_all[rows, cols].astype(BF16), ones_ext], axis=1))
            lhs.append(jnp.concatenate([(s[h] * p).astype(BF16), (q_f[h] * inter_w).astype(BF16)], axis=1))
            rhs.append(jnp.concatenate([v_ext[h], c_ref[h].astype(BF16)], axis=0))
        yield MLP_PHASES_AFTER["ml_stage"][1]

        nd = [jnp.dot(lhs[h], rhs[h], preferred_element_type=F32) for h in heads]
        upd = [jnp.dot((k_t[h] * row_ref[3, h:h + 1, :]).astype(BF16), v_ext[h], preferred_element_type=F32)
               for h in heads]
        yield MLP_PHASES_AFTER["ml_stage"][2]

        for h in heads:
            cols = slice(h * head_dim, (h + 1) * head_dim)
            hh = nd[h][:, :head_dim] / jnp.maximum(jnp.abs(nd[h][:, head_dim:]), jnp.exp(-ms_b[h]))
            hh = _sigmoid(o_all[rows, cols]) * hh
            y_ref[rows, d_rg + h * head_dim:d_rg + (h + 1) * head_dim] = _rms(
                hh, ml_gain_ref[:, cols]).astype(BF16)
            s_old = row_ref[4, h:h + 1, :]
            s_new = row_ref[5, h:h + 1, :]
            c_ref[h] = (jnp.concatenate([s_old, s_old], axis=1) * c_ref[h]
                        + jnp.concatenate([s_new, s_new], axis=1) * upd[h])
        yield MLP_PHASES_AFTER["ml_stage"][3]

    mix = jnp.dot(y_ref[...], w_out_ref[...], preferred_element_type=F32)
    yield MLP_PHASES_AFTER["w_out"]
    return x + _rms(mix, post_g_ref[...])


def _mlp_tile(h_ref, pre_g_ref, w_up_ref, w_down_ref, post_g_ref):
    vb = _rms(h_ref[...], pre_g_ref[...]).astype(BF16)
    d_ff = w_up_ref.shape[1]
    acc = jnp.zeros(h_ref.shape, F32)
    for c in range(d_ff // MLP_FF_CHUNK):
        cols = slice(c * MLP_FF_CHUNK, (c + 1) * MLP_FF_CHUNK)
        f = jnp.maximum(jnp.dot(vb, w_up_ref[:, cols], preferred_element_type=F32), 0.0)
        yield
        acc = acc + jnp.dot((f * f).astype(BF16), w_down_ref[cols, :], preferred_element_type=F32)
        yield
    return h_ref[...] + _rms(acc, post_g_ref[...])


def _run(gen):
    while True:
        try:
            next(gen)
        except StopIteration as stop:
            return stop.value


def _interleave(primary, secondary, lead=0, per=1):
    results = [None, None]

    def advance(idx, gen):
        if results[idx] is None:
            try:
                return next(gen)
            except StopIteration as stop:
                results[idx] = (stop.value,)
        return 0

    for _ in range(lead):
        advance(1, secondary)
    while results[0] is None:
        for _ in range(advance(0, primary) * per):
            advance(1, secondary)
    while results[1] is None:
        advance(1, secondary)
    return results[0][0], results[1][0]


def _weight_copy_jobs(hbm, vmem, src_row_starts=None):
    rows, cols = vmem.shape
    jobs = []
    for cb in range(cols // WEIGHT_BLOCK):
        for rb in range(rows // WEIGHT_BLOCK):
            dst = (vmem, rb * WEIGHT_BLOCK, cb * WEIGHT_BLOCK)
            if src_row_starts is None:
                jobs.append((hbm, rb * WEIGHT_BLOCK, cb * WEIGHT_BLOCK) + dst + (False,))
            else:
                jobs.append((hbm, src_row_starts[cb], rb * WEIGHT_BLOCK) + dst + (True,))
    return jobs


def _load_weights(jobs, stage_ref, sem_ref):
    def copy(i):
        src, r0, c0 = jobs[i][:3]
        slot = i % 2
        return pltpu.make_async_copy(
            src.at[pl.ds(r0, WEIGHT_BLOCK), pl.ds(c0, WEIGHT_BLOCK)], stage_ref.at[slot], sem_ref.at[slot])

    copy(0).start()
    for i in range(len(jobs)):
        if i + 1 < len(jobs):
            copy(i + 1).start()
        copy(i).wait()
        dst, r0, c0, transposed = jobs[i][3:]
        block = stage_ref[i % 2]
        dst[r0:r0 + WEIGHT_BLOCK, c0:c0 + WEIGHT_BLOCK] = (block.T if transposed else block).astype(BF16)
        yield


def _layer_kernel(x_ref, *refs, tiles_per_seq, n_tiles, n_small, w_in_row_starts, **dims):
    (pre_g, w_if, b_if, conv_w, conv_b, w_gate, b_r, b_i, lam, rg_gain, ml_gain, post_g,
     mlp_pre_g, mlp_post_g) = refs[:n_small]
    w_in_t_hbm, w_out_hbm, w_up_hbm, w_down_hbm = refs[n_small:n_small + 4]
    out_ref = refs[n_small + 4]
    w_main, w_out, w_up, w_down, stage_ref, sem_ref, h1_ref = refs[n_small + 5:n_small + 12]
    state = refs[n_small + 12:]
    s = pl.program_id(0)

    @pl.when(s == 0)
    def _():
        h1_ref[...] = jnp.zeros_like(h1_ref)
        _run(_load_weights(_weight_copy_jobs(w_in_t_hbm, w_main, w_in_row_starts)
                           + _weight_copy_jobs(w_out_hbm, w_out) + _weight_copy_jobs(w_up_hbm, w_up)
                           + _weight_copy_jobs(w_down_hbm, w_down), stage_ref, sem_ref))

    h1_new, out = _interleave(
        _mixer_tile(x_ref[0], s % tiles_per_seq == 0, pre_g, w_main, w_if, b_if, conv_w, conv_b,
                    w_gate, b_r, b_i, lam, rg_gain, ml_gain, w_out, post_g, *state, **dims),
        _mlp_tile(h1_ref, mlp_pre_g, w_up, w_down, mlp_post_g),
        MLP_PHASES_AFTER["start"])
    out_ref[0] = out
    h1_ref[...] = h1_new


def _const_spec(shape):
    zeros = (0,) * len(shape)
    return pl.BlockSpec(shape, lambda *_: zeros, pipeline_mode=pl.Buffered(1))


def _block_diag_gate(w):
    nb, bd, _ = w.shape
    per = MXU_WIDTH // bd
    w = w.reshape(nb // per, per, bd, bd)
    eye = jnp.eye(per, dtype=w.dtype)
    return jnp.einsum("gpij,pq->gpiqj", w, eye).reshape(nb // per, MXU_WIDTH, MXU_WIDTH)


def _layer(x, pre_gain, w_in, rg_conv_w, rg_conv_b, gate_r_w, gate_r_b, gate_i_w, gate_i_b, lam,
           ml_conv_w, ml_conv_b, igate_b, fgate_b, rg_gain, ml_gain, w_out, post_gain,
           pre_mlp_gain, w_up, w_down, post_mlp_gain):
    bsz, seq, d_model = x.shape
    assert w_up.shape[1] % MLP_FF_CHUNK == 0
    d_rg = lam.shape[0]
    d_ml = ml_gain.shape[0]
    head_dim = d_ml // ML_HEADS
    assert head_dim == LANES and ML_CHUNK == LANES and 2 * ML_HEADS == SUBLANES
    assert seq % SEQ_TILE == 0 and SEQ_TILE % ML_CHUNK == 0 and d_rg % MXU_WIDTH == 0
    d_conv = d_rg + 2 * d_ml

    assert d_rg == WEIGHT_BLOCK and d_ml == WEIGHT_BLOCK
    starts = dict(rg_x=0, rg_gate=d_rg, q=2 * d_rg, k=2 * d_rg + d_ml, v=2 * d_rg + 2 * d_ml, o=2 * d_rg + 3 * d_ml)
    w_in_row_starts = tuple(starts[n] for n in ("rg_x", "q", "k", "rg_gate", "v", "o"))
    d_main = 2 * d_rg + 4 * d_ml
    if_w = w_in[:, d_main:]
    w_if = jnp.pad(if_w, ((0, 0), (0, LANES - 2 * ML_HEADS))).astype(BF16)
    b_if = jnp.pad(jnp.concatenate([igate_b, fgate_b]), (0, LANES - 2 * ML_HEADS)).reshape(1, LANES)
    conv_w = jnp.concatenate([rg_conv_w, ml_conv_w], axis=1)
    conv_b = jnp.concatenate([rg_conv_b, ml_conv_b]).reshape(1, d_conv)
    w_gate = jnp.concatenate([_block_diag_gate(gate_r_w), _block_diag_gate(gate_i_w)], axis=2).astype(BF16)

    row = lambda v: v.reshape(1, -1)
    small = (row(pre_gain), w_if, b_if, conv_w, conv_b, w_gate, row(gate_r_b), row(gate_i_b), row(lam),
             row(rg_gain), row(ml_gain), row(post_gain), row(pre_mlp_gain), row(post_mlp_gain))
    big = (jnp.swapaxes(w_in, 0, 1), w_out, w_up, w_down)
    d_ff = w_up.shape[1]
    for w in (w_out, w_up, w_down):
        assert w.shape[0] % WEIGHT_BLOCK == 0 and w.shape[1] % WEIGHT_BLOCK == 0

    tiles_per_seq = seq // SEQ_TILE
    n_tiles = bsz * tiles_per_seq

    def tile_block(t):
        return (t // tiles_per_seq, t % tiles_per_seq, 0)

    x_spec = pl.BlockSpec((1, SEQ_TILE, d_model), lambda s: tile_block(jnp.minimum(s, n_tiles - 1)))
    out_spec = pl.BlockSpec((1, SEQ_TILE, d_model), lambda s: tile_block(jnp.maximum(s - 1, 0)))
    in_specs = ([x_spec] + [_const_spec(op.shape) for op in small]
                + [pl.BlockSpec(memory_space=pl.ANY)] * len(big))
    kern = functools.partial(_layer_kernel, tiles_per_seq=tiles_per_seq, n_tiles=n_tiles, n_small=len(small),
                             w_in_row_starts=w_in_row_starts, d_rg=d_rg, d_ml=d_ml, head_dim=head_dim)
    return pl.pallas_call(
        kern,
        grid=(n_tiles + 1,),
        in_specs=in_specs,
        out_specs=out_spec,
        out_shape=jax.ShapeDtypeStruct(x.shape, x.dtype),
        scratch_shapes=[
            pltpu.VMEM((d_model, d_main), BF16),
            pltpu.VMEM((d_rg + d_ml, d_model), BF16),
            pltpu.VMEM((d_model, d_ff), BF16),
            pltpu.VMEM((d_ff, d_model), BF16),
            pltpu.VMEM((2, WEIGHT_BLOCK, WEIGHT_BLOCK), F32),
            pltpu.SemaphoreType.DMA((2,)),
            pltpu.VMEM((SEQ_TILE, d_model), F32),
            pltpu.VMEM(((d_conv + d_rg) // LANES, SUBLANES + SEQ_TILE, LANES), F32),
            pltpu.VMEM((2 * d_ml // LANES, SEQ_TILE, LANES), F32),
            pltpu.VMEM((d_rg // LANES, SEQ_TILE, LANES), F32),
            pltpu.VMEM((SUBLANES, d_rg), F32),
            pltpu.VMEM((ML_HEADS, head_dim, 2 * head_dim), F32),
            pltpu.VMEM((SUBLANES, ML_CHUNK), F32),
            pltpu.VMEM((7, SUBLANES, ML_CHUNK), F32),
            pltpu.VMEM((SEQ_TILE, d_rg + d_ml), BF16),
        ],
        compiler_params=pltpu.CompilerParams(
            dimension_semantics=("arbitrary",), vmem_limit_bytes=VMEM_LIMIT_BYTES),
        name="layer",
    )(x, *small, *big)


def kernel(x, pre_mix_gain, w_in, rg_conv_w, rg_conv_b, rg_gate_r_w, rg_gate_r_b, rg_gate_i_w, rg_gate_i_b, rg_lambda, ml_conv_w, ml_conv_b, ml_igate_b, ml_fgate_b, rg_out_gain, ml_out_gain, w_out, post_mix_gain, pre_mlp_gain, mlp_w_up, mlp_w_down, post_mlp_gain):
    h = x
    for l in range(w_in.shape[0]):
        h = _layer(h, pre_mix_gain[l], w_in[l], rg_conv_w[l], rg_conv_b[l], rg_gate_r_w[l], rg_gate_r_b[l],
                   rg_gate_i_w[l], rg_gate_i_b[l], rg_lambda[l], ml_conv_w[l], ml_conv_b[l], ml_igate_b[l],
                   ml_fgate_b[l], rg_out_gain[l], ml_out_gain[l], w_out[l], post_mix_gain[l],
                   pre_mlp_gain[l], mlp_w_up[l], mlp_w_down[l], post_mlp_gain[l])
    return h
```

```python
import functools

import jax
import jax.numpy as jnp
from jax.experimental import pallas as pl
from jax.experimental.pallas import tpu as pltpu

F32 = jnp.float32
BF16 = jnp.bfloat16

RG_BLOCKS = 8
RG_C = 8.0
ML_HEADS = 4
CONV_WIDTH = 4
EPS = 1e-6

SUBLANES = 8
LANES = 128
MXU_WIDTH = 256

SEQ_TILE = 512
ML_CHUNK = LANES
PHASES = 4
MLP_FF_CHUNK = 2048
WEIGHT_BLOCK = 512
MLP_PHASES_AFTER = {"start": 0, "w_in": 1, "conv": (0, 0, 1, 0, 0, 0), "rg_gates": 0, "rg_scan": (1, 0, 0),
                    "ml_stage": (1, 0, 0, 0), "w_out": 0}
VMEM_LIMIT_BYTES = 60 * 1024 * 1024


def _rms(x, gain):
    return x * jax.lax.rsqrt(jnp.mean(x * x, axis=-1, keepdims=True) + EPS) * gain


LOG2E = 1.4426950408889634


def _sigmoid(x):
    return 1.0 / (1.0 + jnp.exp2(x * (-LOG2E)))


def _softplus(x):
    return jnp.maximum(x, 0.0) + jnp.log1p(jnp.exp(-jnp.abs(x)))


def _gelu_tanh(x):
    c = 0.7978845608028654
    return 0.5 * x * (1.0 + jnp.tanh(c * (x + 0.044715 * (x * x * x))))


def _lane_scan(x, op, fill):
    n = x.shape[1]
    lane = jax.lax.broadcasted_iota(jnp.int32, x.shape, 1)
    d = 1
    while d < n:
        shifted = jnp.where(lane < d, fill, pltpu.roll(x, d, axis=1))
        x = op(x, shifted)
        d *= 2
    return x


def _affine_row_scan(sa, sb):
    n, width = sa.shape
    row8 = jax.lax.broadcasted_iota(jnp.int32, (SUBLANES, width), 0)
    d = 1
    while d < n:
        if d < SUBLANES:
            ra = pltpu.roll(sa, d, axis=0)
            rb = pltpu.roll(sb, d, axis=0)
            a_sh = jnp.concatenate([jnp.where(row8 < d, 1.0, ra[:SUBLANES]), ra[SUBLANES:]], axis=0)
            b_sh = jnp.concatenate([jnp.where(row8 < d, 0.0, rb[:SUBLANES]), rb[SUBLANES:]], axis=0)
            sb = sb + sa * b_sh
            sa = sa * a_sh
        else:
            sb = jnp.concatenate([sb[:d], sb[d:] + sa[d:] * sb[:n - d]], axis=0)
            sa = jnp.concatenate([sa[:d], sa[d:] * sa[:n - d]], axis=0)
        d *= 2
    return sa, sb


def _mixer_tile(x, seq_start, pre_g_ref, w_main_ref, w_if_ref, b_if_ref, conv_w_ref, conv_b_ref, w_gate_ref,
                b_r_ref, b_i_ref, lam_ref, rg_gain_ref, ml_gain_ref, w_out_ref, post_g_ref,
                cbuf_ref, qk_ref, yrg_ref, h_ref, c_ref, m_ref, row_ref, y_ref,
                *, d_rg, d_ml, head_dim):
    ts = x.shape[0]
    d_conv = d_rg + 2 * d_ml
    n_chunks = ts // ML_CHUNK
    L = ML_CHUNK

    @pl.when(seq_start)
    def _():
        cbuf_ref[:, ts:ts + SUBLANES, :] = jnp.zeros((cbuf_ref.shape[0], SUBLANES, LANES), F32)
        h_ref[...] = jnp.zeros_like(h_ref)
        c_ref[...] = jnp.zeros_like(c_ref)
        m_ref[...] = jnp.zeros_like(m_ref)

    ub = _rms(x, pre_g_ref[...]).astype(BF16)
    proj = jnp.dot(ub, w_main_ref[...], preferred_element_type=F32)
    gates = jnp.dot(ub, w_if_ref[...], preferred_element_type=F32) + b_if_ref[...]
    yield MLP_PHASES_AFTER["w_in"]

    n_cs = d_conv // LANES
    n_rs = d_rg // LANES
    G = ts // PHASES
    cbuf_ref[:n_cs, :SUBLANES, :] = cbuf_ref[:n_cs, ts:ts + SUBLANES, :]
    for j in range(n_cs + n_rs):
        cbuf_ref[j, SUBLANES:, :] = proj[:, j * LANES:(j + 1) * LANES]

    def phase(j, e):
        return cbuf_ref[j, pl.ds(SUBLANES + e, G, stride=PHASES), :]

    conv_ph = [[None] * n_cs for _ in range(PHASES)]
    for j in range(n_cs):
        cols = slice(j * LANES, (j + 1) * LANES)
        taps = {e: phase(j, e) for e in range(1 - CONV_WIDTH, PHASES)}
        for r in range(PHASES):
            acc = taps[r] * conv_w_ref[CONV_WIDTH - 1:CONV_WIDTH, cols] + conv_b_ref[:, cols]
            for k in range(1, CONV_WIDTH):
                acc = acc + taps[r - k] * conv_w_ref[CONV_WIDTH - 1 - k:CONV_WIDTH - k, cols]
            conv_ph[r][j] = acc
        if j >= n_rs:
            scale = head_dim ** -0.5 if j < n_rs + d_ml // LANES else 1.0
            for r in range(PHASES):
                val = conv_ph[r][j]
                val = val * _sigmoid(val)
                qk_ref[j - n_rs, pl.ds(r, G, stride=PHASES), :] = val * scale if scale != 1.0 else val
        if j % 2 == 1:
            yield MLP_PHASES_AFTER["conv"][j // 2]

    xc = jnp.concatenate([jnp.concatenate(conv_ph[r][:n_rs], axis=1) for r in range(PHASES)], axis=0)
    r_parts, i_parts = [], []
    for g in range(d_rg // MXU_WIDTH):
        gg = jnp.dot(xc[:, g * MXU_WIDTH:(g + 1) * MXU_WIDTH].astype(BF16), w_gate_ref[g],
                     preferred_element_type=F32)
        r_parts.append(gg[:, :MXU_WIDTH])
        i_parts.append(gg[:, MXU_WIDTH:])
    r = _sigmoid(jnp.concatenate(r_parts, axis=1) + b_r_ref[...])
    i_gate = _sigmoid(jnp.concatenate(i_parts, axis=1) + b_i_ref[...])
    neg_log_a_unit = RG_C * _softplus(-lam_ref[...])
    a = jnp.exp2(r * (neg_log_a_unit * (-LOG2E)))
    z = jnp.tanh(r * neg_log_a_unit) * (a * a + 1.0)
    b_in = jnp.where(z > 0.0, z * jax.lax.rsqrt(z), 0.0) * (i_gate * xc)
    yield MLP_PHASES_AFTER["rg_gates"]

    comp_a, comp_b = [a[:G]], [b_in[:G]]
    for r in range(1, PHASES):
        a_r, b_r = a[r * G:(r + 1) * G], b_in[r * G:(r + 1) * G]
        comp_b.append(a_r * comp_b[-1] + b_r)
        comp_a.append(a_r * comp_a[-1])
    ga, gb = _affine_row_scan(comp_a[-1], comp_b[-1])
    carry = h_ref[SUBLANES - 1:SUBLANES, :]
    h_end = ga * carry + gb
    h_ref[...] = h_end[G - SUBLANES:]
    rolled = pltpu.roll(h_end, 1, axis=0)
    row8r = jax.lax.broadcasted_iota(jnp.int32, (SUBLANES, d_rg), 0)
    h_prev = jnp.concatenate([jnp.where(row8r < 1, carry, rolled[:SUBLANES]), rolled[SUBLANES:]], axis=0)
    yield MLP_PHASES_AFTER["rg_scan"][0]
    for r in range(PHASES):
        h_r = comp_a[r] * h_prev + comp_b[r]
        gate_r = jnp.concatenate([phase(n_cs + j, r) for j in range(n_rs)], axis=1)
        y_r = _rms(h_r * _gelu_tanh(gate_r), rg_gain_ref[...])
        for j in range(n_rs):
            yrg_ref[j, pl.ds(r, G, stride=PHASES), :] = y_r[:, j * LANES:(j + 1) * LANES]
        if r % 2 == 1:
            yield MLP_PHASES_AFTER["rg_scan"][1 + r // 2]
    for j in range(n_rs):
        y_ref[:, j * LANES:(j + 1) * LANES] = yrg_ref[j].astype(BF16)

    v_all = proj[:, d_conv + d_rg:d_conv + d_rg + d_ml]
    o_all = proj[:, d_conv + d_rg + d_ml:]

    lane_g = jax.lax.broadcasted_iota(jnp.int32, gates.shape, 1)
    log_sig = jnp.minimum(gates, 0.0) - jnp.log1p(jnp.exp(-jnp.abs(gates)))
    gates_t = jnp.where(lane_g < ML_HEADS, gates, log_sig).T[:SUBLANES]

    causal = (jax.lax.broadcasted_iota(jnp.int32, (L, L), 1)
              <= jax.lax.broadcasted_iota(jnp.int32, (L, L), 0))
    ones_ext = jnp.ones((L, head_dim), BF16)

    for c in range(n_chunks):
        rows = slice(c * L, (c + 1) * L)
        li = gates_t[:, rows]
        lf = pltpu.roll(li, ML_HEADS, axis=0)
        bcum = _lane_scan(lf, jnp.add, 0.0)
        b_last = jnp.sum(lf, axis=1, keepdims=True)
        row_b = li - bcum
        cmax = _lane_scan(row_b, jnp.maximum, -jnp.inf)
        w_loc = b_last + row_b
        m_loc = jnp.max(w_loc, axis=1, keepdims=True)
        m_prev_b = m_ref[...]
        m_prev = jnp.max(m_prev_b, axis=1, keepdims=True)
        m_s = jnp.maximum(bcum + m_prev, bcum + cmax)
        m_new = jnp.maximum(b_last + m_prev, m_loc)
        row_ref[0] = bcum - m_s
        row_ref[1] = m_s
        row_ref[2] = row_b
        row_ref[3] = jnp.exp(w_loc - m_loc)
        row_ref[4] = jnp.broadcast_to(jnp.exp(b_last + m_prev - m_new), (SUBLANES, L))
        row_ref[5] = jnp.broadcast_to(jnp.exp(m_loc - m_new), (SUBLANES, L))
        row_ref[6] = m_prev_b
        m_ref[...] = jnp.broadcast_to(m_new, (SUBLANES, L))

        heads = range(ML_HEADS)
        q_f = [qk_ref[h, rows, :] for h in heads]
        k_t = [qk_ref[ML_HEADS + h, rows, :].T for h in heads]
        s = [jnp.dot(q_f[h].astype(BF16), k_t[h].astype(BF16), preferred_element_type=F32) for h in heads]
        yield MLP_PHASES_AFTER["ml_stage"][0]

        lhs, rhs, v_ext, ms_b = [], [], [], []
        for h in heads:
            cols = slice(h * head_dim, (h + 1) * head_dim)
            ca_b = jnp.broadcast_to(row_ref[0, h:h + 1, :], (head_dim, L)).T
            ms_b.append(jnp.broadcast_to(row_ref[1, h:h + 1, :], (head_dim, L)).T)
            p = jnp.exp(jnp.where(causal, ca_b + row_ref[2, h:h + 1, :], -jnp.inf))
            inter_w = jnp.exp(ca_b + row_ref[6, h:h + 1, :])
            v_ext.append(jnp.concatenate([v_all[rows, cols].astype(BF16), ones_ext], axis=1))
            lhs.append(jnp.concatenate([(s[h] * p).astype(BF16), (q_f[h] * inter_w).astype(BF16)], axis=1))
            rhs.append(jnp.concatenate([v_ext[h], c_ref[h].astype(BF16)], axis=0))
        yield MLP_PHASES_AFTER["ml_stage"][1]

        nd = [jnp.dot(lhs[h], rhs[h], preferred_element_type=F32) for h in heads]
        upd = [jnp.dot((k_t[h] * row_ref[3, h:h + 1, :]).astype(BF16), v_ext[h], preferred_element_type=F32)
               for h in heads]
        yield MLP_PHASES_AFTER["ml_stage"][2]

        for h in heads:
            cols = slice(h * head_dim, (h + 1) * head_dim)
            hh = nd[h][:, :head_dim] / jnp.maximum(jnp.abs(nd[h][:, head_dim:]), jnp.exp(-ms_b[h]))
            hh = _sigmoid(o_all[rows, cols]) * hh
            y_ref[rows, d_rg + h * head_dim:d_rg + (h + 1) * head_dim] = _rms(
                hh, ml_gain_ref[:, cols]).astype(BF16)
            s_old = row_ref[4, h:h + 1, :]
            s_new = row_ref[5, h:h + 1, :]
            c_ref[h] = (jnp.concatenate([s_old, s_old], axis=1) * c_ref[h]
                        + jnp.concatenate([s_new, s_new], axis=1) * upd[h])
        yield MLP_PHASES_AFTER["ml_stage"][3]

    mix = jnp.dot(y_ref[...], w_out_ref[...], preferred_element_type=F32)
    yield MLP_PHASES_AFTER["w_out"]
    return x + _rms(mix, post_g_ref[...])


def _mlp_tile(h_ref, pre_g_ref, w_up_ref, w_down_ref, post_g_ref):
    vb = _rms(h_ref[...], pre_g_ref[...]).astype(BF16)
    d_ff = w_up_ref.shape[1]
    acc = jnp.zeros(h_ref.shape, F32)
    for c in range(d_ff // MLP_FF_CHUNK):
        cols = slice(c * MLP_FF_CHUNK, (c + 1) * MLP_FF_CHUNK)
        f = jnp.maximum(jnp.dot(vb, w_up_ref[:, cols], preferred_element_type=F32), 0.0)
        yield
        acc = acc + jnp.dot((f * f).astype(BF16), w_down_ref[cols, :], preferred_element_type=F32)
        yield
    return h_ref[...] + _rms(acc, post_g_ref[...])


def _run(gen):
    while True:
        try:
            next(gen)
        except StopIteration as stop:
            return stop.value


def _interleave(primary, secondary, lead=0, per=1):
    results = [None, None]

    def advance(idx, gen):
        if results[idx] is None:
            try:
                return next(gen)
            except StopIteration as stop:
                results[idx] = (stop.value,)
        return 0

    for _ in range(lead):
        advance(1, secondary)
    while results[0] is None:
        for _ in range(advance(0, primary) * per):
            advance(1, secondary)
    while results[1] is None:
        advance(1, secondary)
    return results[0][0], results[1][0]


def _weight_copy_jobs(hbm, vmem, src_row_starts=None):
    rows, cols = vmem.shape
    jobs = []
    for cb in range(cols // WEIGHT_BLOCK):
        for rb in range(rows // WEIGHT_BLOCK):
            dst = (vmem, rb * WEIGHT_BLOCK, cb * WEIGHT_BLOCK)
            if src_row_starts is None:
                jobs.append((hbm, rb * WEIGHT_BLOCK, cb * WEIGHT_BLOCK) + dst + (False,))
            else:
                jobs.append((hbm, src_row_starts[cb], rb * WEIGHT_BLOCK) + dst + (True,))
    return jobs


def _load_weights(jobs, stage_ref, sem_ref):
    def copy(i):
        src, r0, c0 = jobs[i][:3]
        slot = i % 2
        return pltpu.make_async_copy(
            src.at[pl.ds(r0, WEIGHT_BLOCK), pl.ds(c0, WEIGHT_BLOCK)], stage_ref.at[slot], sem_ref.at[slot])

    copy(0).start()
    for i in range(len(jobs)):
        if i + 1 < len(jobs):
            copy(i + 1).start()
        copy(i).wait()
        dst, r0, c0, transposed = jobs[i][3:]
        block = stage_ref[i % 2]
        dst[r0:r0 + WEIGHT_BLOCK, c0:c0 + WEIGHT_BLOCK] = (block.T if transposed else block).astype(BF16)
        yield


def _layer_kernel(x_ref, *refs, tiles_per_seq, n_tiles, n_small, w_in_row_starts, **dims):
    (pre_g, w_if, b_if, conv_w, conv_b, w_gate, b_r, b_i, lam, rg_gain, ml_gain, post_g,
     mlp_pre_g, mlp_post_g) = refs[:n_small]
    w_in_t_hbm, w_out_hbm, w_up_hbm, w_down_hbm = refs[n_small:n_small + 4]
    out_ref = refs[n_small + 4]
    w_main, w_out, w_up, w_down, stage_ref, sem_ref, h1_ref = refs[n_small + 5:n_small + 12]
    state = refs[n_small + 12:]
    s = pl.program_id(0)

    @pl.when(s == 0)
    def _():
        h1_ref[...] = jnp.zeros_like(h1_ref)
        _run(_load_weights(_weight_copy_jobs(w_in_t_hbm, w_main, w_in_row_starts)
                           + _weight_copy_jobs(w_out_hbm, w_out) + _weight_copy_jobs(w_up_hbm, w_up)
                           + _weight_copy_jobs(w_down_hbm, w_down), stage_ref, sem_ref))

    h1_new, out = _interleave(
        _mixer_tile(x_ref[0], s % tiles_per_seq == 0, pre_g, w_main, w_if, b_if, conv_w, conv_b,
                    w_gate, b_r, b_i, lam, rg_gain, ml_gain, w_out, post_g, *state, **dims),
        _mlp_tile(h1_ref, mlp_pre_g, w_up, w_down, mlp_post_g),
        MLP_PHASES_AFTER["start"])
    out_ref[0] = out
    h1_ref[...] = h1_new


def _const_spec(shape):
    zeros = (0,) * len(shape)
    return pl.BlockSpec(shape, lambda *_: zeros, pipeline_mode=pl.Buffered(1))


def _block_diag_gate(w):
    nb, bd, _ = w.shape
    per = MXU_WIDTH // bd
    w = w.reshape(nb // per, per, bd, bd)
    eye = jnp.eye(per, dtype=w.dtype)
    return jnp.einsum("gpij,pq->gpiqj", w, eye).reshape(nb // per, MXU_WIDTH, MXU_WIDTH)


def _layer(x, pre_gain, w_in, rg_conv_w, rg_conv_b, gate_r_w, gate_r_b, gate_i_w, gate_i_b, lam,
           ml_conv_w, ml_conv_b, igate_b, fgate_b, rg_gain, ml_gain, w_out, post_gain,
           pre_mlp_gain, w_up, w_down, post_mlp_gain):
    bsz, seq, d_model = x.shape
    assert w_up.shape[1] % MLP_FF_CHUNK == 0
    d_rg = lam.shape[0]
    d_ml = ml_gain.shape[0]
    head_dim = d_ml // ML_HEADS
    assert head_dim == LANES and ML_CHUNK == LANES and 2 * ML_HEADS == SUBLANES
    assert seq % SEQ_TILE == 0 and SEQ_TILE % ML_CHUNK == 0 and d_rg % MXU_WIDTH == 0
    d_conv = d_rg + 2 * d_ml

    assert d_rg == WEIGHT_BLOCK and d_ml == WEIGHT_BLOCK
    starts = dict(rg_x=0, rg_gate=d_rg, q=2 * d_rg, k=2 * d_rg + d_ml, v=2 * d_rg + 2 * d_ml, o=2 * d_rg + 3 * d_ml)
    w_in_row_starts = tuple(starts[n] for n in ("rg_x", "q", "k", "rg_gate", "v", "o"))
    d_main = 2 * d_rg + 4 * d_ml
    if_w = w_in[:, d_main:]
    w_if = jnp.pad(if_w, ((0, 0), (0, LANES - 2 * ML_HEADS))).astype(BF16)
    b_if = jnp.pad(jnp.concatenate([igate_b, fgate_b]), (0, LANES - 2 * ML_HEADS)).reshape(1, LANES)
    conv_w = jnp.concatenate([rg_conv_w, ml_conv_w], axis=1)
    conv_b = jnp.concatenate([rg_conv_b, ml_conv_b]).reshape(1, d_conv)
    w_gate = jnp.concatenate([_block_diag_gate(gate_r_w), _block_diag_gate(gate_i_w)], axis=2).astype(BF16)

    row = lambda v: v.reshape(1, -1)
    small = (row(pre_gain), w_if, b_if, conv_w, conv_b, w_gate, row(gate_r_b), row(gate_i_b), row(lam),
             row(rg_gain), row(ml_gain), row(post_gain), row(pre_mlp_gain), row(post_mlp_gain))
    big = (jnp.swapaxes(w_in, 0, 1), w_out, w_up, w_down)
    d_ff = w_up.shape[1]
    for w in (w_out, w_up, w_down):
        assert w.shape[0] % WEIGHT_BLOCK == 0 and w.shape[1] % WEIGHT_BLOCK == 0

    tiles_per_seq = seq // SEQ_TILE
    n_tiles = bsz * tiles_per_seq

    def tile_block(t):
        return (t // tiles_per_seq, t % tiles_per_seq, 0)

    x_spec = pl.BlockSpec((1, SEQ_TILE, d_model), lambda s: tile_block(jnp.minimum(s, n_tiles - 1)))
    out_spec = pl.BlockSpec((1, SEQ_TILE, d_model), lambda s: tile_block(jnp.maximum(s - 1, 0)))
    in_specs = ([x_spec] + [_const_spec(op.shape) for op in small]
                + [pl.BlockSpec(memory_space=pl.ANY)] * len(big))
    kern = functools.partial(_layer_kernel, tiles_per_seq=tiles_per_seq, n_tiles=n_tiles, n_small=len(small),
                             w_in_row_starts=w_in_row_starts, d_rg=d_rg, d_ml=d_ml, head_dim=head_dim)
    return pl.pallas_call(
        kern,
        grid=(n_tiles + 1,),
        in_specs=in_specs,
        out_specs=out_spec,
        out_shape=jax.ShapeDtypeStruct(x.shape, x.dtype),
        scratch_shapes=[
            pltpu.VMEM((d_model, d_main), BF16),
            pltpu.VMEM((d_rg + d_ml, d_model), BF16),
            pltpu.VMEM((d_model, d_ff), BF16),
            pltpu.VMEM((d_ff, d_model), BF16),
            pltpu.VMEM((2, WEIGHT_BLOCK, WEIGHT_BLOCK), F32),
            pltpu.SemaphoreType.DMA((2,)),
            pltpu.VMEM((SEQ_TILE, d_model), F32),
            pltpu.VMEM(((d_conv + d_rg) // LANES, SUBLANES + SEQ_TILE, LANES), F32),
            pltpu.VMEM((2 * d_ml // LANES, SEQ_TILE, LANES), F32),
            pltpu.VMEM((d_rg // LANES, SEQ_TILE, LANES), F32),
            pltpu.VMEM((SUBLANES, d_rg), F32),
            pltpu.VMEM((ML_HEADS, head_dim, 2 * head_dim), F32),
            pltpu.VMEM((SUBLANES, ML_CHUNK), F32),
            pltpu.VMEM((7, SUBLANES, ML_CHUNK), F32),
            pltpu.VMEM((SEQ_TILE, d_rg + d_ml), BF16),
        ],
        compiler_params=pltpu.CompilerParams(
            dimension_semantics=("arbitrary",), vmem_limit_bytes=VMEM_LIMIT_BYTES),
        name="layer",
    )(x, *small, *big)


def kernel(x, pre_mix_gain, w_in, rg_conv_w, rg_conv_b, rg_gate_r_w, rg_gate_r_b, rg_gate_i_w, rg_gate_i_b, rg_lambda, ml_conv_w, ml_conv_b, ml_igate_b, ml_fgate_b, rg_out_gain, ml_out_gain, w_out, post_mix_gain, pre_mlp_gain, mlp_w_up, mlp_w_down, post_mlp_gain):
    h = x
    for l in range(w_in.shape[0]):
        h = _layer(h, pre_mix_gain[l], w_in[l], rg_conv_w[l], rg_conv_b[l], rg_gate_r_w[l], rg_gate_r_b[l],
                   rg_gate_i_w[l], rg_gate_i_b[l], rg_lambda[l], ml_conv_w[l], ml_conv_b[l], ml_igate_b[l],
                   ml_fgate_b[l], rg_out_gain[l], ml_out_gain[l], w_out[l], post_mix_gain[l],
                   pre_mlp_gain[l], mlp_w_up[l], mlp_w_down[l], post_mlp_gain[l])
    return h
```

```python
import functools

import jax
import jax.numpy as jnp
from jax.experimental import pallas as pl
from jax.experimental.pallas import tpu as pltpu

F32 = jnp.float32
BF16 = jnp.bfloat16

RG_BLOCKS = 8
RG_C = 8.0
ML_HEADS = 4
CONV_WIDTH = 4
EPS = 1e-6

SUBLANES = 8
LANES = 128
MXU_WIDTH = 256

SEQ_TILE = 512
ML_CHUNK = LANES
PHASES = 4
MLP_FF_CHUNK = 1024
WEIGHT_BLOCK = 512
MLP_PHASES_AFTER = {"start": 0, "w_in": 1, "conv": (0, 1, 0, 0, 1, 0), "rg_gates": 0, "rg_scan": (1, 0, 0),
                    "ml_stage": (1, 0, 0, 0), "w_out": 0}
VMEM_LIMIT_BYTES = 60 * 1024 * 1024


def _rms(x, gain):
    return x * jax.lax.rsqrt(jnp.mean(x * x, axis=-1, keepdims=True) + EPS) * gain


LOG2E = 1.4426950408889634


def _sigmoid(x):
    return 1.0 / (1.0 + jnp.exp2(x * (-LOG2E)))


def _softplus(x):
    return jnp.maximum(x, 0.0) + jnp.log1p(jnp.exp(-jnp.abs(x)))


def _gelu_tanh(x):
    c = 0.7978845608028654
    return 0.5 * x * (1.0 + jnp.tanh(c * (x + 0.044715 * (x * x * x))))


def _lane_scan(x, op, fill):
    n = x.shape[1]
    lane = jax.lax.broadcasted_iota(jnp.int32, x.shape, 1)
    d = 1
    while d < n:
        shifted = jnp.where(lane < d, fill, pltpu.roll(x, d, axis=1))
        x = op(x, shifted)
        d *= 2
    return x


def _affine_row_scan(sa, sb):
    n, width = sa.shape
    row8 = jax.lax.broadcasted_iota(jnp.int32, (SUBLANES, width), 0)
    d = 1
    while d < n:
        if d < SUBLANES:
            ra = pltpu.roll(sa, d, axis=0)
            rb = pltpu.roll(sb, d, axis=0)
            a_sh = jnp.concatenate([jnp.where(row8 < d, 1.0, ra[:SUBLANES]), ra[SUBLANES:]], axis=0)
            b_sh = jnp.concatenate([jnp.where(row8 < d, 0.0, rb[:SUBLANES]), rb[SUBLANES:]], axis=0)
            sb = sb + sa * b_sh
            sa = sa * a_sh
        else:
            sb = jnp.concatenate([sb[:d], sb[d:] + sa[d:] * sb[:n - d]], axis=0)
            sa = jnp.concatenate([sa[:d], sa[d:] * sa[:n - d]], axis=0)
        d *= 2
    return sa, sb


def _mixer_tile(x, seq_start, pre_g_ref, w_main_ref, w_if_ref, b_if_ref, conv_w_ref, conv_b_ref, w_gate_ref,
                b_r_ref, b_i_ref, lam_ref, rg_gain_ref, ml_gain_ref, w_out_ref, post_g_ref,
                cbuf_ref, qk_ref, yrg_ref, h_ref, c_ref, m_ref, row_ref, y_ref,
                *, d_rg, d_ml, head_dim):
    ts = x.shape[0]
    d_conv = d_rg + 2 * d_ml
    n_chunks = ts // ML_CHUNK
    L = ML_CHUNK

    @pl.when(seq_start)
    def _():
        cbuf_ref[:, ts:ts + SUBLANES, :] = jnp.zeros((cbuf_ref.shape[0], SUBLANES, LANES), F32)
        h_ref[...] = jnp.zeros_like(h_ref)
        c_ref[...] = jnp.zeros_like(c_ref)
        m_ref[...] = jnp.zeros_like(m_ref)

    ub = _rms(x, pre_g_ref[...]).astype(BF16)
    proj = jnp.dot(ub, w_main_ref[...], preferred_element_type=F32)
    gates = jnp.dot(ub, w_if_ref[...], preferred_element_type=F32) + b_if_ref[...]
    yield MLP_PHASES_AFTER["w_in"]

    n_cs = d_conv // LANES
    n_rs = d_rg // LANES
    G = ts // PHASES
    cbuf_ref[:n_cs, :SUBLANES, :] = cbuf_ref[:n_cs, ts:ts + SUBLANES, :]
    for j in range(n_cs + n_rs):
        cbuf_ref[j, SUBLANES:, :] = proj[:, j * LANES:(j + 1) * LANES]

    def phase(j, e):
        return cbuf_ref[j, pl.ds(SUBLANES + e, G, stride=PHASES), :]

    conv_ph = [[None] * n_cs for _ in range(PHASES)]
    for j in range(n_cs):
        cols = slice(j * LANES, (j + 1) * LANES)
        taps = {e: phase(j, e) for e in range(1 - CONV_WIDTH, PHASES)}
        for r in range(PHASES):
            acc = taps[r] * conv_w_ref[CONV_WIDTH - 1:CONV_WIDTH, cols] + conv_b_ref[:, cols]
            for k in range(1, CONV_WIDTH):
                acc = acc + taps[r - k] * conv_w_ref[CONV_WIDTH - 1 - k:CONV_WIDTH - k, cols]
            conv_ph[r][j] = acc
        if j >= n_rs:
            scale = head_dim ** -0.5 if j < n_rs + d_ml // LANES else 1.0
            for r in range(PHASES):
                val = conv_ph[r][j]
                val = val * _sigmoid(val)
                qk_ref[j - n_rs, pl.ds(r, G, stride=PHASES), :] = val * scale if scale != 1.0 else val
        if j % 2 == 1:
            yield MLP_PHASES_AFTER["conv"][j // 2]

    xc = jnp.concatenate([jnp.concatenate(conv_ph[r][:n_rs], axis=1) for r in range(PHASES)], axis=0)
    r_parts, i_parts = [], []
    for g in range(d_rg // MXU_WIDTH):
        gg = jnp.dot(xc[:, g * MXU_WIDTH:(g + 1) * MXU_WIDTH].astype(BF16), w_gate_ref[g],
                     preferred_element_type=F32)
        r_parts.append(gg[:, :MXU_WIDTH])
        i_parts.append(gg[:, MXU_WIDTH:])
    r = _sigmoid(jnp.concatenate(r_parts, axis=1) + b_r_ref[...])
    i_gate = _sigmoid(jnp.concatenate(i_parts, axis=1) + b_i_ref[...])
    neg_log_a_unit = RG_C * _softplus(-lam_ref[...])
    a = jnp.exp2(r * (neg_log_a_unit * (-LOG2E)))
    z = jnp.tanh(r * neg_log_a_unit) * (a * a + 1.0)
    b_in = jnp.where(z > 0.0, z * jax.lax.rsqrt(z), 0.0) * (i_gate * xc)
    yield MLP_PHASES_AFTER["rg_gates"]

    comp_a, comp_b = [a[:G]], [b_in[:G]]
    for r in range(1, PHASES):
        a_r, b_r = a[r * G:(r + 1) * G], b_in[r * G:(r + 1) * G]
        comp_b.append(a_r * comp_b[-1] + b_r)
        comp_a.append(a_r * comp_a[-1])
    ga, gb = _affine_row_scan(comp_a[-1], comp_b[-1])
    carry = h_ref[SUBLANES - 1:SUBLANES, :]
    h_end = ga * carry + gb
    h_ref[...] = h_end[G - SUBLANES:]
    rolled = pltpu.roll(h_end, 1, axis=0)
    row8r = jax.lax.broadcasted_iota(jnp.int32, (SUBLANES, d_rg), 0)
    h_prev = jnp.concatenate([jnp.where(row8r < 1, carry, rolled[:SUBLANES]), rolled[SUBLANES:]], axis=0)
    yield MLP_PHASES_AFTER["rg_scan"][0]
    for r in range(PHASES):
        h_r = comp_a[r] * h_prev + comp_b[r]
        gate_r = jnp.concatenate([phase(n_cs + j, r) for j in range(n_rs)], axis=1)
        y_r = _rms(h_r * _gelu_tanh(gate_r), rg_gain_ref[...])
        for j in range(n_rs):
            yrg_ref[j, pl.ds(r, G, stride=PHASES), :] = y_r[:, j * LANES:(j + 1) * LANES]
        if r % 2 == 1:
            yield MLP_PHASES_AFTER["rg_scan"][1 + r // 2]
    for j in range(n_rs):
        y_ref[:, j * LANES:(j + 1) * LANES] = yrg_ref[j].astype(BF16)

    v_all = proj[:, d_conv + d_rg:d_conv + d_rg + d_ml]
    o_all = proj[:, d_conv + d_rg + d_ml:]

    lane_g = jax.lax.broadcasted_iota(jnp.int32, gates.shape, 1)
    log_sig = jnp.minimum(gates, 0.0) - jnp.log1p(jnp.exp(-jnp.abs(gates)))
    gates_t = jnp.where(lane_g < ML_HEADS, gates, log_sig).T[:SUBLANES]

    causal = (jax.lax.broadcasted_iota(jnp.int32, (L, L), 1)
              <= jax.lax.broadcasted_iota(jnp.int32, (L, L), 0))
    ones_ext = jnp.ones((L, head_dim), BF16)

    for c in range(n_chunks):
        rows = slice(c * L, (c + 1) * L)
        li = gates_t[:, rows]
        lf = pltpu.roll(li, ML_HEADS, axis=0)
        bcum = _lane_scan(lf, jnp.add, 0.0)
        b_last = jnp.sum(lf, axis=1, keepdims=True)
        row_b = li - bcum
        cmax = _lane_scan(row_b, jnp.maximum, -jnp.inf)
        w_loc = b_last + row_b
        m_loc = jnp.max(w_loc, axis=1, keepdims=True)
        m_prev_b = m_ref[...]
        m_prev = jnp.max(m_prev_b, axis=1, keepdims=True)
        m_s = jnp.maximum(bcum + m_prev, bcum + cmax)
        m_new = jnp.maximum(b_last + m_prev, m_loc)
        row_ref[0] = bcum - m_s
        row_ref[1] = m_s
        row_ref[2] = row_b
        row_ref[3] = jnp.exp(w_loc - m_loc)
        row_ref[4] = jnp.broadcast_to(jnp.exp(b_last + m_prev - m_new), (SUBLANES, L))
        row_ref[5] = jnp.broadcast_to(jnp.exp(m_loc - m_new), (SUBLANES, L))
        row_ref[6] = m_prev_b
        m_ref[...] = jnp.broadcast_to(m_new, (SUBLANES, L))

        heads = range(ML_HEADS)
        q_f = [qk_ref[h, rows, :] for h in heads]
        k_t = [qk_ref[ML_HEADS + h, rows, :].T for h in heads]
        s = [jnp.dot(q_f[h].astype(BF16), k_t[h].astype(BF16), preferred_element_type=F32) for h in heads]
        yield MLP_PHASES_AFTER["ml_stage"][0]

        lhs, rhs, v_ext, ms_b = [], [], [], []
        for h in heads:
            cols = slice(h * head_dim, (h + 1) * head_dim)
            ca_b = jnp.broadcast_to(row_ref[0, h:h + 1, :], (head_dim, L)).T
            ms_b.append(jnp.broadcast_to(row_ref[1, h:h + 1, :], (head_dim, L)).T)
            p = jnp.exp(jnp.where(causal, ca_b + row_ref[2, h:h + 1, :], -jnp.inf))
            inter_w = jnp.exp(ca_b + row_ref[6, h:h + 1, :])
            v_ext.append(jnp.concatenate([v_all[rows, cols].astype(BF16), ones_ext], axis=1))
            lhs.append(jnp.concatenate([(s[h] * p).astype(BF16), (q_f[h] * inter_w).astype(BF16)], axis=1))
            rhs.append(jnp.concatenate([v_ext[h], c_ref[h].astype(BF16)], axis=0))
        yield MLP_PHASES_AFTER["ml_stage"][1]

        nd = [jnp.dot(lhs[h], rhs[h], preferred_element_type=F32) for h in heads]
        upd = [jnp.dot((k_t[h] * row_ref[3, h:h + 1, :]).astype(BF16), v_ext[h], preferred_element_type=F32)
               for h in heads]
        yield MLP_PHASES_AFTER["ml_stage"][2]

        for h in heads:
            cols = slice(h * head_dim, (h + 1) * head_dim)
            hh = nd[h][:, :head_dim] / jnp.maximum(jnp.abs(nd[h][:, head_dim:]), jnp.exp(-ms_b[h]))
            hh = _sigmoid(o_all[rows, cols]) * hh
            y_ref[rows, d_rg + h * head_dim:d_rg + (h + 1) * head_dim] = _rms(
                hh, ml_gain_ref[:, cols]).astype(BF16)
            s_old = row_ref[4, h:h + 1, :]
            s_new = row_ref[5, h:h + 1, :]
            c_ref[h] = (jnp.concatenate([s_old, s_old], axis=1) * c_ref[h]
                        + jnp.concatenate([s_new, s_new], axis=1) * upd[h])
        yield MLP_PHASES_AFTER["ml_stage"][3]

    mix = jnp.dot(y_ref[...], w_out_ref[...], preferred_element_type=F32)
    yield MLP_PHASES_AFTER["w_out"]
    return x + _rms(mix, post_g_ref[...])


def _mlp_tile(h_ref, pre_g_ref, w_up_ref, w_down_ref, post_g_ref):
    vb = _rms(h_ref[...], pre_g_ref[...]).astype(BF16)
    d_ff = w_up_ref.shape[1]
    acc = jnp.zeros(h_ref.shape, F32)
    for c in range(d_ff // MLP_FF_CHUNK):
        cols = slice(c * MLP_FF_CHUNK, (c + 1) * MLP_FF_CHUNK)
        f = jnp.maximum(jnp.dot(vb, w_up_ref[:, cols], preferred_element_type=F32), 0.0)
        yield
        acc = acc + jnp.dot((f * f).astype(BF16), w_down_ref[cols, :], preferred_element_type=F32)
        yield
    return h_ref[...] + _rms(acc, post_g_ref[...])


def _run(gen):
    while True:
        try:
            next(gen)
        except StopIteration as stop:
            return stop.value


def _interleave(primary, secondary, lead=0, per=1):
    results = [None, None]

    def advance(idx, gen):
        if results[idx] is None:
            try:
                return next(gen)
            except StopIteration as stop:
                results[idx] = (stop.value,)
        return 0

    for _ in range(lead):
        advance(1, secondary)
    while results[0] is None:
        for _ in range(advance(0, primary) * per):
            advance(1, secondary)
    while results[1] is None:
        advance(1, secondary)
    return results[0][0], results[1][0]


def _weight_copy_jobs(hbm, vmem, src_row_starts=None):
    rows, cols = vmem.shape
    jobs = []
    for cb in range(cols // WEIGHT_BLOCK):
        for rb in range(rows // WEIGHT_BLOCK):
            dst = (vmem, rb * WEIGHT_BLOCK, cb * WEIGHT_BLOCK)
            if src_row_starts is None:
                jobs.append((hbm, rb * WEIGHT_BLOCK, cb * WEIGHT_BLOCK) + dst + (False,))
            else:
                jobs.append((hbm, src_row_starts[cb], rb * WEIGHT_BLOCK) + dst + (True,))
    return jobs


def _load_weights(jobs, stage_ref, sem_ref):
    def copy(i):
        src, r0, c0 = jobs[i][:3]
        slot = i % 2
        return pltpu.make_async_copy(
            src.at[pl.ds(r0, WEIGHT_BLOCK), pl.ds(c0, WEIGHT_BLOCK)], stage_ref.at[slot], sem_ref.at[slot])

    copy(0).start()
    for i in range(len(jobs)):
        if i + 1 < len(jobs):
            copy(i + 1).start()
        copy(i).wait()
        dst, r0, c0, transposed = jobs[i][3:]
        block = stage_ref[i % 2]
        dst[r0:r0 + WEIGHT_BLOCK, c0:c0 + WEIGHT_BLOCK] = (block.T if transposed else block).astype(BF16)
        yield


def _layer_kernel(x_ref, *refs, tiles_per_seq, n_tiles, n_small, w_in_row_starts, **dims):
    (pre_g, w_if, b_if, conv_w, conv_b, w_gate, b_r, b_i, lam, rg_gain, ml_gain, post_g,
     mlp_pre_g, mlp_post_g) = refs[:n_small]
    w_in_t_hbm, w_out_hbm, w_up_hbm, w_down_hbm = refs[n_small:n_small + 4]
    out_ref = refs[n_small + 4]
    w_main, w_out, w_up, w_down, stage_ref, sem_ref, h1_ref = refs[n_small + 5:n_small + 12]
    state = refs[n_small + 12:]
    s = pl.program_id(0)

    @pl.when(s == 0)
    def _():
        h1_ref[...] = jnp.zeros_like(h1_ref)
        _run(_load_weights(_weight_copy_jobs(w_in_t_hbm, w_main, w_in_row_starts)
                           + _weight_copy_jobs(w_out_hbm, w_out) + _weight_copy_jobs(w_up_hbm, w_up)
                           + _weight_copy_jobs(w_down_hbm, w_down), stage_ref, sem_ref))

    h1_new, out = _interleave(
        _mixer_tile(x_ref[0], s % tiles_per_seq == 0, pre_g, w_main, w_if, b_if, conv_w, conv_b,
                    w_gate, b_r, b_i, lam, rg_gain, ml_gain, w_out, post_g, *state, **dims),
        _mlp_tile(h1_ref, mlp_pre_g, w_up, w_down, mlp_post_g),
        MLP_PHASES_AFTER["start"])
    out_ref[0] = out
    h1_ref[...] = h1_new


def _const_spec(shape):
    zeros = (0,) * len(shape)
    return pl.BlockSpec(shape, lambda *_: zeros, pipeline_mode=pl.Buffered(1))


def _block_diag_gate(w):
    nb, bd, _ = w.shape
    per = MXU_WIDTH // bd
    w = w.reshape(nb // per, per, bd, bd)
    eye = jnp.eye(per, dtype=w.dtype)
    return jnp.einsum("gpij,pq->gpiqj", w, eye).reshape(nb // per, MXU_WIDTH, MXU_WIDTH)


def _layer(x, pre_gain, w_in, rg_conv_w, rg_conv_b, gate_r_w, gate_r_b, gate_i_w, gate_i_b, lam,
           ml_conv_w, ml_conv_b, igate_b, fgate_b, rg_gain, ml_gain, w_out, post_gain,
           pre_mlp_gain, w_up, w_down, post_mlp_gain):
    bsz, seq, d_model = x.shape
    assert w_up.shape[1] % MLP_FF_CHUNK == 0
    d_rg = lam.shape[0]
    d_ml = ml_gain.shape[0]
    head_dim = d_ml // ML_HEADS
    assert head_dim == LANES and ML_CHUNK == LANES and 2 * ML_HEADS == SUBLANES
    assert seq % SEQ_TILE == 0 and SEQ_TILE % ML_CHUNK == 0 and d_rg % MXU_WIDTH == 0
    d_conv = d_rg + 2 * d_ml

    assert d_rg == WEIGHT_BLOCK and d_ml == WEIGHT_BLOCK
    starts = dict(rg_x=0, rg_gate=d_rg, q=2 * d_rg, k=2 * d_rg + d_ml, v=2 * d_rg + 2 * d_ml, o=2 * d_rg + 3 * d_ml)
    w_in_row_starts = tuple(starts[n] for n in ("rg_x", "q", "k", "rg_gate", "v", "o"))
    d_main = 2 * d_rg + 4 * d_ml
    if_w = w_in[:, d_main:]
    w_if = jnp.pad(if_w, ((0, 0), (0, LANES - 2 * ML_HEADS))).astype(BF16)
    b_if = jnp.pad(jnp.concatenate([igate_b, fgate_b]), (0, LANES - 2 * ML_HEADS)).reshape(1, LANES)
    conv_w = jnp.concatenate([rg_conv_w, ml_conv_w], axis=1)
    conv_b = jnp.concatenate([rg_conv_b, ml_conv_b]).reshape(1, d_conv)
    w_gate = jnp.concatenate([_block_diag_gate(gate_r_w), _block_diag_gate(gate_i_w)], axis=2).astype(BF16)

    row = lambda v: v.reshape(1, -1)
    small = (row(pre_gain), w_if, b_if, conv_w, conv_b, w_gate, row(gate_r_b), row(gate_i_b), row(lam),
             row(rg_gain), row(ml_gain), row(post_gain), row(pre_mlp_gain), row(post_mlp_gain))
    big = (jnp.swapaxes(w_in, 0, 1), w_out, w_up, w_down)
    d_ff = w_up.shape[1]
    for w in (w_out, w_up, w_down):
        assert w.shape[0] % WEIGHT_BLOCK == 0 and w.shape[1] % WEIGHT_BLOCK == 0

    tiles_per_seq = seq // SEQ_TILE
    n_tiles = bsz * tiles_per_seq

    def tile_block(t):
        return (t // tiles_per_seq, t % tiles_per_seq, 0)

    x_spec = pl.BlockSpec((1, SEQ_TILE, d_model), lambda s: tile_block(jnp.minimum(s, n_tiles - 1)))
    out_spec = pl.BlockSpec((1, SEQ_TILE, d_model), lambda s: tile_block(jnp.maximum(s - 1, 0)))
    in_specs = ([x_spec] + [_const_spec(op.shape) for op in small]
                + [pl.BlockSpec(memory_space=pl.ANY)] * len(big))
    kern = functools.partial(_layer_kernel, tiles_per_seq=tiles_per_seq, n_tiles=n_tiles, n_small=len(small),
                             w_in_row_starts=w_in_row_starts, d_rg=d_rg, d_ml=d_ml, head_dim=head_dim)
    return pl.pallas_call(
        kern,
        grid=(n_tiles + 1,),
        in_specs=in_specs,
        out_specs=out_spec,
        out_shape=jax.ShapeDtypeStruct(x.shape, x.dtype),
        scratch_shapes=[
            pltpu.VMEM((d_model, d_main), BF16),
            pltpu.VMEM((d_rg + d_ml, d_model), BF16),
            pltpu.VMEM((d_model, d_ff), BF16),
            pltpu.VMEM((d_ff, d_model), BF16),
            pltpu.VMEM((2, WEIGHT_BLOCK, WEIGHT_BLOCK), F32),
            pltpu.SemaphoreType.DMA((2,)),
            pltpu.VMEM((SEQ_TILE, d_model), F32),
            pltpu.VMEM(((d_conv + d_rg) // LANES, SUBLANES + SEQ_TILE, LANES), F32),
            pltpu.VMEM((2 * d_ml // LANES, SEQ_TILE, LANES), F32),
            pltpu.VMEM((d_rg // LANES, SEQ_TILE, LANES), F32),
            pltpu.VMEM((SUBLANES, d_rg), F32),
            pltpu.VMEM((ML_HEADS, head_dim, 2 * head_dim), F32),
            pltpu.VMEM((SUBLANES, ML_CHUNK), F32),
            pltpu.VMEM((7, SUBLANES, ML_CHUNK), F32),
            pltpu.VMEM((SEQ_TILE, d_rg + d_ml), BF16),
        ],
        compiler_params=pltpu.CompilerParams(
            dimension_semantics=("arbitrary",), vmem_limit_bytes=VMEM_LIMIT_BYTES),
        name="layer",
    )(x, *small, *big)


def kernel(x, pre_mix_gain, w_in, rg_conv_w, rg_conv_b, rg_gate_r_w, rg_gate_r_b, rg_gate_i_w, rg_gate_i_b, rg_lambda, ml_conv_w, ml_conv_b, ml_igate_b, ml_fgate_b, rg_out_gain, ml_out_gain, w_out, post_mix_gain, pre_mlp_gain, mlp_w_up, mlp_w_down, post_mlp_gain):
    h = x
    for l in range(w_in.shape[0]):
        h = _layer(h, pre_mix_gain[l], w_in[l], rg_conv_w[l], rg_conv_b[l], rg_gate_r_w[l], rg_gate_r_b[l],
                   rg_gate_i_w[l], rg_gate_i_b[l], rg_lambda[l], ml_conv_w[l], ml_conv_b[l], ml_igate_b[l],
                   ml_fgate_b[l], rg_out_gain[l], ml_out_gain[l], w_out[l], post_mix_gain[l],
                   pre_mlp_gain[l], mlp_w_up[l], mlp_w_down[l], post_mlp_gain[l])
    return h
```

```python
import functools

import jax
import jax.numpy as jnp
from jax.experimental import pallas as pl
from jax.experimental.pallas import tpu as pltpu

F32 = jnp.float32
BF16 = jnp.bfloat16

RG_BLOCKS = 8
RG_C = 8.0
ML_HEADS = 4
CONV_WIDTH = 4
EPS = 1e-6

SUBLANES = 8
LANES = 128
MXU_WIDTH = 256

SEQ_TILE = 512
ML_CHUNK = LANES
PHASES = 4
MLP_FF_CHUNK = 1024
WEIGHT_BLOCK = 512
MLP_PHASES_AFTER = {"start": 0, "w_in": 0, "conv": (0, 1, 0, 0, 1, 0), "rg_gates": 1, "rg_scan": (1, 0, 1),
                    "ml_stage": (1, 0, 0, 0), "w_out": 0}
VMEM_LIMIT_BYTES = 60 * 1024 * 1024


def _rms(x, gain):
    return x * jax.lax.rsqrt(jnp.mean(x * x, axis=-1, keepdims=True) + EPS) * gain


LOG2E = 1.4426950408889634


def _sigmoid(x):
    return 1.0 / (1.0 + jnp.exp2(x * (-LOG2E)))


def _softplus(x):
    return jnp.maximum(x, 0.0) + jnp.log1p(jnp.exp(-jnp.abs(x)))


def _gelu_tanh(x):
    c = 0.7978845608028654
    return 0.5 * x * (1.0 + jnp.tanh(c * (x + 0.044715 * (x * x * x))))


def _lane_scan(x, op, fill):
    n = x.shape[1]
    lane = jax.lax.broadcasted_iota(jnp.int32, x.shape, 1)
    d = 1
    while d < n:
        shifted = jnp.where(lane < d, fill, pltpu.roll(x, d, axis=1))
        x = op(x, shifted)
        d *= 2
    return x


def _affine_row_scan(sa, sb):
    n, width = sa.shape
    row8 = jax.lax.broadcasted_iota(jnp.int32, (SUBLANES, width), 0)
    d = 1
    while d < n:
        if d < SUBLANES:
            ra = pltpu.roll(sa, d, axis=0)
            rb = pltpu.roll(sb, d, axis=0)
            a_sh = jnp.concatenate([jnp.where(row8 < d, 1.0, ra[:SUBLANES]), ra[SUBLANES:]], axis=0)
            b_sh = jnp.concatenate([jnp.where(row8 < d, 0.0, rb[:SUBLANES]), rb[SUBLANES:]], axis=0)
            sb = sb + sa * b_sh
            sa = sa * a_sh
        else:
            sb = jnp.concatenate([sb[:d], sb[d:] + sa[d:] * sb[:n - d]], axis=0)
            sa = jnp.concatenate([sa[:d], sa[d:] * sa[:n - d]], axis=0)
        d *= 2
    return sa, sb


def _mixer_tile(x, seq_start, pre_g_ref, w_main_ref, w_if_ref, b_if_ref, conv_w_ref, conv_b_ref, w_gate_ref,
                b_r_ref, b_i_ref, lam_ref, rg_gain_ref, ml_gain_ref, w_out_ref, post_g_ref,
                cbuf_ref, qk_ref, yrg_ref, h_ref, c_ref, m_ref, row_ref, y_ref,
                *, d_rg, d_ml, head_dim):
    ts = x.shape[0]
    d_conv = d_rg + 2 * d_ml
    n_chunks = ts // ML_CHUNK
    L = ML_CHUNK

    @pl.when(seq_start)
    def _():
        cbuf_ref[:, ts:ts + SUBLANES, :] = jnp.zeros((cbuf_ref.shape[0], SUBLANES, LANES), F32)
        h_ref[...] = jnp.zeros_like(h_ref)
        c_ref[...] = jnp.zeros_like(c_ref)
        m_ref[...] = jnp.zeros_like(m_ref)

    ub = _rms(x, pre_g_ref[...]).astype(BF16)
    proj = jnp.dot(ub, w_main_ref[...], preferred_element_type=F32)
    gates = jnp.dot(ub, w_if_ref[...], preferred_element_type=F32) + b_if_ref[...]
    yield MLP_PHASES_AFTER["w_in"]

    n_cs = d_conv // LANES
    n_rs = d_rg // LANES
    G = ts // PHASES
    cbuf_ref[:n_cs, :SUBLANES, :] = cbuf_ref[:n_cs, ts:ts + SUBLANES, :]
    for j in range(n_cs + n_rs):
        cbuf_ref[j, SUBLANES:, :] = proj[:, j * LANES:(j + 1) * LANES]

    def phase(j, e):
        return cbuf_ref[j, pl.ds(SUBLANES + e, G, stride=PHASES), :]

    conv_ph = [[None] * n_cs for _ in range(PHASES)]
    for j in range(n_cs):
        cols = slice(j * LANES, (j + 1) * LANES)
        taps = {e: phase(j, e) for e in range(1 - CONV_WIDTH, PHASES)}
        for r in range(PHASES):
            acc = taps[r] * conv_w_ref[CONV_WIDTH - 1:CONV_WIDTH, cols] + conv_b_ref[:, cols]
            for k in range(1, CONV_WIDTH):
                acc = acc + taps[r - k] * conv_w_ref[CONV_WIDTH - 1 - k:CONV_WIDTH - k, cols]
            conv_ph[r][j] = acc
        if j >= n_rs:
            scale = head_dim ** -0.5 if j < n_rs + d_ml // LANES else 1.0
            for r in range(PHASES):
                val = conv_ph[r][j]
                val = val * _sigmoid(val)
                qk_ref[j - n_rs, pl.ds(r, G, stride=PHASES), :] = val * scale if scale != 1.0 else val
        if j % 2 == 1:
            yield MLP_PHASES_AFTER["conv"][j // 2]

    xc = jnp.concatenate([jnp.concatenate(conv_ph[r][:n_rs], axis=1) for r in range(PHASES)], axis=0)
    r_parts, i_parts = [], []
    for g in range(d_rg // MXU_WIDTH):
        gg = jnp.dot(xc[:, g * MXU_WIDTH:(g + 1) * MXU_WIDTH].astype(BF16), w_gate_ref[g],
                     preferred_element_type=F32)
        r_parts.append(gg[:, :MXU_WIDTH])
        i_parts.append(gg[:, MXU_WIDTH:])
    r = _sigmoid(jnp.concatenate(r_parts, axis=1) + b_r_ref[...])
    i_gate = _sigmoid(jnp.concatenate(i_parts, axis=1) + b_i_ref[...])
    neg_log_a_unit = RG_C * _softplus(-lam_ref[...])
    a = jnp.exp2(r * (neg_log_a_unit * (-LOG2E)))
    z = jnp.tanh(r * neg_log_a_unit) * (a * a + 1.0)
    b_in = jnp.where(z > 0.0, z * jax.lax.rsqrt(z), 0.0) * (i_gate * xc)
    yield MLP_PHASES_AFTER["rg_gates"]

    comp_a, comp_b = [a[:G]], [b_in[:G]]
    for r in range(1, PHASES):
        a_r, b_r = a[r * G:(r + 1) * G], b_in[r * G:(r + 1) * G]
        comp_b.append(a_r * comp_b[-1] + b_r)
        comp_a.append(a_r * comp_a[-1])
    ga, gb = _affine_row_scan(comp_a[-1], comp_b[-1])
    carry = h_ref[SUBLANES - 1:SUBLANES, :]
    h_end = ga * carry + gb
    h_ref[...] = h_end[G - SUBLANES:]
    rolled = pltpu.roll(h_end, 1, axis=0)
    row8r = jax.lax.broadcasted_iota(jnp.int32, (SUBLANES, d_rg), 0)
    h_prev = jnp.concatenate([jnp.where(row8r < 1, carry, rolled[:SUBLANES]), rolled[SUBLANES:]], axis=0)
    yield MLP_PHASES_AFTER["rg_scan"][0]
    for r in range(PHASES):
        h_r = comp_a[r] * h_prev + comp_b[r]
        gate_r = jnp.concatenate([phase(n_cs + j, r) for j in range(n_rs)], axis=1)
        y_r = _rms(h_r * _gelu_tanh(gate_r), rg_gain_ref[...])
        for j in range(n_rs):
            yrg_ref[j, pl.ds(r, G, stride=PHASES), :] = y_r[:, j * LANES:(j + 1) * LANES]
        if r % 2 == 1:
            yield MLP_PHASES_AFTER["rg_scan"][1 + r // 2]
    for j in range(n_rs):
        y_ref[:, j * LANES:(j + 1) * LANES] = yrg_ref[j].astype(BF16)

    v_all = proj[:, d_conv + d_rg:d_conv + d_rg + d_ml]
    o_all = proj[:, d_conv + d_rg + d_ml:]

    lane_g = jax.lax.broadcasted_iota(jnp.int32, gates.shape, 1)
    log_sig = jnp.minimum(gates, 0.0) - jnp.log1p(jnp.exp(-jnp.abs(gates)))
    gates_t = jnp.where(lane_g < ML_HEADS, gates, log_sig).T[:SUBLANES]

    causal = (jax.lax.broadcasted_iota(jnp.int32, (L, L), 1)
              <= jax.lax.broadcasted_iota(jnp.int32, (L, L), 0))
    ones_ext = jnp.ones((L, head_dim), BF16)

    for c in range(n_chunks):
        rows = slice(c * L, (c + 1) * L)
        li = gates_t[:, rows]
        lf = pltpu.roll(li, ML_HEADS, axis=0)
        bcum = _lane_scan(lf, jnp.add, 0.0)
        b_last = jnp.sum(lf, axis=1, keepdims=True)
        row_b = li - bcum
        cmax = _lane_scan(row_b, jnp.maximum, -jnp.inf)
        w_loc = b_last + row_b
        m_loc = jnp.max(w_loc, axis=1, keepdims=True)
        m_prev_b = m_ref[...]
        m_prev = jnp.max(m_prev_b, axis=1, keepdims=True)
        m_s = jnp.maximum(bcum + m_prev, bcum + cmax)
        m_new = jnp.maximum(b_last + m_prev, m_loc)
        row_ref[0] = bcum - m_s
        row_ref[1] = m_s
        row_ref[2] = row_b
        row_ref[3] = jnp.exp(w_loc - m_loc)
        row_ref[4] = jnp.broadcast_to(jnp.exp(b_last + m_prev - m_new), (SUBLANES, L))
        row_ref[5] = jnp.broadcast_to(jnp.exp(m_loc - m_new), (SUBLANES, L))
        row_ref[6] = m_prev_b
        m_ref[...] = jnp.broadcast_to(m_new, (SUBLANES, L))

        heads = range(ML_HEADS)
        q_f = [qk_ref[h, rows, :] for h in heads]
        k_t = [qk_ref[ML_HEADS + h, rows, :].T for h in heads]
        s = [jnp.dot(q_f[h].astype(BF16), k_t[h].astype(BF16), preferred_element_type=F32) for h in heads]
        yield MLP_PHASES_AFTER["ml_stage"][0]

        lhs, rhs, v_ext, ms_b = [], [], [], []
        for h in heads:
            cols = slice(h * head_dim, (h + 1) * head_dim)
            ca_b = jnp.broadcast_to(row_ref[0, h:h + 1, :], (head_dim, L)).T
            ms_b.append(jnp.broadcast_to(row_ref[1, h:h + 1, :], (head_dim, L)).T)
            p = jnp.exp(jnp.where(causal, ca_b + row_ref[2, h:h + 1, :], -jnp.inf))
            inter_w = jnp.exp(ca_b + row_ref[6, h:h + 1, :])
            v_ext.append(jnp.concatenate([v_all[rows, cols].astype(BF16), ones_ext], axis=1))
            lhs.append(jnp.concatenate([(s[h] * p).astype(BF16), (q_f[h] * inter_w).astype(BF16)], axis=1))
            rhs.append(jnp.concatenate([v_ext[h], c_ref[h].astype(BF16)], axis=0))
        yield MLP_PHASES_AFTER["ml_stage"][1]

        nd = [jnp.dot(lhs[h], rhs[h], preferred_element_type=F32) for h in heads]
        upd = [jnp.dot((k_t[h] * row_ref[3, h:h + 1, :]).astype(BF16), v_ext[h], preferred_element_type=F32)
               for h in heads]
        yield MLP_PHASES_AFTER["ml_stage"][2]

        for h in heads:
            cols = slice(h * head_dim, (h + 1) * head_dim)
            hh = nd[h][:, :head_dim] / jnp.maximum(jnp.abs(nd[h][:, head_dim:]), jnp.exp(-ms_b[h]))
            hh = _sigmoid(o_all[rows, cols]) * hh
            y_ref[rows, d_rg + h * head_dim:d_rg + (h + 1) * head_dim] = _rms(
                hh, ml_gain_ref[:, cols]).astype(BF16)
            s_old = row_ref[4, h:h + 1, :]
            s_new = row_ref[5, h:h + 1, :]
            c_ref[h] = (jnp.concatenate([s_old, s_old], axis=1) * c_ref[h]
                        + jnp.concatenate([s_new, s_new], axis=1) * upd[h])
        yield MLP_PHASES_AFTER["ml_stage"][3]

    mix = jnp.dot(y_ref[...], w_out_ref[...], preferred_element_type=F32)
    yield MLP_PHASES_AFTER["w_out"]
    return x + _rms(mix, post_g_ref[...])


def _mlp_tile(h_ref, pre_g_ref, w_up_ref, w_down_ref, post_g_ref):
    vb = _rms(h_ref[...], pre_g_ref[...]).astype(BF16)
    d_ff = w_up_ref.shape[1]
    acc = jnp.zeros(h_ref.shape, F32)
    for c in range(d_ff // MLP_FF_CHUNK):
        cols = slice(c * MLP_FF_CHUNK, (c + 1) * MLP_FF_CHUNK)
        f = jnp.maximum(jnp.dot(vb, w_up_ref[:, cols], preferred_element_type=F32), 0.0)
        yield
        acc = acc + jnp.dot((f * f).astype(BF16), w_down_ref[cols, :], preferred_element_type=F32)
        yield
    return h_ref[...] + _rms(acc, post_g_ref[...])


def _run(gen):
    while True:
        try:
            next(gen)
        except StopIteration as stop:
            return stop.value


def _interleave(primary, secondary, lead=0, per=1):
    results = [None, None]

    def advance(idx, gen):
        if results[idx] is None:
            try:
                return next(gen)
            except StopIteration as stop:
                results[idx] = (stop.value,)
        return 0

    for _ in range(lead):
        advance(1, secondary)
    while results[0] is None:
        for _ in range(advance(0, primary) * per):
            advance(1, secondary)
    while results[1] is None:
        advance(1, secondary)
    return results[0][0], results[1][0]


def _weight_copy_jobs(hbm, vmem, src_row_starts=None):
    rows, cols = vmem.shape
    jobs = []
    for cb in range(cols // WEIGHT_BLOCK):
        for rb in range(rows // WEIGHT_BLOCK):
            dst = (vmem, rb * WEIGHT_BLOCK, cb * WEIGHT_BLOCK)
            if src_row_starts is None:
                jobs.append((hbm, rb * WEIGHT_BLOCK, cb * WEIGHT_BLOCK) + dst + (False,))
            else:
                jobs.append((hbm, src_row_starts[cb], rb * WEIGHT_BLOCK) + dst + (True,))
    return jobs


def _load_weights(jobs, stage_ref, sem_ref):
    def copy(i):
        src, r0, c0 = jobs[i][:3]
        slot = i % 2
        return pltpu.make_async_copy(
            src.at[pl.ds(r0, WEIGHT_BLOCK), pl.ds(c0, WEIGHT_BLOCK)], stage_ref.at[slot], sem_ref.at[slot])

    copy(0).start()
    for i in range(len(jobs)):
        if i + 1 < len(jobs):
            copy(i + 1).start()
        copy(i).wait()
        dst, r0, c0, transposed = jobs[i][3:]
        block = stage_ref[i % 2]
        dst[r0:r0 + WEIGHT_BLOCK, c0:c0 + WEIGHT_BLOCK] = (block.T if transposed else block).astype(BF16)
        yield


def _layer_kernel(x_ref, *refs, tiles_per_seq, n_tiles, n_small, w_in_row_starts, **dims):
    (pre_g, w_if, b_if, conv_w, conv_b, w_gate, b_r, b_i, lam, rg_gain, ml_gain, post_g,
     mlp_pre_g, mlp_post_g) = refs[:n_small]
    w_in_t_hbm, w_out_hbm, w_up_hbm, w_down_hbm = refs[n_small:n_small + 4]
    out_ref = refs[n_small + 4]
    w_main, w_out, w_up, w_down, stage_ref, sem_ref, h1_ref = refs[n_small + 5:n_small + 12]
    state = refs[n_small + 12:]
    s = pl.program_id(0)

    @pl.when(s == 0)
    def _():
        h1_ref[...] = jnp.zeros_like(h1_ref)
        _run(_load_weights(_weight_copy_jobs(w_in_t_hbm, w_main, w_in_row_starts)
                           + _weight_copy_jobs(w_out_hbm, w_out) + _weight_copy_jobs(w_up_hbm, w_up)
                           + _weight_copy_jobs(w_down_hbm, w_down), stage_ref, sem_ref))

    h1_new, out = _interleave(
        _mixer_tile(x_ref[0], s % tiles_per_seq == 0, pre_g, w_main, w_if, b_if, conv_w, conv_b,
                    w_gate, b_r, b_i, lam, rg_gain, ml_gain, w_out, post_g, *state, **dims),
        _mlp_tile(h1_ref, mlp_pre_g, w_up, w_down, mlp_post_g),
        MLP_PHASES_AFTER["start"])
    out_ref[0] = out
    h1_ref[...] = h1_new


def _const_spec(shape):
    zeros = (0,) * len(shape)
    return pl.BlockSpec(shape, lambda *_: zeros, pipeline_mode=pl.Buffered(1))


def _block_diag_gate(w):
    nb, bd, _ = w.shape
    per = MXU_WIDTH // bd
    w = w.reshape(nb // per, per, bd, bd)
    eye = jnp.eye(per, dtype=w.dtype)
    return jnp.einsum("gpij,pq->gpiqj", w, eye).reshape(nb // per, MXU_WIDTH, MXU_WIDTH)


def _layer(x, pre_gain, w_in, rg_conv_w, rg_conv_b, gate_r_w, gate_r_b, gate_i_w, gate_i_b, lam,
           ml_conv_w, ml_conv_b, igate_b, fgate_b, rg_gain, ml_gain, w_out, post_gain,
           pre_mlp_gain, w_up, w_down, post_mlp_gain):
    bsz, seq, d_model = x.shape
    assert w_up.shape[1] % MLP_FF_CHUNK == 0
    d_rg = lam.shape[0]
    d_ml = ml_gain.shape[0]
    head_dim = d_ml // ML_HEADS
    assert head_dim == LANES and ML_CHUNK == LANES and 2 * ML_HEADS == SUBLANES
    assert seq % SEQ_TILE == 0 and SEQ_TILE % ML_CHUNK == 0 and d_rg % MXU_WIDTH == 0
    d_conv = d_rg + 2 * d_ml

    assert d_rg == WEIGHT_BLOCK and d_ml == WEIGHT_BLOCK
    starts = dict(rg_x=0, rg_gate=d_rg, q=2 * d_rg, k=2 * d_rg + d_ml, v=2 * d_rg + 2 * d_ml, o=2 * d_rg + 3 * d_ml)
    w_in_row_starts = tuple(starts[n] for n in ("rg_x", "q", "k", "rg_gate", "v", "o"))
    d_main = 2 * d_rg + 4 * d_ml
    if_w = w_in[:, d_main:]
    w_if = jnp.pad(if_w, ((0, 0), (0, LANES - 2 * ML_HEADS))).astype(BF16)
    b_if = jnp.pad(jnp.concatenate([igate_b, fgate_b]), (0, LANES - 2 * ML_HEADS)).reshape(1, LANES)
    conv_w = jnp.concatenate([rg_conv_w, ml_conv_w], axis=1)
    conv_b = jnp.concatenate([rg_conv_b, ml_conv_b]).reshape(1, d_conv)
    w_gate = jnp.concatenate([_block_diag_gate(gate_r_w), _block_diag_gate(gate_i_w)], axis=2).astype(BF16)

    row = lambda v: v.reshape(1, -1)
    small = (row(pre_gain), w_if, b_if, conv_w, conv_b, w_gate, row(gate_r_b), row(gate_i_b), row(lam),
             row(rg_gain), row(ml_gain), row(post_gain), row(pre_mlp_gain), row(post_mlp_gain))
    big = (jnp.swapaxes(w_in, 0, 1), w_out, w_up, w_down)
    d_ff = w_up.shape[1]
    for w in (w_out, w_up, w_down):
        assert w.shape[0] % WEIGHT_BLOCK == 0 and w.shape[1] % WEIGHT_BLOCK == 0

    tiles_per_seq = seq // SEQ_TILE
    n_tiles = bsz * tiles_per_seq

    def tile_block(t):
        return (t // tiles_per_seq, t % tiles_per_seq, 0)

    x_spec = pl.BlockSpec((1, SEQ_TILE, d_model), lambda s: tile_block(jnp.minimum(s, n_tiles - 1)))
    out_spec = pl.BlockSpec((1, SEQ_TILE, d_model), lambda s: tile_block(jnp.maximum(s - 1, 0)))
    in_specs = ([x_spec] + [_const_spec(op.shape) for op in small]
                + [pl.BlockSpec(memory_space=pl.ANY)] * len(big))
    kern = functools.partial(_layer_kernel, tiles_per_seq=tiles_per_seq, n_tiles=n_tiles, n_small=len(small),
                             w_in_row_starts=w_in_row_starts, d_rg=d_rg, d_ml=d_ml, head_dim=head_dim)
    return pl.pallas_call(
        kern,
        grid=(n_tiles + 1,),
        in_specs=in_specs,
        out_specs=out_spec,
        out_shape=jax.ShapeDtypeStruct(x.shape, x.dtype),
        scratch_shapes=[
            pltpu.VMEM((d_model, d_main), BF16),
            pltpu.VMEM((d_rg + d_ml, d_model), BF16),
            pltpu.VMEM((d_model, d_ff), BF16),
            pltpu.VMEM((d_ff, d_model), BF16),
            pltpu.VMEM((2, WEIGHT_BLOCK, WEIGHT_BLOCK), F32),
            pltpu.SemaphoreType.DMA((2,)),
            pltpu.VMEM((SEQ_TILE, d_model), F32),
            pltpu.VMEM(((d_conv + d_rg) // LANES, SUBLANES + SEQ_TILE, LANES), F32),
            pltpu.VMEM((2 * d_ml // LANES, SEQ_TILE, LANES), F32),
            pltpu.VMEM((d_rg // LANES, SEQ_TILE, LANES), F32),
            pltpu.VMEM((SUBLANES, d_rg), F32),
            pltpu.VMEM((ML_HEADS, head_dim, 2 * head_dim), F32),
            pltpu.VMEM((SUBLANES, ML_CHUNK), F32),
            pltpu.VMEM((7, SUBLANES, ML_CHUNK), F32),
            pltpu.VMEM((SEQ_TILE, d_rg + d_ml), BF16),
        ],
        compiler_params=pltpu.CompilerParams(
            dimension_semantics=("arbitrary",), vmem_limit_bytes=VMEM_LIMIT_BYTES),
        name="layer",
    )(x, *small, *big)


def kernel(x, pre_mix_gain, w_in, rg_conv_w, rg_conv_b, rg_gate_r_w, rg_gate_r_b, rg_gate_i_w, rg_gate_i_b, rg_lambda, ml_conv_w, ml_conv_b, ml_igate_b, ml_fgate_b, rg_out_gain, ml_out_gain, w_out, post_mix_gain, pre_mlp_gain, mlp_w_up, mlp_w_down, post_mlp_gain):
    h = x
    for l in range(w_in.shape[0]):
        h = _layer(h, pre_mix_gain[l], w_in[l], rg_conv_w[l], rg_conv_b[l], rg_gate_r_w[l], rg_gate_r_b[l],
                   rg_gate_i_w[l], rg_gate_i_b[l], rg_lambda[l], ml_conv_w[l], ml_conv_b[l], ml_igate_b[l],
                   ml_fgate_b[l], rg_out_gain[l], ml_out_gain[l], w_out[l], post_mix_gain[l],
                   pre_mlp_gain[l], mlp_w_up[l], mlp_w_down[l], post_mlp_gain[l])
    return h
```

```python
import functools

import jax
import jax.numpy as jnp
from jax.experimental import pallas as pl
from jax.experimental.pallas import tpu as pltpu

F32 = jnp.float32
BF16 = jnp.bfloat16

RG_BLOCKS = 8
RG_C = 8.0
ML_HEADS = 4
CONV_WIDTH = 4
EPS = 1e-6

SUBLANES = 8
LANES = 128
MXU_WIDTH = 256

SEQ_TILE = 512
ML_CHUNK = LANES
PHASES = 4
MLP_FF_CHUNK = 1024
WEIGHT_BLOCK = 512
MLP_PHASES_AFTER = {"start": 0, "w_in": 1, "conv": (0, 0, 1, 0, 0, 0), "rg_gates": 0, "rg_scan": (0, 0, 0),
                    "ml_stage": (1, 0, 1, 0), "w_out": 0}
VMEM_LIMIT_BYTES = 60 * 1024 * 1024


def _rms(x, gain):
    return x * jax.lax.rsqrt(jnp.mean(x * x, axis=-1, keepdims=True) + EPS) * gain


LOG2E = 1.4426950408889634


def _sigmoid(x):
    return 1.0 / (1.0 + jnp.exp2(x * (-LOG2E)))


def _softplus(x):
    return jnp.maximum(x, 0.0) + jnp.log1p(jnp.exp(-jnp.abs(x)))


def _gelu_tanh(x):
    c = 0.7978845608028654
    return 0.5 * x * (1.0 + jnp.tanh(c * (x + 0.044715 * (x * x * x))))


def _lane_scan(x, op, fill):
    n = x.shape[1]
    lane = jax.lax.broadcasted_iota(jnp.int32, x.shape, 1)
    d = 1
    while d < n:
        shifted = jnp.where(lane < d, fill, pltpu.roll(x, d, axis=1))
        x = op(x, shifted)
        d *= 2
    return x


def _affine_row_scan(sa, sb):
    n, width = sa.shape
    row8 = jax.lax.broadcasted_iota(jnp.int32, (SUBLANES, width), 0)
    d = 1
    while d < n:
        if d < SUBLANES:
            ra = pltpu.roll(sa, d, axis=0)
            rb = pltpu.roll(sb, d, axis=0)
            a_sh = jnp.concatenate([jnp.where(row8 < d, 1.0, ra[:SUBLANES]), ra[SUBLANES:]], axis=0)
            b_sh = jnp.concatenate([jnp.where(row8 < d, 0.0, rb[:SUBLANES]), rb[SUBLANES:]], axis=0)
            sb = sb + sa * b_sh
            sa = sa * a_sh
        else:
            sb = jnp.concatenate([sb[:d], sb[d:] + sa[d:] * sb[:n - d]], axis=0)
            sa = jnp.concatenate([sa[:d], sa[d:] * sa[:n - d]], axis=0)
        d *= 2
    return sa, sb


def _mixer_tile(x, seq_start, pre_g_ref, w_main_ref, w_if_ref, b_if_ref, conv_w_ref, conv_b_ref, w_gate_ref,
                b_r_ref, b_i_ref, lam_ref, rg_gain_ref, ml_gain_ref, w_out_ref, post_g_ref,
                cbuf_ref, qk_ref, yrg_ref, h_ref, c_ref, m_ref, row_ref, y_ref,
                *, d_rg, d_ml, head_dim):
    ts = x.shape[0]
    d_conv = d_rg + 2 * d_ml
    n_chunks = ts // ML_CHUNK
    L = ML_CHUNK

    @pl.when(seq_start)
    def _():
        cbuf_ref[:, ts:ts + SUBLANES, :] = jnp.zeros((cbuf_ref.shape[0], SUBLANES, LANES), F32)
        h_ref[...] = jnp.zeros_like(h_ref)
        c_ref[...] = jnp.zeros_like(c_ref)
        m_ref[...] = jnp.zeros_like(m_ref)

    ub = _rms(x, pre_g_ref[...]).astype(BF16)
    proj = jnp.dot(ub, w_main_ref[...], preferred_element_type=F32)
    gates = jnp.dot(ub, w_if_ref[...], preferred_element_type=F32) + b_if_ref[...]
    yield MLP_PHASES_AFTER["w_in"]

    n_cs = d_conv // LANES
    n_rs = d_rg // LANES
    G = ts // PHASES
    cbuf_ref[:n_cs, :SUBLANES, :] = cbuf_ref[:n_cs, ts:ts + SUBLANES, :]
    for j in range(n_cs + n_rs):
        cbuf_ref[j, SUBLANES:, :] = proj[:, j * LANES:(j + 1) * LANES]

    def phase(j, e):
        return cbuf_ref[j, pl.ds(SUBLANES + e, G, stride=PHASES), :]

    conv_ph = [[None] * n_cs for _ in range(PHASES)]
    for j in range(n_cs):
        cols = slice(j * LANES, (j + 1) * LANES)
        taps = {e: phase(j, e) for e in range(1 - CONV_WIDTH, PHASES)}
        for r in range(PHASES):
            acc = taps[r] * conv_w_ref[CONV_WIDTH - 1:CONV_WIDTH, cols] + conv_b_ref[:, cols]
            for k in range(1, CONV_WIDTH):
                acc = acc + taps[r - k] * conv_w_ref[CONV_WIDTH - 1 - k:CONV_WIDTH - k, cols]
            conv_ph[r][j] = acc
        if j >= n_rs:
            scale = head_dim ** -0.5 if j < n_rs + d_ml // LANES else 1.0
            for r in range(PHASES):
                val = conv_ph[r][j]
                val = val * _sigmoid(val)
                qk_ref[j - n_rs, pl.ds(r, G, stride=PHASES), :] = val * scale if scale != 1.0 else val
        if j % 2 == 1:
            yield MLP_PHASES_AFTER["conv"][j // 2]

    xc = jnp.concatenate([jnp.concatenate(conv_ph[r][:n_rs], axis=1) for r in range(PHASES)], axis=0)
    r_parts, i_parts = [], []
    for g in range(d_rg // MXU_WIDTH):
        gg = jnp.dot(xc[:, g * MXU_WIDTH:(g + 1) * MXU_WIDTH].astype(BF16), w_gate_ref[g],
                     preferred_element_type=F32)
        r_parts.append(gg[:, :MXU_WIDTH])
        i_parts.append(gg[:, MXU_WIDTH:])
    r = _sigmoid(jnp.concatenate(r_parts, axis=1) + b_r_ref[...])
    i_gate = _sigmoid(jnp.concatenate(i_parts, axis=1) + b_i_ref[...])
    neg_log_a_unit = RG_C * _softplus(-lam_ref[...])
    a = jnp.exp2(r * (neg_log_a_unit * (-LOG2E)))
    z = jnp.tanh(r * neg_log_a_unit) * (a * a + 1.0)
    b_in = jnp.where(z > 0.0, z * jax.lax.rsqrt(z), 0.0) * (i_gate * xc)
    yield MLP_PHASES_AFTER["rg_gates"]

    comp_a, comp_b = [a[:G]], [b_in[:G]]
    for r in range(1, PHASES):
        a_r, b_r = a[r * G:(r + 1) * G], b_in[r * G:(r + 1) * G]
        comp_b.append(a_r * comp_b[-1] + b_r)
        comp_a.append(a_r * comp_a[-1])
    ga, gb = _affine_row_scan(comp_a[-1], comp_b[-1])
    carry = h_ref[SUBLANES - 1:SUBLANES, :]
    h_end = ga * carry + gb
    h_ref[...] = h_end[G - SUBLANES:]
    rolled = pltpu.roll(h_end, 1, axis=0)
    row8r = jax.lax.broadcasted_iota(jnp.int32, (SUBLANES, d_rg), 0)
    h_prev = jnp.concatenate([jnp.where(row8r < 1, carry, rolled[:SUBLANES]), rolled[SUBLANES:]], axis=0)
    yield MLP_PHASES_AFTER["rg_scan"][0]
    for r in range(PHASES):
        h_r = comp_a[r] * h_prev + comp_b[r]
        gate_r = jnp.concatenate([phase(n_cs + j, r) for j in range(n_rs)], axis=1)
        y_r = _rms(h_r * _gelu_tanh(gate_r), rg_gain_ref[...])
        for j in range(n_rs):
            yrg_ref[j, pl.ds(r, G, stride=PHASES), :] = y_r[:, j * LANES:(j + 1) * LANES]
        if r % 2 == 1:
            yield MLP_PHASES_AFTER["rg_scan"][1 + r // 2]
    for j in range(n_rs):
        y_ref[:, j * LANES:(j + 1) * LANES] = yrg_ref[j].astype(BF16)

    v_all = proj[:, d_conv + d_rg:d_conv + d_rg + d_ml]
    o_all = proj[:, d_conv + d_rg + d_ml:]

    lane_g = jax.lax.broadcasted_iota(jnp.int32, gates.shape, 1)
    log_sig = jnp.minimum(gates, 0.0) - jnp.log1p(jnp.exp(-jnp.abs(gates)))
    gates_t = jnp.where(lane_g < ML_HEADS, gates, log_sig).T[:SUBLANES]

    causal = (jax.lax.broadcasted_iota(jnp.int32, (L, L), 1)
              <= jax.lax.broadcasted_iota(jnp.int32, (L, L), 0))
    ones_ext = jnp.ones((L, head_dim), BF16)

    for c in range(n_chunks):
        rows = slice(c * L, (c + 1) * L)
        li = gates_t[:, rows]
        lf = pltpu.roll(li, ML_HEADS, axis=0)
        bcum = _lane_scan(lf, jnp.add, 0.0)
        b_last = jnp.sum(lf, axis=1, keepdims=True)
        row_b = li - bcum
        cmax = _lane_scan(row_b, jnp.maximum, -jnp.inf)
        w_loc = b_last + row_b
        m_loc = jnp.max(w_loc, axis=1, keepdims=True)
        m_prev_b = m_ref[...]
        m_prev = jnp.max(m_prev_b, axis=1, keepdims=True)
        m_s = jnp.maximum(bcum + m_prev, bcum + cmax)
        m_new = jnp.maximum(b_last + m_prev, m_loc)
        row_ref[0] = bcum - m_s
        row_ref[1] = m_s
        row_ref[2] = row_b
        row_ref[3] = jnp.exp(w_loc - m_loc)
        row_ref[4] = jnp.broadcast_to(jnp.exp(b_last + m_prev - m_new), (SUBLANES, L))
        row_ref[5] = jnp.broadcast_to(jnp.exp(m_loc - m_new), (SUBLANES, L))
        row_ref[6] = m_prev_b
        m_ref[...] = jnp.broadcast_to(m_new, (SUBLANES, L))

        heads = range(ML_HEADS)
        q_f = [qk_ref[h, rows, :] for h in heads]
        k_t = [qk_ref[ML_HEADS + h, rows, :].T for h in heads]
        s = [jnp.dot(q_f[h].astype(BF16), k_t[h].astype(BF16), preferred_element_type=F32) for h in heads]
        yield MLP_PHASES_AFTER["ml_stage"][0]

        lhs, rhs, v_ext, ms_b = [], [], [], []
        for h in heads:
            cols = slice(h * head_dim, (h + 1) * head_dim)
            ca_b = jnp.broadcast_to(row_ref[0, h:h + 1, :], (head_dim, L)).T
            ms_b.append(jnp.broadcast_to(row_ref[1, h:h + 1, :], (head_dim, L)).T)
            p = jnp.exp(jnp.where(causal, ca_b + row_ref[2, h:h + 1, :], -jnp.inf))
            inter_w = jnp.exp(ca_b + row_ref[6, h:h + 1, :])
            v_ext.append(jnp.concatenate([v_all[rows, cols].astype(BF16), ones_ext], axis=1))
            lhs.append(jnp.concatenate([(s[h] * p).astype(BF16), (q_f[h] * inter_w).astype(BF16)], axis=1))
            rhs.append(jnp.concatenate([v_ext[h], c_ref[h].astype(BF16)], axis=0))
        yield MLP_PHASES_AFTER["ml_stage"][1]

        nd = [jnp.dot(lhs[h], rhs[h], preferred_element_type=F32) for h in heads]
        upd = [jnp.dot((k_t[h] * row_ref[3, h:h + 1, :]).astype(BF16), v_ext[h], preferred_element_type=F32)
               for h in heads]
        yield MLP_PHASES_AFTER["ml_stage"][2]

        for h in heads:
            cols = slice(h * head_dim, (h + 1) * head_dim)
            hh = nd[h][:, :head_dim] / jnp.maximum(jnp.abs(nd[h][:, head_dim:]), jnp.exp(-ms_b[h]))
            hh = _sigmoid(o_all[rows, cols]) * hh
            y_ref[rows, d_rg + h * head_dim:d_rg + (h + 1) * head_dim] = _rms(
                hh, ml_gain_ref[:, cols]).astype(BF16)
            s_old = row_ref[4, h:h + 1, :]
            s_new = row_ref[5, h:h + 1, :]
            c_ref[h] = (jnp.concatenate([s_old, s_old], axis=1) * c_ref[h]
                        + jnp.concatenate([s_new, s_new], axis=1) * upd[h])
        yield MLP_PHASES_AFTER["ml_stage"][3]

    mix = jnp.dot(y_ref[...], w_out_ref[...], preferred_element_type=F32)
    yield MLP_PHASES_AFTER["w_out"]
    return x + _rms(mix, post_g_ref[...])


def _mlp_tile(h_ref, pre_g_ref, w_up_ref, w_down_ref, post_g_ref):
    vb = _rms(h_ref[...], pre_g_ref[...]).astype(BF16)
    d_ff = w_up_ref.shape[1]
    acc = jnp.zeros(h_ref.shape, F32)
    for c in range(d_ff // MLP_FF_CHUNK):
        cols = slice(c * MLP_FF_CHUNK, (c + 1) * MLP_FF_CHUNK)
        f = jnp.maximum(jnp.dot(vb, w_up_ref[:, cols], preferred_element_type=F32), 0.0)
        yield
        acc = acc + jnp.dot((f * f).astype(BF16), w_down_ref[cols, :], preferred_element_type=F32)
        yield
    return h_ref[...] + _rms(acc, post_g_ref[...])


def _run(gen):
    while True:
        try:
            next(gen)
        except StopIteration as stop:
            return stop.value


def _interleave(primary, secondary, lead=0, per=1):
    results = [None, None]

    def advance(idx, gen):
        if results[idx] is None:
            try:
                return next(gen)
            except StopIteration as stop:
                results[idx] = (stop.value,)
        return 0

    for _ in range(lead):
        advance(1, secondary)
    while results[0] is None:
        for _ in range(advance(0, primary) * per):
            advance(1, secondary)
    while results[1] is None:
        advance(1, secondary)
    return results[0][0], results[1][0]


def _weight_copy_jobs(hbm, vmem, src_row_starts=None):
    rows, cols = vmem.shape
    jobs = []
    for cb in range(cols // WEIGHT_BLOCK):
        for rb in range(rows // WEIGHT_BLOCK):
            dst = (vmem, rb * WEIGHT_BLOCK, cb * WEIGHT_BLOCK)
            if src_row_starts is None:
                jobs.append((hbm, rb * WEIGHT_BLOCK, cb * WEIGHT_BLOCK) + dst + (False,))
            else:
                jobs.append((hbm, src_row_starts[cb], rb * WEIGHT_BLOCK) + dst + (True,))
    return jobs


def _load_weights(jobs, stage_ref, sem_ref):
    def copy(i):
        src, r0, c0 = jobs[i][:3]
        slot = i % 2
        return pltpu.make_async_copy(
            src.at[pl.ds(r0, WEIGHT_BLOCK), pl.ds(c0, WEIGHT_BLOCK)], stage_ref.at[slot], sem_ref.at[slot])

    copy(0).start()
    for i in range(len(jobs)):
        if i + 1 < len(jobs):
            copy(i + 1).start()
        copy(i).wait()
        dst, r0, c0, transposed = jobs[i][3:]
        block = stage_ref[i % 2]
        dst[r0:r0 + WEIGHT_BLOCK, c0:c0 + WEIGHT_BLOCK] = (block.T if transposed else block).astype(BF16)
        yield


def _layer_kernel(x_ref, *refs, tiles_per_seq, n_tiles, n_small, w_in_row_starts, **dims):
    (pre_g, w_if, b_if, conv_w, conv_b, w_gate, b_r, b_i, lam, rg_gain, ml_gain, post_g,
     mlp_pre_g, mlp_post_g) = refs[:n_small]
    w_in_t_hbm, w_out_hbm, w_up_hbm, w_down_hbm = refs[n_small:n_small + 4]
    out_ref = refs[n_small + 4]
    w_main, w_out, w_up, w_down, stage_ref, sem_ref, h1_ref = refs[n_small + 5:n_small + 12]
    state = refs[n_small + 12:]
    s = pl.program_id(0)

    @pl.when(s == 0)
    def _():
        h1_ref[...] = jnp.zeros_like(h1_ref)
        _run(_load_weights(_weight_copy_jobs(w_in_t_hbm, w_main, w_in_row_starts)
                           + _weight_copy_jobs(w_out_hbm, w_out) + _weight_copy_jobs(w_up_hbm, w_up)
                           + _weight_copy_jobs(w_down_hbm, w_down), stage_ref, sem_ref))

    h1_new, out = _interleave(
        _mixer_tile(x_ref[0], s % tiles_per_seq == 0, pre_g, w_main, w_if, b_if, conv_w, conv_b,
                    w_gate, b_r, b_i, lam, rg_gain, ml_gain, w_out, post_g, *state, **dims),
        _mlp_tile(h1_ref, mlp_pre_g, w_up, w_down, mlp_post_g),
        MLP_PHASES_AFTER["start"])
    out_ref[0] = out
    h1_ref[...] = h1_new


def _const_spec(shape):
    zeros = (0,) * len(shape)
    return pl.BlockSpec(shape, lambda *_: zeros, pipeline_mode=pl.Buffered(1))


def _block_diag_gate(w):
    nb, bd, _ = w.shape
    per = MXU_WIDTH // bd
    w = w.reshape(nb // per, per, bd, bd)
    eye = jnp.eye(per, dtype=w.dtype)
    return jnp.einsum("gpij,pq->gpiqj", w, eye).reshape(nb // per, MXU_WIDTH, MXU_WIDTH)


def _layer(x, pre_gain, w_in, rg_conv_w, rg_conv_b, gate_r_w, gate_r_b, gate_i_w, gate_i_b, lam,
           ml_conv_w, ml_conv_b, igate_b, fgate_b, rg_gain, ml_gain, w_out, post_gain,
           pre_mlp_gain, w_up, w_down, post_mlp_gain):
    bsz, seq, d_model = x.shape
    assert w_up.shape[1] % MLP_FF_CHUNK == 0
    d_rg = lam.shape[0]
    d_ml = ml_gain.shape[0]
    head_dim = d_ml // ML_HEADS
    assert head_dim == LANES and ML_CHUNK == LANES and 2 * ML_HEADS == SUBLANES
    assert seq % SEQ_TILE == 0 and SEQ_TILE % ML_CHUNK == 0 and d_rg % MXU_WIDTH == 0
    d_conv = d_rg + 2 * d_ml

    assert d_rg == WEIGHT_BLOCK and d_ml == WEIGHT_BLOCK
    starts = dict(rg_x=0, rg_gate=d_rg, q=2 * d_rg, k=2 * d_rg + d_ml, v=2 * d_rg + 2 * d_ml, o=2 * d_rg + 3 * d_ml)
    w_in_row_starts = tuple(starts[n] for n in ("rg_x", "q", "k", "rg_gate", "v", "o"))
    d_main = 2 * d_rg + 4 * d_ml
    if_w = w_in[:, d_main:]
    w_if = jnp.pad(if_w, ((0, 0), (0, LANES - 2 * ML_HEADS))).astype(BF16)
    b_if = jnp.pad(jnp.concatenate([igate_b, fgate_b]), (0, LANES - 2 * ML_HEADS)).reshape(1, LANES)
    conv_w = jnp.concatenate([rg_conv_w, ml_conv_w], axis=1)
    conv_b = jnp.concatenate([rg_conv_b, ml_conv_b]).reshape(1, d_conv)
    w_gate = jnp.concatenate([_block_diag_gate(gate_r_w), _block_diag_gate(gate_i_w)], axis=2).astype(BF16)

    row = lambda v: v.reshape(1, -1)
    small = (row(pre_gain), w_if, b_if, conv_w, conv_b, w_gate, row(gate_r_b), row(gate_i_b), row(lam),
             row(rg_gain), row(ml_gain), row(post_gain), row(pre_mlp_gain), row(post_mlp_gain))
    big = (jnp.swapaxes(w_in, 0, 1), w_out, w_up, w_down)
    d_ff = w_up.shape[1]
    for w in (w_out, w_up, w_down):
        assert w.shape[0] % WEIGHT_BLOCK == 0 and w.shape[1] % WEIGHT_BLOCK == 0

    tiles_per_seq = seq // SEQ_TILE
    n_tiles = bsz * tiles_per_seq

    def tile_block(t):
        return (t // tiles_per_seq, t % tiles_per_seq, 0)

    x_spec = pl.BlockSpec((1, SEQ_TILE, d_model), lambda s: tile_block(jnp.minimum(s, n_tiles - 1)))
    out_spec = pl.BlockSpec((1, SEQ_TILE, d_model), lambda s: tile_block(jnp.maximum(s - 1, 0)))
    in_specs = ([x_spec] + [_const_spec(op.shape) for op in small]
                + [pl.BlockSpec(memory_space=pl.ANY)] * len(big))
    kern = functools.partial(_layer_kernel, tiles_per_seq=tiles_per_seq, n_tiles=n_tiles, n_small=len(small),
                             w_in_row_starts=w_in_row_starts, d_rg=d_rg, d_ml=d_ml, head_dim=head_dim)
    return pl.pallas_call(
        kern,
        grid=(n_tiles + 1,),
        in_specs=in_specs,
        out_specs=out_spec,
        out_shape=jax.ShapeDtypeStruct(x.shape, x.dtype),
        scratch_shapes=[
            pltpu.VMEM((d_model, d_main), BF16),
            pltpu.VMEM((d_rg + d_ml, d_model), BF16),
            pltpu.VMEM((d_model, d_ff), BF16),
            pltpu.VMEM((d_ff, d_model), BF16),
            pltpu.VMEM((2, WEIGHT_BLOCK, WEIGHT_BLOCK), F32),
            pltpu.SemaphoreType.DMA((2,)),
            pltpu.VMEM((SEQ_TILE, d_model), F32),
            pltpu.VMEM(((d_conv + d_rg) // LANES, SUBLANES + SEQ_TILE, LANES), F32),
            pltpu.VMEM((2 * d_ml // LANES, SEQ_TILE, LANES), F32),
            pltpu.VMEM((d_rg // LANES, SEQ_TILE, LANES), F32),
            pltpu.VMEM((SUBLANES, d_rg), F32),
            pltpu.VMEM((ML_HEADS, head_dim, 2 * head_dim), F32),
            pltpu.VMEM((SUBLANES, ML_CHUNK), F32),
            pltpu.VMEM((7, SUBLANES, ML_CHUNK), F32),
            pltpu.VMEM((SEQ_TILE, d_rg + d_ml), BF16),
        ],
        compiler_params=pltpu.CompilerParams(
            dimension_semantics=("arbitrary",), vmem_limit_bytes=VMEM_LIMIT_BYTES),
        name="layer",
    )(x, *small, *big)


def kernel(x, pre_mix_gain, w_in, rg_conv_w, rg_conv_b, rg_gate_r_w, rg_gate_r_b, rg_gate_i_w, rg_gate_i_b, rg_lambda, ml_conv_w, ml_conv_b, ml_igate_b, ml_fgate_b, rg_out_gain, ml_out_gain, w_out, post_mix_gain, pre_mlp_gain, mlp_w_up, mlp_w_down, post_mlp_gain):
    h = x
    for l in range(w_in.shape[0]):
        h = _layer(h, pre_mix_gain[l], w_in[l], rg_conv_w[l], rg_conv_b[l], rg_gate_r_w[l], rg_gate_r_b[l],
                   rg_gate_i_w[l], rg_gate_i_b[l], rg_lambda[l], ml_conv_w[l], ml_conv_b[l], ml_igate_b[l],
                   ml_fgate_b[l], rg_out_gain[l], ml_out_gain[l], w_out[l], post_mix_gain[l],
                   pre_mlp_gain[l], mlp_w_up[l], mlp_w_down[l], post_mlp_gain[l])
    return h
```

```python
import functools

import jax
import jax.numpy as jnp
from jax.experimental import pallas as pl
from jax.experimental.pallas import tpu as pltpu

F32 = jnp.float32
BF16 = jnp.bfloat16

RG_BLOCKS = 8
RG_C = 8.0
ML_HEADS = 4
CONV_WIDTH = 4
EPS = 1e-6

SUBLANES = 8
LANES = 128
MXU_WIDTH = 256

SEQ_TILE = 512
ML_CHUNK = LANES
PHASES = 4
MLP_FF_CHUNK = 1024
WEIGHT_BLOCK = 512
MLP_PHASES_AFTER = {"start": 1, "w_in": 1, "conv": (0, 0, 1, 0, 0, 0), "rg_gates": 0, "rg_scan": (1, 0, 0),
                    "ml_stage": (1, 0, 0, 0), "w_out": 0}
VMEM_LIMIT_BYTES = 60 * 1024 * 1024


def _rms(x, gain):
    return x * jax.lax.rsqrt(jnp.mean(x * x, axis=-1, keepdims=True) + EPS) * gain


LOG2E = 1.4426950408889634


def _sigmoid(x):
    return 1.0 / (1.0 + jnp.exp2(x * (-LOG2E)))


def _softplus(x):
    return jnp.maximum(x, 0.0) + jnp.log1p(jnp.exp(-jnp.abs(x)))


def _gelu_tanh(x):
    c = 0.7978845608028654
    return 0.5 * x * (1.0 + jnp.tanh(c * (x + 0.044715 * (x * x * x))))


def _lane_scan(x, op, fill):
    n = x.shape[1]
    lane = jax.lax.broadcasted_iota(jnp.int32, x.shape, 1)
    d = 1
    while d < n:
        shifted = jnp.where(lane < d, fill, pltpu.roll(x, d, axis=1))
        x = op(x, shifted)
        d *= 2
    return x


def _affine_row_scan(sa, sb):
    n, width = sa.shape
    row8 = jax.lax.broadcasted_iota(jnp.int32, (SUBLANES, width), 0)
    d = 1
    while d < n:
        if d < SUBLANES:
            ra = pltpu.roll(sa, d, axis=0)
            rb = pltpu.roll(sb, d, axis=0)
            a_sh = jnp.concatenate([jnp.where(row8 < d, 1.0, ra[:SUBLANES]), ra[SUBLANES:]], axis=0)
            b_sh = jnp.concatenate([jnp.where(row8 < d, 0.0, rb[:SUBLANES]), rb[SUBLANES:]], axis=0)
            sb = sb + sa * b_sh
            sa = sa * a_sh
        else:
            sb = jnp.concatenate([sb[:d], sb[d:] + sa[d:] * sb[:n - d]], axis=0)
            sa = jnp.concatenate([sa[:d], sa[d:] * sa[:n - d]], axis=0)
        d *= 2
    return sa, sb


def _mixer_tile(x, seq_start, pre_g_ref, w_main_ref, w_if_ref, b_if_ref, conv_w_ref, conv_b_ref, w_gate_ref,
                b_r_ref, b_i_ref, lam_ref, rg_gain_ref, ml_gain_ref, w_out_ref, post_g_ref,
                cbuf_ref, qk_ref, yrg_ref, h_ref, c_ref, m_ref, row_ref, y_ref,
                *, d_rg, d_ml, head_dim):
    ts = x.shape[0]
    d_conv = d_rg + 2 * d_ml
    n_chunks = ts // ML_CHUNK
    L = ML_CHUNK

    @pl.when(seq_start)
    def _():
        cbuf_ref[:, ts:ts + SUBLANES, :] = jnp.zeros((cbuf_ref.shape[0], SUBLANES, LANES), F32)
        h_ref[...] = jnp.zeros_like(h_ref)
        c_ref[...] = jnp.zeros_like(c_ref)
        m_ref[...] = jnp.zeros_like(m_ref)

    ub = _rms(x, pre_g_ref[...]).astype(BF16)
    proj = jnp.dot(ub, w_main_ref[...], preferred_element_type=F32)
    gates = jnp.dot(ub, w_if_ref[...], preferred_element_type=F32) + b_if_ref[...]
    yield MLP_PHASES_AFTER["w_in"]

    n_cs = d_conv // LANES
    n_rs = d_rg // LANES
    G = ts // PHASES
    cbuf_ref[:n_cs, :SUBLANES, :] = cbuf_ref[:n_cs, ts:ts + SUBLANES, :]
    for j in range(n_cs + n_rs):
        cbuf_ref[j, SUBLANES:, :] = proj[:, j * LANES:(j + 1) * LANES]

    def phase(j, e):
        return cbuf_ref[j, pl.ds(SUBLANES + e, G, stride=PHASES), :]

    conv_ph = [[None] * n_cs for _ in range(PHASES)]
    for j in range(n_cs):
        cols = slice(j * LANES, (j + 1) * LANES)
        taps = {e: phase(j, e) for e in range(1 - CONV_WIDTH, PHASES)}
        for r in range(PHASES):
            acc = taps[r] * conv_w_ref[CONV_WIDTH - 1:CONV_WIDTH, cols] + conv_b_ref[:, cols]
            for k in range(1, CONV_WIDTH):
                acc = acc + taps[r - k] * conv_w_ref[CONV_WIDTH - 1 - k:CONV_WIDTH - k, cols]
            conv_ph[r][j] = acc
        if j >= n_rs:
            scale = head_dim ** -0.5 if j < n_rs + d_ml // LANES else 1.0
            for r in range(PHASES):
                val = conv_ph[r][j]
                val = val * _sigmoid(val)
                qk_ref[j - n_rs, pl.ds(r, G, stride=PHASES), :] = val * scale if scale != 1.0 else val
        if j % 2 == 1:
            yield MLP_PHASES_AFTER["conv"][j // 2]

    xc = jnp.concatenate([jnp.concatenate(conv_ph[r][:n_rs], axis=1) for r in range(PHASES)], axis=0)
    r_parts, i_parts = [], []
    for g in range(d_rg // MXU_WIDTH):
        gg = jnp.dot(xc[:, g * MXU_WIDTH:(g + 1) * MXU_WIDTH].astype(BF16), w_gate_ref[g],
                     preferred_element_type=F32)
        r_parts.append(gg[:, :MXU_WIDTH])
        i_parts.append(gg[:, MXU_WIDTH:])
    r = _sigmoid(jnp.concatenate(r_parts, axis=1) + b_r_ref[...])
    i_gate = _sigmoid(jnp.concatenate(i_parts, axis=1) + b_i_ref[...])
    neg_log_a_unit = RG_C * _softplus(-lam_ref[...])
    a = jnp.exp2(r * (neg_log_a_unit * (-LOG2E)))
    z = jnp.tanh(r * neg_log_a_unit) * (a * a + 1.0)
    b_in = jnp.where(z > 0.0, z * jax.lax.rsqrt(z), 0.0) * (i_gate * xc)
    yield MLP_PHASES_AFTER["rg_gates"]

    comp_a, comp_b = [a[:G]], [b_in[:G]]
    for r in range(1, PHASES):
        a_r, b_r = a[r * G:(r + 1) * G], b_in[r * G:(r + 1) * G]
        comp_b.append(a_r * comp_b[-1] + b_r)
        comp_a.append(a_r * comp_a[-1])
    ga, gb = _affine_row_scan(comp_a[-1], comp_b[-1])
    carry = h_ref[SUBLANES - 1:SUBLANES, :]
    h_end = ga * carry + gb
    h_ref[...] = h_end[G - SUBLANES:]
    rolled = pltpu.roll(h_end, 1, axis=0)
    row8r = jax.lax.broadcasted_iota(jnp.int32, (SUBLANES, d_rg), 0)
    h_prev = jnp.concatenate([jnp.where(row8r < 1, carry, rolled[:SUBLANES]), rolled[SUBLANES:]], axis=0)
    yield MLP_PHASES_AFTER["rg_scan"][0]
    for r in range(PHASES):
        h_r = comp_a[r] * h_prev + comp_b[r]
        gate_r = jnp.concatenate([phase(n_cs + j, r) for j in range(n_rs)], axis=1)
        y_r = _rms(h_r * _gelu_tanh(gate_r), rg_gain_ref[...])
        for j in range(n_rs):
            yrg_ref[j, pl.ds(r, G, stride=PHASES), :] = y_r[:, j * LANES:(j + 1) * LANES]
        if r % 2 == 1:
            yield MLP_PHASES_AFTER["rg_scan"][1 + r // 2]
    for j in range(n_rs):
        y_ref[:, j * LANES:(j + 1) * LANES] = yrg_ref[j].astype(BF16)

    v_all = proj[:, d_conv + d_rg:d_conv + d_rg + d_ml]
    o_all = proj[:, d_conv + d_rg + d_ml:]

    lane_g = jax.lax.broadcasted_iota(jnp.int32, gates.shape, 1)
    log_sig = jnp.minimum(gates, 0.0) - jnp.log1p(jnp.exp(-jnp.abs(gates)))
    gates_t = jnp.where(lane_g < ML_HEADS, gates, log_sig).T[:SUBLANES]

    causal = (jax.lax.broadcasted_iota(jnp.int32, (L, L), 1)
              <= jax.lax.broadcasted_iota(jnp.int32, (L, L), 0))
    ones_ext = jnp.ones((L, head_dim), BF16)

    for c in range(n_chunks):
        rows = slice(c * L, (c + 1) * L)
        li = gates_t[:, rows]
        lf = pltpu.roll(li, ML_HEADS, axis=0)
        bcum = _lane_scan(lf, jnp.add, 0.0)
        b_last = jnp.sum(lf, axis=1, keepdims=True)
        row_b = li - bcum
        cmax = _lane_scan(row_b, jnp.maximum, -jnp.inf)
        w_loc = b_last + row_b
        m_loc = jnp.max(w_loc, axis=1, keepdims=True)
        m_prev_b = m_ref[...]
        m_prev = jnp.max(m_prev_b, axis=1, keepdims=True)
        m_s = jnp.maximum(bcum + m_prev, bcum + cmax)
        m_new = jnp.maximum(b_last + m_prev, m_loc)
        row_ref[0] = bcum - m_s
        row_ref[1] = m_s
        row_ref[2] = row_b
        row_ref[3] = jnp.exp(w_loc - m_loc)
        row_ref[4] = jnp.broadcast_to(jnp.exp(b_last + m_prev - m_new), (SUBLANES, L))
        row_ref[5] = jnp.broadcast_to(jnp.exp(m_loc - m_new), (SUBLANES, L))
        row_ref[6] = m_prev_b
        m_ref[...] = jnp.broadcast_to(m_new, (SUBLANES, L))

        heads = range(ML_HEADS)
        q_f = [qk_ref[h, rows, :] for h in heads]
        k_t = [qk_ref[ML_HEADS + h, rows, :].T for h in heads]
        s = [jnp.dot(q_f[h].astype(BF16), k_t[h].astype(BF16), preferred_element_type=F32) for h in heads]
        yield MLP_PHASES_AFTER["ml_stage"][0]

        lhs, rhs, v_ext, ms_b = [], [], [], []
        for h in heads:
            cols = slice(h * head_dim, (h + 1) * head_dim)
            ca_b = jnp.broadcast_to(row_ref[0, h:h + 1, :], (head_dim, L)).T
            ms_b.append(jnp.broadcast_to(row_ref[1, h:h + 1, :], (head_dim, L)).T)
            p = jnp.exp(jnp.where(causal, ca_b + row_ref[2, h:h + 1, :], -jnp.inf))
            inter_w = jnp.exp(ca_b + row_ref[6, h:h + 1, :])
            v_ext.append(jnp.concatenate([v_all[rows, cols].astype(BF16), ones_ext], axis=1))
            lhs.append(jnp.concatenate([(s[h] * p).astype(BF16), (q_f[h] * inter_w).astype(BF16)], axis=1))
            rhs.append(jnp.concatenate([v_ext[h], c_ref[h].astype(BF16)], axis=0))
        yield MLP_PHASES_AFTER["ml_stage"][1]

        nd = [jnp.dot(lhs[h], rhs[h], preferred_element_type=F32) for h in heads]
        upd = [jnp.dot((k_t[h] * row_ref[3, h:h + 1, :]).astype(BF16), v_ext[h], preferred_element_type=F32)
               for h in heads]
        yield MLP_PHASES_AFTER["ml_stage"][2]

        for h in heads:
            cols = slice(h * head_dim, (h + 1) * head_dim)
            hh = nd[h][:, :head_dim] / jnp.maximum(jnp.abs(nd[h][:, head_dim:]), jnp.exp(-ms_b[h]))
            hh = _sigmoid(o_all[rows, cols]) * hh
            y_ref[rows, d_rg + h * head_dim:d_rg + (h + 1) * head_dim] = _rms(
                hh, ml_gain_ref[:, cols]).astype(BF16)
            s_old = row_ref[4, h:h + 1, :]
            s_new = row_ref[5, h:h + 1, :]
            c_ref[h] = (jnp.concatenate([s_old, s_old], axis=1) * c_ref[h]
                        + jnp.concatenate([s_new, s_new], axis=1) * upd[h])
        yield MLP_PHASES_AFTER["ml_stage"][3]

    mix = jnp.dot(y_ref[...], w_out_ref[...], preferred_element_type=F32)
    yield MLP_PHASES_AFTER["w_out"]
    return x + _rms(mix, post_g_ref[...])


def _mlp_tile(h_ref, pre_g_ref, w_up_ref, w_down_ref, post_g_ref):
    vb = _rms(h_ref[...], pre_g_ref[...]).astype(BF16)
    d_ff = w_up_ref.shape[1]
    acc = jnp.zeros(h_ref.shape, F32)
    for c in range(d_ff // MLP_FF_CHUNK):
        cols = slice(c * MLP_FF_CHUNK, (c + 1) * MLP_FF_CHUNK)
        f = jnp.maximum(jnp.dot(vb, w_up_ref[:, cols], preferred_element_type=F32), 0.0)
        yield
        acc = acc + jnp.dot((f * f).astype(BF16), w_down_ref[cols, :], preferred_element_type=F32)
        yield
    return h_ref[...] + _rms(acc, post_g_ref[...])


def _run(gen):
    while True:
        try:
            next(gen)
        except StopIteration as stop:
            return stop.value


def _interleave(primary, secondary, lead=0, per=1):
    results = [None, None]

    def advance(idx, gen):
        if results[idx] is None:
            try:
                return next(gen)
            except StopIteration as stop:
                results[idx] = (stop.value,)
        return 0

    for _ in range(lead):
        advance(1, secondary)
    while results[0] is None:
        for _ in range(advance(0, primary) * per):
            advance(1, secondary)
    while results[1] is None:
        advance(1, secondary)
    return results[0][0], results[1][0]


def _weight_copy_jobs(hbm, vmem, src_row_starts=None):
    rows, cols = vmem.shape
    jobs = []
    for cb in range(cols // WEIGHT_BLOCK):
        for rb in range(rows // WEIGHT_BLOCK):
            dst = (vmem, rb * WEIGHT_BLOCK, cb * WEIGHT_BLOCK)
            if src_row_starts is None:
                jobs.append((hbm, rb * WEIGHT_BLOCK, cb * WEIGHT_BLOCK) + dst + (False,))
            else:
                jobs.append((hbm, src_row_starts[cb], rb * WEIGHT_BLOCK) + dst + (True,))
    return jobs


def _load_weights(jobs, stage_ref, sem_ref):
    def copy(i):
        src, r0, c0 = jobs[i][:3]
        slot = i % 2
        return pltpu.make_async_copy(
            src.at[pl.ds(r0, WEIGHT_BLOCK), pl.ds(c0, WEIGHT_BLOCK)], stage_ref.at[slot], sem_ref.at[slot])

    copy(0).start()
    for i in range(len(jobs)):
        if i + 1 < len(jobs):
            copy(i + 1).start()
        copy(i).wait()
        dst, r0, c0, transposed = jobs[i][3:]
        block = stage_ref[i % 2]
        dst[r0:r0 + WEIGHT_BLOCK, c0:c0 + WEIGHT_BLOCK] = (block.T if transposed else block).astype(BF16)
        yield


def _layer_kernel(x_ref, *refs, tiles_per_seq, n_tiles, n_small, w_in_row_starts, **dims):
    (pre_g, w_if, b_if, conv_w, conv_b, w_gate, b_r, b_i, lam, rg_gain, ml_gain, post_g,
     mlp_pre_g, mlp_post_g) = refs[:n_small]
    w_in_t_hbm, w_out_hbm, w_up_hbm, w_down_hbm = refs[n_small:n_small + 4]
    out_ref = refs[n_small + 4]
    w_main, w_out, w_up, w_down, stage_ref, sem_ref, h1_ref = refs[n_small + 5:n_small + 12]
    state = refs[n_small + 12:]
    s = pl.program_id(0)

    @pl.when(s == 0)
    def _():
        h1_ref[...] = jnp.zeros_like(h1_ref)
        _run(_load_weights(_weight_copy_jobs(w_in_t_hbm, w_main, w_in_row_starts)
                           + _weight_copy_jobs(w_out_hbm, w_out) + _weight_copy_jobs(w_up_hbm, w_up)
                           + _weight_copy_jobs(w_down_hbm, w_down), stage_ref, sem_ref))

    h1_new, out = _interleave(
        _mixer_tile(x_ref[0], s % tiles_per_seq == 0, pre_g, w_main, w_if, b_if, conv_w, conv_b,
                    w_gate, b_r, b_i, lam, rg_gain, ml_gain, w_out, post_g, *state, **dims),
        _mlp_tile(h1_ref, mlp_pre_g, w_up, w_down, mlp_post_g),
        MLP_PHASES_AFTER["start"])
    out_ref[0] = out
    h1_ref[...] = h1_new


def _const_spec(shape):
    zeros = (0,) * len(shape)
    return pl.BlockSpec(shape, lambda *_: zeros, pipeline_mode=pl.Buffered(1))


def _block_diag_gate(w):
    nb, bd, _ = w.shape
    per = MXU_WIDTH // bd
    w = w.reshape(nb // per, per, bd, bd)
    eye = jnp.eye(per, dtype=w.dtype)
    return jnp.einsum("gpij,pq->gpiqj", w, eye).reshape(nb // per, MXU_WIDTH, MXU_WIDTH)


def _layer(x, pre_gain, w_in, rg_conv_w, rg_conv_b, gate_r_w, gate_r_b, gate_i_w, gate_i_b, lam,
           ml_conv_w, ml_conv_b, igate_b, fgate_b, rg_gain, ml_gain, w_out, post_gain,
           pre_mlp_gain, w_up, w_down, post_mlp_gain):
    bsz, seq, d_model = x.shape
    assert w_up.shape[1] % MLP_FF_CHUNK == 0
    d_rg = lam.shape[0]
    d_ml = ml_gain.shape[0]
    head_dim = d_ml // ML_HEADS
    assert head_dim == LANES and ML_CHUNK == LANES and 2 * ML_HEADS == SUBLANES
    assert seq % SEQ_TILE == 0 and SEQ_TILE % ML_CHUNK == 0 and d_rg % MXU_WIDTH == 0
    d_conv = d_rg + 2 * d_ml

    assert d_rg == WEIGHT_BLOCK and d_ml == WEIGHT_BLOCK
    starts = dict(rg_x=0, rg_gate=d_rg, q=2 * d_rg, k=2 * d_rg + d_ml, v=2 * d_rg + 2 * d_ml, o=2 * d_rg + 3 * d_ml)
    w_in_row_starts = tuple(starts[n] for n in ("rg_x", "q", "k", "rg_gate", "v", "o"))
    d_main = 2 * d_rg + 4 * d_ml
    if_w = w_in[:, d_main:]
    w_if = jnp.pad(if_w, ((0, 0), (0, LANES - 2 * ML_HEADS))).astype(BF16)
    b_if = jnp.pad(jnp.concatenate([igate_b, fgate_b]), (0, LANES - 2 * ML_HEADS)).reshape(1, LANES)
    conv_w = jnp.concatenate([rg_conv_w, ml_conv_w], axis=1)
    conv_b = jnp.concatenate([rg_conv_b, ml_conv_b]).reshape(1, d_conv)
    w_gate = jnp.concatenate([_block_diag_gate(gate_r_w), _block_diag_gate(gate_i_w)], axis=2).astype(BF16)

    row = lambda v: v.reshape(1, -1)
    small = (row(pre_gain), w_if, b_if, conv_w, conv_b, w_gate, row(gate_r_b), row(gate_i_b), row(lam),
             row(rg_gain), row(ml_gain), row(post_gain), row(pre_mlp_gain), row(post_mlp_gain))
    big = (jnp.swapaxes(w_in, 0, 1), w_out, w_up, w_down)
    d_ff = w_up.shape[1]
    for w in (w_out, w_up, w_down):
        assert w.shape[0] % WEIGHT_BLOCK == 0 and w.shape[1] % WEIGHT_BLOCK == 0

    tiles_per_seq = seq // SEQ_TILE
    n_tiles = bsz * tiles_per_seq

    def tile_block(t):
        return (t // tiles_per_seq, t % tiles_per_seq, 0)

    x_spec = pl.BlockSpec((1, SEQ_TILE, d_model), lambda s: tile_block(jnp.minimum(s, n_tiles - 1)))
    out_spec = pl.BlockSpec((1, SEQ_TILE, d_model), lambda s: tile_block(jnp.maximum(s - 1, 0)))
    in_specs = ([x_spec] + [_const_spec(op.shape) for op in small]
                + [pl.BlockSpec(memory_space=pl.ANY)] * len(big))
    kern = functools.partial(_layer_kernel, tiles_per_seq=tiles_per_seq, n_tiles=n_tiles, n_small=len(small),
                             w_in_row_starts=w_in_row_starts, d_rg=d_rg, d_ml=d_ml, head_dim=head_dim)
    return pl.pallas_call(
        kern,
        grid=(n_tiles + 1,),
        in_specs=in_specs,
        out_specs=out_spec,
        out_shape=jax.ShapeDtypeStruct(x.shape, x.dtype),
        scratch_shapes=[
            pltpu.VMEM((d_model, d_main), BF16),
            pltpu.VMEM((d_rg + d_ml, d_model), BF16),
            pltpu.VMEM((d_model, d_ff), BF16),
            pltpu.VMEM((d_ff, d_model), BF16),
            pltpu.VMEM((2, WEIGHT_BLOCK, WEIGHT_BLOCK), F32),
            pltpu.SemaphoreType.DMA((2,)),
            pltpu.VMEM((SEQ_TILE, d_model), F32),
            pltpu.VMEM(((d_conv + d_rg) // LANES, SUBLANES + SEQ_TILE, LANES), F32),
            pltpu.VMEM((2 * d_ml // LANES, SEQ_TILE, LANES), F32),
            pltpu.VMEM((d_rg // LANES, SEQ_TILE, LANES), F32),
            pltpu.VMEM((SUBLANES, d_rg), F32),
            pltpu.VMEM((ML_HEADS, head_dim, 2 * head_dim), F32),
            pltpu.VMEM((SUBLANES, ML_CHUNK), F32),
            pltpu.VMEM((7, SUBLANES, ML_CHUNK), F32),
            pltpu.VMEM((SEQ_TILE, d_rg + d_ml), BF16),
        ],
        compiler_params=pltpu.CompilerParams(
            dimension_semantics=("arbitrary",), vmem_limit_bytes=VMEM_LIMIT_BYTES),
        name="layer",
    )(x, *small, *big)


def kernel(x, pre_mix_gain, w_in, rg_conv_w, rg_conv_b, rg_gate_r_w, rg_gate_r_b, rg_gate_i_w, rg_gate_i_b, rg_lambda, ml_conv_w, ml_conv_b, ml_igate_b, ml_fgate_b, rg_out_gain, ml_out_gain, w_out, post_mix_gain, pre_mlp_gain, mlp_w_up, mlp_w_down, post_mlp_gain):
    h = x
    for l in range(w_in.shape[0]):
        h = _layer(h, pre_mix_gain[l], w_in[l], rg_conv_w[l], rg_conv_b[l], rg_gate_r_w[l], rg_gate_r_b[l],
                   rg_gate_i_w[l], rg_gate_i_b[l], rg_lambda[l], ml_conv_w[l], ml_conv_b[l], ml_igate_b[l],
                   ml_fgate_b[l], rg_out_gain[l], ml_out_gain[l], w_out[l], post_mix_gain[l],
                   pre_mlp_gain[l], mlp_w_up[l], mlp_w_down[l], post_mlp_gain[l])
    return h
```

```python
import functools

import jax
import jax.numpy as jnp
from jax.experimental import pallas as pl
from jax.experimental.pallas import tpu as pltpu

F32 = jnp.float32
BF16 = jnp.bfloat16

RG_BLOCKS = 8
RG_C = 8.0
ML_HEADS = 4
CONV_WIDTH = 4
EPS = 1e-6

SUBLANES = 8
LANES = 128
MXU_WIDTH = 256

SEQ_TILE = 512
ML_CHUNK = LANES
PHASES = 4
MLP_FF_CHUNK = 1024
WEIGHT_BLOCK = 512
MLP_PHASES_AFTER = {"start": 0, "w_in": 1, "conv": (0, 1, 0, 0, 1, 0), "rg_gates": 0, "rg_scan": (0, 0, 1),
                    "ml_stage": (1, 0, 0, 0), "w_out": 0}
VMEM_LIMIT_BYTES = 60 * 1024 * 1024


def _rms(x, gain):
    return x * jax.lax.rsqrt(jnp.mean(x * x, axis=-1, keepdims=True) + EPS) * gain


LOG2E = 1.4426950408889634


def _sigmoid(x):
    return 1.0 / (1.0 + jnp.exp2(x * (-LOG2E)))


def _softplus(x):
    return jnp.maximum(x, 0.0) + jnp.log1p(jnp.exp(-jnp.abs(x)))


def _gelu_tanh(x):
    c = 0.7978845608028654
    return 0.5 * x * (1.0 + jnp.tanh(c * (x + 0.044715 * (x * x * x))))


def _lane_scan(x, op, fill):
    n = x.shape[1]
    lane = jax.lax.broadcasted_iota(jnp.int32, x.shape, 1)
    d = 1
    while d < n:
        shifted = jnp.where(lane < d, fill, pltpu.roll(x, d, axis=1))
        x = op(x, shifted)
        d *= 2
    return x


def _affine_row_scan(sa, sb):
    n, width = sa.shape
    row8 = jax.lax.broadcasted_iota(jnp.int32, (SUBLANES, width), 0)
    d = 1
    while d < n:
        if d < SUBLANES:
            ra = pltpu.roll(sa, d, axis=0)
            rb = pltpu.roll(sb, d, axis=0)
            a_sh = jnp.concatenate([jnp.where(row8 < d, 1.0, ra[:SUBLANES]), ra[SUBLANES:]], axis=0)
            b_sh = jnp.concatenate([jnp.where(row8 < d, 0.0, rb[:SUBLANES]), rb[SUBLANES:]], axis=0)
            sb = sb + sa * b_sh
            sa = sa * a_sh
        else:
            sb = jnp.concatenate([sb[:d], sb[d:] + sa[d:] * sb[:n - d]], axis=0)
            sa = jnp.concatenate([sa[:d], sa[d:] * sa[:n - d]], axis=0)
        d *= 2
    return sa, sb


def _mixer_tile(x, seq_start, pre_g_ref, w_main_ref, w_if_ref, b_if_ref, conv_w_ref, conv_b_ref, w_gate_ref,
                b_r_ref, b_i_ref, lam_ref, rg_gain_ref, ml_gain_ref, w_out_ref, post_g_ref,
                cbuf_ref, qk_ref, yrg_ref, h_ref, c_ref, m_ref, row_ref, y_ref,
                *, d_rg, d_ml, head_dim):
    ts = x.shape[0]
    d_conv = d_rg + 2 * d_ml
    n_chunks = ts // ML_CHUNK
    L = ML_CHUNK

    @pl.when(seq_start)
    def _():
        cbuf_ref[:, ts:ts + SUBLANES, :] = jnp.zeros((cbuf_ref.shape[0], SUBLANES, LANES), F32)
        h_ref[...] = jnp.zeros_like(h_ref)
        c_ref[...] = jnp.zeros_like(c_ref)
        m_ref[...] = jnp.zeros_like(m_ref)

    ub = _rms(x, pre_g_ref[...]).astype(BF16)
    proj = jnp.dot(ub, w_main_ref[...], preferred_element_type=F32)
    gates = jnp.dot(ub, w_if_ref[...], preferred_element_type=F32) + b_if_ref[...]
    yield MLP_PHASES_AFTER["w_in"]

    n_cs = d_conv // LANES
    n_rs = d_rg // LANES
    G = ts // PHASES
    cbuf_ref[:n_cs, :SUBLANES, :] = cbuf_ref[:n_cs, ts:ts + SUBLANES, :]
    for j in range(n_cs + n_rs):
        cbuf_ref[j, SUBLANES:, :] = proj[:, j * LANES:(j + 1) * LANES]

    def phase(j, e):
        return cbuf_ref[j, pl.ds(SUBLANES + e, G, stride=PHASES), :]

    conv_ph = [[None] * n_cs for _ in range(PHASES)]
    for j in range(n_cs):
        cols = slice(j * LANES, (j + 1) * LANES)
        taps = {e: phase(j, e) for e in range(1 - CONV_WIDTH, PHASES)}
        for r in range(PHASES):
            acc = taps[r] * conv_w_ref[CONV_WIDTH - 1:CONV_WIDTH, cols] + conv_b_ref[:, cols]
            for k in range(1, CONV_WIDTH):
                acc = acc + taps[r - k] * conv_w_ref[CONV_WIDTH - 1 - k:CONV_WIDTH - k, cols]
            conv_ph[r][j] = acc
        if j >= n_rs:
            scale = head_dim ** -0.5 if j < n_rs + d_ml // LANES else 1.0
            for r in range(PHASES):
                val = conv_ph[r][j]
                val = val * _sigmoid(val)
                qk_ref[j - n_rs, pl.ds(r, G, stride=PHASES), :] = val * scale if scale != 1.0 else val
        if j % 2 == 1:
            yield MLP_PHASES_AFTER["conv"][j // 2]

    xc = jnp.concatenate([jnp.concatenate(conv_ph[r][:n_rs], axis=1) for r in range(PHASES)], axis=0)
    r_parts, i_parts = [], []
    for g in range(d_rg // MXU_WIDTH):
        gg = jnp.dot(xc[:, g * MXU_WIDTH:(g + 1) * MXU_WIDTH].astype(BF16), w_gate_ref[g],
                     preferred_element_type=F32)
        r_parts.append(gg[:, :MXU_WIDTH])
        i_parts.append(gg[:, MXU_WIDTH:])
    r = _sigmoid(jnp.concatenate(r_parts, axis=1) + b_r_ref[...])
    i_gate = _sigmoid(jnp.concatenate(i_parts, axis=1) + b_i_ref[...])
    neg_log_a_unit = RG_C * _softplus(-lam_ref[...])
    a = jnp.exp2(r * (neg_log_a_unit * (-LOG2E)))
    z = jnp.tanh(r * neg_log_a_unit) * (a * a + 1.0)
    b_in = jnp.where(z > 0.0, z * jax.lax.rsqrt(z), 0.0) * (i_gate * xc)
    yield MLP_PHASES_AFTER["rg_gates"]

    comp_a, comp_b = [a[:G]], [b_in[:G]]
    for r in range(1, PHASES):
        a_r, b_r = a[r * G:(r + 1) * G], b_in[r * G:(r + 1) * G]
        comp_b.append(a_r * comp_b[-1] + b_r)
        comp_a.append(a_r * comp_a[-1])
    ga, gb = _affine_row_scan(comp_a[-1], comp_b[-1])
    carry = h_ref[SUBLANES - 1:SUBLANES, :]
    h_end = ga * carry + gb
    h_ref[...] = h_end[G - SUBLANES:]
    rolled = pltpu.roll(h_end, 1, axis=0)
    row8r = jax.lax.broadcasted_iota(jnp.int32, (SUBLANES, d_rg), 0)
    h_prev = jnp.concatenate([jnp.where(row8r < 1, carry, rolled[:SUBLANES]), rolled[SUBLANES:]], axis=0)
    yield MLP_PHASES_AFTER["rg_scan"][0]
    for r in range(PHASES):
        h_r = comp_a[r] * h_prev + comp_b[r]
        gate_r = jnp.concatenate([phase(n_cs + j, r) for j in range(n_rs)], axis=1)
        y_r = _rms(h_r * _gelu_tanh(gate_r), rg_gain_ref[...])
        for j in range(n_rs):
            yrg_ref[j, pl.ds(r, G, stride=PHASES), :] = y_r[:, j * LANES:(j + 1) * LANES]
        if r % 2 == 1:
            yield MLP_PHASES_AFTER["rg_scan"][1 + r // 2]
    for j in range(n_rs):
        y_ref[:, j * LANES:(j + 1) * LANES] = yrg_ref[j].astype(BF16)

    v_all = proj[:, d_conv + d_rg:d_conv + d_rg + d_ml]
    o_all = proj[:, d_conv + d_rg + d_ml:]

    lane_g = jax.lax.broadcasted_iota(jnp.int32, gates.shape, 1)
    log_sig = jnp.minimum(gates, 0.0) - jnp.log1p(jnp.exp(-jnp.abs(gates)))
    gates_t = jnp.where(lane_g < ML_HEADS, gates, log_sig).T[:SUBLANES]

    causal = (jax.lax.broadcasted_iota(jnp.int32, (L, L), 1)
              <= jax.lax.broadcasted_iota(jnp.int32, (L, L), 0))
    ones_ext = jnp.ones((L, head_dim), BF16)

    for c in range(n_chunks):
        rows = slice(c * L, (c + 1) * L)
        li = gates_t[:, rows]
        lf = pltpu.roll(li, ML_HEADS, axis=0)
        bcum = _lane_scan(lf, jnp.add, 0.0)
        b_last = jnp.sum(lf, axis=1, keepdims=True)
        row_b = li - bcum
        cmax = _lane_scan(row_b, jnp.maximum, -jnp.inf)
        w_loc = b_last + row_b
        m_loc = jnp.max(w_loc, axis=1, keepdims=True)
        m_prev_b = m_ref[...]
        m_prev = jnp.max(m_prev_b, axis=1, keepdims=True)
        m_s = jnp.maximum(bcum + m_prev, bcum + cmax)
        m_new = jnp.maximum(b_last + m_prev, m_loc)
        row_ref[0] = bcum - m_s
        row_ref[1] = m_s
        row_ref[2] = row_b
        row_ref[3] = jnp.exp(w_loc - m_loc)
        row_ref[4] = jnp.broadcast_to(jnp.exp(b_last + m_prev - m_new), (SUBLANES, L))
        row_ref[5] = jnp.broadcast_to(jnp.exp(m_loc - m_new), (SUBLANES, L))
        row_ref[6] = m_prev_b
        m_ref[...] = jnp.broadcast_to(m_new, (SUBLANES, L))

        heads = range(ML_HEADS)
        q_f = [qk_ref[h, rows, :] for h in heads]
        k_t = [qk_ref[ML_HEADS + h, rows, :].T for h in heads]
        s = [jnp.dot(q_f[h].astype(BF16), k_t[h].astype(BF16), preferred_element_type=F32) for h in heads]
        yield MLP_PHASES_AFTER["ml_stage"][0]

        lhs, rhs, v_ext, ms_b = [], [], [], []
        for h in heads:
            cols = slice(h * head_dim, (h + 1) * head_dim)
            ca_b = jnp.broadcast_to(row_ref[0, h:h + 1, :], (head_dim, L)).T
            ms_b.append(jnp.broadcast_to(row_ref[1, h:h + 1, :], (head_dim, L)).T)
            p = jnp.exp(jnp.where(causal, ca_b + row_ref[2, h:h + 1, :], -jnp.inf))
            inter_w = jnp.exp(ca_b + row_ref[6, h:h + 1, :])
            v_ext.append(jnp.concatenate([v_all[rows, cols].astype(BF16), ones_ext], axis=1))
            lhs.append(jnp.concatenate([(s[h] * p).astype(BF16), (q_f[h] * inter_w).astype(BF16)], axis=1))
            rhs.append(jnp.concatenate([v_ext[h], c_ref[h].astype(BF16)], axis=0))
        yield MLP_PHASES_AFTER["ml_stage"][1]

        nd = [jnp.dot(lhs[h], rhs[h], preferred_element_type=F32) for h in heads]
        upd = [jnp.dot((k_t[h] * row_ref[3, h:h + 1, :]).astype(BF16), v_ext[h], preferred_element_type=F32)
               for h in heads]
        yield MLP_PHASES_AFTER["ml_stage"][2]

        for h in heads:
            cols = slice(h * head_dim, (h + 1) * head_dim)
            hh = nd[h][:, :head_dim] / jnp.maximum(jnp.abs(nd[h][:, head_dim:]), jnp.exp(-ms_b[h]))
            hh = _sigmoid(o_all[rows, cols]) * hh
            y_ref[rows, d_rg + h * head_dim:d_rg + (h + 1) * head_dim] = _rms(
                hh, ml_gain_ref[:, cols]).astype(BF16)
            s_old = row_ref[4, h:h + 1, :]
            s_new = row_ref[5, h:h + 1, :]
            c_ref[h] = (jnp.concatenate([s_old, s_old], axis=1) * c_ref[h]
                        + jnp.concatenate([s_new, s_new], axis=1) * upd[h])
        yield MLP_PHASES_AFTER["ml_stage"][3]

    mix = jnp.dot(y_ref[...], w_out_ref[...], preferred_element_type=F32)
    yield MLP_PHASES_AFTER["w_out"]
    return x + _rms(mix, post_g_ref[...])


def _mlp_tile(h_ref, pre_g_ref, w_up_ref, w_down_ref, post_g_ref):
    vb = _rms(h_ref[...], pre_g_ref[...]).astype(BF16)
    d_ff = w_up_ref.shape[1]
    acc = jnp.zeros(h_ref.shape, F32)
    for c in range(d_ff // MLP_FF_CHUNK):
        cols = slice(c * MLP_FF_CHUNK, (c + 1) * MLP_FF_CHUNK)
        f = jnp.maximum(jnp.dot(vb, w_up_ref[:, cols], preferred_element_type=F32), 0.0)
        yield
        acc = acc + jnp.dot((f * f).astype(BF16), w_down_ref[cols, :], preferred_element_type=F32)
        yield
    return h_ref[...] + _rms(acc, post_g_ref[...])


def _run(gen):
    while True:
        try:
            next(gen)
        except StopIteration as stop:
            return stop.value


def _interleave(primary, secondary, lead=0, per=1):
    results = [None, None]

    def advance(idx, gen):
        if results[idx] is None:
            try:
                return next(gen)
            except StopIteration as stop:
                results[idx] = (stop.value,)
        return 0

    for _ in range(lead):
        advance(1, secondary)
    while results[0] is None:
        for _ in range(advance(0, primary) * per):
            advance(1, secondary)
    while results[1] is None:
        advance(1, secondary)
    return results[0][0], results[1][0]


def _weight_copy_jobs(hbm, vmem, src_row_starts=None):
    rows, cols = vmem.shape
    jobs = []
    for cb in range(cols // WEIGHT_BLOCK):
        for rb in range(rows // WEIGHT_BLOCK):
            dst = (vmem, rb * WEIGHT_BLOCK, cb * WEIGHT_BLOCK)
            if src_row_starts is None:
                jobs.append((hbm, rb * WEIGHT_BLOCK, cb * WEIGHT_BLOCK) + dst + (False,))
            else:
                jobs.append((hbm, src_row_starts[cb], rb * WEIGHT_BLOCK) + dst + (True,))
    return jobs


def _load_weights(jobs, stage_ref, sem_ref):
    def copy(i):
        src, r0, c0 = jobs[i][:3]
        slot = i % 2
        return pltpu.make_async_copy(
            src.at[pl.ds(r0, WEIGHT_BLOCK), pl.ds(c0, WEIGHT_BLOCK)], stage_ref.at[slot], sem_ref.at[slot])

    copy(0).start()
    for i in range(len(jobs)):
        if i + 1 < len(jobs):
            copy(i + 1).start()
        copy(i).wait()
        dst, r0, c0, transposed = jobs[i][3:]
        block = stage_ref[i % 2]
        dst[r0:r0 + WEIGHT_BLOCK, c0:c0 + WEIGHT_BLOCK] = (block.T if transposed else block).astype(BF16)
        yield


def _layer_kernel(x_ref, *refs, tiles_per_seq, n_tiles, n_small, w_in_row_starts, **dims):
    (pre_g, w_if, b_if, conv_w, conv_b, w_gate, b_r, b_i, lam, rg_gain, ml_gain, post_g,
     mlp_pre_g, mlp_post_g) = refs[:n_small]
    w_in_t_hbm, w_out_hbm, w_up_hbm, w_down_hbm = refs[n_small:n_small + 4]
    out_ref = refs[n_small + 4]
    w_main, w_out, w_up, w_down, stage_ref, sem_ref, h1_ref = refs[n_small + 5:n_small + 12]
    state = refs[n_small + 12:]
    s = pl.program_id(0)

    @pl.when(s == 0)
    def _():
        h1_ref[...] = jnp.zeros_like(h1_ref)
        _run(_load_weights(_weight_copy_jobs(w_in_t_hbm, w_main, w_in_row_starts)
                           + _weight_copy_jobs(w_out_hbm, w_out) + _weight_copy_jobs(w_up_hbm, w_up)
                           + _weight_copy_jobs(w_down_hbm, w_down), stage_ref, sem_ref))

    h1_new, out = _interleave(
        _mixer_tile(x_ref[0], s % tiles_per_seq == 0, pre_g, w_main, w_if, b_if, conv_w, conv_b,
                    w_gate, b_r, b_i, lam, rg_gain, ml_gain, w_out, post_g, *state, **dims),
        _mlp_tile(h1_ref, mlp_pre_g, w_up, w_down, mlp_post_g),
        MLP_PHASES_AFTER["start"])
    out_ref[0] = out
    h1_ref[...] = h1_new


def _const_spec(shape):
    zeros = (0,) * len(shape)
    return pl.BlockSpec(shape, lambda *_: zeros, pipeline_mode=pl.Buffered(1))


def _block_diag_gate(w):
    nb, bd, _ = w.shape
    per = MXU_WIDTH // bd
    w = w.reshape(nb // per, per, bd, bd)
    eye = jnp.eye(per, dtype=w.dtype)
    return jnp.einsum("gpij,pq->gpiqj", w, eye).reshape(nb // per, MXU_WIDTH, MXU_WIDTH)


def _layer(x, pre_gain, w_in, rg_conv_w, rg_conv_b, gate_r_w, gate_r_b, gate_i_w, gate_i_b, lam,
           ml_conv_w, ml_conv_b, igate_b, fgate_b, rg_gain, ml_gain, w_out, post_gain,
           pre_mlp_gain, w_up, w_down, post_mlp_gain):
    bsz, seq, d_model = x.shape
    assert w_up.shape[1] % MLP_FF_CHUNK == 0
    d_rg = lam.shape[0]
    d_ml = ml_gain.shape[0]
    head_dim = d_ml // ML_HEADS
    assert head_dim == LANES and ML_CHUNK == LANES and 2 * ML_HEADS == SUBLANES
    assert seq % SEQ_TILE == 0 and SEQ_TILE % ML_CHUNK == 0 and d_rg % MXU_WIDTH == 0
    d_conv = d_rg + 2 * d_ml

    assert d_rg == WEIGHT_BLOCK and d_ml == WEIGHT_BLOCK
    starts = dict(rg_x=0, rg_gate=d_rg, q=2 * d_rg, k=2 * d_rg + d_ml, v=2 * d_rg + 2 * d_ml, o=2 * d_rg + 3 * d_ml)
    w_in_row_starts = tuple(starts[n] for n in ("rg_x", "q", "k", "rg_gate", "v", "o"))
    d_main = 2 * d_rg + 4 * d_ml
    if_w = w_in[:, d_main:]
    w_if = jnp.pad(if_w, ((0, 0), (0, LANES - 2 * ML_HEADS))).astype(BF16)
    b_if = jnp.pad(jnp.concatenate([igate_b, fgate_b]), (0, LANES - 2 * ML_HEADS)).reshape(1, LANES)
    conv_w = jnp.concatenate([rg_conv_w, ml_conv_w], axis=1)
    conv_b = jnp.concatenate([rg_conv_b, ml_conv_b]).reshape(1, d_conv)
    w_gate = jnp.concatenate([_block_diag_gate(gate_r_w), _block_diag_gate(gate_i_w)], axis=2).astype(BF16)

    row = lambda v: v.reshape(1, -1)
    small = (row(pre_gain), w_if, b_if, conv_w, conv_b, w_gate, row(gate_r_b), row(gate_i_b), row(lam),
             row(rg_gain), row(ml_gain), row(post_gain), row(pre_mlp_gain), row(post_mlp_gain))
    big = (jnp.swapaxes(w_in, 0, 1), w_out, w_up, w_down)
    d_ff = w_up.shape[1]
    for w in (w_out, w_up, w_down):
        assert w.shape[0] % WEIGHT_BLOCK == 0 and w.shape[1] % WEIGHT_BLOCK == 0

    tiles_per_seq = seq // SEQ_TILE
    n_tiles = bsz * tiles_per_seq

    def tile_block(t):
        return (t // tiles_per_seq, t % tiles_per_seq, 0)

    x_spec = pl.BlockSpec((1, SEQ_TILE, d_model), lambda s: tile_block(jnp.minimum(s, n_tiles - 1)))
    out_spec = pl.BlockSpec((1, SEQ_TILE, d_model), lambda s: tile_block(jnp.maximum(s - 1, 0)))
    in_specs = ([x_spec] + [_const_spec(op.shape) for op in small]
                + [pl.BlockSpec(memory_space=pl.ANY)] * len(big))
    kern = functools.partial(_layer_kernel, tiles_per_seq=tiles_per_seq, n_tiles=n_tiles, n_small=len(small),
                             w_in_row_starts=w_in_row_starts, d_rg=d_rg, d_ml=d_ml, head_dim=head_dim)
    return pl.pallas_call(
        kern,
        grid=(n_tiles + 1,),
        in_specs=in_specs,
        out_specs=out_spec,
        out_shape=jax.ShapeDtypeStruct(x.shape, x.dtype),
        scratch_shapes=[
            pltpu.VMEM((d_model, d_main), BF16),
            pltpu.VMEM((d_rg + d_ml, d_model), BF16),
            pltpu.VMEM((d_model, d_ff), BF16),
            pltpu.VMEM((d_ff, d_model), BF16),
            pltpu.VMEM((2, WEIGHT_BLOCK, WEIGHT_BLOCK), F32),
            pltpu.SemaphoreType.DMA((2,)),
            pltpu.VMEM((SEQ_TILE, d_model), F32),
            pltpu.VMEM(((d_conv + d_rg) // LANES, SUBLANES + SEQ_TILE, LANES), F32),
            pltpu.VMEM((2 * d_ml // LANES, SEQ_TILE, LANES), F32),
            pltpu.VMEM((d_rg // LANES, SEQ_TILE, LANES), F32),
            pltpu.VMEM((SUBLANES, d_rg), F32),
            pltpu.VMEM((ML_HEADS, head_dim, 2 * head_dim), F32),
            pltpu.VMEM((SUBLANES, ML_CHUNK), F32),
            pltpu.VMEM((7, SUBLANES, ML_CHUNK), F32),
            pltpu.VMEM((SEQ_TILE, d_rg + d_ml), BF16),
        ],
        compiler_params=pltpu.CompilerParams(
            dimension_semantics=("arbitrary",), vmem_limit_bytes=VMEM_LIMIT_BYTES),
        name="layer",
    )(x, *small, *big)


def kernel(x, pre_mix_gain, w_in, rg_conv_w, rg_conv_b, rg_gate_r_w, rg_gate_r_b, rg_gate_i_w, rg_gate_i_b, rg_lambda, ml_conv_w, ml_conv_b, ml_igate_b, ml_fgate_b, rg_out_gain, ml_out_gain, w_out, post_mix_gain, pre_mlp_gain, mlp_w_up, mlp_w_down, post_mlp_gain):
    h = x
    for l in range(w_in.shape[0]):
        h = _layer(h, pre_mix_gain[l], w_in[l], rg_conv_w[l], rg_conv_b[l], rg_gate_r_w[l], rg_gate_r_b[l],
                   rg_gate_i_w[l], rg_gate_i_b[l], rg_lambda[l], ml_conv_w[l], ml_conv_b[l], ml_igate_b[l],
                   ml_fgate_b[l], rg_out_gain[l], ml_out_gain[l], w_out[l], post_mix_gain[l],
                   pre_mlp_gain[l], mlp_w_up[l], mlp_w_down[l], post_mlp_gain[l])
    return h
```

```python
import functools

import jax
import jax.numpy as jnp
from jax.experimental import pallas as pl
from jax.experimental.pallas import tpu as pltpu

F32 = jnp.float32
BF16 = jnp.bfloat16

RG_BLOCKS = 8
RG_C = 8.0
ML_HEADS = 4
CONV_WIDTH = 4
EPS = 1e-6

SUBLANES = 8
LANES = 128
MXU_WIDTH = 256

SEQ_TILE = 512
ML_CHUNK = LANES
PHASES = 4
MLP_FF_CHUNK = 1024
WEIGHT_BLOCK = 512
MLP_PHASES_AFTER = {"start": 0, "w_in": 1, "conv": (0, 1, 0, 0, 1, 0), "rg_gates": 0, "rg_scan": (1, 0, 0),
                    "ml_stage": (0, 1, 0, 0), "w_out": 0}
VMEM_LIMIT_BYTES = 60 * 1024 * 1024


def _rms(x, gain):
    return x * jax.lax.rsqrt(jnp.mean(x * x, axis=-1, keepdims=True) + EPS) * gain


LOG2E = 1.4426950408889634


def _sigmoid(x):
    return 1.0 / (1.0 + jnp.exp2(x * (-LOG2E)))


def _softplus(x):
    return jnp.maximum(x, 0.0) + jnp.log1p(jnp.exp(-jnp.abs(x)))


def _gelu_tanh(x):
    c = 0.7978845608028654
    return 0.5 * x * (1.0 + jnp.tanh(c * (x + 0.044715 * (x * x * x))))


def _lane_scan(x, op, fill):
    n = x.shape[1]
    lane = jax.lax.broadcasted_iota(jnp.int32, x.shape, 1)
    d = 1
    while d < n:
        shifted = jnp.where(lane < d, fill, pltpu.roll(x, d, axis=1))
        x = op(x, shifted)
        d *= 2
    return x


def _affine_row_scan(sa, sb):
    n, width = sa.shape
    row8 = jax.lax.broadcasted_iota(jnp.int32, (SUBLANES, width), 0)
    d = 1
    while d < n:
        if d < SUBLANES:
            ra = pltpu.roll(sa, d, axis=0)
            rb = pltpu.roll(sb, d, axis=0)
            a_sh = jnp.concatenate([jnp.where(row8 < d, 1.0, ra[:SUBLANES]), ra[SUBLANES:]], axis=0)
            b_sh = jnp.concatenate([jnp.where(row8 < d, 0.0, rb[:SUBLANES]), rb[SUBLANES:]], axis=0)
            sb = sb + sa * b_sh
            sa = sa * a_sh
        else:
            sb = jnp.concatenate([sb[:d], sb[d:] + sa[d:] * sb[:n - d]], axis=0)
            sa = jnp.concatenate([sa[:d], sa[d:] * sa[:n - d]], axis=0)
        d *= 2
    return sa, sb


def _mixer_tile(x, seq_start, pre_g_ref, w_main_ref, w_if_ref, b_if_ref, conv_w_ref, conv_b_ref, w_gate_ref,
                b_r_ref, b_i_ref, lam_ref, rg_gain_ref, ml_gain_ref, w_out_ref, post_g_ref,
                cbuf_ref, qk_ref, yrg_ref, h_ref, c_ref, m_ref, row_ref, y_ref,
                *, d_rg, d_ml, head_dim):
    ts = x.shape[0]
    d_conv = d_rg + 2 * d_ml
    n_chunks = ts // ML_CHUNK
    L = ML_CHUNK

    @pl.when(seq_start)
    def _():
        cbuf_ref[:, ts:ts + SUBLANES, :] = jnp.zeros((cbuf_ref.shape[0], SUBLANES, LANES), F32)
        h_ref[...] = jnp.zeros_like(h_ref)
        c_ref[...] = jnp.zeros_like(c_ref)
        m_ref[...] = jnp.zeros_like(m_ref)

    ub = _rms(x, pre_g_ref[...]).astype(BF16)
    proj = jnp.dot(ub, w_main_ref[...], preferred_element_type=F32)
    gates = jnp.dot(ub, w_if_ref[...], preferred_element_type=F32) + b_if_ref[...]
    yield MLP_PHASES_AFTER["w_in"]

    n_cs = d_conv // LANES
    n_rs = d_rg // LANES
    G = ts // PHASES
    cbuf_ref[:n_cs, :SUBLANES, :] = cbuf_ref[:n_cs, ts:ts + SUBLANES, :]
    for j in range(n_cs + n_rs):
        cbuf_ref[j, SUBLANES:, :] = proj[:, j * LANES:(j + 1) * LANES]

    def phase(j, e):
        return cbuf_ref[j, pl.ds(SUBLANES + e, G, stride=PHASES), :]

    conv_ph = [[None] * n_cs for _ in range(PHASES)]
    for j in range(n_cs):
        cols = slice(j * LANES, (j + 1) * LANES)
        taps = {e: phase(j, e) for e in range(1 - CONV_WIDTH, PHASES)}
        for r in range(PHASES):
            acc = taps[r] * conv_w_ref[CONV_WIDTH - 1:CONV_WIDTH, cols] + conv_b_ref[:, cols]
            for k in range(1, CONV_WIDTH):
                acc = acc + taps[r - k] * conv_w_ref[CONV_WIDTH - 1 - k:CONV_WIDTH - k, cols]
            conv_ph[r][j] = acc
        if j >= n_rs:
            scale = head_dim ** -0.5 if j < n_rs + d_ml // LANES else 1.0
            for r in range(PHASES):
                val = conv_ph[r][j]
                val = val * _sigmoid(val)
                qk_ref[j - n_rs, pl.ds(r, G, stride=PHASES), :] = val * scale if scale != 1.0 else val
        if j % 2 == 1:
            yield MLP_PHASES_AFTER["conv"][j // 2]

    xc = jnp.concatenate([jnp.concatenate(conv_ph[r][:n_rs], axis=1) for r in range(PHASES)], axis=0)
    r_parts, i_parts = [], []
    for g in range(d_rg // MXU_WIDTH):
        gg = jnp.dot(xc[:, g * MXU_WIDTH:(g + 1) * MXU_WIDTH].astype(BF16), w_gate_ref[g],
                     preferred_element_type=F32)
        r_parts.append(gg[:, :MXU_WIDTH])
        i_parts.append(gg[:, MXU_WIDTH:])
    r = _sigmoid(jnp.concatenate(r_parts, axis=1) + b_r_ref[...])
    i_gate = _sigmoid(jnp.concatenate(i_parts, axis=1) + b_i_ref[...])
    neg_log_a_unit = RG_C * _softplus(-lam_ref[...])
    a = jnp.exp2(r * (neg_log_a_unit * (-LOG2E)))
    z = jnp.tanh(r * neg_log_a_unit) * (a * a + 1.0)
    b_in = jnp.where(z > 0.0, z * jax.lax.rsqrt(z), 0.0) * (i_gate * xc)
    yield MLP_PHASES_AFTER["rg_gates"]

    comp_a, comp_b = [a[:G]], [b_in[:G]]
    for r in range(1, PHASES):
        a_r, b_r = a[r * G:(r + 1) * G], b_in[r * G:(r + 1) * G]
        comp_b.append(a_r * comp_b[-1] + b_r)
        comp_a.append(a_r * comp_a[-1])
    ga, gb = _affine_row_scan(comp_a[-1], comp_b[-1])
    carry = h_ref[SUBLANES - 1:SUBLANES, :]
    h_end = ga * carry + gb
    h_ref[...] = h_end[G - SUBLANES:]
    rolled = pltpu.roll(h_end, 1, axis=0)
    row8r = jax.lax.broadcasted_iota(jnp.int32, (SUBLANES, d_rg), 0)
    h_prev = jnp.concatenate([jnp.where(row8r < 1, carry, rolled[:SUBLANES]), rolled[SUBLANES:]], axis=0)
    yield MLP_PHASES_AFTER["rg_scan"][0]
    for r in range(PHASES):
        h_r = comp_a[r] * h_prev + comp_b[r]
        gate_r = jnp.concatenate([phase(n_cs + j, r) for j in range(n_rs)], axis=1)
        y_r = _rms(h_r * _gelu_tanh(gate_r), rg_gain_ref[...])
        for j in range(n_rs):
            yrg_ref[j, pl.ds(r, G, stride=PHASES), :] = y_r[:, j * LANES:(j + 1) * LANES]
        if r % 2 == 1:
            yield MLP_PHASES_AFTER["rg_scan"][1 + r // 2]
    for j in range(n_rs):
        y_ref[:, j * LANES:(j + 1) * LANES] = yrg_ref[j].astype(BF16)

    v_all = proj[:, d_conv + d_rg:d_conv + d_rg + d_ml]
    o_all = proj[:, d_conv + d_rg + d_ml:]

    lane_g = jax.lax.broadcasted_iota(jnp.int32, gates.shape, 1)
    log_sig = jnp.minimum(gates, 0.0) - jnp.log1p(jnp.exp(-jnp.abs(gates)))
    gates_t = jnp.where(lane_g < ML_HEADS, gates, log_sig).T[:SUBLANES]

    causal = (jax.lax.broadcasted_iota(jnp.int32, (L, L), 1)
              <= jax.lax.broadcasted_iota(jnp.int32, (L, L), 0))
    ones_ext = jnp.ones((L, head_dim), BF16)

    for c in range(n_chunks):
        rows = slice(c * L, (c + 1) * L)
        li = gates_t[:, rows]
        lf = pltpu.roll(li, ML_HEADS, axis=0)
        bcum = _lane_scan(lf, jnp.add, 0.0)
        b_last = jnp.sum(lf, axis=1, keepdims=True)
        row_b = li - bcum
        cmax = _lane_scan(row_b, jnp.maximum, -jnp.inf)
        w_loc = b_last + row_b
        m_loc = jnp.max(w_loc, axis=1, keepdims=True)
        m_prev_b = m_ref[...]
        m_prev = jnp.max(m_prev_b, axis=1, keepdims=True)
        m_s = jnp.maximum(bcum + m_prev, bcum + cmax)
        m_new = jnp.maximum(b_last + m_prev, m_loc)
        row_ref[0] = bcum - m_s
        row_ref[1] = m_s
        row_ref[2] = row_b
        row_ref[3] = jnp.exp(w_loc - m_loc)
        row_ref[4] = jnp.broadcast_to(jnp.exp(b_last + m_prev - m_new), (SUBLANES, L))
        row_ref[5] = jnp.broadcast_to(jnp.exp(m_loc - m_new), (SUBLANES, L))
        row_ref[6] = m_prev_b
        m_ref[...] = jnp.broadcast_to(m_new, (SUBLANES, L))

        heads = range(ML_HEADS)
        q_f = [qk_ref[h, rows, :] for h in heads]
        k_t = [qk_ref[ML_HEADS + h, rows, :].T for h in heads]
        s = [jnp.dot(q_f[h].astype(BF16), k_t[h].astype(BF16), preferred_element_type=F32) for h in heads]
        yield MLP_PHASES_AFTER["ml_stage"][0]

        lhs, rhs, v_ext, ms_b = [], [], [], []
        for h in heads:
            cols = slice(h * head_dim, (h + 1) * head_dim)
            ca_b = jnp.broadcast_to(row_ref[0, h:h + 1, :], (head_dim, L)).T
            ms_b.append(jnp.broadcast_to(row_ref[1, h:h + 1, :], (head_dim, L)).T)
            p = jnp.exp(jnp.where(causal, ca_b + row_ref[2, h:h + 1, :], -jnp.inf))
            inter_w = jnp.exp(ca_b + row_ref[6, h:h + 1, :])
            v_ext.append(jnp.concatenate([v_all[rows, cols].astype(BF16), ones_ext], axis=1))
            lhs.append(jnp.concatenate([(s[h] * p).astype(BF16), (q_f[h] * inter_w).astype(BF16)], axis=1))
            rhs.append(jnp.concatenate([v_ext[h], c_ref[h].astype(BF16)], axis=0))
        yield MLP_PHASES_AFTER["ml_stage"][1]

        nd = [jnp.dot(lhs[h], rhs[h], preferred_element_type=F32) for h in heads]
        upd = [jnp.dot((k_t[h] * row_ref[3, h:h + 1, :]).astype(BF16), v_ext[h], preferred_element_type=F32)
               for h in heads]
        yield MLP_PHASES_AFTER["ml_stage"][2]

        for h in heads:
            cols = slice(h * head_dim, (h + 1) * head_dim)
            hh = nd[h][:, :head_dim] / jnp.maximum(jnp.abs(nd[h][:, head_dim:]), jnp.exp(-ms_b[h]))
            hh = _sigmoid(o_all[rows, cols]) * hh
            y_ref[rows, d_rg + h * head_dim:d_rg + (h + 1) * head_dim] = _rms(
                hh, ml_gain_ref[:, cols]).astype(BF16)
            s_old = row_ref[4, h:h + 1, :]
            s_new = row_ref[5, h:h + 1, :]
            c_ref[h] = (jnp.concatenate([s_old, s_old], axis=1) * c_ref[h]
                        + jnp.concatenate([s_new, s_new], axis=1) * upd[h])
        yield MLP_PHASES_AFTER["ml_stage"][3]

    mix = jnp.dot(y_ref[...], w_out_ref[...], preferred_element_type=F32)
    yield MLP_PHASES_AFTER["w_out"]
    return x + _rms(mix, post_g_ref[...])


def _mlp_tile(h_ref, pre_g_ref, w_up_ref, w_down_ref, post_g_ref):
    vb = _rms(h_ref[...], pre_g_ref[...]).astype(BF16)
    d_ff = w_up_ref.shape[1]
    acc = jnp.zeros(h_ref.shape, F32)
    for c in range(d_ff // MLP_FF_CHUNK):
        cols = slice(c * MLP_FF_CHUNK, (c + 1) * MLP_FF_CHUNK)
        f = jnp.maximum(jnp.dot(vb, w_up_ref[:, cols], preferred_element_type=F32), 0.0)
        yield
        acc = acc + jnp.dot((f * f).astype(BF16), w_down_ref[cols, :], preferred_element_type=F32)
        yield
    return h_ref[...] + _rms(acc, post_g_ref[...])


def _run(gen):
    while True:
        try:
            next(gen)
        except StopIteration as stop:
            return stop.value


def _interleave(primary, secondary, lead=0, per=1):
    results = [None, None]

    def advance(idx, gen):
        if results[idx] is None:
            try:
                return next(gen)
            except StopIteration as stop:
                results[idx] = (stop.value,)
        return 0

    for _ in range(lead):
        advance(1, secondary)
    while results[0] is None:
        for _ in range(advance(0, primary) * per):
            advance(1, secondary)
    while results[1] is None:
        advance(1, secondary)
    return results[0][0], results[1][0]


def _weight_copy_jobs(hbm, vmem, src_row_starts=None):
    rows, cols = vmem.shape
    jobs = []
    for cb in range(cols // WEIGHT_BLOCK):
        for rb in range(rows // WEIGHT_BLOCK):
            dst = (vmem, rb * WEIGHT_BLOCK, cb * WEIGHT_BLOCK)
            if src_row_starts is None:
                jobs.append((hbm, rb * WEIGHT_BLOCK, cb * WEIGHT_BLOCK) + dst + (False,))
            else:
                jobs.append((hbm, src_row_starts[cb], rb * WEIGHT_BLOCK) + dst + (True,))
    return jobs


def _load_weights(jobs, stage_ref, sem_ref):
    def copy(i):
        src, r0, c0 = jobs[i][:3]
        slot = i % 2
        return pltpu.make_async_copy(
            src.at[pl.ds(r0, WEIGHT_BLOCK), pl.ds(c0, WEIGHT_BLOCK)], stage_ref.at[slot], sem_ref.at[slot])

    copy(0).start()
    for i in range(len(jobs)):
        if i + 1 < len(jobs):
            copy(i + 1).start()
        copy(i).wait()
        dst, r0, c0, transposed = jobs[i][3:]
        block = stage_ref[i % 2]
        dst[r0:r0 + WEIGHT_BLOCK, c0:c0 + WEIGHT_BLOCK] = (block.T if transposed else block).astype(BF16)
        yield


def _layer_kernel(x_ref, *refs, tiles_per_seq, n_tiles, n_small, w_in_row_starts, **dims):
    (pre_g, w_if, b_if, conv_w, conv_b, w_gate, b_r, b_i, lam, rg_gain, ml_gain, post_g,
     mlp_pre_g, mlp_post_g) = refs[:n_small]
    w_in_t_hbm, w_out_hbm, w_up_hbm, w_down_hbm = refs[n_small:n_small + 4]
    out_ref = refs[n_small + 4]
    w_main, w_out, w_up, w_down, stage_ref, sem_ref, h1_ref = refs[n_small + 5:n_small + 12]
    state = refs[n_small + 12:]
    s = pl.program_id(0)

    @pl.when(s == 0)
    def _():
        h1_ref[...] = jnp.zeros_like(h1_ref)
        _run(_load_weights(_weight_copy_jobs(w_in_t_hbm, w_main, w_in_row_starts)
                           + _weight_copy_jobs(w_out_hbm, w_out) + _weight_copy_jobs(w_up_hbm, w_up)
                           + _weight_copy_jobs(w_down_hbm, w_down), stage_ref, sem_ref))

    h1_new, out = _interleave(
        _mixer_tile(x_ref[0], s % tiles_per_seq == 0, pre_g, w_main, w_if, b_if, conv_w, conv_b,
                    w_gate, b_r, b_i, lam, rg_gain, ml_gain, w_out, post_g, *state, **dims),
        _mlp_tile(h1_ref, mlp_pre_g, w_up, w_down, mlp_post_g),
        MLP_PHASES_AFTER["start"])
    out_ref[0] = out
    h1_ref[...] = h1_new


def _const_spec(shape):
    zeros = (0,) * len(shape)
    return pl.BlockSpec(shape, lambda *_: zeros, pipeline_mode=pl.Buffered(1))


def _block_diag_gate(w):
    nb, bd, _ = w.shape
    per = MXU_WIDTH // bd
    w = w.reshape(nb // per, per, bd, bd)
    eye = jnp.eye(per, dtype=w.dtype)
    return jnp.einsum("gpij,pq->gpiqj", w, eye).reshape(nb // per, MXU_WIDTH, MXU_WIDTH)


def _layer(x, pre_gain, w_in, rg_conv_w, rg_conv_b, gate_r_w, gate_r_b, gate_i_w, gate_i_b, lam,
           ml_conv_w, ml_conv_b, igate_b, fgate_b, rg_gain, ml_gain, w_out, post_gain,
           pre_mlp_gain, w_up, w_down, post_mlp_gain):
    bsz, seq, d_model = x.shape
    assert w_up.shape[1] % MLP_FF_CHUNK == 0
    d_rg = lam.shape[0]
    d_ml = ml_gain.shape[0]
    head_dim = d_ml // ML_HEADS
    assert head_dim == LANES and ML_CHUNK == LANES and 2 * ML_HEADS == SUBLANES
    assert seq % SEQ_TILE == 0 and SEQ_TILE % ML_CHUNK == 0 and d_rg % MXU_WIDTH == 0
    d_conv = d_rg + 2 * d_ml

    assert d_rg == WEIGHT_BLOCK and d_ml == WEIGHT_BLOCK
    starts = dict(rg_x=0, rg_gate=d_rg, q=2 * d_rg, k=2 * d_rg + d_ml, v=2 * d_rg + 2 * d_ml, o=2 * d_rg + 3 * d_ml)
    w_in_row_starts = tuple(starts[n] for n in ("rg_x", "q", "k", "rg_gate", "v", "o"))
    d_main = 2 * d_rg + 4 * d_ml
    if_w = w_in[:, d_main:]
    w_if = jnp.pad(if_w, ((0, 0), (0, LANES - 2 * ML_HEADS))).astype(BF16)
    b_if = jnp.pad(jnp.concatenate([igate_b, fgate_b]), (0, LANES - 2 * ML_HEADS)).reshape(1, LANES)
    conv_w = jnp.concatenate([rg_conv_w, ml_conv_w], axis=1)
    conv_b = jnp.concatenate([rg_conv_b, ml_conv_b]).reshape(1, d_conv)
    w_gate = jnp.concatenate([_block_diag_gate(gate_r_w), _block_diag_gate(gate_i_w)], axis=2).astype(BF16)

    row = lambda v: v.reshape(1, -1)
    small = (row(pre_gain), w_if, b_if, conv_w, conv_b, w_gate, row(gate_r_b), row(gate_i_b), row(lam),
             row(rg_gain), row(ml_gain), row(post_gain), row(pre_mlp_gain), row(post_mlp_gain))
    big = (jnp.swapaxes(w_in, 0, 1), w_out, w_up, w_down)
    d_ff = w_up.shape[1]
    for w in (w_out, w_up, w_down):
        assert w.shape[0] % WEIGHT_BLOCK == 0 and w.shape[1] % WEIGHT_BLOCK == 0

    tiles_per_seq = seq // SEQ_TILE
    n_tiles = bsz * tiles_per_seq

    def tile_block(t):
        return (t // tiles_per_seq, t % tiles_per_seq, 0)

    x_spec = pl.BlockSpec((1, SEQ_TILE, d_model), lambda s: tile_block(jnp.minimum(s, n_tiles - 1)))
    out_spec = pl.BlockSpec((1, SEQ_TILE, d_model), lambda s: tile_block(jnp.maximum(s - 1, 0)))
    in_specs = ([x_spec] + [_const_spec(op.shape) for op in small]
                + [pl.BlockSpec(memory_space=pl.ANY)] * len(big))
    kern = functools.partial(_layer_kernel, tiles_per_seq=tiles_per_seq, n_tiles=n_tiles, n_small=len(small),
                             w_in_row_starts=w_in_row_starts, d_rg=d_rg, d_ml=d_ml, head_dim=head_dim)
    return pl.pallas_call(
        kern,
        grid=(n_tiles + 1,),
        in_specs=in_specs,
        out_specs=out_spec,
        out_shape=jax.ShapeDtypeStruct(x.shape, x.dtype),
        scratch_shapes=[
            pltpu.VMEM((d_model, d_main), BF16),
            pltpu.VMEM((d_rg + d_ml, d_model), BF16),
            pltpu.VMEM((d_model, d_ff), BF16),
            pltpu.VMEM((d_ff, d_model), BF16),
            pltpu.VMEM((2, WEIGHT_BLOCK, WEIGHT_BLOCK), F32),
            pltpu.SemaphoreType.DMA((2,)),
            pltpu.VMEM((SEQ_TILE, d_model), F32),
            pltpu.VMEM(((d_conv + d_rg) // LANES, SUBLANES + SEQ_TILE, LANES), F32),
            pltpu.VMEM((2 * d_ml // LANES, SEQ_TILE, LANES), F32),
            pltpu.VMEM((d_rg // LANES, SEQ_TILE, LANES), F32),
            pltpu.VMEM((SUBLANES, d_rg), F32),
            pltpu.VMEM((ML_HEADS, head_dim, 2 * head_dim), F32),
            pltpu.VMEM((SUBLANES, ML_CHUNK), F32),
            pltpu.VMEM((7, SUBLANES, ML_CHUNK), F32),
            pltpu.VMEM((SEQ_TILE, d_rg + d_ml), BF16),
        ],
        compiler_params=pltpu.CompilerParams(
            dimension_semantics=("arbitrary",), vmem_limit_bytes=VMEM_LIMIT_BYTES),
        name="layer",
    )(x, *small, *big)


def kernel(x, pre_mix_gain, w_in, rg_conv_w, rg_conv_b, rg_gate_r_w, rg_gate_r_b, rg_gate_i_w, rg_gate_i_b, rg_lambda, ml_conv_w, ml_conv_b, ml_igate_b, ml_fgate_b, rg_out_gain, ml_out_gain, w_out, post_mix_gain, pre_mlp_gain, mlp_w_up, mlp_w_down, post_mlp_gain):
    h = x
    for l in range(w_in.shape[0]):
        h = _layer(h, pre_mix_gain[l], w_in[l], rg_conv_w[l], rg_conv_b[l], rg_gate_r_w[l], rg_gate_r_b[l],
                   rg_gate_i_w[l], rg_gate_i_b[l], rg_lambda[l], ml_conv_w[l], ml_conv_b[l], ml_igate_b[l],
                   ml_fgate_b[l], rg_out_gain[l], ml_out_gain[l], w_out[l], post_mix_gain[l],
                   pre_mlp_gain[l], mlp_w_up[l], mlp_w_down[l], post_mlp_gain[l])
    return h
```

```python
import functools

import jax
import jax.numpy as jnp
from jax.experimental import pallas as pl
from jax.experimental.pallas import tpu as pltpu

F32 = jnp.float32
BF16 = jnp.bfloat16

RG_BLOCKS = 8
RG_C = 8.0
ML_HEADS = 4
CONV_WIDTH = 4
EPS = 1e-6

SUBLANES = 8
LANES = 128
MXU_WIDTH = 256

SEQ_TILE = 512
ML_CHUNK = LANES
PHASES = 4
MLP_FF_CHUNK = 1024
WEIGHT_BLOCK = 512
MLP_PHASES_AFTER = {"start": 0, "w_in": 1, "conv": (0, 1, 0, 0, 1, 0), "rg_gates": 0, "rg_scan": (1, 0, 0),
                    "ml_stage": (1, 0, 0, 0), "w_out": 0}
VMEM_LIMIT_BYTES = 60 * 1024 * 1024


def _rms(x, gain):
    return x * jax.lax.rsqrt(jnp.mean(x * x, axis=-1, keepdims=True) + EPS) * gain


LOG2E = 1.4426950408889634


def _sigmoid(x):
    return 1.0 / (1.0 + jnp.exp2(x * (-LOG2E)))


def _softplus(x):
    return jnp.maximum(x, 0.0) + jnp.log1p(jnp.exp(-jnp.abs(x)))


def _gelu_tanh(x):
    c = 0.7978845608028654
    return 0.5 * x * (1.0 + jnp.tanh(c * (x + 0.044715 * (x * x * x))))


def _lane_scan(x, op, fill):
    n = x.shape[1]
    lane = jax.lax.broadcasted_iota(jnp.int32, x.shape, 1)
    d = 1
    while d < n:
        shifted = jnp.where(lane < d, fill, pltpu.roll(x, d, axis=1))
        x = op(x, shifted)
        d *= 2
    return x


def _affine_row_scan(sa, sb):
    n, width = sa.shape
    row8 = jax.lax.broadcasted_iota(jnp.int32, (SUBLANES, width), 0)
    d = 1
    while d < n:
        if d < SUBLANES:
            ra = pltpu.roll(sa, d, axis=0)
            rb = pltpu.roll(sb, d, axis=0)
            a_sh = jnp.concatenate([jnp.where(row8 < d, 1.0, ra[:SUBLANES]), ra[SUBLANES:]], axis=0)
            b_sh = jnp.concatenate([jnp.where(row8 < d, 0.0, rb[:SUBLANES]), rb[SUBLANES:]], axis=0)
            sb = sb + sa * b_sh
            sa = sa * a_sh
        else:
            sb = jnp.concatenate([sb[:d], sb[d:] + sa[d:] * sb[:n - d]], axis=0)
            sa = jnp.concatenate([sa[:d], sa[d:] * sa[:n - d]], axis=0)
        d *= 2
    return sa, sb


def _mixer_tile(x, seq_start, pre_g_ref, w_main_ref, w_if_ref, b_if_ref, conv_w_ref, conv_b_ref, w_gate_ref,
                b_r_ref, b_i_ref, lam_ref, rg_gain_ref, ml_gain_ref, w_out_ref, post_g_ref,
                cbuf_ref, qk_ref, yrg_ref, h_ref, c_ref, m_ref, row_ref, y_ref,
                *, d_rg, d_ml, head_dim):
    ts = x.shape[0]
    d_conv = d_rg + 2 * d_ml
    n_chunks = ts // ML_CHUNK
    L = ML_CHUNK

    @pl.when(seq_start)
    def _():
        cbuf_ref[:, ts:ts + SUBLANES, :] = jnp.zeros((cbuf_ref.shape[0], SUBLANES, LANES), F32)
        h_ref[...] = jnp.zeros_like(h_ref)
        c_ref[...] = jnp.zeros_like(c_ref)
        m_ref[...] = jnp.zeros_like(m_ref)

    ub = _rms(x, pre_g_ref[...]).astype(BF16)
    proj = jnp.dot(ub, w_main_ref[...], preferred_element_type=F32)
    gates = jnp.dot(ub, w_if_ref[...], preferred_element_type=F32) + b_if_ref[...]
    yield MLP_PHASES_AFTER["w_in"]

    n_cs = d_conv // LANES
    n_rs = d_rg // LANES
    G = ts // PHASES
    cbuf_ref[:n_cs, :SUBLANES, :] = cbuf_ref[:n_cs, ts:ts + SUBLANES, :]
    for j in range(n_cs + n_rs):
        cbuf_ref[j, SUBLANES:, :] = proj[:, j * LANES:(j + 1) * LANES]

    def phase(j, e):
        return cbuf_ref[j, pl.ds(SUBLANES + e, G, stride=PHASES), :]

    conv_ph = [[None] * n_cs for _ in range(PHASES)]
    for j in range(n_cs):
        cols = slice(j * LANES, (j + 1) * LANES)
        taps = {e: phase(j, e) for e in range(1 - CONV_WIDTH, PHASES)}
        for r in range(PHASES):
            acc = taps[r] * conv_w_ref[CONV_WIDTH - 1:CONV_WIDTH, cols] + conv_b_ref[:, cols]
            for k in range(1, CONV_WIDTH):
                acc = acc + taps[r - k] * conv_w_ref[CONV_WIDTH - 1 - k:CONV_WIDTH - k, cols]
            conv_ph[r][j] = acc
        if j >= n_rs:
            scale = head_dim ** -0.5 if j < n_rs + d_ml // LANES else 1.0
            for r in range(PHASES):
                val = conv_ph[r][j]
                val = val * _sigmoid(val)
                qk_ref[j - n_rs, pl.ds(r, G, stride=PHASES), :] = val * scale if scale != 1.0 else val
        if j % 2 == 1:
            yield MLP_PHASES_AFTER["conv"][j // 2]

    xc = jnp.concatenate([jnp.concatenate(conv_ph[r][:n_rs], axis=1) for r in range(PHASES)], axis=0)
    r_parts, i_parts = [], []
    for g in range(d_rg // MXU_WIDTH):
        gg = jnp.dot(xc[:, g * MXU_WIDTH:(g + 1) * MXU_WIDTH].astype(BF16), w_gate_ref[g],
                     preferred_element_type=F32)
        r_parts.append(gg[:, :MXU_WIDTH])
        i_parts.append(gg[:, MXU_WIDTH:])
    r = _sigmoid(jnp.concatenate(r_parts, axis=1) + b_r_ref[...])
    i_gate = _sigmoid(jnp.concatenate(i_parts, axis=1) + b_i_ref[...])
    neg_log_a_unit = RG_C * _softplus(-lam_ref[...])
    a = jnp.exp2(r * (neg_log_a_unit * (-LOG2E)))
    z = jnp.tanh(r * neg_log_a_unit) * (a * a + 1.0)
    b_in = jnp.where(z > 0.0, z * jax.lax.rsqrt(z), 0.0) * (i_gate * xc)
    yield MLP_PHASES_AFTER["rg_gates"]

    comp_a, comp_b = [a[:G]], [b_in[:G]]
    for r in range(1, PHASES):
        a_r, b_r = a[r * G:(r + 1) * G], b_in[r * G:(r + 1) * G]
        comp_b.append(a_r * comp_b[-1] + b_r)
        comp_a.append(a_r * comp_a[-1])
    ga, gb = _affine_row_scan(comp_a[-1], comp_b[-1])
    carry = h_ref[SUBLANES - 1:SUBLANES, :]
    h_end = ga * carry + gb
    h_ref[...] = h_end[G - SUBLANES:]
    rolled = pltpu.roll(h_end, 1, axis=0)
    row8r = jax.lax.broadcasted_iota(jnp.int32, (SUBLANES, d_rg), 0)
    h_prev = jnp.concatenate([jnp.where(row8r < 1, carry, rolled[:SUBLANES]), rolled[SUBLANES:]], axis=0)
    yield MLP_PHASES_AFTER["rg_scan"][0]
    for r in range(PHASES):
        h_r = comp_a[r] * h_prev + comp_b[r]
        gate_r = jnp.concatenate([phase(n_cs + j, r) for j in range(n_rs)], axis=1)
        y_r = _rms(h_r * _gelu_tanh(gate_r), rg_gain_ref[...])
        for j in range(n_rs):
            yrg_ref[j, pl.ds(r, G, stride=PHASES), :] = y_r[:, j * LANES:(j + 1) * LANES]
        if r % 2 == 1:
            yield MLP_PHASES_AFTER["rg_scan"][1 + r // 2]
    for j in range(n_rs):
        y_ref[:, j * LANES:(j + 1) * LANES] = yrg_ref[j].astype(BF16)

    v_all = proj[:, d_conv + d_rg:d_conv + d_rg + d_ml]
    o_all = proj[:, d_conv + d_rg + d_ml:]

    lane_g = jax.lax.broadcasted_iota(jnp.int32, gates.shape, 1)
    log_sig = jnp.minimum(gates, 0.0) - jnp.log1p(jnp.exp(-jnp.abs(gates)))
    gates_t = jnp.where(lane_g < ML_HEADS, gates, log_sig).T[:SUBLANES]

    causal = (jax.lax.broadcasted_iota(jnp.int32, (L, L), 1)
              <= jax.lax.broadcasted_iota(jnp.int32, (L, L), 0))
    ones_ext = jnp.ones((L, head_dim), BF16)

    for c in range(n_chunks):
        rows = slice(c * L, (c + 1) * L)
        li = gates_t[:, rows]
        lf = pltpu.roll(li, ML_HEADS, axis=0)
        bcum = _lane_scan(lf, jnp.add, 0.0)
        b_last = jnp.sum(lf, axis=1, keepdims=True)
        row_b = li - bcum
        cmax = _lane_scan(row_b, jnp.maximum, -jnp.inf)
        w_loc = b_last + row_b
        m_loc = jnp.max(w_loc, axis=1, keepdims=True)
        m_prev_b = m_ref[...]
        m_prev = jnp.max(m_prev_b, axis=1, keepdims=True)
        m_s = jnp.maximum(bcum + m_prev, bcum + cmax)
        m_new = jnp.maximum(b_last + m_prev, m_loc)
        row_ref[0] = bcum - m_s
        row_ref[1] = jnp.exp(-m_s)
        row_ref[2] = row_b
        row_ref[3] = jnp.exp(w_loc - m_loc)
        row_ref[4] = jnp.broadcast_to(jnp.exp(b_last + m_prev - m_new), (SUBLANES, L))
        row_ref[5] = jnp.broadcast_to(jnp.exp(m_loc - m_new), (SUBLANES, L))
        row_ref[6] = jnp.exp(bcum + m_prev - m_s)
        m_ref[...] = jnp.broadcast_to(m_new, (SUBLANES, L))

        heads = range(ML_HEADS)
        q_f = [qk_ref[h, rows, :] for h in heads]
        k_t = [qk_ref[ML_HEADS + h, rows, :].T for h in heads]
        s = [jnp.dot(q_f[h].astype(BF16), k_t[h].astype(BF16), preferred_element_type=F32) for h in heads]
        yield MLP_PHASES_AFTER["ml_stage"][0]

        lhs, rhs, v_ext, em_b = [], [], [], []
        for h in heads:
            cols = slice(h * head_dim, (h + 1) * head_dim)
            ca_b = jnp.broadcast_to(row_ref[0, h:h + 1, :], (head_dim, L)).T
            em_b.append(jnp.broadcast_to(row_ref[1, h:h + 1, :], (head_dim, L)).T)
            inter_w = jnp.broadcast_to(row_ref[6, h:h + 1, :], (head_dim, L)).T
            p = jnp.exp(jnp.where(causal, ca_b + row_ref[2, h:h + 1, :], -jnp.inf))
            v_ext.append(jnp.concatenate([v_all[rows, cols].astype(BF16), ones_ext], axis=1))
            lhs.append(jnp.concatenate([(s[h] * p).astype(BF16), (q_f[h] * inter_w).astype(BF16)], axis=1))
            rhs.append(jnp.concatenate([v_ext[h], c_ref[h].astype(BF16)], axis=0))
        yield MLP_PHASES_AFTER["ml_stage"][1]

        nd = [jnp.dot(lhs[h], rhs[h], preferred_element_type=F32) for h in heads]
        upd = [jnp.dot((k_t[h] * row_ref[3, h:h + 1, :]).astype(BF16), v_ext[h], preferred_element_type=F32)
               for h in heads]
        yield MLP_PHASES_AFTER["ml_stage"][2]

        for h in heads:
            cols = slice(h * head_dim, (h + 1) * head_dim)
            hh = nd[h][:, :head_dim] / jnp.maximum(jnp.abs(nd[h][:, head_dim:]), em_b[h])
            hh = _sigmoid(o_all[rows, cols]) * hh
            y_ref[rows, d_rg + h * head_dim:d_rg + (h + 1) * head_dim] = _rms(
                hh, ml_gain_ref[:, cols]).astype(BF16)
            s_old = row_ref[4, h:h + 1, :]
            s_new = row_ref[5, h:h + 1, :]
            c_ref[h] = (jnp.concatenate([s_old, s_old], axis=1) * c_ref[h]
                        + jnp.concatenate([s_new, s_new], axis=1) * upd[h])
        yield MLP_PHASES_AFTER["ml_stage"][3]

    mix = jnp.dot(y_ref[...], w_out_ref[...], preferred_element_type=F32)
    yield MLP_PHASES_AFTER["w_out"]
    return x + _rms(mix, post_g_ref[...])


def _mlp_tile(h_ref, pre_g_ref, w_up_ref, w_down_ref, post_g_ref):
    vb = _rms(h_ref[...], pre_g_ref[...]).astype(BF16)
    d_ff = w_up_ref.shape[1]
    acc = jnp.zeros(h_ref.shape, F32)
    for c in range(d_ff // MLP_FF_CHUNK):
        cols = slice(c * MLP_FF_CHUNK, (c + 1) * MLP_FF_CHUNK)
        f = jnp.maximum(jnp.dot(vb, w_up_ref[:, cols], preferred_element_type=F32), 0.0)
        yield
        acc = acc + jnp.dot((f * f).astype(BF16), w_down_ref[cols, :], preferred_element_type=F32)
        yield
    return h_ref[...] + _rms(acc, post_g_ref[...])


def _run(gen):
    while True:
        try:
            next(gen)
        except StopIteration as stop:
            return stop.value


def _interleave(primary, secondary, lead=0, per=1):
    results = [None, None]

    def advance(idx, gen):
        if results[idx] is None:
            try:
                return next(gen)
            except StopIteration as stop:
                results[idx] = (stop.value,)
        return 0

    for _ in range(lead):
        advance(1, secondary)
    while results[0] is None:
        for _ in range(advance(0, primary) * per):
            advance(1, secondary)
    while results[1] is None:
        advance(1, secondary)
    return results[0][0], results[1][0]


def _weight_copy_jobs(hbm, vmem, src_row_starts=None):
    rows, cols = vmem.shape
    jobs = []
    for cb in range(cols // WEIGHT_BLOCK):
        for rb in range(rows // WEIGHT_BLOCK):
            dst = (vmem, rb * WEIGHT_BLOCK, cb * WEIGHT_BLOCK)
            if src_row_starts is None:
                jobs.append((hbm, rb * WEIGHT_BLOCK, cb * WEIGHT_BLOCK) + dst + (False,))
            else:
                jobs.append((hbm, src_row_starts[cb], rb * WEIGHT_BLOCK) + dst + (True,))
    return jobs


def _load_weights(jobs, stage_ref, sem_ref):
    def copy(i):
        src, r0, c0 = jobs[i][:3]
        slot = i % 2
        return pltpu.make_async_copy(
            src.at[pl.ds(r0, WEIGHT_BLOCK), pl.ds(c0, WEIGHT_BLOCK)], stage_ref.at[slot], sem_ref.at[slot])

    copy(0).start()
    for i in range(len(jobs)):
        if i + 1 < len(jobs):
            copy(i + 1).start()
        copy(i).wait()
        dst, r0, c0, transposed = jobs[i][3:]
        block = stage_ref[i % 2]
        dst[r0:r0 + WEIGHT_BLOCK, c0:c0 + WEIGHT_BLOCK] = (block.T if transposed else block).astype(BF16)
        yield


def _layer_kernel(x_ref, *refs, tiles_per_seq, n_tiles, n_small, w_in_row_starts, **dims):
    (pre_g, w_if, b_if, conv_w, conv_b, w_gate, b_r, b_i, lam, rg_gain, ml_gain, post_g,
     mlp_pre_g, mlp_post_g) = refs[:n_small]
    w_in_t_hbm, w_out_hbm, w_up_hbm, w_down_hbm = refs[n_small:n_small + 4]
    out_ref = refs[n_small + 4]
    w_main, w_out, w_up, w_down, stage_ref, sem_ref, h1_ref = refs[n_small + 5:n_small + 12]
    state = refs[n_small + 12:]
    s = pl.program_id(0)

    @pl.when(s == 0)
    def _():
        h1_ref[...] = jnp.zeros_like(h1_ref)
        _run(_load_weights(_weight_copy_jobs(w_in_t_hbm, w_main, w_in_row_starts)
                           + _weight_copy_jobs(w_out_hbm, w_out) + _weight_copy_jobs(w_up_hbm, w_up)
                           + _weight_copy_jobs(w_down_hbm, w_down), stage_ref, sem_ref))

    h1_new, out = _interleave(
        _mixer_tile(x_ref[0], s % tiles_per_seq == 0, pre_g, w_main, w_if, b_if, conv_w, conv_b,
                    w_gate, b_r, b_i, lam, rg_gain, ml_gain, w_out, post_g, *state, **dims),
        _mlp_tile(h1_ref, mlp_pre_g, w_up, w_down, mlp_post_g),
        MLP_PHASES_AFTER["start"])
    out_ref[0] = out
    h1_ref[...] = h1_new


def _const_spec(shape):
    zeros = (0,) * len(shape)
    return pl.BlockSpec(shape, lambda *_: zeros, pipeline_mode=pl.Buffered(1))


def _block_diag_gate(w):
    nb, bd, _ = w.shape
    per = MXU_WIDTH // bd
    w = w.reshape(nb // per, per, bd, bd)
    eye = jnp.eye(per, dtype=w.dtype)
    return jnp.einsum("gpij,pq->gpiqj", w, eye).reshape(nb // per, MXU_WIDTH, MXU_WIDTH)


def _layer(x, pre_gain, w_in, rg_conv_w, rg_conv_b, gate_r_w, gate_r_b, gate_i_w, gate_i_b, lam,
           ml_conv_w, ml_conv_b, igate_b, fgate_b, rg_gain, ml_gain, w_out, post_gain,
           pre_mlp_gain, w_up, w_down, post_mlp_gain):
    bsz, seq, d_model = x.shape
    assert w_up.shape[1] % MLP_FF_CHUNK == 0
    d_rg = lam.shape[0]
    d_ml = ml_gain.shape[0]
    head_dim = d_ml // ML_HEADS
    assert head_dim == LANES and ML_CHUNK == LANES and 2 * ML_HEADS == SUBLANES
    assert seq % SEQ_TILE == 0 and SEQ_TILE % ML_CHUNK == 0 and d_rg % MXU_WIDTH == 0
    d_conv = d_rg + 2 * d_ml

    assert d_rg == WEIGHT_BLOCK and d_ml == WEIGHT_BLOCK
    starts = dict(rg_x=0, rg_gate=d_rg, q=2 * d_rg, k=2 * d_rg + d_ml, v=2 * d_rg + 2 * d_ml, o=2 * d_rg + 3 * d_ml)
    w_in_row_starts = tuple(starts[n] for n in ("rg_x", "q", "k", "rg_gate", "v", "o"))
    d_main = 2 * d_rg + 4 * d_ml
    if_w = w_in[:, d_main:]
    w_if = jnp.pad(if_w, ((0, 0), (0, LANES - 2 * ML_HEADS))).astype(BF16)
    b_if = jnp.pad(jnp.concatenate([igate_b, fgate_b]), (0, LANES - 2 * ML_HEADS)).reshape(1, LANES)
    conv_w = jnp.concatenate([rg_conv_w, ml_conv_w], axis=1)
    conv_b = jnp.concatenate([rg_conv_b, ml_conv_b]).reshape(1, d_conv)
    w_gate = jnp.concatenate([_block_diag_gate(gate_r_w), _block_diag_gate(gate_i_w)], axis=2).astype(BF16)

    row = lambda v: v.reshape(1, -1)
    small = (row(pre_gain), w_if, b_if, conv_w, conv_b, w_gate, row(gate_r_b), row(gate_i_b), row(lam),
             row(rg_gain), row(ml_gain), row(post_gain), row(pre_mlp_gain), row(post_mlp_gain))
    big = (jnp.swapaxes(w_in, 0, 1), w_out, w_up, w_down)
    d_ff = w_up.shape[1]
    for w in (w_out, w_up, w_down):
        assert w.shape[0] % WEIGHT_BLOCK == 0 and w.shape[1] % WEIGHT_BLOCK == 0

    tiles_per_seq = seq // SEQ_TILE
    n_tiles = bsz * tiles_per_seq

    def tile_block(t):
        return (t // tiles_per_seq, t % tiles_per_seq, 0)

    x_spec = pl.BlockSpec((1, SEQ_TILE, d_model), lambda s: tile_block(jnp.minimum(s, n_tiles - 1)))
    out_spec = pl.BlockSpec((1, SEQ_TILE, d_model), lambda s: tile_block(jnp.maximum(s - 1, 0)))
    in_specs = ([x_spec] + [_const_spec(op.shape) for op in small]
                + [pl.BlockSpec(memory_space=pl.ANY)] * len(big))
    kern = functools.partial(_layer_kernel, tiles_per_seq=tiles_per_seq, n_tiles=n_tiles, n_small=len(small),
                             w_in_row_starts=w_in_row_starts, d_rg=d_rg, d_ml=d_ml, head_dim=head_dim)
    return pl.pallas_call(
        kern,
        grid=(n_tiles + 1,),
        in_specs=in_specs,
        out_specs=out_spec,
        out_shape=jax.ShapeDtypeStruct(x.shape, x.dtype),
        scratch_shapes=[
            pltpu.VMEM((d_model, d_main), BF16),
            pltpu.VMEM((d_rg + d_ml, d_model), BF16),
            pltpu.VMEM((d_model, d_ff), BF16),
            pltpu.VMEM((d_ff, d_model), BF16),
            pltpu.VMEM((2, WEIGHT_BLOCK, WEIGHT_BLOCK), F32),
            pltpu.SemaphoreType.DMA((2,)),
            pltpu.VMEM((SEQ_TILE, d_model), F32),
            pltpu.VMEM(((d_conv + d_rg) // LANES, SUBLANES + SEQ_TILE, LANES), F32),
            pltpu.VMEM((2 * d_ml // LANES, SEQ_TILE, LANES), F32),
            pltpu.VMEM((d_rg // LANES, SEQ_TILE, LANES), F32),
            pltpu.VMEM((SUBLANES, d_rg), F32),
            pltpu.VMEM((ML_HEADS, head_dim, 2 * head_dim), F32),
            pltpu.VMEM((SUBLANES, ML_CHUNK), F32),
            pltpu.VMEM((7, SUBLANES, ML_CHUNK), F32),
            pltpu.VMEM((SEQ_TILE, d_rg + d_ml), BF16),
        ],
        compiler_params=pltpu.CompilerParams(
            dimension_semantics=("arbitrary",), vmem_limit_bytes=VMEM_LIMIT_BYTES),
        name="layer",
    )(x, *small, *big)


def kernel(x, pre_mix_gain, w_in, rg_conv_w, rg_conv_b, rg_gate_r_w, rg_gate_r_b, rg_gate_i_w, rg_gate_i_b, rg_lambda, ml_conv_w, ml_conv_b, ml_igate_b, ml_fgate_b, rg_out_gain, ml_out_gain, w_out, post_mix_gain, pre_mlp_gain, mlp_w_up, mlp_w_down, post_mlp_gain):
    h = x
    for l in range(w_in.shape[0]):
        h = _layer(h, pre_mix_gain[l], w_in[l], rg_conv_w[l], rg_conv_b[l], rg_gate_r_w[l], rg_gate_r_b[l],
                   rg_gate_i_w[l], rg_gate_i_b[l], rg_lambda[l], ml_conv_w[l], ml_conv_b[l], ml_igate_b[l],
                   ml_fgate_b[l], rg_out_gain[l], ml_out_gain[l], w_out[l], post_mix_gain[l],
                   pre_mlp_gain[l], mlp_w_up[l], mlp_w_down[l], post_mlp_gain[l])
    return h
```

```python
import functools

import jax
import jax.numpy as jnp
from jax.experimental import pallas as pl
from jax.experimental.pallas import tpu as pltpu

F32 = jnp.float32
BF16 = jnp.bfloat16

RG_BLOCKS = 8
RG_C = 8.0
ML_HEADS = 4
CONV_WIDTH = 4
EPS = 1e-6

SUBLANES = 8
LANES = 128
MXU_WIDTH = 256

SEQ_TILE = 512
ML_CHUNK = LANES
PHASES = 4
MLP_FF_CHUNK = 1024
WEIGHT_BLOCK = 512
WEIGHT_SLOTS = 3
MLP_PHASES_AFTER = {"start": 0, "w_in": 1, "conv": (0, 1, 0, 0, 1, 0), "rg_gates": 0, "rg_scan": (1, 0, 0),
                    "ml_stage": (1, 0, 0, 0), "w_out": 0}
VMEM_LIMIT_BYTES = 60 * 1024 * 1024


def _rms(x, gain):
    return x * jax.lax.rsqrt(jnp.mean(x * x, axis=-1, keepdims=True) + EPS) * gain


LOG2E = 1.4426950408889634


def _sigmoid(x):
    return 1.0 / (1.0 + jnp.exp2(x * (-LOG2E)))


def _softplus(x):
    return jnp.maximum(x, 0.0) + jnp.log1p(jnp.exp(-jnp.abs(x)))


def _gelu_tanh(x):
    c = 0.7978845608028654
    return 0.5 * x * (1.0 + jnp.tanh(c * (x + 0.044715 * (x * x * x))))


def _lane_scan(x, op, fill):
    n = x.shape[1]
    lane = jax.lax.broadcasted_iota(jnp.int32, x.shape, 1)
    d = 1
    while d < n:
        shifted = jnp.where(lane < d, fill, pltpu.roll(x, d, axis=1))
        x = op(x, shifted)
        d *= 2
    return x


def _affine_row_scan(sa, sb):
    n, width = sa.shape
    row8 = jax.lax.broadcasted_iota(jnp.int32, (SUBLANES, width), 0)
    d = 1
    while d < n:
        if d < SUBLANES:
            ra = pltpu.roll(sa, d, axis=0)
            rb = pltpu.roll(sb, d, axis=0)
            a_sh = jnp.concatenate([jnp.where(row8 < d, 1.0, ra[:SUBLANES]), ra[SUBLANES:]], axis=0)
            b_sh = jnp.concatenate([jnp.where(row8 < d, 0.0, rb[:SUBLANES]), rb[SUBLANES:]], axis=0)
            sb = sb + sa * b_sh
            sa = sa * a_sh
        else:
            sb = jnp.concatenate([sb[:d], sb[d:] + sa[d:] * sb[:n - d]], axis=0)
            sa = jnp.concatenate([sa[:d], sa[d:] * sa[:n - d]], axis=0)
        d *= 2
    return sa, sb


def _mixer_tile(x, seq_start, pre_g_ref, w_main_ref, w_if_ref, b_if_ref, conv_w_ref, conv_b_ref, w_gate_ref,
                b_r_ref, b_i_ref, lam_ref, rg_gain_ref, ml_gain_ref, w_out_ref, post_g_ref,
                cbuf_ref, qk_ref, yrg_ref, h_ref, c_ref, m_ref, row_ref, y_ref,
                *, d_rg, d_ml, head_dim):
    ts = x.shape[0]
    d_conv = d_rg + 2 * d_ml
    n_chunks = ts // ML_CHUNK
    L = ML_CHUNK

    @pl.when(seq_start)
    def _():
        cbuf_ref[:, ts:ts + SUBLANES, :] = jnp.zeros((cbuf_ref.shape[0], SUBLANES, LANES), F32)
        h_ref[...] = jnp.zeros_like(h_ref)
        c_ref[...] = jnp.zeros_like(c_ref)
        m_ref[...] = jnp.zeros_like(m_ref)

    ub = _rms(x, pre_g_ref[...]).astype(BF16)
    proj = jnp.dot(ub, w_main_ref[...], preferred_element_type=F32)
    gates = jnp.dot(ub, w_if_ref[...], preferred_element_type=F32) + b_if_ref[...]
    yield MLP_PHASES_AFTER["w_in"]

    n_cs = d_conv // LANES
    n_rs = d_rg // LANES
    G = ts // PHASES
    cbuf_ref[:n_cs, :SUBLANES, :] = cbuf_ref[:n_cs, ts:ts + SUBLANES, :]
    for j in range(n_cs + n_rs):
        cbuf_ref[j, SUBLANES:, :] = proj[:, j * LANES:(j + 1) * LANES]

    def phase(j, e):
        return cbuf_ref[j, pl.ds(SUBLANES + e, G, stride=PHASES), :]

    conv_ph = [[None] * n_cs for _ in range(PHASES)]
    for j in range(n_cs):
        cols = slice(j * LANES, (j + 1) * LANES)
        taps = {e: phase(j, e) for e in range(1 - CONV_WIDTH, PHASES)}
        for r in range(PHASES):
            acc = taps[r] * conv_w_ref[CONV_WIDTH - 1:CONV_WIDTH, cols] + conv_b_ref[:, cols]
            for k in range(1, CONV_WIDTH):
                acc = acc + taps[r - k] * conv_w_ref[CONV_WIDTH - 1 - k:CONV_WIDTH - k, cols]
            conv_ph[r][j] = acc
        if j >= n_rs:
            scale = head_dim ** -0.5 if j < n_rs + d_ml // LANES else 1.0
            for r in range(PHASES):
                val = conv_ph[r][j]
                val = val * _sigmoid(val)
                qk_ref[j - n_rs, pl.ds(r, G, stride=PHASES), :] = val * scale if scale != 1.0 else val
        if j % 2 == 1:
            yield MLP_PHASES_AFTER["conv"][j // 2]

    xc = jnp.concatenate([jnp.concatenate(conv_ph[r][:n_rs], axis=1) for r in range(PHASES)], axis=0)
    r_parts, i_parts = [], []
    for g in range(d_rg // MXU_WIDTH):
        gg = jnp.dot(xc[:, g * MXU_WIDTH:(g + 1) * MXU_WIDTH].astype(BF16), w_gate_ref[g],
                     preferred_element_type=F32)
        r_parts.append(gg[:, :MXU_WIDTH])
        i_parts.append(gg[:, MXU_WIDTH:])
    r = _sigmoid(jnp.concatenate(r_parts, axis=1) + b_r_ref[...])
    i_gate = _sigmoid(jnp.concatenate(i_parts, axis=1) + b_i_ref[...])
    neg_log_a_unit = RG_C * _softplus(-lam_ref[...])
    a = jnp.exp2(r * (neg_log_a_unit * (-LOG2E)))
    z = jnp.tanh(r * neg_log_a_unit) * (a * a + 1.0)
    b_in = jnp.where(z > 0.0, z * jax.lax.rsqrt(z), 0.0) * (i_gate * xc)
    yield MLP_PHASES_AFTER["rg_gates"]

    comp_a, comp_b = [a[:G]], [b_in[:G]]
    for r in range(1, PHASES):
        a_r, b_r = a[r * G:(r + 1) * G], b_in[r * G:(r + 1) * G]
        comp_b.append(a_r * comp_b[-1] + b_r)
        comp_a.append(a_r * comp_a[-1])
    ga, gb = _affine_row_scan(comp_a[-1], comp_b[-1])
    carry = h_ref[SUBLANES - 1:SUBLANES, :]
    h_end = ga * carry + gb
    h_ref[...] = h_end[G - SUBLANES:]
    rolled = pltpu.roll(h_end, 1, axis=0)
    row8r = jax.lax.broadcasted_iota(jnp.int32, (SUBLANES, d_rg), 0)
    h_prev = jnp.concatenate([jnp.where(row8r < 1, carry, rolled[:SUBLANES]), rolled[SUBLANES:]], axis=0)
    yield MLP_PHASES_AFTER["rg_scan"][0]
    for r in range(PHASES):
        h_r = comp_a[r] * h_prev + comp_b[r]
        gate_r = jnp.concatenate([phase(n_cs + j, r) for j in range(n_rs)], axis=1)
        y_r = _rms(h_r * _gelu_tanh(gate_r), rg_gain_ref[...])
        for j in range(n_rs):
            yrg_ref[j, pl.ds(r, G, stride=PHASES), :] = y_r[:, j * LANES:(j + 1) * LANES]
        if r % 2 == 1:
            yield MLP_PHASES_AFTER["rg_scan"][1 + r // 2]
    for j in range(n_rs):
        y_ref[:, j * LANES:(j + 1) * LANES] = yrg_ref[j].astype(BF16)

    v_all = proj[:, d_conv + d_rg:d_conv + d_rg + d_ml]
    o_all = proj[:, d_conv + d_rg + d_ml:]

    lane_g = jax.lax.broadcasted_iota(jnp.int32, gates.shape, 1)
    log_sig = jnp.minimum(gates, 0.0) - jnp.log1p(jnp.exp(-jnp.abs(gates)))
    gates_t = jnp.where(lane_g < ML_HEADS, gates, log_sig).T[:SUBLANES]

    causal = (jax.lax.broadcasted_iota(jnp.int32, (L, L), 1)
              <= jax.lax.broadcasted_iota(jnp.int32, (L, L), 0))
    ones_ext = jnp.ones((L, head_dim), BF16)

    for c in range(n_chunks):
        rows = slice(c * L, (c + 1) * L)
        li = gates_t[:, rows]
        lf = pltpu.roll(li, ML_HEADS, axis=0)
        bcum = _lane_scan(lf, jnp.add, 0.0)
        b_last = jnp.sum(lf, axis=1, keepdims=True)
        row_b = li - bcum
        cmax = _lane_scan(row_b, jnp.maximum, -jnp.inf)
        w_loc = b_last + row_b
        m_loc = jnp.max(w_loc, axis=1, keepdims=True)
        m_prev_b = m_ref[...]
        m_prev = jnp.max(m_prev_b, axis=1, keepdims=True)
        m_s = jnp.maximum(bcum + m_prev, bcum + cmax)
        m_new = jnp.maximum(b_last + m_prev, m_loc)
        row_ref[0] = bcum - m_s
        row_ref[1] = m_s
        row_ref[2] = row_b
        row_ref[3] = jnp.exp(w_loc - m_loc)
        row_ref[4] = jnp.broadcast_to(jnp.exp(b_last + m_prev - m_new), (SUBLANES, L))
        row_ref[5] = jnp.broadcast_to(jnp.exp(m_loc - m_new), (SUBLANES, L))
        row_ref[6] = m_prev_b
        m_ref[...] = jnp.broadcast_to(m_new, (SUBLANES, L))

        heads = range(ML_HEADS)
        q_f = [qk_ref[h, rows, :] for h in heads]
        k_t = [qk_ref[ML_HEADS + h, rows, :].T for h in heads]
        s = [jnp.dot(q_f[h].astype(BF16), k_t[h].astype(BF16), preferred_element_type=F32) for h in heads]
        yield MLP_PHASES_AFTER["ml_stage"][0]

        lhs, rhs, v_ext, ms_b = [], [], [], []
        for h in heads:
            cols = slice(h * head_dim, (h + 1) * head_dim)
            ca_b = jnp.broadcast_to(row_ref[0, h:h + 1, :], (head_dim, L)).T
            ms_b.append(jnp.broadcast_to(row_ref[1, h:h + 1, :], (head_dim, L)).T)
            p = jnp.exp(jnp.where(causal, ca_b + row_ref[2, h:h + 1, :], -jnp.inf))
            inter_w = jnp.exp(ca_b + row_ref[6, h:h + 1, :])
            v_ext.append(jnp.concatenate([v_all[rows, cols].astype(BF16), ones_ext], axis=1))
            lhs.append(jnp.concatenate([(s[h] * p).astype(BF16), (q_f[h] * inter_w).astype(BF16)], axis=1))
            rhs.append(jnp.concatenate([v_ext[h], c_ref[h].astype(BF16)], axis=0))
        yield MLP_PHASES_AFTER["ml_stage"][1]

        nd = [jnp.dot(lhs[h], rhs[h], preferred_element_type=F32) for h in heads]
        upd = [jnp.dot((k_t[h] * row_ref[3, h:h + 1, :]).astype(BF16), v_ext[h], preferred_element_type=F32)
               for h in heads]
        yield MLP_PHASES_AFTER["ml_stage"][2]

        for h in heads:
            cols = slice(h * head_dim, (h + 1) * head_dim)
            hh = nd[h][:, :head_dim] / jnp.maximum(jnp.abs(nd[h][:, head_dim:]), jnp.exp(-ms_b[h]))
            hh = _sigmoid(o_all[rows, cols]) * hh
            y_ref[rows, d_rg + h * head_dim:d_rg + (h + 1) * head_dim] = _rms(
                hh, ml_gain_ref[:, cols]).astype(BF16)
            s_old = row_ref[4, h:h + 1, :]
            s_new = row_ref[5, h:h + 1, :]
            c_ref[h] = (jnp.concatenate([s_old, s_old], axis=1) * c_ref[h]
                        + jnp.concatenate([s_new, s_new], axis=1) * upd[h])
        yield MLP_PHASES_AFTER["ml_stage"][3]

    mix = jnp.dot(y_ref[...], w_out_ref[...], preferred_element_type=F32)
    yield MLP_PHASES_AFTER["w_out"]
    return x + _rms(mix, post_g_ref[...])


def _mlp_tile(h_ref, pre_g_ref, w_up_ref, w_down_ref, post_g_ref):
    vb = _rms(h_ref[...], pre_g_ref[...]).astype(BF16)
    d_ff = w_up_ref.shape[1]
    acc = jnp.zeros(h_ref.shape, F32)
    for c in range(d_ff // MLP_FF_CHUNK):
        cols = slice(c * MLP_FF_CHUNK, (c + 1) * MLP_FF_CHUNK)
        f = jnp.maximum(jnp.dot(vb, w_up_ref[:, cols], preferred_element_type=F32), 0.0)
        yield
        acc = acc + jnp.dot((f * f).astype(BF16), w_down_ref[cols, :], preferred_element_type=F32)
        yield
    return h_ref[...] + _rms(acc, post_g_ref[...])


def _run(gen):
    while True:
        try:
            next(gen)
        except StopIteration as stop:
            return stop.value


def _interleave(primary, secondary, lead=0, per=1):
    results = [None, None]

    def advance(idx, gen):
        if results[idx] is None:
            try:
                return next(gen)
            except StopIteration as stop:
                results[idx] = (stop.value,)
        return 0

    for _ in range(lead):
        advance(1, secondary)
    while results[0] is None:
        for _ in range(advance(0, primary) * per):
            advance(1, secondary)
    while results[1] is None:
        advance(1, secondary)
    return results[0][0], results[1][0]


def _weight_copy_jobs(hbm, vmem, src_row_starts=None):
    rows, cols = vmem.shape
    jobs = []
    for cb in range(cols // WEIGHT_BLOCK):
        for rb in range(rows // WEIGHT_BLOCK):
            dst = (vmem, rb * WEIGHT_BLOCK, cb * WEIGHT_BLOCK)
            if src_row_starts is None:
                jobs.append((hbm, rb * WEIGHT_BLOCK, cb * WEIGHT_BLOCK) + dst + (False,))
            else:
                jobs.append((hbm, src_row_starts[cb], rb * WEIGHT_BLOCK) + dst + (True,))
    return jobs


def _load_weights(jobs, stage_ref, sem_ref):
    ahead = WEIGHT_SLOTS - 1

    def copy(i):
        src, r0, c0 = jobs[i][:3]
        slot = i % WEIGHT_SLOTS
        return pltpu.make_async_copy(
            src.at[pl.ds(r0, WEIGHT_BLOCK), pl.ds(c0, WEIGHT_BLOCK)], stage_ref.at[slot], sem_ref.at[slot])

    for i in range(min(ahead, len(jobs))):
        copy(i).start()
    for i in range(len(jobs)):
        if i + ahead < len(jobs):
            copy(i + ahead).start()
        copy(i).wait()
        dst, r0, c0, transposed = jobs[i][3:]
        block = stage_ref[i % WEIGHT_SLOTS]
        dst[r0:r0 + WEIGHT_BLOCK, c0:c0 + WEIGHT_BLOCK] = (block.T if transposed else block).astype(BF16)
        yield


def _layer_kernel(x_ref, *refs, tiles_per_seq, n_tiles, n_small, w_in_row_starts, **dims):
    (pre_g, w_if, b_if, conv_w, conv_b, w_gate, b_r, b_i, lam, rg_gain, ml_gain, post_g,
     mlp_pre_g, mlp_post_g) = refs[:n_small]
    w_in_t_hbm, w_out_hbm, w_up_hbm, w_down_hbm = refs[n_small:n_small + 4]
    out_ref = refs[n_small + 4]
    w_main, w_out, w_up, w_down, stage_ref, sem_ref, h1_ref = refs[n_small + 5:n_small + 12]
    state = refs[n_small + 12:]
    s = pl.program_id(0)

    @pl.when(s == 0)
    def _():
        h1_ref[...] = jnp.zeros_like(h1_ref)
        _run(_load_weights(_weight_copy_jobs(w_in_t_hbm, w_main, w_in_row_starts)
                           + _weight_copy_jobs(w_out_hbm, w_out) + _weight_copy_jobs(w_up_hbm, w_up)
                           + _weight_copy_jobs(w_down_hbm, w_down), stage_ref, sem_ref))

    h1_new, out = _interleave(
        _mixer_tile(x_ref[0], s % tiles_per_seq == 0, pre_g, w_main, w_if, b_if, conv_w, conv_b,
                    w_gate, b_r, b_i, lam, rg_gain, ml_gain, w_out, post_g, *state, **dims),
        _mlp_tile(h1_ref, mlp_pre_g, w_up, w_down, mlp_post_g),
        MLP_PHASES_AFTER["start"])
    out_ref[0] = out
    h1_ref[...] = h1_new


def _const_spec(shape):
    zeros = (0,) * len(shape)
    return pl.BlockSpec(shape, lambda *_: zeros, pipeline_mode=pl.Buffered(1))


def _block_diag_gate(w):
    nb, bd, _ = w.shape
    per = MXU_WIDTH // bd
    w = w.reshape(nb // per, per, bd, bd)
    eye = jnp.eye(per, dtype=w.dtype)
    return jnp.einsum("gpij,pq->gpiqj", w, eye).reshape(nb // per, MXU_WIDTH, MXU_WIDTH)


def _layer(x, pre_gain, w_in, rg_conv_w, rg_conv_b, gate_r_w, gate_r_b, gate_i_w, gate_i_b, lam,
           ml_conv_w, ml_conv_b, igate_b, fgate_b, rg_gain, ml_gain, w_out, post_gain,
           pre_mlp_gain, w_up, w_down, post_mlp_gain):
    bsz, seq, d_model = x.shape
    assert w_up.shape[1] % MLP_FF_CHUNK == 0
    d_rg = lam.shape[0]
    d_ml = ml_gain.shape[0]
    head_dim = d_ml // ML_HEADS
    assert head_dim == LANES and ML_CHUNK == LANES and 2 * ML_HEADS == SUBLANES
    assert seq % SEQ_TILE == 0 and SEQ_TILE % ML_CHUNK == 0 and d_rg % MXU_WIDTH == 0
    d_conv = d_rg + 2 * d_ml

    assert d_rg == WEIGHT_BLOCK and d_ml == WEIGHT_BLOCK
    starts = dict(rg_x=0, rg_gate=d_rg, q=2 * d_rg, k=2 * d_rg + d_ml, v=2 * d_rg + 2 * d_ml, o=2 * d_rg + 3 * d_ml)
    w_in_row_starts = tuple(starts[n] for n in ("rg_x", "q", "k", "rg_gate", "v", "o"))
    d_main = 2 * d_rg + 4 * d_ml
    if_w = w_in[:, d_main:]
    w_if = jnp.pad(if_w, ((0, 0), (0, LANES - 2 * ML_HEADS))).astype(BF16)
    b_if = jnp.pad(jnp.concatenate([igate_b, fgate_b]), (0, LANES - 2 * ML_HEADS)).reshape(1, LANES)
    conv_w = jnp.concatenate([rg_conv_w, ml_conv_w], axis=1)
    conv_b = jnp.concatenate([rg_conv_b, ml_conv_b]).reshape(1, d_conv)
    w_gate = jnp.concatenate([_block_diag_gate(gate_r_w), _block_diag_gate(gate_i_w)], axis=2).astype(BF16)

    row = lambda v: v.reshape(1, -1)
    small = (row(pre_gain), w_if, b_if, conv_w, conv_b, w_gate, row(gate_r_b), row(gate_i_b), row(lam),
             row(rg_gain), row(ml_gain), row(post_gain), row(pre_mlp_gain), row(post_mlp_gain))
    big = (jnp.swapaxes(w_in, 0, 1), w_out, w_up, w_down)
    d_ff = w_up.shape[1]
    for w in (w_out, w_up, w_down):
        assert w.shape[0] % WEIGHT_BLOCK == 0 and w.shape[1] % WEIGHT_BLOCK == 0

    tiles_per_seq = seq // SEQ_TILE
    n_tiles = bsz * tiles_per_seq

    def tile_block(t):
        return (t // tiles_per_seq, t % tiles_per_seq, 0)

    x_spec = pl.BlockSpec((1, SEQ_TILE, d_model), lambda s: tile_block(jnp.minimum(s, n_tiles - 1)))
    out_spec = pl.BlockSpec((1, SEQ_TILE, d_model), lambda s: tile_block(jnp.maximum(s - 1, 0)))
    in_specs = ([x_spec] + [_const_spec(op.shape) for op in small]
                + [pl.BlockSpec(memory_space=pl.ANY)] * len(big))
    kern = functools.partial(_layer_kernel, tiles_per_seq=tiles_per_seq, n_tiles=n_tiles, n_small=len(small),
                             w_in_row_starts=w_in_row_starts, d_rg=d_rg, d_ml=d_ml, head_dim=head_dim)
    return pl.pallas_call(
        kern,
        grid=(n_tiles + 1,),
        in_specs=in_specs,
        out_specs=out_spec,
        out_shape=jax.ShapeDtypeStruct(x.shape, x.dtype),
        scratch_shapes=[
            pltpu.VMEM((d_model, d_main), BF16),
            pltpu.VMEM((d_rg + d_ml, d_model), BF16),
            pltpu.VMEM((d_model, d_ff), BF16),
            pltpu.VMEM((d_ff, d_model), BF16),
            pltpu.VMEM((WEIGHT_SLOTS, WEIGHT_BLOCK, WEIGHT_BLOCK), F32),
            pltpu.SemaphoreType.DMA((WEIGHT_SLOTS,)),
            pltpu.VMEM((SEQ_TILE, d_model), F32),
            pltpu.VMEM(((d_conv + d_rg) // LANES, SUBLANES + SEQ_TILE, LANES), F32),
            pltpu.VMEM((2 * d_ml // LANES, SEQ_TILE, LANES), F32),
            pltpu.VMEM((d_rg // LANES, SEQ_TILE, LANES), F32),
            pltpu.VMEM((SUBLANES, d_rg), F32),
            pltpu.VMEM((ML_HEADS, head_dim, 2 * head_dim), F32),
            pltpu.VMEM((SUBLANES, ML_CHUNK), F32),
            pltpu.VMEM((7, SUBLANES, ML_CHUNK), F32),
            pltpu.VMEM((SEQ_TILE, d_rg + d_ml), BF16),
        ],
        compiler_params=pltpu.CompilerParams(
            dimension_semantics=("arbitrary",), vmem_limit_bytes=VMEM_LIMIT_BYTES),
        name="layer",
    )(x, *small, *big)


def kernel(x, pre_mix_gain, w_in, rg_conv_w, rg_conv_b, rg_gate_r_w, rg_gate_r_b, rg_gate_i_w, rg_gate_i_b, rg_lambda, ml_conv_w, ml_conv_b, ml_igate_b, ml_fgate_b, rg_out_gain, ml_out_gain, w_out, post_mix_gain, pre_mlp_gain, mlp_w_up, mlp_w_down, post_mlp_gain):
    h = x
    for l in range(w_in.shape[0]):
        h = _layer(h, pre_mix_gain[l], w_in[l], rg_conv_w[l], rg_conv_b[l], rg_gate_r_w[l], rg_gate_r_b[l],
                   rg_gate_i_w[l], rg_gate_i_b[l], rg_lambda[l], ml_conv_w[l], ml_conv_b[l], ml_igate_b[l],
                   ml_fgate_b[l], rg_out_gain[l], ml_out_gain[l], w_out[l], post_mix_gain[l],
                   pre_mlp_gain[l], mlp_w_up[l], mlp_w_down[l], post_mlp_gain[l])
    return h
```

```python
import functools

import jax
import jax.numpy as jnp
from jax.experimental import pallas as pl
from jax.experimental.pallas import tpu as pltpu

F32 = jnp.float32
BF16 = jnp.bfloat16

RG_BLOCKS = 8
RG_C = 8.0
ML_HEADS = 4
CONV_WIDTH = 4
EPS = 1e-6

SUBLANES = 8
LANES = 128
MXU_WIDTH = 256

SEQ_TILE = 512
ML_CHUNK = LANES
PHASES = 4
MLP_FF_CHUNK = 1024
WEIGHT_BLOCK = 512
WEIGHT_SLOTS = 4
MLP_PHASES_AFTER = {"start": 0, "w_in": 1, "conv": (0, 1, 0, 0, 1, 0), "rg_gates": 0, "rg_scan": (1, 0, 0),
                    "ml_stage": (1, 0, 0, 0), "w_out": 0}
VMEM_LIMIT_BYTES = 60 * 1024 * 1024


def _rms(x, gain):
    return x * jax.lax.rsqrt(jnp.mean(x * x, axis=-1, keepdims=True) + EPS) * gain


LOG2E = 1.4426950408889634


def _sigmoid(x):
    return 1.0 / (1.0 + jnp.exp2(x * (-LOG2E)))


def _softplus(x):
    return jnp.maximum(x, 0.0) + jnp.log1p(jnp.exp(-jnp.abs(x)))


def _gelu_tanh(x):
    c = 0.7978845608028654
    return 0.5 * x * (1.0 + jnp.tanh(c * (x + 0.044715 * (x * x * x))))


def _lane_scan(x, op, fill):
    n = x.shape[1]
    lane = jax.lax.broadcasted_iota(jnp.int32, x.shape, 1)
    d = 1
    while d < n:
        shifted = jnp.where(lane < d, fill, pltpu.roll(x, d, axis=1))
        x = op(x, shifted)
        d *= 2
    return x


def _affine_row_scan(sa, sb):
    n, width = sa.shape
    row8 = jax.lax.broadcasted_iota(jnp.int32, (SUBLANES, width), 0)
    d = 1
    while d < n:
        if d < SUBLANES:
            ra = pltpu.roll(sa, d, axis=0)
            rb = pltpu.roll(sb, d, axis=0)
            a_sh = jnp.concatenate([jnp.where(row8 < d, 1.0, ra[:SUBLANES]), ra[SUBLANES:]], axis=0)
            b_sh = jnp.concatenate([jnp.where(row8 < d, 0.0, rb[:SUBLANES]), rb[SUBLANES:]], axis=0)
            sb = sb + sa * b_sh
            sa = sa * a_sh
        else:
            sb = jnp.concatenate([sb[:d], sb[d:] + sa[d:] * sb[:n - d]], axis=0)
            sa = jnp.concatenate([sa[:d], sa[d:] * sa[:n - d]], axis=0)
        d *= 2
    return sa, sb


def _mixer_tile(x, seq_start, pre_g_ref, w_main_ref, w_if_ref, b_if_ref, conv_w_ref, conv_b_ref, w_gate_ref,
                b_r_ref, b_i_ref, lam_ref, rg_gain_ref, ml_gain_ref, w_out_ref, post_g_ref,
                cbuf_ref, qk_ref, yrg_ref, h_ref, c_ref, m_ref, row_ref, y_ref,
                *, d_rg, d_ml, head_dim):
    ts = x.shape[0]
    d_conv = d_rg + 2 * d_ml
    n_chunks = ts // ML_CHUNK
    L = ML_CHUNK

    @pl.when(seq_start)
    def _():
        cbuf_ref[:, ts:ts + SUBLANES, :] = jnp.zeros((cbuf_ref.shape[0], SUBLANES, LANES), F32)
        h_ref[...] = jnp.zeros_like(h_ref)
        c_ref[...] = jnp.zeros_like(c_ref)
        m_ref[...] = jnp.zeros_like(m_ref)

    ub = _rms(x, pre_g_ref[...]).astype(BF16)
    proj = jnp.dot(ub, w_main_ref[...], preferred_element_type=F32)
    gates = jnp.dot(ub, w_if_ref[...], preferred_element_type=F32) + b_if_ref[...]
    yield MLP_PHASES_AFTER["w_in"]

    n_cs = d_conv // LANES
    n_rs = d_rg // LANES
    G = ts // PHASES
    cbuf_ref[:n_cs, :SUBLANES, :] = cbuf_ref[:n_cs, ts:ts + SUBLANES, :]
    for j in range(n_cs + n_rs):
        cbuf_ref[j, SUBLANES:, :] = proj[:, j * LANES:(j + 1) * LANES]

    def phase(j, e):
        return cbuf_ref[j, pl.ds(SUBLANES + e, G, stride=PHASES), :]

    conv_ph = [[None] * n_cs for _ in range(PHASES)]
    for j in range(n_cs):
        cols = slice(j * LANES, (j + 1) * LANES)
        taps = {e: phase(j, e) for e in range(1 - CONV_WIDTH, PHASES)}
        for r in range(PHASES):
            acc = taps[r] * conv_w_ref[CONV_WIDTH - 1:CONV_WIDTH, cols] + conv_b_ref[:, cols]
            for k in range(1, CONV_WIDTH):
                acc = acc + taps[r - k] * conv_w_ref[CONV_WIDTH - 1 - k:CONV_WIDTH - k, cols]
            conv_ph[r][j] = acc
        if j >= n_rs:
            scale = head_dim ** -0.5 if j < n_rs + d_ml // LANES else 1.0
            for r in range(PHASES):
                val = conv_ph[r][j]
                val = val * _sigmoid(val)
                qk_ref[j - n_rs, pl.ds(r, G, stride=PHASES), :] = val * scale if scale != 1.0 else val
        if j % 2 == 1:
            yield MLP_PHASES_AFTER["conv"][j // 2]

    xc = jnp.concatenate([jnp.concatenate(conv_ph[r][:n_rs], axis=1) for r in range(PHASES)], axis=0)
    r_parts, i_parts = [], []
    for g in range(d_rg // MXU_WIDTH):
        gg = jnp.dot(xc[:, g * MXU_WIDTH:(g + 1) * MXU_WIDTH].astype(BF16), w_gate_ref[g],
                     preferred_element_type=F32)
        r_parts.append(gg[:, :MXU_WIDTH])
        i_parts.append(gg[:, MXU_WIDTH:])
    r = _sigmoid(jnp.concatenate(r_parts, axis=1) + b_r_ref[...])
    i_gate = _sigmoid(jnp.concatenate(i_parts, axis=1) + b_i_ref[...])
    neg_log_a_unit = RG_C * _softplus(-lam_ref[...])
    a = jnp.exp2(r * (neg_log_a_unit * (-LOG2E)))
    z = jnp.tanh(r * neg_log_a_unit) * (a * a + 1.0)
    b_in = jnp.where(z > 0.0, z * jax.lax.rsqrt(z), 0.0) * (i_gate * xc)
    yield MLP_PHASES_AFTER["rg_gates"]

    comp_a, comp_b = [a[:G]], [b_in[:G]]
    for r in range(1, PHASES):
        a_r, b_r = a[r * G:(r + 1) * G], b_in[r * G:(r + 1) * G]
        comp_b.append(a_r * comp_b[-1] + b_r)
        comp_a.append(a_r * comp_a[-1])
    ga, gb = _affine_row_scan(comp_a[-1], comp_b[-1])
    carry = h_ref[SUBLANES - 1:SUBLANES, :]
    h_end = ga * carry + gb
    h_ref[...] = h_end[G - SUBLANES:]
    rolled = pltpu.roll(h_end, 1, axis=0)
    row8r = jax.lax.broadcasted_iota(jnp.int32, (SUBLANES, d_rg), 0)
    h_prev = jnp.concatenate([jnp.where(row8r < 1, carry, rolled[:SUBLANES]), rolled[SUBLANES:]], axis=0)
    yield MLP_PHASES_AFTER["rg_scan"][0]
    for r in range(PHASES):
        h_r = comp_a[r] * h_prev + comp_b[r]
        gate_r = jnp.concatenate([phase(n_cs + j, r) for j in range(n_rs)], axis=1)
        y_r = _rms(h_r * _gelu_tanh(gate_r), rg_gain_ref[...])
        for j in range(n_rs):
            yrg_ref[j, pl.ds(r, G, stride=PHASES), :] = y_r[:, j * LANES:(j + 1) * LANES]
        if r % 2 == 1:
            yield MLP_PHASES_AFTER["rg_scan"][1 + r // 2]
    for j in range(n_rs):
        y_ref[:, j * LANES:(j + 1) * LANES] = yrg_ref[j].astype(BF16)

    v_all = proj[:, d_conv + d_rg:d_conv + d_rg + d_ml]
    o_all = proj[:, d_conv + d_rg + d_ml:]

    lane_g = jax.lax.broadcasted_iota(jnp.int32, gates.shape, 1)
    log_sig = jnp.minimum(gates, 0.0) - jnp.log1p(jnp.exp(-jnp.abs(gates)))
    gates_t = jnp.where(lane_g < ML_HEADS, gates, log_sig).T[:SUBLANES]

    causal = (jax.lax.broadcasted_iota(jnp.int32, (L, L), 1)
              <= jax.lax.broadcasted_iota(jnp.int32, (L, L), 0))
    ones_ext = jnp.ones((L, head_dim), BF16)

    for c in range(n_chunks):
        rows = slice(c * L, (c + 1) * L)
        li = gates_t[:, rows]
        lf = pltpu.roll(li, ML_HEADS, axis=0)
        bcum = _lane_scan(lf, jnp.add, 0.0)
        b_last = jnp.sum(lf, axis=1, keepdims=True)
        row_b = li - bcum
        cmax = _lane_scan(row_b, jnp.maximum, -jnp.inf)
        w_loc = b_last + row_b
        m_loc = jnp.max(w_loc, axis=1, keepdims=True)
        m_prev_b = m_ref[...]
        m_prev = jnp.max(m_prev_b, axis=1, keepdims=True)
        m_s = jnp.maximum(bcum + m_prev, bcum + cmax)
        m_new = jnp.maximum(b_last + m_prev, m_loc)
        row_ref[0] = bcum - m_s
        row_ref[1] = m_s
        row_ref[2] = row_b
        row_ref[3] = jnp.exp(w_loc - m_loc)
        row_ref[4] = jnp.broadcast_to(jnp.exp(b_last + m_prev - m_new), (SUBLANES, L))
        row_ref[5] = jnp.broadcast_to(jnp.exp(m_loc - m_new), (SUBLANES, L))
        row_ref[6] = m_prev_b
        m_ref[...] = jnp.broadcast_to(m_new, (SUBLANES, L))

        heads = range(ML_HEADS)
        q_f = [qk_ref[h, rows, :] for h in heads]
        k_t = [qk_ref[ML_HEADS + h, rows, :].T for h in heads]
        s = [jnp.dot(q_f[h].astype(BF16), k_t[h].astype(BF16), preferred_element_type=F32) for h in heads]
        yield MLP_PHASES_AFTER["ml_stage"][0]

        lhs, rhs, v_ext, ms_b = [], [], [], []
        for h in heads:
            cols = slice(h * head_dim, (h + 1) * head_dim)
            ca_b = jnp.broadcast_to(row_ref[0, h:h + 1, :], (head_dim, L)).T
            ms_b.append(jnp.broadcast_to(row_ref[1, h:h + 1, :], (head_dim, L)).T)
            p = jnp.exp(jnp.where(causal, ca_b + row_ref[2, h:h + 1, :], -jnp.inf))
            inter_w = jnp.exp(ca_b + row_ref[6, h:h + 1, :])
            v_ext.append(jnp.concatenate([v_all[rows, cols].astype(BF16), ones_ext], axis=1))
            lhs.append(jnp.concatenate([(s[h] * p).astype(BF16), (q_f[h] * inter_w).astype(BF16)], axis=1))
            rhs.append(jnp.concatenate([v_ext[h], c_ref[h].astype(BF16)], axis=0))
        yield MLP_PHASES_AFTER["ml_stage"][1]

        nd = [jnp.dot(lhs[h], rhs[h], preferred_element_type=F32) for h in heads]
        upd = [jnp.dot((k_t[h] * row_ref[3, h:h + 1, :]).astype(BF16), v_ext[h], preferred_element_type=F32)
               for h in heads]
        yield MLP_PHASES_AFTER["ml_stage"][2]

        for h in heads:
            cols = slice(h * head_dim, (h + 1) * head_dim)
            hh = nd[h][:, :head_dim] / jnp.maximum(jnp.abs(nd[h][:, head_dim:]), jnp.exp(-ms_b[h]))
            hh = _sigmoid(o_all[rows, cols]) * hh
            y_ref[rows, d_rg + h * head_dim:d_rg + (h + 1) * head_dim] = _rms(
                hh, ml_gain_ref[:, cols]).astype(BF16)
            s_old = row_ref[4, h:h + 1, :]
            s_new = row_ref[5, h:h + 1, :]
            c_ref[h] = (jnp.concatenate([s_old, s_old], axis=1) * c_ref[h]
                        + jnp.concatenate([s_new, s_new], axis=1) * upd[h])
        yield MLP_PHASES_AFTER["ml_stage"][3]

    mix = jnp.dot(y_ref[...], w_out_ref[...], preferred_element_type=F32)
    yield MLP_PHASES_AFTER["w_out"]
    return x + _rms(mix, post_g_ref[...])


def _mlp_tile(h_ref, pre_g_ref, w_up_ref, w_down_ref, post_g_ref):
    vb = _rms(h_ref[...], pre_g_ref[...]).astype(BF16)
    d_ff = w_up_ref.shape[1]
    acc = jnp.zeros(h_ref.shape, F32)
    for c in range(d_ff // MLP_FF_CHUNK):
        cols = slice(c * MLP_FF_CHUNK, (c + 1) * MLP_FF_CHUNK)
        f = jnp.maximum(jnp.dot(vb, w_up_ref[:, cols], preferred_element_type=F32), 0.0)
        yield
        acc = acc + jnp.dot((f * f).astype(BF16), w_down_ref[cols, :], preferred_element_type=F32)
        yield
    return h_ref[...] + _rms(acc, post_g_ref[...])


def _run(gen):
    while True:
        try:
            next(gen)
        except StopIteration as stop:
            return stop.value


def _interleave(primary, secondary, lead=0, per=1):
    results = [None, None]

    def advance(idx, gen):
        if results[idx] is None:
            try:
                return next(gen)
            except StopIteration as stop:
                results[idx] = (stop.value,)
        return 0

    for _ in range(lead):
        advance(1, secondary)
    while results[0] is None:
        for _ in range(advance(0, primary) * per):
            advance(1, secondary)
    while results[1] is None:
        advance(1, secondary)
    return results[0][0], results[1][0]


def _weight_copy_jobs(hbm, vmem, src_row_starts=None):
    rows, cols = vmem.shape
    jobs = []
    for cb in range(cols // WEIGHT_BLOCK):
        for rb in range(rows // WEIGHT_BLOCK):
            dst = (vmem, rb * WEIGHT_BLOCK, cb * WEIGHT_BLOCK)
            if src_row_starts is None:
                jobs.append((hbm, rb * WEIGHT_BLOCK, cb * WEIGHT_BLOCK) + dst + (False,))
            else:
                jobs.append((hbm, src_row_starts[cb], rb * WEIGHT_BLOCK) + dst + (True,))
    return jobs


def _load_weights(jobs, stage_ref, sem_ref):
    ahead = WEIGHT_SLOTS - 1

    def copy(i):
        src, r0, c0 = jobs[i][:3]
        slot = i % WEIGHT_SLOTS
        return pltpu.make_async_copy(
            src.at[pl.ds(r0, WEIGHT_BLOCK), pl.ds(c0, WEIGHT_BLOCK)], stage_ref.at[slot], sem_ref.at[slot])

    for i in range(min(ahead, len(jobs))):
        copy(i).start()
    for i in range(len(jobs)):
        if i + ahead < len(jobs):
            copy(i + ahead).start()
        copy(i).wait()
        dst, r0, c0, transposed = jobs[i][3:]
        block = stage_ref[i % WEIGHT_SLOTS]
        dst[r0:r0 + WEIGHT_BLOCK, c0:c0 + WEIGHT_BLOCK] = (block.T if transposed else block).astype(BF16)
        yield


def _layer_kernel(x_ref, *refs, tiles_per_seq, n_tiles, n_small, w_in_row_starts, **dims):
    (pre_g, w_if, b_if, conv_w, conv_b, w_gate, b_r, b_i, lam, rg_gain, ml_gain, post_g,
     mlp_pre_g, mlp_post_g) = refs[:n_small]
    w_in_t_hbm, w_out_hbm, w_up_hbm, w_down_hbm = refs[n_small:n_small + 4]
    out_ref = refs[n_small + 4]
    w_main, w_out, w_up, w_down, stage_ref, sem_ref, h1_ref = refs[n_small + 5:n_small + 12]
    state = refs[n_small + 12:]
    s = pl.program_id(0)

    @pl.when(s == 0)
    def _():
        h1_ref[...] = jnp.zeros_like(h1_ref)
        _run(_load_weights(_weight_copy_jobs(w_in_t_hbm, w_main, w_in_row_starts)
                           + _weight_copy_jobs(w_out_hbm, w_out) + _weight_copy_jobs(w_up_hbm, w_up)
                           + _weight_copy_jobs(w_down_hbm, w_down), stage_ref, sem_ref))

    h1_new, out = _interleave(
        _mixer_tile(x_ref[0], s % tiles_per_seq == 0, pre_g, w_main, w_if, b_if, conv_w, conv_b,
                    w_gate, b_r, b_i, lam, rg_gain, ml_gain, w_out, post_g, *state, **dims),
        _mlp_tile(h1_ref, mlp_pre_g, w_up, w_down, mlp_post_g),
        MLP_PHASES_AFTER["start"])
    out_ref[0] = out
    h1_ref[...] = h1_new


def _const_spec(shape):
    zeros = (0,) * len(shape)
    return pl.BlockSpec(shape, lambda *_: zeros, pipeline_mode=pl.Buffered(1))


def _block_diag_gate(w):
    nb, bd, _ = w.shape
    per = MXU_WIDTH // bd
    w = w.reshape(nb // per, per, bd, bd)
    eye = jnp.eye(per, dtype=w.dtype)
    return jnp.einsum("gpij,pq->gpiqj", w, eye).reshape(nb // per, MXU_WIDTH, MXU_WIDTH)


def _layer(x, pre_gain, w_in, rg_conv_w, rg_conv_b, gate_r_w, gate_r_b, gate_i_w, gate_i_b, lam,
           ml_conv_w, ml_conv_b, igate_b, fgate_b, rg_gain, ml_gain, w_out, post_gain,
           pre_mlp_gain, w_up, w_down, post_mlp_gain):
    bsz, seq, d_model = x.shape
    assert w_up.shape[1] % MLP_FF_CHUNK == 0
    d_rg = lam.shape[0]
    d_ml = ml_gain.shape[0]
    head_dim = d_ml // ML_HEADS
    assert head_dim == LANES and ML_CHUNK == LANES and 2 * ML_HEADS == SUBLANES
    assert seq % SEQ_TILE == 0 and SEQ_TILE % ML_CHUNK == 0 and d_rg % MXU_WIDTH == 0
    d_conv = d_rg + 2 * d_ml

    assert d_rg == WEIGHT_BLOCK and d_ml == WEIGHT_BLOCK
    starts = dict(rg_x=0, rg_gate=d_rg, q=2 * d_rg, k=2 * d_rg + d_ml, v=2 * d_rg + 2 * d_ml, o=2 * d_rg + 3 * d_ml)
    w_in_row_starts = tuple(starts[n] for n in ("rg_x", "q", "k", "rg_gate", "v", "o"))
    d_main = 2 * d_rg + 4 * d_ml
    if_w = w_in[:, d_main:]
    w_if = jnp.pad(if_w, ((0, 0), (0, LANES - 2 * ML_HEADS))).astype(BF16)
    b_if = jnp.pad(jnp.concatenate([igate_b, fgate_b]), (0, LANES - 2 * ML_HEADS)).reshape(1, LANES)
    conv_w = jnp.concatenate([rg_conv_w, ml_conv_w], axis=1)
    conv_b = jnp.concatenate([rg_conv_b, ml_conv_b]).reshape(1, d_conv)
    w_gate = jnp.concatenate([_block_diag_gate(gate_r_w), _block_diag_gate(gate_i_w)], axis=2).astype(BF16)

    row = lambda v: v.reshape(1, -1)
    small = (row(pre_gain), w_if, b_if, conv_w, conv_b, w_gate, row(gate_r_b), row(gate_i_b), row(lam),
             row(rg_gain), row(ml_gain), row(post_gain), row(pre_mlp_gain), row(post_mlp_gain))
    big = (jnp.swapaxes(w_in, 0, 1), w_out, w_up, w_down)
    d_ff = w_up.shape[1]
    for w in (w_out, w_up, w_down):
        assert w.shape[0] % WEIGHT_BLOCK == 0 and w.shape[1] % WEIGHT_BLOCK == 0

    tiles_per_seq = seq // SEQ_TILE
    n_tiles = bsz * tiles_per_seq

    def tile_block(t):
        return (t // tiles_per_seq, t % tiles_per_seq, 0)

    x_spec = pl.BlockSpec((1, SEQ_TILE, d_model), lambda s: tile_block(jnp.minimum(s, n_tiles - 1)))
    out_spec = pl.BlockSpec((1, SEQ_TILE, d_model), lambda s: tile_block(jnp.maximum(s - 1, 0)))
    in_specs = ([x_spec] + [_const_spec(op.shape) for op in small]
                + [pl.BlockSpec(memory_space=pl.ANY)] * len(big))
    kern = functools.partial(_layer_kernel, tiles_per_seq=tiles_per_seq, n_tiles=n_tiles, n_small=len(small),
                             w_in_row_starts=w_in_row_starts, d_rg=d_rg, d_ml=d_ml, head_dim=head_dim)
    return pl.pallas_call(
        kern,
        grid=(n_tiles + 1,),
        in_specs=in_specs,
        out_specs=out_spec,
        out_shape=jax.ShapeDtypeStruct(x.shape, x.dtype),
        scratch_shapes=[
            pltpu.VMEM((d_model, d_main), BF16),
            pltpu.VMEM((d_rg + d_ml, d_model), BF16),
            pltpu.VMEM((d_model, d_ff), BF16),
            pltpu.VMEM((d_ff, d_model), BF16),
            pltpu.VMEM((WEIGHT_SLOTS, WEIGHT_BLOCK, WEIGHT_BLOCK), F32),
            pltpu.SemaphoreType.DMA((WEIGHT_SLOTS,)),
            pltpu.VMEM((SEQ_TILE, d_model), F32),
            pltpu.VMEM(((d_conv + d_rg) // LANES, SUBLANES + SEQ_TILE, LANES), F32),
            pltpu.VMEM((2 * d_ml // LANES, SEQ_TILE, LANES), F32),
            pltpu.VMEM((d_rg // LANES, SEQ_TILE, LANES), F32),
            pltpu.VMEM((SUBLANES, d_rg), F32),
            pltpu.VMEM((ML_HEADS, head_dim, 2 * head_dim), F32),
            pltpu.VMEM((SUBLANES, ML_CHUNK), F32),
            pltpu.VMEM((7, SUBLANES, ML_CHUNK), F32),
            pltpu.VMEM((SEQ_TILE, d_rg + d_ml), BF16),
        ],
        compiler_params=pltpu.CompilerParams(
            dimension_semantics=("arbitrary",), vmem_limit_bytes=VMEM_LIMIT_BYTES),
        name="layer",
    )(x, *small, *big)


def kernel(x, pre_mix_gain, w_in, rg_conv_w, rg_conv_b, rg_gate_r_w, rg_gate_r_b, rg_gate_i_w, rg_gate_i_b, rg_lambda, ml_conv_w, ml_conv_b, ml_igate_b, ml_fgate_b, rg_out_gain, ml_out_gain, w_out, post_mix_gain, pre_mlp_gain, mlp_w_up, mlp_w_down, post_mlp_gain):
    h = x
    for l in range(w_in.shape[0]):
        h = _layer(h, pre_mix_gain[l], w_in[l], rg_conv_w[l], rg_conv_b[l], rg_gate_r_w[l], rg_gate_r_b[l],
                   rg_gate_i_w[l], rg_gate_i_b[l], rg_lambda[l], ml_conv_w[l], ml_conv_b[l], ml_igate_b[l],
                   ml_fgate_b[l], rg_out_gain[l], ml_out_gain[l], w_out[l], post_mix_gain[l],
                   pre_mlp_gain[l], mlp_w_up[l], mlp_w_down[l], post_mlp_gain[l])
    return h
```

```python
import functools

import jax
import jax.numpy as jnp
from jax.experimental import pallas as pl
from jax.experimental.pallas import tpu as pltpu

F32 = jnp.float32
BF16 = jnp.bfloat16

RG_BLOCKS = 8
RG_C = 8.0
ML_HEADS = 4
CONV_WIDTH = 4
EPS = 1e-6

SUBLANES = 8
LANES = 128
MXU_WIDTH = 256

SEQ_TILE = 512
ML_CHUNK = LANES
PHASES = 4
MLP_FF_CHUNK = 1024
WEIGHT_ROWS = 256
WEIGHT_BLOCK = 512
WEIGHT_SLOTS = 8
MLP_PHASES_AFTER = {"start": 0, "w_in": 1, "conv": (0, 1, 0, 0, 1, 0), "rg_gates": 0, "rg_scan": (1, 0, 0),
                    "ml_stage": (1, 0, 0, 0), "w_out": 0}
VMEM_LIMIT_BYTES = 60 * 1024 * 1024


def _rms(x, gain):
    return x * jax.lax.rsqrt(jnp.mean(x * x, axis=-1, keepdims=True) + EPS) * gain


LOG2E = 1.4426950408889634


def _sigmoid(x):
    return 1.0 / (1.0 + jnp.exp2(x * (-LOG2E)))


def _softplus(x):
    return jnp.maximum(x, 0.0) + jnp.log1p(jnp.exp(-jnp.abs(x)))


def _gelu_tanh(x):
    c = 0.7978845608028654
    return 0.5 * x * (1.0 + jnp.tanh(c * (x + 0.044715 * (x * x * x))))


def _lane_scan(x, op, fill):
    n = x.shape[1]
    lane = jax.lax.broadcasted_iota(jnp.int32, x.shape, 1)
    d = 1
    while d < n:
        shifted = jnp.where(lane < d, fill, pltpu.roll(x, d, axis=1))
        x = op(x, shifted)
        d *= 2
    return x


def _affine_row_scan(sa, sb):
    n, width = sa.shape
    row8 = jax.lax.broadcasted_iota(jnp.int32, (SUBLANES, width), 0)
    d = 1
    while d < n:
        if d < SUBLANES:
            ra = pltpu.roll(sa, d, axis=0)
            rb = pltpu.roll(sb, d, axis=0)
            a_sh = jnp.concatenate([jnp.where(row8 < d, 1.0, ra[:SUBLANES]), ra[SUBLANES:]], axis=0)
            b_sh = jnp.concatenate([jnp.where(row8 < d, 0.0, rb[:SUBLANES]), rb[SUBLANES:]], axis=0)
            sb = sb + sa * b_sh
            sa = sa * a_sh
        else:
            sb = jnp.concatenate([sb[:d], sb[d:] + sa[d:] * sb[:n - d]], axis=0)
            sa = jnp.concatenate([sa[:d], sa[d:] * sa[:n - d]], axis=0)
        d *= 2
    return sa, sb


def _mixer_tile(x, seq_start, pre_g_ref, w_main_ref, w_if_ref, b_if_ref, conv_w_ref, conv_b_ref, w_gate_ref,
                b_r_ref, b_i_ref, lam_ref, rg_gain_ref, ml_gain_ref, w_out_ref, post_g_ref,
                cbuf_ref, qk_ref, yrg_ref, h_ref, c_ref, m_ref, row_ref, y_ref,
                *, d_rg, d_ml, head_dim):
    ts = x.shape[0]
    d_conv = d_rg + 2 * d_ml
    n_chunks = ts // ML_CHUNK
    L = ML_CHUNK

    @pl.when(seq_start)
    def _():
        cbuf_ref[:, ts:ts + SUBLANES, :] = jnp.zeros((cbuf_ref.shape[0], SUBLANES, LANES), F32)
        h_ref[...] = jnp.zeros_like(h_ref)
        c_ref[...] = jnp.zeros_like(c_ref)
        m_ref[...] = jnp.zeros_like(m_ref)

    ub = _rms(x, pre_g_ref[...]).astype(BF16)
    proj = jnp.dot(ub, w_main_ref[...], preferred_element_type=F32)
    gates = jnp.dot(ub, w_if_ref[...], preferred_element_type=F32) + b_if_ref[...]
    yield MLP_PHASES_AFTER["w_in"]

    n_cs = d_conv // LANES
    n_rs = d_rg // LANES
    G = ts // PHASES
    cbuf_ref[:n_cs, :SUBLANES, :] = cbuf_ref[:n_cs, ts:ts + SUBLANES, :]
    for j in range(n_cs + n_rs):
        cbuf_ref[j, SUBLANES:, :] = proj[:, j * LANES:(j + 1) * LANES]

    def phase(j, e):
        return cbuf_ref[j, pl.ds(SUBLANES + e, G, stride=PHASES), :]

    conv_ph = [[None] * n_cs for _ in range(PHASES)]
    for j in range(n_cs):
        cols = slice(j * LANES, (j + 1) * LANES)
        taps = {e: phase(j, e) for e in range(1 - CONV_WIDTH, PHASES)}
        for r in range(PHASES):
            acc = taps[r] * conv_w_ref[CONV_WIDTH - 1:CONV_WIDTH, cols] + conv_b_ref[:, cols]
            for k in range(1, CONV_WIDTH):
                acc = acc + taps[r - k] * conv_w_ref[CONV_WIDTH - 1 - k:CONV_WIDTH - k, cols]
            conv_ph[r][j] = acc
        if j >= n_rs:
            scale = head_dim ** -0.5 if j < n_rs + d_ml // LANES else 1.0
            for r in range(PHASES):
                val = conv_ph[r][j]
                val = val * _sigmoid(val)
                qk_ref[j - n_rs, pl.ds(r, G, stride=PHASES), :] = val * scale if scale != 1.0 else val
        if j % 2 == 1:
            yield MLP_PHASES_AFTER["conv"][j // 2]

    xc = jnp.concatenate([jnp.concatenate(conv_ph[r][:n_rs], axis=1) for r in range(PHASES)], axis=0)
    r_parts, i_parts = [], []
    for g in range(d_rg // MXU_WIDTH):
        gg = jnp.dot(xc[:, g * MXU_WIDTH:(g + 1) * MXU_WIDTH].astype(BF16), w_gate_ref[g],
                     preferred_element_type=F32)
        r_parts.append(gg[:, :MXU_WIDTH])
        i_parts.append(gg[:, MXU_WIDTH:])
    r = _sigmoid(jnp.concatenate(r_parts, axis=1) + b_r_ref[...])
    i_gate = _sigmoid(jnp.concatenate(i_parts, axis=1) + b_i_ref[...])
    neg_log_a_unit = RG_C * _softplus(-lam_ref[...])
    a = jnp.exp2(r * (neg_log_a_unit * (-LOG2E)))
    z = jnp.tanh(r * neg_log_a_unit) * (a * a + 1.0)
    b_in = jnp.where(z > 0.0, z * jax.lax.rsqrt(z), 0.0) * (i_gate * xc)
    yield MLP_PHASES_AFTER["rg_gates"]

    comp_a, comp_b = [a[:G]], [b_in[:G]]
    for r in range(1, PHASES):
        a_r, b_r = a[r * G:(r + 1) * G], b_in[r * G:(r + 1) * G]
        comp_b.append(a_r * comp_b[-1] + b_r)
        comp_a.append(a_r * comp_a[-1])
    ga, gb = _affine_row_scan(comp_a[-1], comp_b[-1])
    carry = h_ref[SUBLANES - 1:SUBLANES, :]
    h_end = ga * carry + gb
    h_ref[...] = h_end[G - SUBLANES:]
    rolled = pltpu.roll(h_end, 1, axis=0)
    row8r = jax.lax.broadcasted_iota(jnp.int32, (SUBLANES, d_rg), 0)
    h_prev = jnp.concatenate([jnp.where(row8r < 1, carry, rolled[:SUBLANES]), rolled[SUBLANES:]], axis=0)
    yield MLP_PHASES_AFTER["rg_scan"][0]
    for r in range(PHASES):
        h_r = comp_a[r] * h_prev + comp_b[r]
        gate_r = jnp.concatenate([phase(n_cs + j, r) for j in range(n_rs)], axis=1)
        y_r = _rms(h_r * _gelu_tanh(gate_r), rg_gain_ref[...])
        for j in range(n_rs):
            yrg_ref[j, pl.ds(r, G, stride=PHASES), :] = y_r[:, j * LANES:(j + 1) * LANES]
        if r % 2 == 1:
            yield MLP_PHASES_AFTER["rg_scan"][1 + r // 2]
    for j in range(n_rs):
        y_ref[:, j * LANES:(j + 1) * LANES] = yrg_ref[j].astype(BF16)

    v_all = proj[:, d_conv + d_rg:d_conv + d_rg + d_ml]
    o_all = proj[:, d_conv + d_rg + d_ml:]

    lane_g = jax.lax.broadcasted_iota(jnp.int32, gates.shape, 1)
    log_sig = jnp.minimum(gates, 0.0) - jnp.log1p(jnp.exp(-jnp.abs(gates)))
    gates_t = jnp.where(lane_g < ML_HEADS, gates, log_sig).T[:SUBLANES]

    causal = (jax.lax.broadcasted_iota(jnp.int32, (L, L), 1)
              <= jax.lax.broadcasted_iota(jnp.int32, (L, L), 0))
    ones_ext = jnp.ones((L, head_dim), BF16)

    for c in range(n_chunks):
        rows = slice(c * L, (c + 1) * L)
        li = gates_t[:, rows]
        lf = pltpu.roll(li, ML_HEADS, axis=0)
        bcum = _lane_scan(lf, jnp.add, 0.0)
        b_last = jnp.sum(lf, axis=1, keepdims=True)
        row_b = li - bcum
        cmax = _lane_scan(row_b, jnp.maximum, -jnp.inf)
        w_loc = b_last + row_b
        m_loc = jnp.max(w_loc, axis=1, keepdims=True)
        m_prev_b = m_ref[...]
        m_prev = jnp.max(m_prev_b, axis=1, keepdims=True)
        m_s = jnp.maximum(bcum + m_prev, bcum + cmax)
        m_new = jnp.maximum(b_last + m_prev, m_loc)
        row_ref[0] = bcum - m_s
        row_ref[1] = m_s
        row_ref[2] = row_b
        row_ref[3] = jnp.exp(w_loc - m_loc)
        row_ref[4] = jnp.broadcast_to(jnp.exp(b_last + m_prev - m_new), (SUBLANES, L))
        row_ref[5] = jnp.broadcast_to(jnp.exp(m_loc - m_new), (SUBLANES, L))
        row_ref[6] = m_prev_b
        m_ref[...] = jnp.broadcast_to(m_new, (SUBLANES, L))

        heads = range(ML_HEADS)
        q_f = [qk_ref[h, rows, :] for h in heads]
        k_t = [qk_ref[ML_HEADS + h, rows, :].T for h in heads]
        s = [jnp.dot(q_f[h].astype(BF16), k_t[h].astype(BF16), preferred_element_type=F32) for h in heads]
        yield MLP_PHASES_AFTER["ml_stage"][0]

        lhs, rhs, v_ext, ms_b = [], [], [], []
        for h in heads:
            cols = slice(h * head_dim, (h + 1) * head_dim)
            ca_b = jnp.broadcast_to(row_ref[0, h:h + 1, :], (head_dim, L)).T
            ms_b.append(jnp.broadcast_to(row_ref[1, h:h + 1, :], (head_dim, L)).T)
            p = jnp.exp(jnp.where(causal, ca_b + row_ref[2, h:h + 1, :], -jnp.inf))
            inter_w = jnp.exp(ca_b + row_ref[6, h:h + 1, :])
            v_ext.append(jnp.concatenate([v_all[rows, cols].astype(BF16), ones_ext], axis=1))
            lhs.append(jnp.concatenate([(s[h] * p).astype(BF16), (q_f[h] * inter_w).astype(BF16)], axis=1))
            rhs.append(jnp.concatenate([v_ext[h], c_ref[h].astype(BF16)], axis=0))
        yield MLP_PHASES_AFTER["ml_stage"][1]

        nd = [jnp.dot(lhs[h], rhs[h], preferred_element_type=F32) for h in heads]
        upd = [jnp.dot((k_t[h] * row_ref[3, h:h + 1, :]).astype(BF16), v_ext[h], preferred_element_type=F32)
               for h in heads]
        yield MLP_PHASES_AFTER["ml_stage"][2]

        for h in heads:
            cols = slice(h * head_dim, (h + 1) * head_dim)
            hh = nd[h][:, :head_dim] / jnp.maximum(jnp.abs(nd[h][:, head_dim:]), jnp.exp(-ms_b[h]))
            hh = _sigmoid(o_all[rows, cols]) * hh
            y_ref[rows, d_rg + h * head_dim:d_rg + (h + 1) * head_dim] = _rms(
                hh, ml_gain_ref[:, cols]).astype(BF16)
            s_old = row_ref[4, h:h + 1, :]
            s_new = row_ref[5, h:h + 1, :]
            c_ref[h] = (jnp.concatenate([s_old, s_old], axis=1) * c_ref[h]
                        + jnp.concatenate([s_new, s_new], axis=1) * upd[h])
        yield MLP_PHASES_AFTER["ml_stage"][3]

    mix = jnp.dot(y_ref[...], w_out_ref[...], preferred_element_type=F32)
    yield MLP_PHASES_AFTER["w_out"]
    return x + _rms(mix, post_g_ref[...])


def _mlp_tile(h_ref, pre_g_ref, w_up_ref, w_down_ref, post_g_ref):
    vb = _rms(h_ref[...], pre_g_ref[...]).astype(BF16)
    d_ff = w_up_ref.shape[1]
    acc = jnp.zeros(h_ref.shape, F32)
    for c in range(d_ff // MLP_FF_CHUNK):
        cols = slice(c * MLP_FF_CHUNK, (c + 1) * MLP_FF_CHUNK)
        f = jnp.maximum(jnp.dot(vb, w_up_ref[:, cols], preferred_element_type=F32), 0.0)
        yield
        acc = acc + jnp.dot((f * f).astype(BF16), w_down_ref[cols, :], preferred_element_type=F32)
        yield
    return h_ref[...] + _rms(acc, post_g_ref[...])


def _run(gen):
    while True:
        try:
            next(gen)
        except StopIteration as stop:
            return stop.value


def _interleave(primary, secondary, lead=0, per=1):
    results = [None, None]

    def advance(idx, gen):
        if results[idx] is None:
            try:
                return next(gen)
            except StopIteration as stop:
                results[idx] = (stop.value,)
        return 0

    for _ in range(lead):
        advance(1, secondary)
    while results[0] is None:
        for _ in range(advance(0, primary) * per):
            advance(1, secondary)
    while results[1] is None:
        advance(1, secondary)
    return results[0][0], results[1][0]


def _weight_copy_jobs(hbm, vmem, src_row_starts=None):
    rows, cols = vmem.shape
    jobs = []
    if src_row_starts is None:
        for cb in range(cols // WEIGHT_BLOCK):
            for rb in range(rows // WEIGHT_ROWS):
                r0, c0 = rb * WEIGHT_ROWS, cb * WEIGHT_BLOCK
                jobs.append((hbm, r0, c0, vmem, r0, c0, False))
    else:
        for cb, start in enumerate(src_row_starts):
            for sub in range(WEIGHT_BLOCK // WEIGHT_ROWS):
                for kb in range(rows // WEIGHT_BLOCK):
                    jobs.append((hbm, start + sub * WEIGHT_ROWS, kb * WEIGHT_BLOCK,
                                 vmem, kb * WEIGHT_BLOCK, cb * WEIGHT_BLOCK + sub * WEIGHT_ROWS, True))
    return jobs


def _load_weights(jobs, stage_ref, sem_ref):
    ahead = WEIGHT_SLOTS - 1

    def copy(i):
        src, r0, c0 = jobs[i][:3]
        slot = i % WEIGHT_SLOTS
        return pltpu.make_async_copy(
            src.at[pl.ds(r0, WEIGHT_ROWS), pl.ds(c0, WEIGHT_BLOCK)], stage_ref.at[slot], sem_ref.at[slot])

    for i in range(min(ahead, len(jobs))):
        copy(i).start()
    for i in range(len(jobs)):
        if i + ahead < len(jobs):
            copy(i + ahead).start()
        copy(i).wait()
        dst, r0, c0, transposed = jobs[i][3:]
        block = stage_ref[i % WEIGHT_SLOTS]
        if transposed:
            dst[r0:r0 + WEIGHT_BLOCK, c0:c0 + WEIGHT_ROWS] = block.T.astype(BF16)
        else:
            dst[r0:r0 + WEIGHT_ROWS, c0:c0 + WEIGHT_BLOCK] = block.astype(BF16)
        yield


def _layer_kernel(x_ref, *refs, tiles_per_seq, n_tiles, n_small, w_in_row_starts, **dims):
    (pre_g, w_if, b_if, conv_w, conv_b, w_gate, b_r, b_i, lam, rg_gain, ml_gain, post_g,
     mlp_pre_g, mlp_post_g) = refs[:n_small]
    w_in_t_hbm, w_out_hbm, w_up_hbm, w_down_hbm = refs[n_small:n_small + 4]
    out_ref = refs[n_small + 4]
    w_main, w_out, w_up, w_down, stage_ref, sem_ref, h1_ref = refs[n_small + 5:n_small + 12]
    state = refs[n_small + 12:]
    s = pl.program_id(0)

    @pl.when(s == 0)
    def _():
        h1_ref[...] = jnp.zeros_like(h1_ref)
        _run(_load_weights(_weight_copy_jobs(w_in_t_hbm, w_main, w_in_row_starts)
                           + _weight_copy_jobs(w_out_hbm, w_out) + _weight_copy_jobs(w_up_hbm, w_up)
                           + _weight_copy_jobs(w_down_hbm, w_down), stage_ref, sem_ref))

    h1_new, out = _interleave(
        _mixer_tile(x_ref[0], s % tiles_per_seq == 0, pre_g, w_main, w_if, b_if, conv_w, conv_b,
                    w_gate, b_r, b_i, lam, rg_gain, ml_gain, w_out, post_g, *state, **dims),
        _mlp_tile(h1_ref, mlp_pre_g, w_up, w_down, mlp_post_g),
        MLP_PHASES_AFTER["start"])
    out_ref[0] = out
    h1_ref[...] = h1_new


def _const_spec(shape):
    zeros = (0,) * len(shape)
    return pl.BlockSpec(shape, lambda *_: zeros, pipeline_mode=pl.Buffered(1))


def _block_diag_gate(w):
    nb, bd, _ = w.shape
    per = MXU_WIDTH // bd
    w = w.reshape(nb // per, per, bd, bd)
    eye = jnp.eye(per, dtype=w.dtype)
    return jnp.einsum("gpij,pq->gpiqj", w, eye).reshape(nb // per, MXU_WIDTH, MXU_WIDTH)


def _layer(x, pre_gain, w_in, rg_conv_w, rg_conv_b, gate_r_w, gate_r_b, gate_i_w, gate_i_b, lam,
           ml_conv_w, ml_conv_b, igate_b, fgate_b, rg_gain, ml_gain, w_out, post_gain,
           pre_mlp_gain, w_up, w_down, post_mlp_gain):
    bsz, seq, d_model = x.shape
    assert w_up.shape[1] % MLP_FF_CHUNK == 0
    d_rg = lam.shape[0]
    d_ml = ml_gain.shape[0]
    head_dim = d_ml // ML_HEADS
    assert head_dim == LANES and ML_CHUNK == LANES and 2 * ML_HEADS == SUBLANES
    assert seq % SEQ_TILE == 0 and SEQ_TILE % ML_CHUNK == 0 and d_rg % MXU_WIDTH == 0
    d_conv = d_rg + 2 * d_ml

    assert d_rg == WEIGHT_BLOCK and d_ml == WEIGHT_BLOCK
    starts = dict(rg_x=0, rg_gate=d_rg, q=2 * d_rg, k=2 * d_rg + d_ml, v=2 * d_rg + 2 * d_ml, o=2 * d_rg + 3 * d_ml)
    w_in_row_starts = tuple(starts[n] for n in ("rg_x", "q", "k", "rg_gate", "v", "o"))
    d_main = 2 * d_rg + 4 * d_ml
    if_w = w_in[:, d_main:]
    w_if = jnp.pad(if_w, ((0, 0), (0, LANES - 2 * ML_HEADS))).astype(BF16)
    b_if = jnp.pad(jnp.concatenate([igate_b, fgate_b]), (0, LANES - 2 * ML_HEADS)).reshape(1, LANES)
    conv_w = jnp.concatenate([rg_conv_w, ml_conv_w], axis=1)
    conv_b = jnp.concatenate([rg_conv_b, ml_conv_b]).reshape(1, d_conv)
    w_gate = jnp.concatenate([_block_diag_gate(gate_r_w), _block_diag_gate(gate_i_w)], axis=2).astype(BF16)

    row = lambda v: v.reshape(1, -1)
    small = (row(pre_gain), w_if, b_if, conv_w, conv_b, w_gate, row(gate_r_b), row(gate_i_b), row(lam),
             row(rg_gain), row(ml_gain), row(post_gain), row(pre_mlp_gain), row(post_mlp_gain))
    big = (jnp.swapaxes(w_in, 0, 1), w_out, w_up, w_down)
    d_ff = w_up.shape[1]
    for w in (w_out, w_up, w_down):
        assert w.shape[0] % WEIGHT_BLOCK == 0 and w.shape[1] % WEIGHT_BLOCK == 0

    tiles_per_seq = seq // SEQ_TILE
    n_tiles = bsz * tiles_per_seq

    def tile_block(t):
        return (t // tiles_per_seq, t % tiles_per_seq, 0)

    x_spec = pl.BlockSpec((1, SEQ_TILE, d_model), lambda s: tile_block(jnp.minimum(s, n_tiles - 1)))
    out_spec = pl.BlockSpec((1, SEQ_TILE, d_model), lambda s: tile_block(jnp.maximum(s - 1, 0)))
    in_specs = ([x_spec] + [_const_spec(op.shape) for op in small]
                + [pl.BlockSpec(memory_space=pl.ANY)] * len(big))
    kern = functools.partial(_layer_kernel, tiles_per_seq=tiles_per_seq, n_tiles=n_tiles, n_small=len(small),
                             w_in_row_starts=w_in_row_starts, d_rg=d_rg, d_ml=d_ml, head_dim=head_dim)
    return pl.pallas_call(
        kern,
        grid=(n_tiles + 1,),
        in_specs=in_specs,
        out_specs=out_spec,
        out_shape=jax.ShapeDtypeStruct(x.shape, x.dtype),
        scratch_shapes=[
            pltpu.VMEM((d_model, d_main), BF16),
            pltpu.VMEM((d_rg + d_ml, d_model), BF16),
            pltpu.VMEM((d_model, d_ff), BF16),
            pltpu.VMEM((d_ff, d_model), BF16),
            pltpu.VMEM((WEIGHT_SLOTS, WEIGHT_ROWS, WEIGHT_BLOCK), F32),
            pltpu.SemaphoreType.DMA((WEIGHT_SLOTS,)),
            pltpu.VMEM((SEQ_TILE, d_model), F32),
            pltpu.VMEM(((d_conv + d_rg) // LANES, SUBLANES + SEQ_TILE, LANES), F32),
            pltpu.VMEM((2 * d_ml // LANES, SEQ_TILE, LANES), F32),
            pltpu.VMEM((d_rg // LANES, SEQ_TILE, LANES), F32),
            pltpu.VMEM((SUBLANES, d_rg), F32),
            pltpu.VMEM((ML_HEADS, head_dim, 2 * head_dim), F32),
            pltpu.VMEM((SUBLANES, ML_CHUNK), F32),
            pltpu.VMEM((7, SUBLANES, ML_CHUNK), F32),
            pltpu.VMEM((SEQ_TILE, d_rg + d_ml), BF16),
        ],
        compiler_params=pltpu.CompilerParams(
            dimension_semantics=("arbitrary",), vmem_limit_bytes=VMEM_LIMIT_BYTES),
        name="layer",
    )(x, *small, *big)


def kernel(x, pre_mix_gain, w_in, rg_conv_w, rg_conv_b, rg_gate_r_w, rg_gate_r_b, rg_gate_i_w, rg_gate_i_b, rg_lambda, ml_conv_w, ml_conv_b, ml_igate_b, ml_fgate_b, rg_out_gain, ml_out_gain, w_out, post_mix_gain, pre_mlp_gain, mlp_w_up, mlp_w_down, post_mlp_gain):
    h = x
    for l in range(w_in.shape[0]):
        h = _layer(h, pre_mix_gain[l], w_in[l], rg_conv_w[l], rg_conv_b[l], rg_gate_r_w[l], rg_gate_r_b[l],
                   rg_gate_i_w[l], rg_gate_i_b[l], rg_lambda[l], ml_conv_w[l], ml_conv_b[l], ml_igate_b[l],
                   ml_fgate_b[l], rg_out_gain[l], ml_out_gain[l], w_out[l], post_mix_gain[l],
                   pre_mlp_gain[l], mlp_w_up[l], mlp_w_down[l], post_mlp_gain[l])
    return h
```

```python
import functools

import jax
import jax.numpy as jnp
from jax.experimental import pallas as pl
from jax.experimental.pallas import tpu as pltpu

F32 = jnp.float32
BF16 = jnp.bfloat16

RG_BLOCKS = 8
RG_C = 8.0
ML_HEADS = 4
CONV_WIDTH = 4
EPS = 1e-6

SUBLANES = 8
LANES = 128
MXU_WIDTH = 256

SEQ_TILE = 512
ML_CHUNK = LANES
PHASES = 4
MLP_FF_CHUNK = 1024
WEIGHT_ROWS = 256
WEIGHT_BLOCK = 512
WEIGHT_SLOTS = 8
MLP_PHASES_AFTER = {"start": 0, "w_in": 1, "conv": (0, 1, 0, 0, 1, 0), "rg_gates": 0, "rg_scan": (1, 0, 0),
                    "ml_stage": (1, 0, 0, 0), "w_out": 0}
VMEM_LIMIT_BYTES = 60 * 1024 * 1024


def _rms(x, gain):
    return x * jax.lax.rsqrt(jnp.mean(x * x, axis=-1, keepdims=True) + EPS) * gain


LOG2E = 1.4426950408889634


def _sigmoid(x):
    return 1.0 / (1.0 + jnp.exp2(x * (-LOG2E)))


def _softplus(x):
    return jnp.maximum(x, 0.0) + jnp.log1p(jnp.exp(-jnp.abs(x)))


def _gelu_tanh(x):
    c = 0.7978845608028654
    return 0.5 * x * (1.0 + jnp.tanh(c * (x + 0.044715 * (x * x * x))))


def _lane_scan(x, op, fill):
    n = x.shape[1]
    lane = jax.lax.broadcasted_iota(jnp.int32, x.shape, 1)
    d = 1
    while d < n:
        shifted = jnp.where(lane < d, fill, pltpu.roll(x, d, axis=1))
        x = op(x, shifted)
        d *= 2
    return x


def _affine_row_scan(sa, sb):
    n, width = sa.shape
    row8 = jax.lax.broadcasted_iota(jnp.int32, (SUBLANES, width), 0)
    d = 1
    while d < n:
        if d < SUBLANES:
            ra = pltpu.roll(sa, d, axis=0)
            rb = pltpu.roll(sb, d, axis=0)
            a_sh = jnp.concatenate([jnp.where(row8 < d, 1.0, ra[:SUBLANES]), ra[SUBLANES:]], axis=0)
            b_sh = jnp.concatenate([jnp.where(row8 < d, 0.0, rb[:SUBLANES]), rb[SUBLANES:]], axis=0)
            sb = sb + sa * b_sh
            sa = sa * a_sh
        else:
            sb = jnp.concatenate([sb[:d], sb[d:] + sa[d:] * sb[:n - d]], axis=0)
            sa = jnp.concatenate([sa[:d], sa[d:] * sa[:n - d]], axis=0)
        d *= 2
    return sa, sb


def _mixer_tile(x, seq_start, pre_g_ref, w_main_ref, w_if_ref, b_if_ref, rg_cw_ref, ml_cw_ref, rg_cb_ref, ml_cb_ref,
                w_gate_ref,
                b_r_ref, b_i_ref, lam_ref, rg_gain_ref, ml_gain_ref, w_out_ref, post_g_ref,
                cbuf_ref, qk_ref, yrg_ref, h_ref, c_ref, m_ref, row_ref, y_ref,
                *, d_rg, d_ml, head_dim):
    ts = x.shape[0]
    d_conv = d_rg + 2 * d_ml
    n_chunks = ts // ML_CHUNK
    L = ML_CHUNK

    @pl.when(seq_start)
    def _():
        cbuf_ref[:, ts:ts + SUBLANES, :] = jnp.zeros((cbuf_ref.shape[0], SUBLANES, LANES), F32)
        h_ref[...] = jnp.zeros_like(h_ref)
        c_ref[...] = jnp.zeros_like(c_ref)
        m_ref[...] = jnp.zeros_like(m_ref)

    ub = _rms(x, pre_g_ref[...]).astype(BF16)
    proj = jnp.dot(ub, w_main_ref[...], preferred_element_type=F32)
    gates = jnp.dot(ub, w_if_ref[...], preferred_element_type=F32) + b_if_ref[...]
    yield MLP_PHASES_AFTER["w_in"]

    n_cs = d_conv // LANES
    n_rs = d_rg // LANES
    G = ts // PHASES
    cbuf_ref[:n_cs, :SUBLANES, :] = cbuf_ref[:n_cs, ts:ts + SUBLANES, :]
    for j in range(n_cs + n_rs):
        cbuf_ref[j, SUBLANES:, :] = proj[:, j * LANES:(j + 1) * LANES]

    def phase(j, e):
        return cbuf_ref[j, pl.ds(SUBLANES + e, G, stride=PHASES), :]

    conv_ph = [[None] * n_cs for _ in range(PHASES)]
    for j in range(n_cs):
        conv_w_ref, conv_b_ref, jj = (rg_cw_ref, rg_cb_ref, j) if j < n_rs else (ml_cw_ref, ml_cb_ref, j - n_rs)
        cols = slice(jj * LANES, (jj + 1) * LANES)
        taps = {e: phase(j, e) for e in range(1 - CONV_WIDTH, PHASES)}
        for r in range(PHASES):
            acc = taps[r] * conv_w_ref[CONV_WIDTH - 1:CONV_WIDTH, cols] + conv_b_ref[:, cols]
            for k in range(1, CONV_WIDTH):
                acc = acc + taps[r - k] * conv_w_ref[CONV_WIDTH - 1 - k:CONV_WIDTH - k, cols]
            conv_ph[r][j] = acc
        if j >= n_rs:
            scale = head_dim ** -0.5 if j < n_rs + d_ml // LANES else 1.0
            for r in range(PHASES):
                val = conv_ph[r][j]
                val = val * _sigmoid(val)
                qk_ref[j - n_rs, pl.ds(r, G, stride=PHASES), :] = val * scale if scale != 1.0 else val
        if j % 2 == 1:
            yield MLP_PHASES_AFTER["conv"][j // 2]

    xc = jnp.concatenate([jnp.concatenate(conv_ph[r][:n_rs], axis=1) for r in range(PHASES)], axis=0)
    r_parts, i_parts = [], []
    for g in range(d_rg // MXU_WIDTH):
        gg = jnp.dot(xc[:, g * MXU_WIDTH:(g + 1) * MXU_WIDTH].astype(BF16), w_gate_ref[g],
                     preferred_element_type=F32)
        r_parts.append(gg[:, :MXU_WIDTH])
        i_parts.append(gg[:, MXU_WIDTH:])
    r = _sigmoid(jnp.concatenate(r_parts, axis=1) + b_r_ref[...])
    i_gate = _sigmoid(jnp.concatenate(i_parts, axis=1) + b_i_ref[...])
    neg_log_a_unit = RG_C * _softplus(-lam_ref[...])
    a = jnp.exp2(r * (neg_log_a_unit * (-LOG2E)))
    z = jnp.tanh(r * neg_log_a_unit) * (a * a + 1.0)
    b_in = jnp.where(z > 0.0, z * jax.lax.rsqrt(z), 0.0) * (i_gate * xc)
    yield MLP_PHASES_AFTER["rg_gates"]

    comp_a, comp_b = [a[:G]], [b_in[:G]]
    for r in range(1, PHASES):
        a_r, b_r = a[r * G:(r + 1) * G], b_in[r * G:(r + 1) * G]
        comp_b.append(a_r * comp_b[-1] + b_r)
        comp_a.append(a_r * comp_a[-1])
    ga, gb = _affine_row_scan(comp_a[-1], comp_b[-1])
    carry = h_ref[SUBLANES - 1:SUBLANES, :]
    h_end = ga * carry + gb
    h_ref[...] = h_end[G - SUBLANES:]
    rolled = pltpu.roll(h_end, 1, axis=0)
    row8r = jax.lax.broadcasted_iota(jnp.int32, (SUBLANES, d_rg), 0)
    h_prev = jnp.concatenate([jnp.where(row8r < 1, carry, rolled[:SUBLANES]), rolled[SUBLANES:]], axis=0)
    yield MLP_PHASES_AFTER["rg_scan"][0]
    for r in range(PHASES):
        h_r = comp_a[r] * h_prev + comp_b[r]
        gate_r = jnp.concatenate([phase(n_cs + j, r) for j in range(n_rs)], axis=1)
        y_r = _rms(h_r * _gelu_tanh(gate_r), rg_gain_ref[...])
        for j in range(n_rs):
            yrg_ref[j, pl.ds(r, G, stride=PHASES), :] = y_r[:, j * LANES:(j + 1) * LANES]
        if r % 2 == 1:
            yield MLP_PHASES_AFTER["rg_scan"][1 + r // 2]
    for j in range(n_rs):
        y_ref[:, j * LANES:(j + 1) * LANES] = yrg_ref[j].astype(BF16)

    v_all = proj[:, d_conv + d_rg:d_conv + d_rg + d_ml]
    o_all = proj[:, d_conv + d_rg + d_ml:]

    lane_g = jax.lax.broadcasted_iota(jnp.int32, gates.shape, 1)
    log_sig = jnp.minimum(gates, 0.0) - jnp.log1p(jnp.exp(-jnp.abs(gates)))
    gates_t = jnp.where(lane_g < ML_HEADS, gates, log_sig).T[:SUBLANES]

    causal = (jax.lax.broadcasted_iota(jnp.int32, (L, L), 1)
              <= jax.lax.broadcasted_iota(jnp.int32, (L, L), 0))
    ones_ext = jnp.ones((L, head_dim), BF16)

    for c in range(n_chunks):
        rows = slice(c * L, (c + 1) * L)
        li = gates_t[:, rows]
        lf = pltpu.roll(li, ML_HEADS, axis=0)
        bcum = _lane_scan(lf, jnp.add, 0.0)
        b_last = jnp.sum(lf, axis=1, keepdims=True)
        row_b = li - bcum
        cmax = _lane_scan(row_b, jnp.maximum, -jnp.inf)
        w_loc = b_last + row_b
        m_loc = jnp.max(w_loc, axis=1, keepdims=True)
        m_prev_b = m_ref[...]
        m_prev = jnp.max(m_prev_b, axis=1, keepdims=True)
        m_s = jnp.maximum(bcum + m_prev, bcum + cmax)
        m_new = jnp.maximum(b_last + m_prev, m_loc)
        row_ref[0] = bcum - m_s
        row_ref[1] = m_s
        row_ref[2] = row_b
        row_ref[3] = jnp.exp(w_loc - m_loc)
        row_ref[4] = jnp.broadcast_to(jnp.exp(b_last + m_prev - m_new), (SUBLANES, L))
        row_ref[5] = jnp.broadcast_to(jnp.exp(m_loc - m_new), (SUBLANES, L))
        row_ref[6] = m_prev_b
        m_ref[...] = jnp.broadcast_to(m_new, (SUBLANES, L))

        heads = range(ML_HEADS)
        q_f = [qk_ref[h, rows, :] for h in heads]
        k_t = [qk_ref[ML_HEADS + h, rows, :].T for h in heads]
        s = [jnp.dot(q_f[h].astype(BF16), k_t[h].astype(BF16), preferred_element_type=F32) for h in heads]
        yield MLP_PHASES_AFTER["ml_stage"][0]

        lhs, rhs, v_ext, ms_b = [], [], [], []
        for h in heads:
            cols = slice(h * head_dim, (h + 1) * head_dim)
            ca_b = jnp.broadcast_to(row_ref[0, h:h + 1, :], (head_dim, L)).T
            ms_b.append(jnp.broadcast_to(row_ref[1, h:h + 1, :], (head_dim, L)).T)
            p = jnp.exp(jnp.where(causal, ca_b + row_ref[2, h:h + 1, :], -jnp.inf))
            inter_w = jnp.exp(ca_b + row_ref[6, h:h + 1, :])
            v_ext.append(jnp.concatenate([v_all[rows, cols].astype(BF16), ones_ext], axis=1))
            lhs.append(jnp.concatenate([(s[h] * p).astype(BF16), (q_f[h] * inter_w).astype(BF16)], axis=1))
            rhs.append(jnp.concatenate([v_ext[h], c_ref[h].astype(BF16)], axis=0))
        yield MLP_PHASES_AFTER["ml_stage"][1]

        nd = [jnp.dot(lhs[h], rhs[h], preferred_element_type=F32) for h in heads]
        upd = [jnp.dot((k_t[h] * row_ref[3, h:h + 1, :]).astype(BF16), v_ext[h], preferred_element_type=F32)
               for h in heads]
        yield MLP_PHASES_AFTER["ml_stage"][2]

        for h in heads:
            cols = slice(h * head_dim, (h + 1) * head_dim)
            hh = nd[h][:, :head_dim] / jnp.maximum(jnp.abs(nd[h][:, head_dim:]), jnp.exp(-ms_b[h]))
            hh = _sigmoid(o_all[rows, cols]) * hh
            y_ref[rows, d_rg + h * head_dim:d_rg + (h + 1) * head_dim] = _rms(
                hh, ml_gain_ref[:, cols]).astype(BF16)
            s_old = row_ref[4, h:h + 1, :]
            s_new = row_ref[5, h:h + 1, :]
            c_ref[h] = (jnp.concatenate([s_old, s_old], axis=1) * c_ref[h]
                        + jnp.concatenate([s_new, s_new], axis=1) * upd[h])
        yield MLP_PHASES_AFTER["ml_stage"][3]

    mix = jnp.dot(y_ref[...], w_out_ref[...], preferred_element_type=F32)
    yield MLP_PHASES_AFTER["w_out"]
    return x + _rms(mix, post_g_ref[...])


def _mlp_tile(h_ref, pre_g_ref, w_up_ref, w_down_ref, post_g_ref):
    vb = _rms(h_ref[...], pre_g_ref[...]).astype(BF16)
    d_ff = w_up_ref.shape[1]
    acc = jnp.zeros(h_ref.shape, F32)
    for c in range(d_ff // MLP_FF_CHUNK):
        cols = slice(c * MLP_FF_CHUNK, (c + 1) * MLP_FF_CHUNK)
        f = jnp.maximum(jnp.dot(vb, w_up_ref[:, cols], preferred_element_type=F32), 0.0)
        yield
        acc = acc + jnp.dot((f * f).astype(BF16), w_down_ref[cols, :], preferred_element_type=F32)
        yield
    return h_ref[...] + _rms(acc, post_g_ref[...])


def _run(gen):
    while True:
        try:
            next(gen)
        except StopIteration as stop:
            return stop.value


def _interleave(primary, secondary, lead=0, per=1):
    results = [None, None]

    def advance(idx, gen):
        if results[idx] is None:
            try:
                return next(gen)
            except StopIteration as stop:
                results[idx] = (stop.value,)
        return 0

    for _ in range(lead):
        advance(1, secondary)
    while results[0] is None:
        for _ in range(advance(0, primary) * per):
            advance(1, secondary)
    while results[1] is None:
        advance(1, secondary)
    return results[0][0], results[1][0]


def _weight_copy_jobs(hbm, vmem, src_row_starts=None):
    rows, cols = vmem.shape
    jobs = []
    if src_row_starts is None:
        for cb in range(cols // WEIGHT_BLOCK):
            for rb in range(rows // WEIGHT_ROWS):
                r0, c0 = rb * WEIGHT_ROWS, cb * WEIGHT_BLOCK
                jobs.append((hbm, r0, c0, vmem, r0, c0, False))
    else:
        for cb, start in enumerate(src_row_starts):
            for sub in range(WEIGHT_BLOCK // WEIGHT_ROWS):
                for kb in range(rows // WEIGHT_BLOCK):
                    jobs.append((hbm, start + sub * WEIGHT_ROWS, kb * WEIGHT_BLOCK,
                                 vmem, kb * WEIGHT_BLOCK, cb * WEIGHT_BLOCK + sub * WEIGHT_ROWS, True))
    return jobs


def _load_weights(jobs, stage_ref, sem_ref):
    ahead = WEIGHT_SLOTS - 1

    def copy(i):
        src, r0, c0 = jobs[i][:3]
        slot = i % WEIGHT_SLOTS
        return pltpu.make_async_copy(
            src.at[pl.ds(r0, WEIGHT_ROWS), pl.ds(c0, WEIGHT_BLOCK)], stage_ref.at[slot], sem_ref.at[slot])

    for i in range(min(ahead, len(jobs))):
        copy(i).start()
    for i in range(len(jobs)):
        if i + ahead < len(jobs):
            copy(i + ahead).start()
        copy(i).wait()
        dst, r0, c0, transposed = jobs[i][3:]
        block = stage_ref[i % WEIGHT_SLOTS]
        if transposed:
            dst[r0:r0 + WEIGHT_BLOCK, c0:c0 + WEIGHT_ROWS] = block.T.astype(BF16)
        else:
            dst[r0:r0 + WEIGHT_ROWS, c0:c0 + WEIGHT_BLOCK] = block.astype(BF16)
        yield


def _layer_kernel(x_ref, *refs, tiles_per_seq, n_tiles, n_small, w_in_row_starts, **dims):
    (pre_g, w_if, b_if, rg_cw, ml_cw, rg_cb, ml_cb, w_gate, b_r, b_i, lam, rg_gain, ml_gain, post_g,
     mlp_pre_g, mlp_post_g) = refs[:n_small]
    w_in_t_hbm, w_out_hbm, w_up_hbm, w_down_hbm = refs[n_small:n_small + 4]
    out_ref = refs[n_small + 4]
    w_main, w_out, w_up, w_down, stage_ref, sem_ref, h1_ref = refs[n_small + 5:n_small + 12]
    state = refs[n_small + 12:]
    s = pl.program_id(0)

    @pl.when(s == 0)
    def _():
        h1_ref[...] = jnp.zeros_like(h1_ref)
        _run(_load_weights(_weight_copy_jobs(w_in_t_hbm, w_main, w_in_row_starts)
                           + _weight_copy_jobs(w_out_hbm, w_out) + _weight_copy_jobs(w_up_hbm, w_up)
                           + _weight_copy_jobs(w_down_hbm, w_down), stage_ref, sem_ref))

    h1_new, out = _interleave(
        _mixer_tile(x_ref[0], s % tiles_per_seq == 0, pre_g, w_main, w_if, b_if, rg_cw, ml_cw, rg_cb, ml_cb,
                    w_gate, b_r, b_i, lam, rg_gain, ml_gain, w_out, post_g, *state, **dims),
        _mlp_tile(h1_ref, mlp_pre_g, w_up, w_down, mlp_post_g),
        MLP_PHASES_AFTER["start"])
    out_ref[0] = out
    h1_ref[...] = h1_new


def _const_spec(shape):
    zeros = (0,) * len(shape)
    return pl.BlockSpec(shape, lambda *_: zeros, pipeline_mode=pl.Buffered(1))


def _block_diag_gates(w_r, w_i):
    nb, bd, _ = w_r.shape
    per = MXU_WIDTH // bd
    w = jnp.stack([w_r, w_i], axis=1).reshape(nb // per, per, 2, bd, bd)
    eye = jnp.eye(per, dtype=w.dtype)
    return jnp.einsum("gptij,pq->gpitqj", w, eye).reshape(nb // per, MXU_WIDTH, 2 * MXU_WIDTH)


def _layer(x, pre_gain, w_in, rg_conv_w, rg_conv_b, gate_r_w, gate_r_b, gate_i_w, gate_i_b, lam,
           ml_conv_w, ml_conv_b, igate_b, fgate_b, rg_gain, ml_gain, w_out, post_gain,
           pre_mlp_gain, w_up, w_down, post_mlp_gain):
    bsz, seq, d_model = x.shape
    assert w_up.shape[1] % MLP_FF_CHUNK == 0
    d_rg = lam.shape[0]
    d_ml = ml_gain.shape[0]
    head_dim = d_ml // ML_HEADS
    assert head_dim == LANES and ML_CHUNK == LANES and 2 * ML_HEADS == SUBLANES
    assert seq % SEQ_TILE == 0 and SEQ_TILE % ML_CHUNK == 0 and d_rg % MXU_WIDTH == 0
    d_conv = d_rg + 2 * d_ml

    assert d_rg == WEIGHT_BLOCK and d_ml == WEIGHT_BLOCK
    starts = dict(rg_x=0, rg_gate=d_rg, q=2 * d_rg, k=2 * d_rg + d_ml, v=2 * d_rg + 2 * d_ml, o=2 * d_rg + 3 * d_ml)
    w_in_row_starts = tuple(starts[n] for n in ("rg_x", "q", "k", "rg_gate", "v", "o"))
    d_main = 2 * d_rg + 4 * d_ml
    if_w = w_in[:, d_main:]
    w_if = jnp.pad(if_w, ((0, 0), (0, LANES - 2 * ML_HEADS))).astype(BF16)
    b_if = jnp.pad(jnp.concatenate([igate_b, fgate_b]), (0, LANES - 2 * ML_HEADS)).reshape(1, LANES)
    w_gate = _block_diag_gates(gate_r_w, gate_i_w).astype(BF16)

    row = lambda v: v.reshape(1, -1)
    small = (row(pre_gain), w_if, b_if, rg_conv_w, ml_conv_w, row(rg_conv_b), row(ml_conv_b), w_gate,
             row(gate_r_b), row(gate_i_b), row(lam),
             row(rg_gain), row(ml_gain), row(post_gain), row(pre_mlp_gain), row(post_mlp_gain))
    big = (jnp.swapaxes(w_in, 0, 1), w_out, w_up, w_down)
    d_ff = w_up.shape[1]
    for w in (w_out, w_up, w_down):
        assert w.shape[0] % WEIGHT_BLOCK == 0 and w.shape[1] % WEIGHT_BLOCK == 0

    tiles_per_seq = seq // SEQ_TILE
    n_tiles = bsz * tiles_per_seq

    def tile_block(t):
        return (t // tiles_per_seq, t % tiles_per_seq, 0)

    x_spec = pl.BlockSpec((1, SEQ_TILE, d_model), lambda s: tile_block(jnp.minimum(s, n_tiles - 1)))
    out_spec = pl.BlockSpec((1, SEQ_TILE, d_model), lambda s: tile_block(jnp.maximum(s - 1, 0)))
    in_specs = ([x_spec] + [_const_spec(op.shape) for op in small]
                + [pl.BlockSpec(memory_space=pl.ANY)] * len(big))
    kern = functools.partial(_layer_kernel, tiles_per_seq=tiles_per_seq, n_tiles=n_tiles, n_small=len(small),
                             w_in_row_starts=w_in_row_starts, d_rg=d_rg, d_ml=d_ml, head_dim=head_dim)
    return pl.pallas_call(
        kern,
        grid=(n_tiles + 1,),
        in_specs=in_specs,
        out_specs=out_spec,
        out_shape=jax.ShapeDtypeStruct(x.shape, x.dtype),
        scratch_shapes=[
            pltpu.VMEM((d_model, d_main), BF16),
            pltpu.VMEM((d_rg + d_ml, d_model), BF16),
            pltpu.VMEM((d_model, d_ff), BF16),
            pltpu.VMEM((d_ff, d_model), BF16),
            pltpu.VMEM((WEIGHT_SLOTS, WEIGHT_ROWS, WEIGHT_BLOCK), F32),
            pltpu.SemaphoreType.DMA((WEIGHT_SLOTS,)),
            pltpu.VMEM((SEQ_TILE, d_model), F32),
            pltpu.VMEM(((d_conv + d_rg) // LANES, SUBLANES + SEQ_TILE, LANES), F32),
            pltpu.VMEM((2 * d_ml // LANES, SEQ_TILE, LANES), F32),
            pltpu.VMEM((d_rg // LANES, SEQ_TILE, LANES), F32),
            pltpu.VMEM((SUBLANES, d_rg), F32),
            pltpu.VMEM((ML_HEADS, head_dim, 2 * head_dim), F32),
            pltpu.VMEM((SUBLANES, ML_CHUNK), F32),
            pltpu.VMEM((7, SUBLANES, ML_CHUNK), F32),
            pltpu.VMEM((SEQ_TILE, d_rg + d_ml), BF16),
        ],
        compiler_params=pltpu.CompilerParams(
            dimension_semantics=("arbitrary",), vmem_limit_bytes=VMEM_LIMIT_BYTES),
        name="layer",
    )(x, *small, *big)


def kernel(x, pre_mix_gain, w_in, rg_conv_w, rg_conv_b, rg_gate_r_w, rg_gate_r_b, rg_gate_i_w, rg_gate_i_b, rg_lambda, ml_conv_w, ml_conv_b, ml_igate_b, ml_fgate_b, rg_out_gain, ml_out_gain, w_out, post_mix_gain, pre_mlp_gain, mlp_w_up, mlp_w_down, post_mlp_gain):
    h = x
    for l in range(w_in.shape[0]):
        h = _layer(h, pre_mix_gain[l], w_in[l], rg_conv_w[l], rg_conv_b[l], rg_gate_r_w[l], rg_gate_r_b[l],
                   rg_gate_i_w[l], rg_gate_i_b[l], rg_lambda[l], ml_conv_w[l], ml_conv_b[l], ml_igate_b[l],
                   ml_fgate_b[l], rg_out_gain[l], ml_out_gain[l], w_out[l], post_mix_gain[l],
                   pre_mlp_gain[l], mlp_w_up[l], mlp_w_down[l], post_mlp_gain[l])
    return h
```

```python
import functools

import jax
import jax.numpy as jnp
from jax.experimental import pallas as pl
from jax.experimental.pallas import tpu as pltpu

F32 = jnp.float32
BF16 = jnp.bfloat16

RG_C = 8.0
ML_HEADS = 4
CONV_WIDTH = 4
EPS = 1e-6

SUBLANES = 8
LANES = 128
MXU_WIDTH = 256

SEQ_TILE = 512
ML_CHUNK = LANES
PHASES = 4
MLP_FF_CHUNK = 1024
WEIGHT_ROWS = 256
WEIGHT_BLOCK = 512
WEIGHT_SLOTS = 8
MLP_PHASES_AFTER = {"start": 0, "w_in": 1, "conv": (0, 1, 0, 0, 1, 0), "rg_gates": 0, "rg_scan": (1, 0, 0),
                    "ml_stage": (1, 0, 0, 0), "w_out": 0}
VMEM_LIMIT_BYTES = 60 * 1024 * 1024


def _rms(x, gain):
    return x * jax.lax.rsqrt(jnp.mean(x * x, axis=-1, keepdims=True) + EPS) * gain


LOG2E = 1.4426950408889634


def _sigmoid(x):
    return 1.0 / (1.0 + jnp.exp2(x * (-LOG2E)))


def _softplus(x):
    return jnp.maximum(x, 0.0) + jnp.log1p(jnp.exp(-jnp.abs(x)))


def _gelu_tanh(x):
    c = 0.7978845608028654
    return 0.5 * x * (1.0 + jnp.tanh(c * (x + 0.044715 * (x * x * x))))


def _lane_scan(x, op, fill):
    n = x.shape[1]
    lane = jax.lax.broadcasted_iota(jnp.int32, x.shape, 1)
    d = 1
    while d < n:
        shifted = jnp.where(lane < d, fill, pltpu.roll(x, d, axis=1))
        x = op(x, shifted)
        d *= 2
    return x


def _affine_row_scan(sa, sb):
    n, width = sa.shape
    row8 = jax.lax.broadcasted_iota(jnp.int32, (SUBLANES, width), 0)
    d = 1
    while d < n:
        if d < SUBLANES:
            ra = pltpu.roll(sa, d, axis=0)
            rb = pltpu.roll(sb, d, axis=0)
            a_sh = jnp.concatenate([jnp.where(row8 < d, 1.0, ra[:SUBLANES]), ra[SUBLANES:]], axis=0)
            b_sh = jnp.concatenate([jnp.where(row8 < d, 0.0, rb[:SUBLANES]), rb[SUBLANES:]], axis=0)
            sb = sb + sa * b_sh
            sa = sa * a_sh
        else:
            sb = jnp.concatenate([sb[:d], sb[d:] + sa[d:] * sb[:n - d]], axis=0)
            sa = jnp.concatenate([sa[:d], sa[d:] * sa[:n - d]], axis=0)
        d *= 2
    return sa, sb


def _mixer_tile(x, seq_start, pre_g_ref, w_main_ref, w_if_ref, b_if_ref, rg_cw_ref, ml_cw_ref, rg_cb_ref, ml_cb_ref,
                w_gate_ref,
                b_r_ref, b_i_ref, lam_ref, rg_gain_ref, ml_gain_ref, w_out_ref, post_g_ref,
                cbuf_ref, qk_ref, yrg_ref, h_ref, c_ref, m_ref, row_ref, y_ref,
                *, d_rg, d_ml, head_dim):
    ts = x.shape[0]
    d_conv = d_rg + 2 * d_ml
    n_chunks = ts // ML_CHUNK
    L = ML_CHUNK

    @pl.when(seq_start)
    def _():
        cbuf_ref[:, ts:ts + SUBLANES, :] = jnp.zeros((cbuf_ref.shape[0], SUBLANES, LANES), F32)
        h_ref[...] = jnp.zeros_like(h_ref)
        c_ref[...] = jnp.zeros_like(c_ref)
        m_ref[...] = jnp.zeros_like(m_ref)

    ub = _rms(x, pre_g_ref[...]).astype(BF16)
    proj = jnp.dot(ub, w_main_ref[...], preferred_element_type=F32)
    gates = jnp.dot(ub, w_if_ref[...], preferred_element_type=F32) + b_if_ref[...]
    yield MLP_PHASES_AFTER["w_in"]

    n_cs = d_conv // LANES
    n_rs = d_rg // LANES
    G = ts // PHASES
    cbuf_ref[:n_cs, :SUBLANES, :] = cbuf_ref[:n_cs, ts:ts + SUBLANES, :]
    for j in range(n_cs + n_rs):
        cbuf_ref[j, SUBLANES:, :] = proj[:, j * LANES:(j + 1) * LANES]

    def phase(j, e):
        return cbuf_ref[j, pl.ds(SUBLANES + e, G, stride=PHASES), :]

    conv_ph = [[None] * n_cs for _ in range(PHASES)]
    for j in range(n_cs):
        conv_w_ref, conv_b_ref, jj = (rg_cw_ref, rg_cb_ref, j) if j < n_rs else (ml_cw_ref, ml_cb_ref, j - n_rs)
        cols = slice(jj * LANES, (jj + 1) * LANES)
        taps = {e: phase(j, e) for e in range(1 - CONV_WIDTH, PHASES)}
        for r in range(PHASES):
            acc = taps[r] * conv_w_ref[CONV_WIDTH - 1:CONV_WIDTH, cols] + conv_b_ref[:, cols]
            for k in range(1, CONV_WIDTH):
                acc = acc + taps[r - k] * conv_w_ref[CONV_WIDTH - 1 - k:CONV_WIDTH - k, cols]
            conv_ph[r][j] = acc
        if j >= n_rs:
            scale = head_dim ** -0.5 if j < n_rs + d_ml // LANES else 1.0
            for r in range(PHASES):
                val = conv_ph[r][j]
                val = val * _sigmoid(val)
                qk_ref[j - n_rs, pl.ds(r, G, stride=PHASES), :] = val * scale if scale != 1.0 else val
        if j % 2 == 1:
            yield MLP_PHASES_AFTER["conv"][j // 2]

    xc = jnp.concatenate([jnp.concatenate(conv_ph[r][:n_rs], axis=1) for r in range(PHASES)], axis=0)
    r_parts, i_parts = [], []
    for g in range(d_rg // MXU_WIDTH):
        gg = jnp.dot(xc[:, g * MXU_WIDTH:(g + 1) * MXU_WIDTH].astype(BF16), w_gate_ref[g],
                     preferred_element_type=F32)
        r_parts.append(gg[:, :MXU_WIDTH])
        i_parts.append(gg[:, MXU_WIDTH:])
    r = _sigmoid(jnp.concatenate(r_parts, axis=1) + b_r_ref[...])
    i_gate = _sigmoid(jnp.concatenate(i_parts, axis=1) + b_i_ref[...])
    neg_log_a_unit = RG_C * _softplus(-lam_ref[...])
    a = jnp.exp2(r * (neg_log_a_unit * (-LOG2E)))
    z = jnp.tanh(r * neg_log_a_unit) * (a * a + 1.0)
    b_in = jnp.where(z > 0.0, z * jax.lax.rsqrt(z), 0.0) * (i_gate * xc)
    yield MLP_PHASES_AFTER["rg_gates"]

    comp_a, comp_b = [a[:G]], [b_in[:G]]
    for r in range(1, PHASES):
        a_r, b_r = a[r * G:(r + 1) * G], b_in[r * G:(r + 1) * G]
        comp_b.append(a_r * comp_b[-1] + b_r)
        comp_a.append(a_r * comp_a[-1])
    ga, gb = _affine_row_scan(comp_a[-1], comp_b[-1])
    carry = h_ref[SUBLANES - 1:SUBLANES, :]
    h_end = ga * carry + gb
    h_ref[...] = h_end[G - SUBLANES:]
    rolled = pltpu.roll(h_end, 1, axis=0)
    row8r = jax.lax.broadcasted_iota(jnp.int32, (SUBLANES, d_rg), 0)
    h_prev = jnp.concatenate([jnp.where(row8r < 1, carry, rolled[:SUBLANES]), rolled[SUBLANES:]], axis=0)
    yield MLP_PHASES_AFTER["rg_scan"][0]
    for r in range(PHASES):
        h_r = comp_a[r] * h_prev + comp_b[r]
        gate_r = jnp.concatenate([phase(n_cs + j, r) for j in range(n_rs)], axis=1)
        y_r = _rms(h_r * _gelu_tanh(gate_r), rg_gain_ref[...])
        for j in range(n_rs):
            yrg_ref[j, pl.ds(r, G, stride=PHASES), :] = y_r[:, j * LANES:(j + 1) * LANES]
        if r % 2 == 1:
            yield MLP_PHASES_AFTER["rg_scan"][1 + r // 2]
    for j in range(n_rs):
        y_ref[:, j * LANES:(j + 1) * LANES] = yrg_ref[j].astype(BF16)

    v_all = proj[:, d_conv + d_rg:d_conv + d_rg + d_ml]
    o_all = proj[:, d_conv + d_rg + d_ml:]

    lane_g = jax.lax.broadcasted_iota(jnp.int32, gates.shape, 1)
    log_sig = jnp.minimum(gates, 0.0) - jnp.log1p(jnp.exp(-jnp.abs(gates)))
    gates_t = jnp.where(lane_g < ML_HEADS, gates, log_sig).T[:SUBLANES]

    causal = (jax.lax.broadcasted_iota(jnp.int32, (L, L), 1)
              <= jax.lax.broadcasted_iota(jnp.int32, (L, L), 0))
    ones_ext = jnp.ones((L, head_dim), BF16)

    for c in range(n_chunks):
        rows = slice(c * L, (c + 1) * L)
        li = gates_t[:, rows]
        lf = pltpu.roll(li, ML_HEADS, axis=0)
        bcum = _lane_scan(lf, jnp.add, 0.0)
        b_last = jnp.sum(lf, axis=1, keepdims=True)
        row_b = li - bcum
        cmax = _lane_scan(row_b, jnp.maximum, -jnp.inf)
        w_loc = b_last + row_b
        m_loc = jnp.max(w_loc, axis=1, keepdims=True)
        m_prev_b = m_ref[...]
        m_prev = jnp.max(m_prev_b, axis=1, keepdims=True)
        m_s = jnp.maximum(bcum + m_prev, bcum + cmax)
        m_new = jnp.maximum(b_last + m_prev, m_loc)
        row_ref[0] = bcum - m_s
        row_ref[1] = m_s
        row_ref[2] = row_b
        row_ref[3] = jnp.exp(w_loc - m_loc)
        row_ref[4] = jnp.broadcast_to(jnp.exp(b_last + m_prev - m_new), (SUBLANES, L))
        row_ref[5] = jnp.broadcast_to(jnp.exp(m_loc - m_new), (SUBLANES, L))
        row_ref[6] = m_prev_b
        m_ref[...] = jnp.broadcast_to(m_new, (SUBLANES, L))

        heads = range(ML_HEADS)
        q_f = [qk_ref[h, rows, :] for h in heads]
        k_t = [qk_ref[ML_HEADS + h, rows, :].T for h in heads]
        s = [jnp.dot(q_f[h].astype(BF16), k_t[h].astype(BF16), preferred_element_type=F32) for h in heads]
        yield MLP_PHASES_AFTER["ml_stage"][0]

        lhs, rhs, v_ext, ms_b = [], [], [], []
        for h in heads:
            cols = slice(h * head_dim, (h + 1) * head_dim)
            ca_b = jnp.broadcast_to(row_ref[0, h:h + 1, :], (head_dim, L)).T
            ms_b.append(jnp.broadcast_to(row_ref[1, h:h + 1, :], (head_dim, L)).T)
            p = jnp.exp(jnp.where(causal, ca_b + row_ref[2, h:h + 1, :], -jnp.inf))
            inter_w = jnp.exp(ca_b + row_ref[6, h:h + 1, :])
            v_ext.append(jnp.concatenate([v_all[rows, cols].astype(BF16), ones_ext], axis=1))
            lhs.append(jnp.concatenate([(s[h] * p).astype(BF16), (q_f[h] * inter_w).astype(BF16)], axis=1))
            rhs.append(jnp.concatenate([v_ext[h], c_ref[h].astype(BF16)], axis=0))
        yield MLP_PHASES_AFTER["ml_stage"][1]

        nd = [jnp.dot(lhs[h], rhs[h], preferred_element_type=F32) for h in heads]
        upd = [jnp.dot((k_t[h] * row_ref[3, h:h + 1, :]).astype(BF16), v_ext[h], preferred_element_type=F32)
               for h in heads]
        yield MLP_PHASES_AFTER["ml_stage"][2]

        for h in heads:
            cols = slice(h * head_dim, (h + 1) * head_dim)
            hh = nd[h][:, :head_dim] / jnp.maximum(jnp.abs(nd[h][:, head_dim:]), jnp.exp(-ms_b[h]))
            hh = _sigmoid(o_all[rows, cols]) * hh
            y_ref[rows, d_rg + h * head_dim:d_rg + (h + 1) * head_dim] = _rms(
                hh, ml_gain_ref[:, cols]).astype(BF16)
            s_old = row_ref[4, h:h + 1, :]
            s_new = row_ref[5, h:h + 1, :]
            c_ref[h] = (jnp.concatenate([s_old, s_old], axis=1) * c_ref[h]
                        + jnp.concatenate([s_new, s_new], axis=1) * upd[h])
        yield MLP_PHASES_AFTER["ml_stage"][3]

    mix = jnp.dot(y_ref[...], w_out_ref[...], preferred_element_type=F32)
    yield MLP_PHASES_AFTER["w_out"]
    return x + _rms(mix, post_g_ref[...])


def _mlp_tile(h_ref, pre_g_ref, w_up_ref, w_down_ref, post_g_ref):
    vb = _rms(h_ref[...], pre_g_ref[...]).astype(BF16)
    d_ff = w_up_ref.shape[1]
    acc = jnp.zeros(h_ref.shape, F32)
    for c in range(d_ff // MLP_FF_CHUNK):
        cols = slice(c * MLP_FF_CHUNK, (c + 1) * MLP_FF_CHUNK)
        f = jnp.maximum(jnp.dot(vb, w_up_ref[:, cols], preferred_element_type=F32), 0.0)
        yield
        acc = acc + jnp.dot((f * f).astype(BF16), w_down_ref[cols, :], preferred_element_type=F32)
        yield
    return h_ref[...] + _rms(acc, post_g_ref[...])


def _run(gen):
    while True:
        try:
            next(gen)
        except StopIteration as stop:
            return stop.value


def _interleave(primary, secondary, lead=0, per=1):
    results = [None, None]

    def advance(idx, gen):
        if results[idx] is None:
            try:
                return next(gen)
            except StopIteration as stop:
                results[idx] = (stop.value,)
        return 0

    for _ in range(lead):
        advance(1, secondary)
    while results[0] is None:
        for _ in range(advance(0, primary) * per):
            advance(1, secondary)
    while results[1] is None:
        advance(1, secondary)
    return results[0][0], results[1][0]


def _weight_copy_jobs(hbm, vmem, src_row_starts=None):
    rows, cols = vmem.shape
    jobs = []
    if src_row_starts is None:
        for cb in range(cols // WEIGHT_BLOCK):
            for rb in range(rows // WEIGHT_ROWS):
                r0, c0 = rb * WEIGHT_ROWS, cb * WEIGHT_BLOCK
                jobs.append((hbm, r0, c0, vmem, r0, c0, False))
    else:
        for cb, start in enumerate(src_row_starts):
            for sub in range(WEIGHT_BLOCK // WEIGHT_ROWS):
                for kb in range(rows // WEIGHT_BLOCK):
                    jobs.append((hbm, start + sub * WEIGHT_ROWS, kb * WEIGHT_BLOCK,
                                 vmem, kb * WEIGHT_BLOCK, cb * WEIGHT_BLOCK + sub * WEIGHT_ROWS, True))
    return jobs


def _load_weights(jobs, stage_ref, sem_ref):
    ahead = WEIGHT_SLOTS - 1

    def copy(i):
        src, r0, c0 = jobs[i][:3]
        slot = i % WEIGHT_SLOTS
        return pltpu.make_async_copy(
            src.at[pl.ds(r0, WEIGHT_ROWS), pl.ds(c0, WEIGHT_BLOCK)], stage_ref.at[slot], sem_ref.at[slot])

    for i in range(min(ahead, len(jobs))):
        copy(i).start()
    for i in range(len(jobs)):
        if i + ahead < len(jobs):
            copy(i + ahead).start()
        copy(i).wait()
        dst, r0, c0, transposed = jobs[i][3:]
        block = stage_ref[i % WEIGHT_SLOTS]
        if transposed:
            dst[r0:r0 + WEIGHT_BLOCK, c0:c0 + WEIGHT_ROWS] = block.T.astype(BF16)
        else:
            dst[r0:r0 + WEIGHT_ROWS, c0:c0 + WEIGHT_BLOCK] = block.astype(BF16)
        yield


def _layer_kernel(x_ref, *refs, tiles_per_seq, n_tiles, n_small, w_in_row_starts, **dims):
    (pre_g, w_if, b_if, rg_cw, ml_cw, rg_cb, ml_cb, w_gate, b_r, b_i, lam, rg_gain, ml_gain, post_g,
     mlp_pre_g, mlp_post_g) = refs[:n_small]
    w_in_t_hbm, w_out_hbm, w_up_hbm, w_down_hbm = refs[n_small:n_small + 4]
    out_ref = refs[n_small + 4]
    w_main, w_out, w_up, w_down, stage_ref, sem_ref, h1_ref = refs[n_small + 5:n_small + 12]
    state = refs[n_small + 12:]
    s = pl.program_id(0)

    @pl.when(s == 0)
    def _():
        h1_ref[...] = jnp.zeros_like(h1_ref)
        _run(_load_weights(_weight_copy_jobs(w_in_t_hbm, w_main, w_in_row_starts)
                           + _weight_copy_jobs(w_out_hbm, w_out) + _weight_copy_jobs(w_up_hbm, w_up)
                           + _weight_copy_jobs(w_down_hbm, w_down), stage_ref, sem_ref))

    h1_new, out = _interleave(
        _mixer_tile(x_ref[0], s % tiles_per_seq == 0, pre_g, w_main, w_if, b_if, rg_cw, ml_cw, rg_cb, ml_cb,
                    w_gate, b_r, b_i, lam, rg_gain, ml_gain, w_out, post_g, *state, **dims),
        _mlp_tile(h1_ref, mlp_pre_g, w_up, w_down, mlp_post_g),
        MLP_PHASES_AFTER["start"])
    out_ref[0] = out
    h1_ref[...] = h1_new


def _const_spec(shape):
    zeros = (0,) * len(shape)
    return pl.BlockSpec(shape, lambda *_: zeros, pipeline_mode=pl.Buffered(1))


def _block_diag_gates(w_r, w_i):
    nb, bd, _ = w_r.shape
    per = MXU_WIDTH // bd
    w = jnp.stack([w_r, w_i], axis=1).reshape(nb // per, per, 2, bd, bd)
    eye = jnp.eye(per, dtype=w.dtype)
    return jnp.einsum("gptij,pq->gpitqj", w, eye).reshape(nb // per, MXU_WIDTH, 2 * MXU_WIDTH)


def _layer(x, pre_gain, w_in, rg_conv_w, rg_conv_b, gate_r_w, gate_r_b, gate_i_w, gate_i_b, lam,
           ml_conv_w, ml_conv_b, igate_b, fgate_b, rg_gain, ml_gain, w_out, post_gain,
           pre_mlp_gain, w_up, w_down, post_mlp_gain):
    bsz, seq, d_model = x.shape
    assert w_up.shape[1] % MLP_FF_CHUNK == 0
    d_rg = lam.shape[0]
    d_ml = ml_gain.shape[0]
    head_dim = d_ml // ML_HEADS
    assert head_dim == LANES and ML_CHUNK == LANES and 2 * ML_HEADS == SUBLANES
    assert seq % SEQ_TILE == 0 and SEQ_TILE % ML_CHUNK == 0 and d_rg % MXU_WIDTH == 0
    d_conv = d_rg + 2 * d_ml

    assert d_rg == WEIGHT_BLOCK and d_ml == WEIGHT_BLOCK
    starts = dict(rg_x=0, rg_gate=d_rg, q=2 * d_rg, k=2 * d_rg + d_ml, v=2 * d_rg + 2 * d_ml, o=2 * d_rg + 3 * d_ml)
    w_in_row_starts = tuple(starts[n] for n in ("rg_x", "q", "k", "rg_gate", "v", "o"))
    d_main = 2 * d_rg + 4 * d_ml
    if_w = w_in[:, d_main:]
    w_if = jnp.pad(if_w, ((0, 0), (0, LANES - 2 * ML_HEADS))).astype(BF16)
    b_if = jnp.pad(jnp.concatenate([igate_b, fgate_b]), (0, LANES - 2 * ML_HEADS)).reshape(1, LANES)
    w_gate = _block_diag_gates(gate_r_w, gate_i_w).astype(BF16)

    row = lambda v: v.reshape(1, -1)
    small = (row(pre_gain), w_if, b_if, rg_conv_w, ml_conv_w, row(rg_conv_b), row(ml_conv_b), w_gate,
             row(gate_r_b), row(gate_i_b), row(lam),
             row(rg_gain), row(ml_gain), row(post_gain), row(pre_mlp_gain), row(post_mlp_gain))
    big = (jnp.swapaxes(w_in, 0, 1), w_out, w_up, w_down)
    d_ff = w_up.shape[1]
    for w in (w_out, w_up, w_down):
        assert w.shape[0] % WEIGHT_BLOCK == 0 and w.shape[1] % WEIGHT_BLOCK == 0

    tiles_per_seq = seq // SEQ_TILE
    n_tiles = bsz * tiles_per_seq

    def tile_block(t):
        return (t // tiles_per_seq, t % tiles_per_seq, 0)

    x_spec = pl.BlockSpec((1, SEQ_TILE, d_model), lambda s: tile_block(jnp.minimum(s, n_tiles - 1)))
    out_spec = pl.BlockSpec((1, SEQ_TILE, d_model), lambda s: tile_block(jnp.maximum(s - 1, 0)))
    in_specs = ([x_spec] + [_const_spec(op.shape) for op in small]
                + [pl.BlockSpec(memory_space=pl.ANY)] * len(big))
    kern = functools.partial(_layer_kernel, tiles_per_seq=tiles_per_seq, n_tiles=n_tiles, n_small=len(small),
                             w_in_row_starts=w_in_row_starts, d_rg=d_rg, d_ml=d_ml, head_dim=head_dim)
    return pl.pallas_call(
        kern,
        grid=(n_tiles + 1,),
        in_specs=in_specs,
        out_specs=out_spec,
        out_shape=jax.ShapeDtypeStruct(x.shape, x.dtype),
        scratch_shapes=[
            pltpu.VMEM((d_model, d_main), BF16),
            pltpu.VMEM((d_rg + d_ml, d_model), BF16),
            pltpu.VMEM((d_model, d_ff), BF16),
            pltpu.VMEM((d_ff, d_model), BF16),
            pltpu.VMEM((WEIGHT_SLOTS, WEIGHT_ROWS, WEIGHT_BLOCK), F32),
            pltpu.SemaphoreType.DMA((WEIGHT_SLOTS,)),
            pltpu.VMEM((SEQ_TILE, d_model), F32),
            pltpu.VMEM(((d_conv + d_rg) // LANES, SUBLANES + SEQ_TILE, LANES), F32),
            pltpu.VMEM((2 * d_ml // LANES, SEQ_TILE, LANES), F32),
            pltpu.VMEM((d_rg // LANES, SEQ_TILE, LANES), F32),
            pltpu.VMEM((SUBLANES, d_rg), F32),
            pltpu.VMEM((ML_HEADS, head_dim, 2 * head_dim), F32),
            pltpu.VMEM((SUBLANES, ML_CHUNK), F32),
            pltpu.VMEM((7, SUBLANES, ML_CHUNK), F32),
            pltpu.VMEM((SEQ_TILE, d_rg + d_ml), BF16),
        ],
        compiler_params=pltpu.CompilerParams(
            dimension_semantics=("arbitrary",), vmem_limit_bytes=VMEM_LIMIT_BYTES),
        name="layer",
    )(x, *small, *big)


def kernel(x, pre_mix_gain, w_in, rg_conv_w, rg_conv_b, rg_gate_r_w, rg_gate_r_b, rg_gate_i_w, rg_gate_i_b, rg_lambda, ml_conv_w, ml_conv_b, ml_igate_b, ml_fgate_b, rg_out_gain, ml_out_gain, w_out, post_mix_gain, pre_mlp_gain, mlp_w_up, mlp_w_down, post_mlp_gain):
    h = x
    for l in range(w_in.shape[0]):
        h = _layer(h, pre_mix_gain[l], w_in[l], rg_conv_w[l], rg_conv_b[l], rg_gate_r_w[l], rg_gate_r_b[l],
                   rg_gate_i_w[l], rg_gate_i_b[l], rg_lambda[l], ml_conv_w[l], ml_conv_b[l], ml_igate_b[l],
                   ml_fgate_b[l], rg_out_gain[l], ml_out_gain[l], w_out[l], post_mix_gain[l],
                   pre_mlp_gain[l], mlp_w_up[l], mlp_w_down[l], post_mlp_gain[l])
    return h
```

```python
import functools

import jax
import jax.numpy as jnp
from jax.experimental import pallas as pl
from jax.experimental.pallas import tpu as pltpu

F32 = jnp.float32
BF16 = jnp.bfloat16

RG_C = 8.0
ML_HEADS = 4
CONV_WIDTH = 4
EPS = 1e-6

SUBLANES = 8
LANES = 128
MXU_WIDTH = 256

SEQ_TILE = 512
ML_CHUNK = LANES
PHASES = 4
MLP_FF_CHUNK = 1024
WEIGHT_ROWS = 128
WEIGHT_BLOCK = 512
WEIGHT_SLOTS = 16
MLP_PHASES_AFTER = {"start": 0, "w_in": 1, "conv": (0, 1, 0, 0, 1, 0), "rg_gates": 0, "rg_scan": (1, 0, 0),
                    "ml_stage": (1, 0, 0, 0), "w_out": 0}
VMEM_LIMIT_BYTES = 60 * 1024 * 1024


def _rms(x, gain):
    return x * jax.lax.rsqrt(jnp.mean(x * x, axis=-1, keepdims=True) + EPS) * gain


LOG2E = 1.4426950408889634


def _sigmoid(x):
    return 1.0 / (1.0 + jnp.exp2(x * (-LOG2E)))


def _softplus(x):
    return jnp.maximum(x, 0.0) + jnp.log1p(jnp.exp(-jnp.abs(x)))


def _gelu_tanh(x):
    c = 0.7978845608028654
    return 0.5 * x * (1.0 + jnp.tanh(c * (x + 0.044715 * (x * x * x))))


def _lane_scan(x, op, fill):
    n = x.shape[1]
    lane = jax.lax.broadcasted_iota(jnp.int32, x.shape, 1)
    d = 1
    while d < n:
        shifted = jnp.where(lane < d, fill, pltpu.roll(x, d, axis=1))
        x = op(x, shifted)
        d *= 2
    return x


def _affine_row_scan(sa, sb):
    n, width = sa.shape
    row8 = jax.lax.broadcasted_iota(jnp.int32, (SUBLANES, width), 0)
    d = 1
    while d < n:
        if d < SUBLANES:
            ra = pltpu.roll(sa, d, axis=0)
            rb = pltpu.roll(sb, d, axis=0)
            a_sh = jnp.concatenate([jnp.where(row8 < d, 1.0, ra[:SUBLANES]), ra[SUBLANES:]], axis=0)
            b_sh = jnp.concatenate([jnp.where(row8 < d, 0.0, rb[:SUBLANES]), rb[SUBLANES:]], axis=0)
            sb = sb + sa * b_sh
            sa = sa * a_sh
        else:
            sb = jnp.concatenate([sb[:d], sb[d:] + sa[d:] * sb[:n - d]], axis=0)
            sa = jnp.concatenate([sa[:d], sa[d:] * sa[:n - d]], axis=0)
        d *= 2
    return sa, sb


def _mixer_tile(x, seq_start, pre_g_ref, w_main_ref, w_if_ref, b_if_ref, rg_cw_ref, ml_cw_ref, rg_cb_ref, ml_cb_ref,
                w_gate_ref,
                b_r_ref, b_i_ref, lam_ref, rg_gain_ref, ml_gain_ref, w_out_ref, post_g_ref,
                cbuf_ref, qk_ref, yrg_ref, h_ref, c_ref, m_ref, row_ref, y_ref,
                *, d_rg, d_ml, head_dim):
    ts = x.shape[0]
    d_conv = d_rg + 2 * d_ml
    n_chunks = ts // ML_CHUNK
    L = ML_CHUNK

    @pl.when(seq_start)
    def _():
        cbuf_ref[:, ts:ts + SUBLANES, :] = jnp.zeros((cbuf_ref.shape[0], SUBLANES, LANES), F32)
        h_ref[...] = jnp.zeros_like(h_ref)
        c_ref[...] = jnp.zeros_like(c_ref)
        m_ref[...] = jnp.zeros_like(m_ref)

    ub = _rms(x, pre_g_ref[...]).astype(BF16)
    proj = jnp.dot(ub, w_main_ref[...], preferred_element_type=F32)
    gates = jnp.dot(ub, w_if_ref[...], preferred_element_type=F32) + b_if_ref[...]
    yield MLP_PHASES_AFTER["w_in"]

    n_cs = d_conv // LANES
    n_rs = d_rg // LANES
    G = ts // PHASES
    cbuf_ref[:n_cs, :SUBLANES, :] = cbuf_ref[:n_cs, ts:ts + SUBLANES, :]
    for j in range(n_cs + n_rs):
        cbuf_ref[j, SUBLANES:, :] = proj[:, j * LANES:(j + 1) * LANES]

    def phase(j, e):
        return cbuf_ref[j, pl.ds(SUBLANES + e, G, stride=PHASES), :]

    conv_ph = [[None] * n_cs for _ in range(PHASES)]
    for j in range(n_cs):
        conv_w_ref, conv_b_ref, jj = (rg_cw_ref, rg_cb_ref, j) if j < n_rs else (ml_cw_ref, ml_cb_ref, j - n_rs)
        cols = slice(jj * LANES, (jj + 1) * LANES)
        taps = {e: phase(j, e) for e in range(1 - CONV_WIDTH, PHASES)}
        for r in range(PHASES):
            acc = taps[r] * conv_w_ref[CONV_WIDTH - 1:CONV_WIDTH, cols] + conv_b_ref[:, cols]
            for k in range(1, CONV_WIDTH):
                acc = acc + taps[r - k] * conv_w_ref[CONV_WIDTH - 1 - k:CONV_WIDTH - k, cols]
            conv_ph[r][j] = acc
        if j >= n_rs:
            scale = head_dim ** -0.5 if j < n_rs + d_ml // LANES else 1.0
            for r in range(PHASES):
                val = conv_ph[r][j]
                val = val * _sigmoid(val)
                qk_ref[j - n_rs, pl.ds(r, G, stride=PHASES), :] = val * scale if scale != 1.0 else val
        if j % 2 == 1:
            yield MLP_PHASES_AFTER["conv"][j // 2]

    xc = jnp.concatenate([jnp.concatenate(conv_ph[r][:n_rs], axis=1) for r in range(PHASES)], axis=0)
    r_parts, i_parts = [], []
    for g in range(d_rg // MXU_WIDTH):
        gg = jnp.dot(xc[:, g * MXU_WIDTH:(g + 1) * MXU_WIDTH].astype(BF16), w_gate_ref[g],
                     preferred_element_type=F32)
        r_parts.append(gg[:, :MXU_WIDTH])
        i_parts.append(gg[:, MXU_WIDTH:])
    r = _sigmoid(jnp.concatenate(r_parts, axis=1) + b_r_ref[...])
    i_gate = _sigmoid(jnp.concatenate(i_parts, axis=1) + b_i_ref[...])
    neg_log_a_unit = RG_C * _softplus(-lam_ref[...])
    a = jnp.exp2(r * (neg_log_a_unit * (-LOG2E)))
    z = jnp.tanh(r * neg_log_a_unit) * (a * a + 1.0)
    b_in = jnp.where(z > 0.0, z * jax.lax.rsqrt(z), 0.0) * (i_gate * xc)
    yield MLP_PHASES_AFTER["rg_gates"]

    comp_a, comp_b = [a[:G]], [b_in[:G]]
    for r in range(1, PHASES):
        a_r, b_r = a[r * G:(r + 1) * G], b_in[r * G:(r + 1) * G]
        comp_b.append(a_r * comp_b[-1] + b_r)
        comp_a.append(a_r * comp_a[-1])
    ga, gb = _affine_row_scan(comp_a[-1], comp_b[-1])
    carry = h_ref[SUBLANES - 1:SUBLANES, :]
    h_end = ga * carry + gb
    h_ref[...] = h_end[G - SUBLANES:]
    rolled = pltpu.roll(h_end, 1, axis=0)
    row8r = jax.lax.broadcasted_iota(jnp.int32, (SUBLANES, d_rg), 0)
    h_prev = jnp.concatenate([jnp.where(row8r < 1, carry, rolled[:SUBLANES]), rolled[SUBLANES:]], axis=0)
    yield MLP_PHASES_AFTER["rg_scan"][0]
    for r in range(PHASES):
        h_r = comp_a[r] * h_prev + comp_b[r]
        gate_r = jnp.concatenate([phase(n_cs + j, r) for j in range(n_rs)], axis=1)
        y_r = _rms(h_r * _gelu_tanh(gate_r), rg_gain_ref[...])
        for j in range(n_rs):
            yrg_ref[j, pl.ds(r, G, stride=PHASES), :] = y_r[:, j * LANES:(j + 1) * LANES]
        if r % 2 == 1:
            yield MLP_PHASES_AFTER["rg_scan"][1 + r // 2]
    for j in range(n_rs):
        y_ref[:, j * LANES:(j + 1) * LANES] = yrg_ref[j].astype(BF16)

    v_all = proj[:, d_conv + d_rg:d_conv + d_rg + d_ml]
    o_all = proj[:, d_conv + d_rg + d_ml:]

    lane_g = jax.lax.broadcasted_iota(jnp.int32, gates.shape, 1)
    log_sig = jnp.minimum(gates, 0.0) - jnp.log1p(jnp.exp(-jnp.abs(gates)))
    gates_t = jnp.where(lane_g < ML_HEADS, gates, log_sig).T[:SUBLANES]

    causal = (jax.lax.broadcasted_iota(jnp.int32, (L, L), 1)
              <= jax.lax.broadcasted_iota(jnp.int32, (L, L), 0))
    ones_ext = jnp.ones((L, head_dim), BF16)

    for c in range(n_chunks):
        rows = slice(c * L, (c + 1) * L)
        li = gates_t[:, rows]
        lf = pltpu.roll(li, ML_HEADS, axis=0)
        bcum = _lane_scan(lf, jnp.add, 0.0)
        b_last = jnp.sum(lf, axis=1, keepdims=True)
        row_b = li - bcum
        cmax = _lane_scan(row_b, jnp.maximum, -jnp.inf)
        w_loc = b_last + row_b
        m_loc = jnp.max(w_loc, axis=1, keepdims=True)
        m_prev_b = m_ref[...]
        m_prev = jnp.max(m_prev_b, axis=1, keepdims=True)
        m_s = jnp.maximum(bcum + m_prev, bcum + cmax)
        m_new = jnp.maximum(b_last + m_prev, m_loc)
        row_ref[0] = bcum - m_s
        row_ref[1] = m_s
        row_ref[2] = row_b
        row_ref[3] = jnp.exp(w_loc - m_loc)
        row_ref[4] = jnp.broadcast_to(jnp.exp(b_last + m_prev - m_new), (SUBLANES, L))
        row_ref[5] = jnp.broadcast_to(jnp.exp(m_loc - m_new), (SUBLANES, L))
        row_ref[6] = m_prev_b
        m_ref[...] = jnp.broadcast_to(m_new, (SUBLANES, L))

        heads = range(ML_HEADS)
        q_f = [qk_ref[h, rows, :] for h in heads]
        k_t = [qk_ref[ML_HEADS + h, rows, :].T for h in heads]
        s = [jnp.dot(q_f[h].astype(BF16), k_t[h].astype(BF16), preferred_element_type=F32) for h in heads]
        yield MLP_PHASES_AFTER["ml_stage"][0]

        lhs, rhs, v_ext, ms_b = [], [], [], []
        for h in heads:
            cols = slice(h * head_dim, (h + 1) * head_dim)
            ca_b = jnp.broadcast_to(row_ref[0, h:h + 1, :], (head_dim, L)).T
            ms_b.append(jnp.broadcast_to(row_ref[1, h:h + 1, :], (head_dim, L)).T)
            p = jnp.exp(jnp.where(causal, ca_b + row_ref[2, h:h + 1, :], -jnp.inf))
            inter_w = jnp.exp(ca_b + row_ref[6, h:h + 1, :])
            v_ext.append(jnp.concatenate([v_all[rows, cols].astype(BF16), ones_ext], axis=1))
            lhs.append(jnp.concatenate([(s[h] * p).astype(BF16), (q_f[h] * inter_w).astype(BF16)], axis=1))
            rhs.append(jnp.concatenate([v_ext[h], c_ref[h].astype(BF16)], axis=0))
        yield MLP_PHASES_AFTER["ml_stage"][1]

        nd = [jnp.dot(lhs[h], rhs[h], preferred_element_type=F32) for h in heads]
        upd = [jnp.dot((k_t[h] * row_ref[3, h:h + 1, :]).astype(BF16), v_ext[h], preferred_element_type=F32)
               for h in heads]
        yield MLP_PHASES_AFTER["ml_stage"][2]

        for h in heads:
            cols = slice(h * head_dim, (h + 1) * head_dim)
            hh = nd[h][:, :head_dim] / jnp.maximum(jnp.abs(nd[h][:, head_dim:]), jnp.exp(-ms_b[h]))
            hh = _sigmoid(o_all[rows, cols]) * hh
            y_ref[rows, d_rg + h * head_dim:d_rg + (h + 1) * head_dim] = _rms(
                hh, ml_gain_ref[:, cols]).astype(BF16)
            s_old = row_ref[4, h:h + 1, :]
            s_new = row_ref[5, h:h + 1, :]
            c_ref[h] = (jnp.concatenate([s_old, s_old], axis=1) * c_ref[h]
                        + jnp.concatenate([s_new, s_new], axis=1) * upd[h])
        yield MLP_PHASES_AFTER["ml_stage"][3]

    mix = jnp.dot(y_ref[...], w_out_ref[...], preferred_element_type=F32)
    yield MLP_PHASES_AFTER["w_out"]
    return x + _rms(mix, post_g_ref[...])


def _mlp_tile(h_ref, pre_g_ref, w_up_ref, w_down_ref, post_g_ref):
    vb = _rms(h_ref[...], pre_g_ref[...]).astype(BF16)
    d_ff = w_up_ref.shape[1]
    acc = jnp.zeros(h_ref.shape, F32)
    for c in range(d_ff // MLP_FF_CHUNK):
        cols = slice(c * MLP_FF_CHUNK, (c + 1) * MLP_FF_CHUNK)
        f = jnp.maximum(jnp.dot(vb, w_up_ref[:, cols], preferred_element_type=F32), 0.0)
        yield
        acc = acc + jnp.dot((f * f).astype(BF16), w_down_ref[cols, :], preferred_element_type=F32)
        yield
    return h_ref[...] + _rms(acc, post_g_ref[...])


def _run(gen):
    while True:
        try:
            next(gen)
        except StopIteration as stop:
            return stop.value


def _interleave(primary, secondary, lead=0, per=1):
    results = [None, None]

    def advance(idx, gen):
        if results[idx] is None:
            try:
                return next(gen)
            except StopIteration as stop:
                results[idx] = (stop.value,)
        return 0

    for _ in range(lead):
        advance(1, secondary)
    while results[0] is None:
        for _ in range(advance(0, primary) * per):
            advance(1, secondary)
    while results[1] is None:
        advance(1, secondary)
    return results[0][0], results[1][0]


def _weight_copy_jobs(hbm, vmem, src_row_starts=None):
    rows, cols = vmem.shape
    jobs = []
    if src_row_starts is None:
        for cb in range(cols // WEIGHT_BLOCK):
            for rb in range(rows // WEIGHT_ROWS):
                r0, c0 = rb * WEIGHT_ROWS, cb * WEIGHT_BLOCK
                jobs.append((hbm, r0, c0, vmem, r0, c0, False))
    else:
        for cb, start in enumerate(src_row_starts):
            for sub in range(WEIGHT_BLOCK // WEIGHT_ROWS):
                for kb in range(rows // WEIGHT_BLOCK):
                    jobs.append((hbm, start + sub * WEIGHT_ROWS, kb * WEIGHT_BLOCK,
                                 vmem, kb * WEIGHT_BLOCK, cb * WEIGHT_BLOCK + sub * WEIGHT_ROWS, True))
    return jobs


def _load_weights(jobs, stage_ref, sem_ref):
    ahead = WEIGHT_SLOTS - 1

    def copy(i):
        src, r0, c0 = jobs[i][:3]
        slot = i % WEIGHT_SLOTS
        return pltpu.make_async_copy(
            src.at[pl.ds(r0, WEIGHT_ROWS), pl.ds(c0, WEIGHT_BLOCK)], stage_ref.at[slot], sem_ref.at[slot])

    for i in range(min(ahead, len(jobs))):
        copy(i).start()
    for i in range(len(jobs)):
        if i + ahead < len(jobs):
            copy(i + ahead).start()
        copy(i).wait()
        dst, r0, c0, transposed = jobs[i][3:]
        block = stage_ref[i % WEIGHT_SLOTS]
        if transposed:
            dst[r0:r0 + WEIGHT_BLOCK, c0:c0 + WEIGHT_ROWS] = block.T.astype(BF16)
        else:
            dst[r0:r0 + WEIGHT_ROWS, c0:c0 + WEIGHT_BLOCK] = block.astype(BF16)
        yield


def _layer_kernel(x_ref, *refs, tiles_per_seq, n_tiles, n_small, w_in_row_starts, **dims):
    (pre_g, w_if, b_if, rg_cw, ml_cw, rg_cb, ml_cb, w_gate, b_r, b_i, lam, rg_gain, ml_gain, post_g,
     mlp_pre_g, mlp_post_g) = refs[:n_small]
    w_in_t_hbm, w_out_hbm, w_up_hbm, w_down_hbm = refs[n_small:n_small + 4]
    out_ref = refs[n_small + 4]
    w_main, w_out, w_up, w_down, stage_ref, sem_ref, h1_ref = refs[n_small + 5:n_small + 12]
    state = refs[n_small + 12:]
    s = pl.program_id(0)

    @pl.when(s == 0)
    def _():
        h1_ref[...] = jnp.zeros_like(h1_ref)
        _run(_load_weights(_weight_copy_jobs(w_in_t_hbm, w_main, w_in_row_starts)
                           + _weight_copy_jobs(w_out_hbm, w_out) + _weight_copy_jobs(w_up_hbm, w_up)
                           + _weight_copy_jobs(w_down_hbm, w_down), stage_ref, sem_ref))

    h1_new, out = _interleave(
        _mixer_tile(x_ref[0], s % tiles_per_seq == 0, pre_g, w_main, w_if, b_if, rg_cw, ml_cw, rg_cb, ml_cb,
                    w_gate, b_r, b_i, lam, rg_gain, ml_gain, w_out, post_g, *state, **dims),
        _mlp_tile(h1_ref, mlp_pre_g, w_up, w_down, mlp_post_g),
        MLP_PHASES_AFTER["start"])
    out_ref[0] = out
    h1_ref[...] = h1_new


def _const_spec(shape):
    zeros = (0,) * len(shape)
    return pl.BlockSpec(shape, lambda *_: zeros, pipeline_mode=pl.Buffered(1))


def _block_diag_gates(w_r, w_i):
    nb, bd, _ = w_r.shape
    per = MXU_WIDTH // bd
    w = jnp.stack([w_r, w_i], axis=1).reshape(nb // per, per, 2, bd, bd)
    eye = jnp.eye(per, dtype=w.dtype)
    return jnp.einsum("gptij,pq->gpitqj", w, eye).reshape(nb // per, MXU_WIDTH, 2 * MXU_WIDTH)


def _layer(x, pre_gain, w_in, rg_conv_w, rg_conv_b, gate_r_w, gate_r_b, gate_i_w, gate_i_b, lam,
           ml_conv_w, ml_conv_b, igate_b, fgate_b, rg_gain, ml_gain, w_out, post_gain,
           pre_mlp_gain, w_up, w_down, post_mlp_gain):
    bsz, seq, d_model = x.shape
    assert w_up.shape[1] % MLP_FF_CHUNK == 0
    d_rg = lam.shape[0]
    d_ml = ml_gain.shape[0]
    head_dim = d_ml // ML_HEADS
    assert head_dim == LANES and ML_CHUNK == LANES and 2 * ML_HEADS == SUBLANES
    assert seq % SEQ_TILE == 0 and SEQ_TILE % ML_CHUNK == 0 and d_rg % MXU_WIDTH == 0
    d_conv = d_rg + 2 * d_ml

    assert d_rg == WEIGHT_BLOCK and d_ml == WEIGHT_BLOCK
    starts = dict(rg_x=0, rg_gate=d_rg, q=2 * d_rg, k=2 * d_rg + d_ml, v=2 * d_rg + 2 * d_ml, o=2 * d_rg + 3 * d_ml)
    w_in_row_starts = tuple(starts[n] for n in ("rg_x", "q", "k", "rg_gate", "v", "o"))
    d_main = 2 * d_rg + 4 * d_ml
    if_w = w_in[:, d_main:]
    w_if = jnp.pad(if_w, ((0, 0), (0, LANES - 2 * ML_HEADS))).astype(BF16)
    b_if = jnp.pad(jnp.concatenate([igate_b, fgate_b]), (0, LANES - 2 * ML_HEADS)).reshape(1, LANES)
    w_gate = _block_diag_gates(gate_r_w, gate_i_w).astype(BF16)

    row = lambda v: v.reshape(1, -1)
    small = (row(pre_gain), w_if, b_if, rg_conv_w, ml_conv_w, row(rg_conv_b), row(ml_conv_b), w_gate,
             row(gate_r_b), row(gate_i_b), row(lam),
             row(rg_gain), row(ml_gain), row(post_gain), row(pre_mlp_gain), row(post_mlp_gain))
    big = (jnp.swapaxes(w_in, 0, 1), w_out, w_up, w_down)
    d_ff = w_up.shape[1]
    for w in (w_out, w_up, w_down):
        assert w.shape[0] % WEIGHT_BLOCK == 0 and w.shape[1] % WEIGHT_BLOCK == 0

    tiles_per_seq = seq // SEQ_TILE
    n_tiles = bsz * tiles_per_seq

    def tile_block(t):
        return (t // tiles_per_seq, t % tiles_per_seq, 0)

    x_spec = pl.BlockSpec((1, SEQ_TILE, d_model), lambda s: tile_block(jnp.minimum(s, n_tiles - 1)))
    out_spec = pl.BlockSpec((1, SEQ_TILE, d_model), lambda s: tile_block(jnp.maximum(s - 1, 0)))
    in_specs = ([x_spec] + [_const_spec(op.shape) for op in small]
                + [pl.BlockSpec(memory_space=pl.ANY)] * len(big))
    kern = functools.partial(_layer_kernel, tiles_per_seq=tiles_per_seq, n_tiles=n_tiles, n_small=len(small),
                             w_in_row_starts=w_in_row_starts, d_rg=d_rg, d_ml=d_ml, head_dim=head_dim)
    return pl.pallas_call(
        kern,
        grid=(n_tiles + 1,),
        in_specs=in_specs,
        out_specs=out_spec,
        out_shape=jax.ShapeDtypeStruct(x.shape, x.dtype),
        scratch_shapes=[
            pltpu.VMEM((d_model, d_main), BF16),
            pltpu.VMEM((d_rg + d_ml, d_model), BF16),
            pltpu.VMEM((d_model, d_ff), BF16),
            pltpu.VMEM((d_ff, d_model), BF16),
            pltpu.VMEM((WEIGHT_SLOTS, WEIGHT_ROWS, WEIGHT_BLOCK), F32),
            pltpu.SemaphoreType.DMA((WEIGHT_SLOTS,)),
            pltpu.VMEM((SEQ_TILE, d_model), F32),
            pltpu.VMEM(((d_conv + d_rg) // LANES, SUBLANES + SEQ_TILE, LANES), F32),
            pltpu.VMEM((2 * d_ml // LANES, SEQ_TILE, LANES), F32),
            pltpu.VMEM((d_rg // LANES, SEQ_TILE, LANES), F32),
            pltpu.VMEM((SUBLANES, d_rg), F32),
            pltpu.VMEM((ML_HEADS, head_dim, 2 * head_dim), F32),
            pltpu.VMEM((SUBLANES, ML_CHUNK), F32),
            pltpu.VMEM((7, SUBLANES, ML_CHUNK), F32),
            pltpu.VMEM((SEQ_TILE, d_rg + d_ml), BF16),
        ],
        compiler_params=pltpu.CompilerParams(
            dimension_semantics=("arbitrary",), vmem_limit_bytes=VMEM_LIMIT_BYTES),
        name="layer",
    )(x, *small, *big)


def kernel(x, pre_mix_gain, w_in, rg_conv_w, rg_conv_b, rg_gate_r_w, rg_gate_r_b, rg_gate_i_w, rg_gate_i_b, rg_lambda, ml_conv_w, ml_conv_b, ml_igate_b, ml_fgate_b, rg_out_gain, ml_out_gain, w_out, post_mix_gain, pre_mlp_gain, mlp_w_up, mlp_w_down, post_mlp_gain):
    h = x
    for l in range(w_in.shape[0]):
        h = _layer(h, pre_mix_gain[l], w_in[l], rg_conv_w[l], rg_conv_b[l], rg_gate_r_w[l], rg_gate_r_b[l],
                   rg_gate_i_w[l], rg_gate_i_b[l], rg_lambda[l], ml_conv_w[l], ml_conv_b[l], ml_igate_b[l],
                   ml_fgate_b[l], rg_out_gain[l], ml_out_gain[l], w_out[l], post_mix_gain[l],
                   pre_mlp_gain[l], mlp_w_up[l], mlp_w_down[l], post_mlp_gain[l])
    return h
```

```python
import functools

import jax
import jax.numpy as jnp
from jax.experimental import pallas as pl
from jax.experimental.pallas import tpu as pltpu

F32 = jnp.float32
BF16 = jnp.bfloat16

RG_C = 8.0
ML_HEADS = 4
CONV_WIDTH = 4
EPS = 1e-6

SUBLANES = 8
LANES = 128
MXU_WIDTH = 256

SEQ_TILE = 512
ML_CHUNK = LANES
PHASES = 4
MLP_FF_CHUNK = 1024
WEIGHT_ROWS = 256
WEIGHT_BLOCK = 512
WEIGHT_SLOTS = 8
MLP_PHASES_AFTER = {"start": 0, "w_in": 1, "conv": (0, 1, 0, 0, 1, 0), "rg_gates": 0, "rg_scan": (1, 0, 0),
                    "ml_stage": (1, 0, 0, 0), "w_out": 0}
VMEM_LIMIT_BYTES = 60 * 1024 * 1024


def _rms(x, gain):
    return x * jax.lax.rsqrt(jnp.mean(x * x, axis=-1, keepdims=True) + EPS) * gain


LOG2E = 1.4426950408889634


def _sigmoid(x):
    return 1.0 / (1.0 + jnp.exp2(x * (-LOG2E)))


def _softplus(x):
    return jnp.maximum(x, 0.0) + jnp.log1p(jnp.exp(-jnp.abs(x)))


def _gelu_tanh(x):
    c = 0.7978845608028654
    return 0.5 * x * (1.0 + jnp.tanh(c * (x + 0.044715 * (x * x * x))))


def _lane_scan(x, op, fill):
    n = x.shape[1]
    lane = jax.lax.broadcasted_iota(jnp.int32, x.shape, 1)
    d = 1
    while d < n:
        shifted = jnp.where(lane < d, fill, pltpu.roll(x, d, axis=1))
        x = op(x, shifted)
        d *= 2
    return x


def _affine_row_scan(sa, sb):
    n, width = sa.shape
    row8 = jax.lax.broadcasted_iota(jnp.int32, (SUBLANES, width), 0)
    d = 1
    while d < n:
        if d < SUBLANES:
            ra = pltpu.roll(sa, d, axis=0)
            rb = pltpu.roll(sb, d, axis=0)
            a_sh = jnp.concatenate([jnp.where(row8 < d, 1.0, ra[:SUBLANES]), ra[SUBLANES:]], axis=0)
            b_sh = jnp.concatenate([jnp.where(row8 < d, 0.0, rb[:SUBLANES]), rb[SUBLANES:]], axis=0)
            sb = sb + sa * b_sh
            sa = sa * a_sh
        else:
            sb = jnp.concatenate([sb[:d], sb[d:] + sa[d:] * sb[:n - d]], axis=0)
            sa = jnp.concatenate([sa[:d], sa[d:] * sa[:n - d]], axis=0)
        d *= 2
    return sa, sb


def _mixer_tile(x_ref, seq_start, pre_g_ref, w_main_ref, w_if_ref, b_if_ref, rg_cw_ref, ml_cw_ref, rg_cb_ref, ml_cb_ref,
                w_gate_ref,
                b_r_ref, b_i_ref, lam_ref, rg_gain_ref, ml_gain_ref, w_out_ref, post_g_ref,
                cbuf_ref, qk_ref, yrg_ref, h_ref, c_ref, m_ref, row_ref, y_ref,
                *, d_rg, d_ml, head_dim):
    ts = x_ref.shape[1]
    d_conv = d_rg + 2 * d_ml
    n_chunks = ts // ML_CHUNK
    L = ML_CHUNK

    @pl.when(seq_start)
    def _():
        cbuf_ref[:, ts:ts + SUBLANES, :] = jnp.zeros((cbuf_ref.shape[0], SUBLANES, LANES), F32)
        h_ref[...] = jnp.zeros_like(h_ref)
        c_ref[...] = jnp.zeros_like(c_ref)
        m_ref[...] = jnp.zeros_like(m_ref)

    ub = _rms(x_ref[0], pre_g_ref[...]).astype(BF16)
    proj = jnp.dot(ub, w_main_ref[...], preferred_element_type=F32)
    gates = jnp.dot(ub, w_if_ref[...], preferred_element_type=F32) + b_if_ref[...]
    yield MLP_PHASES_AFTER["w_in"]

    n_cs = d_conv // LANES
    n_rs = d_rg // LANES
    G = ts // PHASES
    cbuf_ref[:n_cs, :SUBLANES, :] = cbuf_ref[:n_cs, ts:ts + SUBLANES, :]
    for j in range(n_cs + n_rs):
        cbuf_ref[j, SUBLANES:, :] = proj[:, j * LANES:(j + 1) * LANES]

    def phase(j, e):
        return cbuf_ref[j, pl.ds(SUBLANES + e, G, stride=PHASES), :]

    conv_ph = [[None] * n_cs for _ in range(PHASES)]
    for j in range(n_cs):
        conv_w_ref, conv_b_ref, jj = (rg_cw_ref, rg_cb_ref, j) if j < n_rs else (ml_cw_ref, ml_cb_ref, j - n_rs)
        cols = slice(jj * LANES, (jj + 1) * LANES)
        taps = {e: phase(j, e) for e in range(1 - CONV_WIDTH, PHASES)}
        for r in range(PHASES):
            acc = taps[r] * conv_w_ref[CONV_WIDTH - 1:CONV_WIDTH, cols] + conv_b_ref[:, cols]
            for k in range(1, CONV_WIDTH):
                acc = acc + taps[r - k] * conv_w_ref[CONV_WIDTH - 1 - k:CONV_WIDTH - k, cols]
            conv_ph[r][j] = acc
        if j >= n_rs:
            scale = head_dim ** -0.5 if j < n_rs + d_ml // LANES else 1.0
            for r in range(PHASES):
                val = conv_ph[r][j]
                val = val * _sigmoid(val)
                qk_ref[j - n_rs, pl.ds(r, G, stride=PHASES), :] = val * scale if scale != 1.0 else val
        if j % 2 == 1:
            yield MLP_PHASES_AFTER["conv"][j // 2]

    xc = jnp.concatenate([jnp.concatenate(conv_ph[r][:n_rs], axis=1) for r in range(PHASES)], axis=0)
    r_parts, i_parts = [], []
    for g in range(d_rg // MXU_WIDTH):
        gg = jnp.dot(xc[:, g * MXU_WIDTH:(g + 1) * MXU_WIDTH].astype(BF16), w_gate_ref[g],
                     preferred_element_type=F32)
        r_parts.append(gg[:, :MXU_WIDTH])
        i_parts.append(gg[:, MXU_WIDTH:])
    r = _sigmoid(jnp.concatenate(r_parts, axis=1) + b_r_ref[...])
    i_gate = _sigmoid(jnp.concatenate(i_parts, axis=1) + b_i_ref[...])
    neg_log_a_unit = RG_C * _softplus(-lam_ref[...])
    a = jnp.exp2(r * (neg_log_a_unit * (-LOG2E)))
    z = jnp.tanh(r * neg_log_a_unit) * (a * a + 1.0)
    b_in = jnp.where(z > 0.0, z * jax.lax.rsqrt(z), 0.0) * (i_gate * xc)
    yield MLP_PHASES_AFTER["rg_gates"]

    comp_a, comp_b = [a[:G]], [b_in[:G]]
    for r in range(1, PHASES):
        a_r, b_r = a[r * G:(r + 1) * G], b_in[r * G:(r + 1) * G]
        comp_b.append(a_r * comp_b[-1] + b_r)
        comp_a.append(a_r * comp_a[-1])
    ga, gb = _affine_row_scan(comp_a[-1], comp_b[-1])
    carry = h_ref[SUBLANES - 1:SUBLANES, :]
    h_end = ga * carry + gb
    h_ref[...] = h_end[G - SUBLANES:]
    rolled = pltpu.roll(h_end, 1, axis=0)
    row8r = jax.lax.broadcasted_iota(jnp.int32, (SUBLANES, d_rg), 0)
    h_prev = jnp.concatenate([jnp.where(row8r < 1, carry, rolled[:SUBLANES]), rolled[SUBLANES:]], axis=0)
    yield MLP_PHASES_AFTER["rg_scan"][0]
    for r in range(PHASES):
        h_r = comp_a[r] * h_prev + comp_b[r]
        gate_r = jnp.concatenate([phase(n_cs + j, r) for j in range(n_rs)], axis=1)
        y_r = _rms(h_r * _gelu_tanh(gate_r), rg_gain_ref[...])
        for j in range(n_rs):
            yrg_ref[j, pl.ds(r, G, stride=PHASES), :] = y_r[:, j * LANES:(j + 1) * LANES]
        if r % 2 == 1:
            yield MLP_PHASES_AFTER["rg_scan"][1 + r // 2]
    for j in range(n_rs):
        y_ref[:, j * LANES:(j + 1) * LANES] = yrg_ref[j].astype(BF16)

    v_all = proj[:, d_conv + d_rg:d_conv + d_rg + d_ml]
    o_all = proj[:, d_conv + d_rg + d_ml:]

    lane_g = jax.lax.broadcasted_iota(jnp.int32, gates.shape, 1)
    log_sig = jnp.minimum(gates, 0.0) - jnp.log1p(jnp.exp(-jnp.abs(gates)))
    gates_t = jnp.where(lane_g < ML_HEADS, gates, log_sig).T[:SUBLANES]

    causal = (jax.lax.broadcasted_iota(jnp.int32, (L, L), 1)
              <= jax.lax.broadcasted_iota(jnp.int32, (L, L), 0))
    ones_ext = jnp.ones((L, head_dim), BF16)

    for c in range(n_chunks):
        rows = slice(c * L, (c + 1) * L)
        li = gates_t[:, rows]
        lf = pltpu.roll(li, ML_HEADS, axis=0)
        bcum = _lane_scan(lf, jnp.add, 0.0)
        b_last = jnp.sum(lf, axis=1, keepdims=True)
        row_b = li - bcum
        cmax = _lane_scan(row_b, jnp.maximum, -jnp.inf)
        w_loc = b_last + row_b
        m_loc = jnp.max(w_loc, axis=1, keepdims=True)
        m_prev_b = m_ref[...]
        m_prev = jnp.max(m_prev_b, axis=1, keepdims=True)
        m_s = jnp.maximum(bcum + m_prev, bcum + cmax)
        m_new = jnp.maximum(b_last + m_prev, m_loc)
        row_ref[0] = bcum - m_s
        row_ref[1] = m_s
        row_ref[2] = row_b
        row_ref[3] = jnp.exp(w_loc - m_loc)
        row_ref[4] = jnp.broadcast_to(jnp.exp(b_last + m_prev - m_new), (SUBLANES, L))
        row_ref[5] = jnp.broadcast_to(jnp.exp(m_loc - m_new), (SUBLANES, L))
        row_ref[6] = m_prev_b
        m_ref[...] = jnp.broadcast_to(m_new, (SUBLANES, L))

        heads = range(ML_HEADS)
        q_f = [qk_ref[h, rows, :] for h in heads]
        k_t = [qk_ref[ML_HEADS + h, rows, :].T for h in heads]
        s = [jnp.dot(q_f[h].astype(BF16), k_t[h].astype(BF16), preferred_element_type=F32) for h in heads]
        yield MLP_PHASES_AFTER["ml_stage"][0]

        lhs, rhs, v_ext, ms_b = [], [], [], []
        for h in heads:
            cols = slice(h * head_dim, (h + 1) * head_dim)
            ca_b = jnp.broadcast_to(row_ref[0, h:h + 1, :], (head_dim, L)).T
            ms_b.append(jnp.broadcast_to(row_ref[1, h:h + 1, :], (head_dim, L)).T)
            p = jnp.exp(jnp.where(causal, ca_b + row_ref[2, h:h + 1, :], -jnp.inf))
            inter_w = jnp.exp(ca_b + row_ref[6, h:h + 1, :])
            v_ext.append(jnp.concatenate([v_all[rows, cols].astype(BF16), ones_ext], axis=1))
            lhs.append(jnp.concatenate([(s[h] * p).astype(BF16), (q_f[h] * inter_w).astype(BF16)], axis=1))
            rhs.append(jnp.concatenate([v_ext[h], c_ref[h].astype(BF16)], axis=0))
        yield MLP_PHASES_AFTER["ml_stage"][1]

        nd = [jnp.dot(lhs[h], rhs[h], preferred_element_type=F32) for h in heads]
        upd = [jnp.dot((k_t[h] * row_ref[3, h:h + 1, :]).astype(BF16), v_ext[h], preferred_element_type=F32)
               for h in heads]
        yield MLP_PHASES_AFTER["ml_stage"][2]

        for h in heads:
            cols = slice(h * head_dim, (h + 1) * head_dim)
            hh = nd[h][:, :head_dim] / jnp.maximum(jnp.abs(nd[h][:, head_dim:]), jnp.exp(-ms_b[h]))
            hh = _sigmoid(o_all[rows, cols]) * hh
            y_ref[rows, d_rg + h * head_dim:d_rg + (h + 1) * head_dim] = _rms(
                hh, ml_gain_ref[:, cols]).astype(BF16)
            s_old = row_ref[4, h:h + 1, :]
            s_new = row_ref[5, h:h + 1, :]
            c_ref[h] = (jnp.concatenate([s_old, s_old], axis=1) * c_ref[h]
                        + jnp.concatenate([s_new, s_new], axis=1) * upd[h])
        yield MLP_PHASES_AFTER["ml_stage"][3]

    mix = jnp.dot(y_ref[...], w_out_ref[...], preferred_element_type=F32)
    yield MLP_PHASES_AFTER["w_out"]
    return x_ref[0] + _rms(mix, post_g_ref[...])


def _mlp_tile(h_ref, pre_g_ref, w_up_ref, w_down_ref, post_g_ref):
    vb = _rms(h_ref[...], pre_g_ref[...]).astype(BF16)
    d_ff = w_up_ref.shape[1]
    acc = jnp.zeros(h_ref.shape, F32)
    for c in range(d_ff // MLP_FF_CHUNK):
        cols = slice(c * MLP_FF_CHUNK, (c + 1) * MLP_FF_CHUNK)
        f = jnp.maximum(jnp.dot(vb, w_up_ref[:, cols], preferred_element_type=F32), 0.0)
        yield
        acc = acc + jnp.dot((f * f).astype(BF16), w_down_ref[cols, :], preferred_element_type=F32)
        yield
    return h_ref[...] + _rms(acc, post_g_ref[...])


def _run(gen):
    while True:
        try:
            next(gen)
        except StopIteration as stop:
            return stop.value


def _interleave(primary, secondary, lead=0, per=1):
    results = [None, None]

    def advance(idx, gen):
        if results[idx] is None:
            try:
                return next(gen)
            except StopIteration as stop:
                results[idx] = (stop.value,)
        return 0

    for _ in range(lead):
        advance(1, secondary)
    while results[0] is None:
        for _ in range(advance(0, primary) * per):
            advance(1, secondary)
    while results[1] is None:
        advance(1, secondary)
    return results[0][0], results[1][0]


def _weight_copy_jobs(hbm, vmem, src_row_starts=None):
    rows, cols = vmem.shape
    jobs = []
    if src_row_starts is None:
        for cb in range(cols // WEIGHT_BLOCK):
            for rb in range(rows // WEIGHT_ROWS):
                r0, c0 = rb * WEIGHT_ROWS, cb * WEIGHT_BLOCK
                jobs.append((hbm, r0, c0, vmem, r0, c0, False))
    else:
        for cb, start in enumerate(src_row_starts):
            for sub in range(WEIGHT_BLOCK // WEIGHT_ROWS):
                for kb in range(rows // WEIGHT_BLOCK):
                    jobs.append((hbm, start + sub * WEIGHT_ROWS, kb * WEIGHT_BLOCK,
                                 vmem, kb * WEIGHT_BLOCK, cb * WEIGHT_BLOCK + sub * WEIGHT_ROWS, True))
    return jobs


def _load_weights(jobs, stage_ref, sem_ref):
    ahead = WEIGHT_SLOTS - 1

    def copy(i):
        src, r0, c0 = jobs[i][:3]
        slot = i % WEIGHT_SLOTS
        return pltpu.make_async_copy(
            src.at[pl.ds(r0, WEIGHT_ROWS), pl.ds(c0, WEIGHT_BLOCK)], stage_ref.at[slot], sem_ref.at[slot])

    for i in range(min(ahead, len(jobs))):
        copy(i).start()
    for i in range(len(jobs)):
        if i + ahead < len(jobs):
            copy(i + ahead).start()
        copy(i).wait()
        dst, r0, c0, transposed = jobs[i][3:]
        block = stage_ref[i % WEIGHT_SLOTS]
        if transposed:
            dst[r0:r0 + WEIGHT_BLOCK, c0:c0 + WEIGHT_ROWS] = block.T.astype(BF16)
        else:
            dst[r0:r0 + WEIGHT_ROWS, c0:c0 + WEIGHT_BLOCK] = block.astype(BF16)
        yield


def _layer_kernel(x_ref, *refs, tiles_per_seq, n_tiles, n_small, w_in_row_starts, **dims):
    (pre_g, w_if, b_if, rg_cw, ml_cw, rg_cb, ml_cb, w_gate, b_r, b_i, lam, rg_gain, ml_gain, post_g,
     mlp_pre_g, mlp_post_g) = refs[:n_small]
    w_in_t_hbm, w_out_hbm, w_up_hbm, w_down_hbm = refs[n_small:n_small + 4]
    out_ref = refs[n_small + 4]
    w_main, w_out, w_up, w_down, stage_ref, sem_ref, h1_ref = refs[n_small + 5:n_small + 12]
    state = refs[n_small + 12:]
    s = pl.program_id(0)

    @pl.when(s == 0)
    def _():
        h1_ref[...] = jnp.zeros_like(h1_ref)
        _run(_load_weights(_weight_copy_jobs(w_in_t_hbm, w_main, w_in_row_starts)
                           + _weight_copy_jobs(w_out_hbm, w_out) + _weight_copy_jobs(w_up_hbm, w_up)
                           + _weight_copy_jobs(w_down_hbm, w_down), stage_ref, sem_ref))

    h1_new, out = _interleave(
        _mixer_tile(x_ref, s % tiles_per_seq == 0, pre_g, w_main, w_if, b_if, rg_cw, ml_cw, rg_cb, ml_cb,
                    w_gate, b_r, b_i, lam, rg_gain, ml_gain, w_out, post_g, *state, **dims),
        _mlp_tile(h1_ref, mlp_pre_g, w_up, w_down, mlp_post_g),
        MLP_PHASES_AFTER["start"])
    out_ref[0] = out
    h1_ref[...] = h1_new


def _const_spec(shape):
    zeros = (0,) * len(shape)
    return pl.BlockSpec(shape, lambda *_: zeros, pipeline_mode=pl.Buffered(1))


def _block_diag_gates(w_r, w_i):
    nb, bd, _ = w_r.shape
    per = MXU_WIDTH // bd
    w = jnp.stack([w_r, w_i], axis=1).reshape(nb // per, per, 2, bd, bd)
    eye = jnp.eye(per, dtype=w.dtype)
    return jnp.einsum("gptij,pq->gpitqj", w, eye).reshape(nb // per, MXU_WIDTH, 2 * MXU_WIDTH)


def _layer(x, pre_gain, w_in, rg_conv_w, rg_conv_b, gate_r_w, gate_r_b, gate_i_w, gate_i_b, lam,
           ml_conv_w, ml_conv_b, igate_b, fgate_b, rg_gain, ml_gain, w_out, post_gain,
           pre_mlp_gain, w_up, w_down, post_mlp_gain):
    bsz, seq, d_model = x.shape
    assert w_up.shape[1] % MLP_FF_CHUNK == 0
    d_rg = lam.shape[0]
    d_ml = ml_gain.shape[0]
    head_dim = d_ml // ML_HEADS
    assert head_dim == LANES and ML_CHUNK == LANES and 2 * ML_HEADS == SUBLANES
    assert seq % SEQ_TILE == 0 and SEQ_TILE % ML_CHUNK == 0 and d_rg % MXU_WIDTH == 0
    d_conv = d_rg + 2 * d_ml

    assert d_rg == WEIGHT_BLOCK and d_ml == WEIGHT_BLOCK
    starts = dict(rg_x=0, rg_gate=d_rg, q=2 * d_rg, k=2 * d_rg + d_ml, v=2 * d_rg + 2 * d_ml, o=2 * d_rg + 3 * d_ml)
    w_in_row_starts = tuple(starts[n] for n in ("rg_x", "q", "k", "rg_gate", "v", "o"))
    d_main = 2 * d_rg + 4 * d_ml
    if_w = w_in[:, d_main:]
    w_if = jnp.pad(if_w, ((0, 0), (0, LANES - 2 * ML_HEADS))).astype(BF16)
    b_if = jnp.pad(jnp.concatenate([igate_b, fgate_b]), (0, LANES - 2 * ML_HEADS)).reshape(1, LANES)
    w_gate = _block_diag_gates(gate_r_w, gate_i_w).astype(BF16)

    row = lambda v: v.reshape(1, -1)
    small = (row(pre_gain), w_if, b_if, rg_conv_w, ml_conv_w, row(rg_conv_b), row(ml_conv_b), w_gate,
             row(gate_r_b), row(gate_i_b), row(lam),
             row(rg_gain), row(ml_gain), row(post_gain), row(pre_mlp_gain), row(post_mlp_gain))
    big = (jnp.swapaxes(w_in, 0, 1), w_out, w_up, w_down)
    d_ff = w_up.shape[1]
    for w in (w_out, w_up, w_down):
        assert w.shape[0] % WEIGHT_BLOCK == 0 and w.shape[1] % WEIGHT_BLOCK == 0

    tiles_per_seq = seq // SEQ_TILE
    n_tiles = bsz * tiles_per_seq

    def tile_block(t):
        return (t // tiles_per_seq, t % tiles_per_seq, 0)

    x_spec = pl.BlockSpec((1, SEQ_TILE, d_model), lambda s: tile_block(jnp.minimum(s, n_tiles - 1)))
    out_spec = pl.BlockSpec((1, SEQ_TILE, d_model), lambda s: tile_block(jnp.maximum(s - 1, 0)))
    in_specs = ([x_spec] + [_const_spec(op.shape) for op in small]
                + [pl.BlockSpec(memory_space=pl.ANY)] * len(big))
    kern = functools.partial(_layer_kernel, tiles_per_seq=tiles_per_seq, n_tiles=n_tiles, n_small=len(small),
                             w_in_row_starts=w_in_row_starts, d_rg=d_rg, d_ml=d_ml, head_dim=head_dim)
    return pl.pallas_call(
        kern,
        grid=(n_tiles + 1,),
        in_specs=in_specs,
        out_specs=out_spec,
        out_shape=jax.ShapeDtypeStruct(x.shape, x.dtype),
        scratch_shapes=[
            pltpu.VMEM((d_model, d_main), BF16),
            pltpu.VMEM((d_rg + d_ml, d_model), BF16),
            pltpu.VMEM((d_model, d_ff), BF16),
            pltpu.VMEM((d_ff, d_model), BF16),
            pltpu.VMEM((WEIGHT_SLOTS, WEIGHT_ROWS, WEIGHT_BLOCK), F32),
            pltpu.SemaphoreType.DMA((WEIGHT_SLOTS,)),
            pltpu.VMEM((SEQ_TILE, d_model), F32),
            pltpu.VMEM(((d_conv + d_rg) // LANES, SUBLANES + SEQ_TILE, LANES), F32),
            pltpu.VMEM((2 * d_ml // LANES, SEQ_TILE, LANES), F32),
            pltpu.VMEM((d_rg // LANES, SEQ_TILE, LANES), F32),
            pltpu.VMEM((SUBLANES, d_rg), F32),
            pltpu.VMEM((ML_HEADS, head_dim, 2 * head_dim), F32),
            pltpu.VMEM((SUBLANES, ML_CHUNK), F32),
            pltpu.VMEM((7, SUBLANES, ML_CHUNK), F32),
            pltpu.VMEM((SEQ_TILE, d_rg + d_ml), BF16),
        ],
        compiler_params=pltpu.CompilerParams(
            dimension_semantics=("arbitrary",), vmem_limit_bytes=VMEM_LIMIT_BYTES),
        name="layer",
    )(x, *small, *big)


def kernel(x, pre_mix_gain, w_in, rg_conv_w, rg_conv_b, rg_gate_r_w, rg_gate_r_b, rg_gate_i_w, rg_gate_i_b, rg_lambda, ml_conv_w, ml_conv_b, ml_igate_b, ml_fgate_b, rg_out_gain, ml_out_gain, w_out, post_mix_gain, pre_mlp_gain, mlp_w_up, mlp_w_down, post_mlp_gain):
    h = x
    for l in range(w_in.shape[0]):
        h = _layer(h, pre_mix_gain[l], w_in[l], rg_conv_w[l], rg_conv_b[l], rg_gate_r_w[l], rg_gate_r_b[l],
                   rg_gate_i_w[l], rg_gate_i_b[l], rg_lambda[l], ml_conv_w[l], ml_conv_b[l], ml_igate_b[l],
                   ml_fgate_b[l], rg_out_gain[l], ml_out_gain[l], w_out[l], post_mix_gain[l],
                   pre_mlp_gain[l], mlp_w_up[l], mlp_w_down[l], post_mlp_gain[l])
    return h
```

```python
import functools

import jax
import jax.numpy as jnp
from jax.experimental import pallas as pl
from jax.experimental.pallas import tpu as pltpu

F32 = jnp.float32
BF16 = jnp.bfloat16

RG_C = 8.0
ML_HEADS = 4
CONV_WIDTH = 4
EPS = 1e-6

SUBLANES = 8
LANES = 128
MXU_WIDTH = 256

SEQ_TILE = 512
ML_CHUNK = LANES
PHASES = 4
MLP_FF_CHUNK = 1024
WEIGHT_ROWS = 256
WEIGHT_BLOCK = 512
WEIGHT_SLOTS = 8
MLP_PHASES_AFTER = {"start": 0, "w_in": 1, "conv": (0, 1, 0, 0, 1, 0), "rg_gates": 0, "rg_scan": (1, 0, 0),
                    "ml_stage": (1, 0, 0, 0), "w_out": 0}
VMEM_LIMIT_BYTES = 60 * 1024 * 1024


def _rms(x, gain):
    return x * jax.lax.rsqrt(jnp.mean(x * x, axis=-1, keepdims=True) + EPS) * gain


LOG2E = 1.4426950408889634


def _sigmoid(x):
    return 1.0 / (1.0 + jnp.exp2(x * (-LOG2E)))


def _softplus(x):
    return jnp.maximum(x, 0.0) + jnp.log1p(jnp.exp(-jnp.abs(x)))


def _gelu_tanh(x):
    c = 0.7978845608028654
    return 0.5 * x * (1.0 + jnp.tanh(c * (x + 0.044715 * (x * x * x))))


def _lane_scan(x, op, fill):
    n = x.shape[1]
    lane = jax.lax.broadcasted_iota(jnp.int32, x.shape, 1)
    d = 1
    while d < n:
        shifted = jnp.where(lane < d, fill, pltpu.roll(x, d, axis=1))
        x = op(x, shifted)
        d *= 2
    return x


def _affine_row_scan(sa, sb):
    n, width = sa.shape
    row8 = jax.lax.broadcasted_iota(jnp.int32, (SUBLANES, width), 0)
    d = 1
    while d < n:
        if d < SUBLANES:
            ra = pltpu.roll(sa, d, axis=0)
            rb = pltpu.roll(sb, d, axis=0)
            a_sh = jnp.concatenate([jnp.where(row8 < d, 1.0, ra[:SUBLANES]), ra[SUBLANES:]], axis=0)
            b_sh = jnp.concatenate([jnp.where(row8 < d, 0.0, rb[:SUBLANES]), rb[SUBLANES:]], axis=0)
            sb = sb + sa * b_sh
            sa = sa * a_sh
        else:
            sb = jnp.concatenate([sb[:d], sb[d:] + sa[d:] * sb[:n - d]], axis=0)
            sa = jnp.concatenate([sa[:d], sa[d:] * sa[:n - d]], axis=0)
        d *= 2
    return sa, sb


def _mixer_tile(x_ref, seq_start, pre_g_ref, w_main_ref, w_if_ref, b_if_ref, rg_cw_ref, ml_cw_ref, rg_cb_ref, ml_cb_ref,
                w_gate_ref,
                b_r_ref, b_i_ref, lam_ref, rg_gain_ref, ml_gain_ref, w_out_ref, post_g_ref,
                cbuf_ref, qk_ref, yrg_ref, h_ref, c_ref, m_ref, row_ref, y_ref,
                *, d_rg, d_ml, head_dim):
    ts = x_ref.shape[1]
    d_conv = d_rg + 2 * d_ml
    n_chunks = ts // ML_CHUNK
    L = ML_CHUNK

    @pl.when(seq_start)
    def _():
        cbuf_ref[:, ts:ts + SUBLANES, :] = jnp.zeros((cbuf_ref.shape[0], SUBLANES, LANES), F32)
        h_ref[...] = jnp.zeros_like(h_ref)
        c_ref[...] = jnp.zeros_like(c_ref)
        m_ref[...] = jnp.zeros_like(m_ref)

    ub = _rms(x_ref[0], pre_g_ref[...]).astype(BF16)
    proj = jnp.dot(ub, w_main_ref[...], preferred_element_type=F32)
    gates = jnp.dot(ub, w_if_ref[...], preferred_element_type=F32) + b_if_ref[...]
    yield MLP_PHASES_AFTER["w_in"]

    n_cs = d_conv // LANES
    n_rs = d_rg // LANES
    G = ts // PHASES
    cbuf_ref[:n_cs, :SUBLANES, :] = cbuf_ref[:n_cs, ts:ts + SUBLANES, :]
    for j in range(n_cs + n_rs):
        cbuf_ref[j, SUBLANES:, :] = proj[:, j * LANES:(j + 1) * LANES]

    def phase(j, e):
        return cbuf_ref[j, pl.ds(SUBLANES + e, G, stride=PHASES), :]

    conv_ph = [[None] * n_cs for _ in range(PHASES)]
    for j in range(n_cs):
        conv_w_ref, conv_b_ref, jj = (rg_cw_ref, rg_cb_ref, j) if j < n_rs else (ml_cw_ref, ml_cb_ref, j - n_rs)
        cols = slice(jj * LANES, (jj + 1) * LANES)
        taps = {e: phase(j, e) for e in range(1 - CONV_WIDTH, PHASES)}
        for r in range(PHASES):
            acc = taps[r] * conv_w_ref[CONV_WIDTH - 1:CONV_WIDTH, cols] + conv_b_ref[:, cols]
            for k in range(1, CONV_WIDTH):
                acc = acc + taps[r - k] * conv_w_ref[CONV_WIDTH - 1 - k:CONV_WIDTH - k, cols]
            conv_ph[r][j] = acc
        if j >= n_rs:
            scale = head_dim ** -0.5 if j < n_rs + d_ml // LANES else 1.0
            for r in range(PHASES):
                val = conv_ph[r][j]
                val = val * _sigmoid(val)
                qk_ref[j - n_rs, pl.ds(r, G, stride=PHASES), :] = val * scale if scale != 1.0 else val
        if j % 2 == 1:
            yield MLP_PHASES_AFTER["conv"][j // 2]

    xc = jnp.concatenate([jnp.concatenate(conv_ph[r][:n_rs], axis=1) for r in range(PHASES)], axis=0)
    r_parts, i_parts = [], []
    for g in range(d_rg // MXU_WIDTH):
        gg = jnp.dot(xc[:, g * MXU_WIDTH:(g + 1) * MXU_WIDTH].astype(BF16), w_gate_ref[g],
                     preferred_element_type=F32)
        r_parts.append(gg[:, :MXU_WIDTH])
        i_parts.append(gg[:, MXU_WIDTH:])
    r = _sigmoid(jnp.concatenate(r_parts, axis=1) + b_r_ref[...])
    i_gate = _sigmoid(jnp.concatenate(i_parts, axis=1) + b_i_ref[...])
    neg_log_a_unit = RG_C * _softplus(-lam_ref[...])
    a = jnp.exp2(r * (neg_log_a_unit * (-LOG2E)))
    z = jnp.tanh(r * neg_log_a_unit) * (a * a + 1.0)
    b_in = jnp.where(z > 0.0, z * jax.lax.rsqrt(z), 0.0) * (i_gate * xc)
    yield MLP_PHASES_AFTER["rg_gates"]

    comp_a, comp_b = [a[:G]], [b_in[:G]]
    for r in range(1, PHASES):
        a_r, b_r = a[r * G:(r + 1) * G], b_in[r * G:(r + 1) * G]
        comp_b.append(a_r * comp_b[-1] + b_r)
        comp_a.append(a_r * comp_a[-1])
    ga, gb = _affine_row_scan(comp_a[-1], comp_b[-1])
    carry = h_ref[SUBLANES - 1:SUBLANES, :]
    h_end = ga * carry + gb
    h_ref[...] = h_end[G - SUBLANES:]
    rolled = pltpu.roll(h_end, 1, axis=0)
    row8r = jax.lax.broadcasted_iota(jnp.int32, (SUBLANES, d_rg), 0)
    h_prev = jnp.concatenate([jnp.where(row8r < 1, carry, rolled[:SUBLANES]), rolled[SUBLANES:]], axis=0)
    yield MLP_PHASES_AFTER["rg_scan"][0]
    for r in range(PHASES):
        h_r = comp_a[r] * h_prev + comp_b[r]
        gate_r = jnp.concatenate([phase(n_cs + j, r) for j in range(n_rs)], axis=1)
        y_r = _rms(h_r * _gelu_tanh(gate_r), rg_gain_ref[...])
        for j in range(n_rs):
            yrg_ref[j, pl.ds(r, G, stride=PHASES), :] = y_r[:, j * LANES:(j + 1) * LANES]
        if r % 2 == 1:
            yield MLP_PHASES_AFTER["rg_scan"][1 + r // 2]
    for j in range(n_rs):
        y_ref[:, j * LANES:(j + 1) * LANES] = yrg_ref[j].astype(BF16)

    v_all = proj[:, d_conv + d_rg:d_conv + d_rg + d_ml]
    o_all = proj[:, d_conv + d_rg + d_ml:]

    lane_g = jax.lax.broadcasted_iota(jnp.int32, gates.shape, 1)
    log_sig = jnp.minimum(gates, 0.0) - jnp.log1p(jnp.exp(-jnp.abs(gates)))
    gates_t = jnp.where(lane_g < ML_HEADS, gates, log_sig).T[:SUBLANES]

    causal = (jax.lax.broadcasted_iota(jnp.int32, (L, L), 1)
              <= jax.lax.broadcasted_iota(jnp.int32, (L, L), 0))
    ones_ext = jnp.ones((L, head_dim), BF16)

    for c in range(n_chunks):
        rows = slice(c * L, (c + 1) * L)
        li = gates_t[:, rows]
        lf = pltpu.roll(li, ML_HEADS, axis=0)
        bcum = _lane_scan(lf, jnp.add, 0.0)
        b_last = jnp.sum(lf, axis=1, keepdims=True)
        row_b = li - bcum
        cmax = _lane_scan(row_b, jnp.maximum, -jnp.inf)
        w_loc = b_last + row_b
        m_loc = jnp.max(w_loc, axis=1, keepdims=True)
        m_prev_b = m_ref[...]
        m_prev = jnp.max(m_prev_b, axis=1, keepdims=True)
        m_s = jnp.maximum(bcum + m_prev, bcum + cmax)
        m_new = jnp.maximum(b_last + m_prev, m_loc)
        row_ref[0] = bcum - m_s
        row_ref[1] = m_s
        row_ref[2] = row_b
        row_ref[3] = jnp.exp(w_loc - m_loc)
        row_ref[4] = jnp.broadcast_to(jnp.exp(b_last + m_prev - m_new), (SUBLANES, L))
        row_ref[5] = jnp.broadcast_to(jnp.exp(m_loc - m_new), (SUBLANES, L))
        row_ref[6] = m_prev_b
        m_ref[...] = jnp.broadcast_to(m_new, (SUBLANES, L))

        heads = range(ML_HEADS)
        q_f = [qk_ref[h, rows, :] for h in heads]
        k_t = [qk_ref[ML_HEADS + h, rows, :].T for h in heads]
        s = [jnp.dot(q_f[h].astype(BF16), k_t[h].astype(BF16), preferred_element_type=F32) for h in heads]
        yield MLP_PHASES_AFTER["ml_stage"][0]

        lhs, rhs, v_ext, ms_b = [], [], [], []
        for h in heads:
            cols = slice(h * head_dim, (h + 1) * head_dim)
            ca_b = jnp.broadcast_to(row_ref[0, h:h + 1, :], (head_dim, L)).T
            ms_b.append(jnp.broadcast_to(row_ref[1, h:h + 1, :], (head_dim, L)).T)
            p = jnp.exp(jnp.where(causal, ca_b + row_ref[2, h:h + 1, :], -jnp.inf))
            inter_w = jnp.exp(ca_b + row_ref[6, h:h + 1, :])
            v_ext.append(jnp.concatenate([v_all[rows, cols].astype(BF16), ones_ext], axis=1))
            lhs.append(jnp.concatenate([(s[h] * p).astype(BF16), (q_f[h] * inter_w).astype(BF16)], axis=1))
            rhs.append(jnp.concatenate([v_ext[h], c_ref[h].astype(BF16)], axis=0))
        yield MLP_PHASES_AFTER["ml_stage"][1]

        nd = [jnp.dot(lhs[h], rhs[h], preferred_element_type=F32) for h in heads]
        upd = [jnp.dot((k_t[h] * row_ref[3, h:h + 1, :]).astype(BF16), v_ext[h], preferred_element_type=F32)
               for h in heads]
        yield MLP_PHASES_AFTER["ml_stage"][2]

        for h in heads:
            cols = slice(h * head_dim, (h + 1) * head_dim)
            hh = nd[h][:, :head_dim] / jnp.maximum(jnp.abs(nd[h][:, head_dim:]), jnp.exp(-ms_b[h]))
            hh = _sigmoid(o_all[rows, cols]) * hh
            y_ref[rows, d_rg + h * head_dim:d_rg + (h + 1) * head_dim] = _rms(
                hh, ml_gain_ref[:, cols]).astype(BF16)
            s_old = row_ref[4, h:h + 1, :]
            s_new = row_ref[5, h:h + 1, :]
            c_ref[h] = (jnp.concatenate([s_old, s_old], axis=1) * c_ref[h]
                        + jnp.concatenate([s_new, s_new], axis=1) * upd[h])
        yield MLP_PHASES_AFTER["ml_stage"][3]

    mix = jnp.dot(y_ref[...], w_out_ref[...], preferred_element_type=F32)
    yield MLP_PHASES_AFTER["w_out"]
    return x_ref[0] + _rms(mix, post_g_ref[...])


def _mlp_tile(h_ref, pre_g_ref, w_up_ref, w_down_ref, post_g_ref):
    vb = _rms(h_ref[...], pre_g_ref[...]).astype(BF16)
    d_ff = w_up_ref.shape[1]
    acc = None
    for c in range(d_ff // MLP_FF_CHUNK):
        cols = slice(c * MLP_FF_CHUNK, (c + 1) * MLP_FF_CHUNK)
        f = jnp.maximum(jnp.dot(vb, w_up_ref[:, cols], preferred_element_type=F32), 0.0)
        yield
        part = jnp.dot((f * f).astype(BF16), w_down_ref[cols, :], preferred_element_type=F32)
        acc = part if acc is None else acc + part
        yield
    return h_ref[...] + _rms(acc, post_g_ref[...])


def _run(gen):
    while True:
        try:
            next(gen)
        except StopIteration as stop:
            return stop.value


def _interleave(primary, secondary, lead=0, per=1):
    results = [None, None]

    def advance(idx, gen):
        if results[idx] is None:
            try:
                return next(gen)
            except StopIteration as stop:
                results[idx] = (stop.value,)
        return 0

    for _ in range(lead):
        advance(1, secondary)
    while results[0] is None:
        for _ in range(advance(0, primary) * per):
            advance(1, secondary)
    while results[1] is None:
        advance(1, secondary)
    return results[0][0], results[1][0]


def _weight_copy_jobs(hbm, vmem, src_row_starts=None):
    rows, cols = vmem.shape
    jobs = []
    if src_row_starts is None:
        for cb in range(cols // WEIGHT_BLOCK):
            for rb in range(rows // WEIGHT_ROWS):
                r0, c0 = rb * WEIGHT_ROWS, cb * WEIGHT_BLOCK
                jobs.append((hbm, r0, c0, vmem, r0, c0, False))
    else:
        for cb, start in enumerate(src_row_starts):
            for sub in range(WEIGHT_BLOCK // WEIGHT_ROWS):
                for kb in range(rows // WEIGHT_BLOCK):
                    jobs.append((hbm, start + sub * WEIGHT_ROWS, kb * WEIGHT_BLOCK,
                                 vmem, kb * WEIGHT_BLOCK, cb * WEIGHT_BLOCK + sub * WEIGHT_ROWS, True))
    return jobs


def _load_weights(jobs, stage_ref, sem_ref):
    ahead = WEIGHT_SLOTS - 1

    def copy(i):
        src, r0, c0 = jobs[i][:3]
        slot = i % WEIGHT_SLOTS
        return pltpu.make_async_copy(
            src.at[pl.ds(r0, WEIGHT_ROWS), pl.ds(c0, WEIGHT_BLOCK)], stage_ref.at[slot], sem_ref.at[slot])

    for i in range(min(ahead, len(jobs))):
        copy(i).start()
    for i in range(len(jobs)):
        if i + ahead < len(jobs):
            copy(i + ahead).start()
        copy(i).wait()
        dst, r0, c0, transposed = jobs[i][3:]
        block = stage_ref[i % WEIGHT_SLOTS]
        if transposed:
            dst[r0:r0 + WEIGHT_BLOCK, c0:c0 + WEIGHT_ROWS] = block.T.astype(BF16)
        else:
            dst[r0:r0 + WEIGHT_ROWS, c0:c0 + WEIGHT_BLOCK] = block.astype(BF16)
        yield


def _layer_kernel(x_ref, *refs, tiles_per_seq, n_tiles, n_small, w_in_row_starts, **dims):
    (pre_g, w_if, b_if, rg_cw, ml_cw, rg_cb, ml_cb, w_gate, b_r, b_i, lam, rg_gain, ml_gain, post_g,
     mlp_pre_g, mlp_post_g) = refs[:n_small]
    w_in_t_hbm, w_out_hbm, w_up_hbm, w_down_hbm = refs[n_small:n_small + 4]
    out_ref = refs[n_small + 4]
    w_main, w_out, w_up, w_down, stage_ref, sem_ref, h1_ref = refs[n_small + 5:n_small + 12]
    state = refs[n_small + 12:]
    s = pl.program_id(0)

    @pl.when(s == 0)
    def _():
        h1_ref[...] = jnp.zeros_like(h1_ref)
        _run(_load_weights(_weight_copy_jobs(w_in_t_hbm, w_main, w_in_row_starts)
                           + _weight_copy_jobs(w_out_hbm, w_out) + _weight_copy_jobs(w_up_hbm, w_up)
                           + _weight_copy_jobs(w_down_hbm, w_down), stage_ref, sem_ref))

    h1_new, out = _interleave(
        _mixer_tile(x_ref, s % tiles_per_seq == 0, pre_g, w_main, w_if, b_if, rg_cw, ml_cw, rg_cb, ml_cb,
                    w_gate, b_r, b_i, lam, rg_gain, ml_gain, w_out, post_g, *state, **dims),
        _mlp_tile(h1_ref, mlp_pre_g, w_up, w_down, mlp_post_g),
        MLP_PHASES_AFTER["start"])
    out_ref[0] = out
    h1_ref[...] = h1_new


def _const_spec(shape):
    zeros = (0,) * len(shape)
    return pl.BlockSpec(shape, lambda *_: zeros, pipeline_mode=pl.Buffered(1))


def _block_diag_gates(w_r, w_i):
    nb, bd, _ = w_r.shape
    per = MXU_WIDTH // bd
    w = jnp.stack([w_r, w_i], axis=1).reshape(nb // per, per, 2, bd, bd)
    eye = jnp.eye(per, dtype=w.dtype)
    return jnp.einsum("gptij,pq->gpitqj", w, eye).reshape(nb // per, MXU_WIDTH, 2 * MXU_WIDTH)


def _layer(x, pre_gain, w_in, rg_conv_w, rg_conv_b, gate_r_w, gate_r_b, gate_i_w, gate_i_b, lam,
           ml_conv_w, ml_conv_b, igate_b, fgate_b, rg_gain, ml_gain, w_out, post_gain,
           pre_mlp_gain, w_up, w_down, post_mlp_gain):
    bsz, seq, d_model = x.shape
    assert w_up.shape[1] % MLP_FF_CHUNK == 0
    d_rg = lam.shape[0]
    d_ml = ml_gain.shape[0]
    head_dim = d_ml // ML_HEADS
    assert head_dim == LANES and ML_CHUNK == LANES and 2 * ML_HEADS == SUBLANES
    assert seq % SEQ_TILE == 0 and SEQ_TILE % ML_CHUNK == 0 and d_rg % MXU_WIDTH == 0
    d_conv = d_rg + 2 * d_ml

    assert d_rg == WEIGHT_BLOCK and d_ml == WEIGHT_BLOCK
    starts = dict(rg_x=0, rg_gate=d_rg, q=2 * d_rg, k=2 * d_rg + d_ml, v=2 * d_rg + 2 * d_ml, o=2 * d_rg + 3 * d_ml)
    w_in_row_starts = tuple(starts[n] for n in ("rg_x", "q", "k", "rg_gate", "v", "o"))
    d_main = 2 * d_rg + 4 * d_ml
    if_w = w_in[:, d_main:]
    w_if = jnp.pad(if_w, ((0, 0), (0, LANES - 2 * ML_HEADS))).astype(BF16)
    b_if = jnp.pad(jnp.concatenate([igate_b, fgate_b]), (0, LANES - 2 * ML_HEADS)).reshape(1, LANES)
    w_gate = _block_diag_gates(gate_r_w, gate_i_w).astype(BF16)

    row = lambda v: v.reshape(1, -1)
    small = (row(pre_gain), w_if, b_if, rg_conv_w, ml_conv_w, row(rg_conv_b), row(ml_conv_b), w_gate,
             row(gate_r_b), row(gate_i_b), row(lam),
             row(rg_gain), row(ml_gain), row(post_gain), row(pre_mlp_gain), row(post_mlp_gain))
    big = (jnp.swapaxes(w_in, 0, 1), w_out, w_up, w_down)
    d_ff = w_up.shape[1]
    for w in (w_out, w_up, w_down):
        assert w.shape[0] % WEIGHT_BLOCK == 0 and w.shape[1] % WEIGHT_BLOCK == 0

    tiles_per_seq = seq // SEQ_TILE
    n_tiles = bsz * tiles_per_seq

    def tile_block(t):
        return (t // tiles_per_seq, t % tiles_per_seq, 0)

    x_spec = pl.BlockSpec((1, SEQ_TILE, d_model), lambda s: tile_block(jnp.minimum(s, n_tiles - 1)))
    out_spec = pl.BlockSpec((1, SEQ_TILE, d_model), lambda s: tile_block(jnp.maximum(s - 1, 0)))
    in_specs = ([x_spec] + [_const_spec(op.shape) for op in small]
                + [pl.BlockSpec(memory_space=pl.ANY)] * len(big))
    kern = functools.partial(_layer_kernel, tiles_per_seq=tiles_per_seq, n_tiles=n_tiles, n_small=len(small),
                             w_in_row_starts=w_in_row_starts, d_rg=d_rg, d_ml=d_ml, head_dim=head_dim)
    return pl.pallas_call(
        kern,
        grid=(n_tiles + 1,),
        in_specs=in_specs,
        out_specs=out_spec,
        out_shape=jax.ShapeDtypeStruct(x.shape, x.dtype),
        scratch_shapes=[
            pltpu.VMEM((d_model, d_main), BF16),
            pltpu.VMEM((d_rg + d_ml, d_model), BF16),
            pltpu.VMEM((d_model, d_ff), BF16),
            pltpu.VMEM((d_ff, d_model), BF16),
            pltpu.VMEM((WEIGHT_SLOTS, WEIGHT_ROWS, WEIGHT_BLOCK), F32),
            pltpu.SemaphoreType.DMA((WEIGHT_SLOTS,)),
            pltpu.VMEM((SEQ_TILE, d_model), F32),
            pltpu.VMEM(((d_conv + d_rg) // LANES, SUBLANES + SEQ_TILE, LANES), F32),
            pltpu.VMEM((2 * d_ml // LANES, SEQ_TILE, LANES), F32),
            pltpu.VMEM((d_rg // LANES, SEQ_TILE, LANES), F32),
            pltpu.VMEM((SUBLANES, d_rg), F32),
            pltpu.VMEM((ML_HEADS, head_dim, 2 * head_dim), F32),
            pltpu.VMEM((SUBLANES, ML_CHUNK), F32),
            pltpu.VMEM((7, SUBLANES, ML_CHUNK), F32),
            pltpu.VMEM((SEQ_TILE, d_rg + d_ml), BF16),
        ],
        compiler_params=pltpu.CompilerParams(
            dimension_semantics=("arbitrary",), vmem_limit_bytes=VMEM_LIMIT_BYTES),
        name="layer",
    )(x, *small, *big)


def kernel(x, pre_mix_gain, w_in, rg_conv_w, rg_conv_b, rg_gate_r_w, rg_gate_r_b, rg_gate_i_w, rg_gate_i_b, rg_lambda, ml_conv_w, ml_conv_b, ml_igate_b, ml_fgate_b, rg_out_gain, ml_out_gain, w_out, post_mix_gain, pre_mlp_gain, mlp_w_up, mlp_w_down, post_mlp_gain):
    h = x
    for l in range(w_in.shape[0]):
        h = _layer(h, pre_mix_gain[l], w_in[l], rg_conv_w[l], rg_conv_b[l], rg_gate_r_w[l], rg_gate_r_b[l],
                   rg_gate_i_w[l], rg_gate_i_b[l], rg_lambda[l], ml_conv_w[l], ml_conv_b[l], ml_igate_b[l],
                   ml_fgate_b[l], rg_out_gain[l], ml_out_gain[l], w_out[l], post_mix_gain[l],
                   pre_mlp_gain[l], mlp_w_up[l], mlp_w_down[l], post_mlp_gain[l])
    return h
```

```python
import functools

import jax
import jax.numpy as jnp
from jax.experimental import pallas as pl
from jax.experimental.pallas import tpu as pltpu

F32 = jnp.float32
BF16 = jnp.bfloat16

RG_C = 8.0
ML_HEADS = 4
CONV_WIDTH = 4
EPS = 1e-6

SUBLANES = 8
LANES = 128
MXU_WIDTH = 256

SEQ_TILE = 512
ML_CHUNK = LANES
PHASES = 4
MLP_FF_CHUNK = 1024
WEIGHT_ROWS = 256
WEIGHT_BLOCK = 512
WEIGHT_SLOTS = 8
MLP_PHASES_AFTER = {"start": 0, "w_in": 1, "conv": (0, 1, 0, 0, 1, 0), "rg_gates": 0, "rg_scan": (1, 0, 0),
                    "ml_stage": (1, 0, 0, 0), "w_out": 0}
VMEM_LIMIT_BYTES = 60 * 1024 * 1024


def _rms(x, gain):
    return x * jax.lax.rsqrt(jnp.mean(x * x, axis=-1, keepdims=True) + EPS) * gain


LOG2E = 1.4426950408889634


def _sigmoid(x):
    return 1.0 / (1.0 + jnp.exp2(x * (-LOG2E)))


def _softplus(x):
    return jnp.maximum(x, 0.0) + jnp.log1p(jnp.exp(-jnp.abs(x)))


def _gelu_tanh(x):
    c = 0.7978845608028654
    return 0.5 * x * (1.0 + jnp.tanh(c * (x + 0.044715 * (x * x * x))))


def _lane_scan(x, op, fill):
    n = x.shape[1]
    lane = jax.lax.broadcasted_iota(jnp.int32, x.shape, 1)
    d = 1
    while d < n:
        shifted = jnp.where(lane < d, fill, pltpu.roll(x, d, axis=1))
        x = op(x, shifted)
        d *= 2
    return x


def _affine_row_scan(sa, sb):
    n, width = sa.shape
    row8 = jax.lax.broadcasted_iota(jnp.int32, (SUBLANES, width), 0)
    d = 1
    while d < n:
        if d < SUBLANES:
            ra = pltpu.roll(sa, d, axis=0)
            rb = pltpu.roll(sb, d, axis=0)
            a_sh = jnp.concatenate([jnp.where(row8 < d, 1.0, ra[:SUBLANES]), ra[SUBLANES:]], axis=0)
            b_sh = jnp.concatenate([jnp.where(row8 < d, 0.0, rb[:SUBLANES]), rb[SUBLANES:]], axis=0)
            sb = sb + sa * b_sh
            sa = sa * a_sh
        else:
            sb = jnp.concatenate([sb[:d], sb[d:] + sa[d:] * sb[:n - d]], axis=0)
            sa = jnp.concatenate([sa[:d], sa[d:] * sa[:n - d]], axis=0)
        d *= 2
    return sa, sb


def _mixer_tile(x_ref, seq_start, pre_g_ref, w_main_ref, w_if_ref, b_if_ref, rg_cw_ref, ml_cw_ref, rg_cb_ref, ml_cb_ref,
                w_gate_ref,
                b_r_ref, b_i_ref, lam_ref, rg_gain_ref, ml_gain_ref, w_out_ref, post_g_ref,
                cbuf_ref, qk_ref, yrg_ref, h_ref, c_ref, m_ref, row_ref, y_ref,
                *, d_rg, d_ml, head_dim):
    ts = x_ref.shape[1]
    d_conv = d_rg + 2 * d_ml
    n_chunks = ts // ML_CHUNK
    L = ML_CHUNK

    @pl.when(seq_start)
    def _():
        cbuf_ref[:, ts:ts + SUBLANES, :] = jnp.zeros((cbuf_ref.shape[0], SUBLANES, LANES), F32)
        h_ref[...] = jnp.zeros_like(h_ref)
        c_ref[...] = jnp.zeros_like(c_ref)
        m_ref[...] = jnp.zeros_like(m_ref)

    ub = _rms(x_ref[0], pre_g_ref[...]).astype(BF16)
    proj = jnp.dot(ub, w_main_ref[...], preferred_element_type=F32)
    gates = jnp.dot(ub, w_if_ref[...], preferred_element_type=F32) + b_if_ref[...]
    yield MLP_PHASES_AFTER["w_in"]

    n_cs = d_conv // LANES
    n_rs = d_rg // LANES
    G = ts // PHASES
    cbuf_ref[:n_cs, :SUBLANES, :] = cbuf_ref[:n_cs, ts:ts + SUBLANES, :]
    for j in range(n_cs + n_rs):
        cbuf_ref[j, SUBLANES:, :] = proj[:, j * LANES:(j + 1) * LANES]

    def phase(j, e):
        return cbuf_ref[j, pl.ds(SUBLANES + e, G, stride=PHASES), :]

    conv_ph = [[None] * n_cs for _ in range(PHASES)]
    for j in range(n_cs):
        conv_w_ref, conv_b_ref, jj = (rg_cw_ref, rg_cb_ref, j) if j < n_rs else (ml_cw_ref, ml_cb_ref, j - n_rs)
        cols = slice(jj * LANES, (jj + 1) * LANES)
        taps = {e: phase(j, e) for e in range(1 - CONV_WIDTH, PHASES)}
        for r in range(PHASES):
            acc = taps[r] * conv_w_ref[CONV_WIDTH - 1:CONV_WIDTH, cols] + conv_b_ref[:, cols]
            for k in range(1, CONV_WIDTH):
                acc = acc + taps[r - k] * conv_w_ref[CONV_WIDTH - 1 - k:CONV_WIDTH - k, cols]
            conv_ph[r][j] = acc
        if j >= n_rs:
            scale = head_dim ** -0.5 if j < n_rs + d_ml // LANES else 1.0
            for r in range(PHASES):
                val = conv_ph[r][j]
                val = val * _sigmoid(val)
                qk_ref[j - n_rs, pl.ds(r, G, stride=PHASES), :] = val * scale if scale != 1.0 else val
        if j % 2 == 1:
            yield MLP_PHASES_AFTER["conv"][j // 2]

    xc = jnp.concatenate([jnp.concatenate(conv_ph[r][:n_rs], axis=1) for r in range(PHASES)], axis=0)
    r_parts, i_parts = [], []
    for g in range(d_rg // MXU_WIDTH):
        gg = jnp.dot(xc[:, g * MXU_WIDTH:(g + 1) * MXU_WIDTH].astype(BF16), w_gate_ref[g],
                     preferred_element_type=F32)
        r_parts.append(gg[:, :MXU_WIDTH])
        i_parts.append(gg[:, MXU_WIDTH:])
    r = _sigmoid(jnp.concatenate(r_parts, axis=1) + b_r_ref[...])
    i_gate = _sigmoid(jnp.concatenate(i_parts, axis=1) + b_i_ref[...])
    neg_log_a_unit = RG_C * _softplus(-lam_ref[...])
    a = jnp.exp2(r * (neg_log_a_unit * (-LOG2E)))
    z = jnp.tanh(r * neg_log_a_unit) * (a * a + 1.0)
    b_in = jnp.where(z > 0.0, z * jax.lax.rsqrt(z), 0.0) * (i_gate * xc)
    yield MLP_PHASES_AFTER["rg_gates"]

    comp_a, comp_b = [a[:G]], [b_in[:G]]
    for r in range(1, PHASES):
        a_r, b_r = a[r * G:(r + 1) * G], b_in[r * G:(r + 1) * G]
        comp_b.append(a_r * comp_b[-1] + b_r)
        comp_a.append(a_r * comp_a[-1])
    ga, gb = _affine_row_scan(comp_a[-1], comp_b[-1])
    carry = h_ref[SUBLANES - 1:SUBLANES, :]
    h_end = ga * carry + gb
    h_ref[...] = h_end[G - SUBLANES:]
    rolled = pltpu.roll(h_end, 1, axis=0)
    row8r = jax.lax.broadcasted_iota(jnp.int32, (SUBLANES, d_rg), 0)
    h_prev = jnp.concatenate([jnp.where(row8r < 1, carry, rolled[:SUBLANES]), rolled[SUBLANES:]], axis=0)
    yield MLP_PHASES_AFTER["rg_scan"][0]
    for r in range(PHASES):
        h_r = comp_a[r] * h_prev + comp_b[r]
        gate_r = jnp.concatenate([phase(n_cs + j, r) for j in range(n_rs)], axis=1)
        y_r = _rms(h_r * _gelu_tanh(gate_r), rg_gain_ref[...])
        for j in range(n_rs):
            yrg_ref[j, pl.ds(r, G, stride=PHASES), :] = y_r[:, j * LANES:(j + 1) * LANES]
        if r % 2 == 1:
            yield MLP_PHASES_AFTER["rg_scan"][1 + r // 2]
    for j in range(n_rs):
        y_ref[:, j * LANES:(j + 1) * LANES] = yrg_ref[j].astype(BF16)

    v_all = proj[:, d_conv + d_rg:d_conv + d_rg + d_ml]
    o_all = proj[:, d_conv + d_rg + d_ml:]

    lane_g = jax.lax.broadcasted_iota(jnp.int32, gates.shape, 1)
    log_sig = jnp.minimum(gates, 0.0) - jnp.log1p(jnp.exp(-jnp.abs(gates)))
    gates_t = jnp.where(lane_g < ML_HEADS, gates, log_sig).T[:SUBLANES]

    causal = (jax.lax.broadcasted_iota(jnp.int32, (L, L), 1)
              <= jax.lax.broadcasted_iota(jnp.int32, (L, L), 0))
    ones_ext = jnp.ones((L, head_dim), BF16)

    for c in range(n_chunks):
        rows = slice(c * L, (c + 1) * L)
        li = gates_t[:, rows]
        lf = pltpu.roll(li, ML_HEADS, axis=0)
        bcum = _lane_scan(lf, jnp.add, 0.0)
        b_last = jnp.sum(lf, axis=1, keepdims=True)
        row_b = li - bcum
        cmax = _lane_scan(row_b, jnp.maximum, -jnp.inf)
        w_loc = b_last + row_b
        m_loc = jnp.max(w_loc, axis=1, keepdims=True)
        m_prev_b = m_ref[...]
        m_prev = jnp.max(m_prev_b, axis=1, keepdims=True)
        m_s = jnp.maximum(bcum + m_prev, bcum + cmax)
        m_new = jnp.maximum(b_last + m_prev, m_loc)
        row_ref[0] = bcum - m_s
        row_ref[1] = m_s
        row_ref[2] = row_b
        row_ref[3] = jnp.exp(w_loc - m_loc)
        row_ref[4] = jnp.broadcast_to(jnp.exp(b_last + m_prev - m_new), (SUBLANES, L))
        row_ref[5] = jnp.broadcast_to(jnp.exp(m_loc - m_new), (SUBLANES, L))
        row_ref[6] = m_prev_b
        m_ref[...] = jnp.broadcast_to(m_new, (SUBLANES, L))

        heads = range(ML_HEADS)
        q_f = [qk_ref[h, rows, :] for h in heads]
        k_t = [qk_ref[ML_HEADS + h, rows, :].T for h in heads]
        s = [jnp.dot(q_f[h].astype(BF16), k_t[h].astype(BF16), preferred_element_type=F32) for h in heads]
        yield MLP_PHASES_AFTER["ml_stage"][0]

        lhs, rhs, v_ext, ms_b = [], [], [], []
        for h in heads:
            cols = slice(h * head_dim, (h + 1) * head_dim)
            ca_b = jnp.broadcast_to(row_ref[0, h:h + 1, :], (head_dim, L)).T
            ms_b.append(jnp.broadcast_to(row_ref[1, h:h + 1, :], (head_dim, L)).T)
            p = jnp.exp(jnp.where(causal, ca_b + row_ref[2, h:h + 1, :], -jnp.inf))
            inter_w = jnp.exp(ca_b + row_ref[6, h:h + 1, :])
            v_ext.append(jnp.concatenate([v_all[rows, cols].astype(BF16), ones_ext], axis=1))
            lhs.append(jnp.concatenate([(s[h] * p).astype(BF16), (q_f[h] * inter_w).astype(BF16)], axis=1))
            rhs.append(jnp.concatenate([v_ext[h], c_ref[h].astype(BF16)], axis=0))
        yield MLP_PHASES_AFTER["ml_stage"][1]

        nd = [jnp.dot(lhs[h], rhs[h], preferred_element_type=F32) for h in heads]
        upd = [jnp.dot((k_t[h] * row_ref[3, h:h + 1, :]).astype(BF16), v_ext[h], preferred_element_type=F32)
               for h in heads]
        yield MLP_PHASES_AFTER["ml_stage"][2]

        for h in heads:
            cols = slice(h * head_dim, (h + 1) * head_dim)
            hh = nd[h][:, :head_dim] / jnp.maximum(jnp.abs(nd[h][:, head_dim:]), jnp.exp(-ms_b[h]))
            hh = _sigmoid(o_all[rows, cols]) * hh
            y_ref[rows, d_rg + h * head_dim:d_rg + (h + 1) * head_dim] = _rms(
                hh, ml_gain_ref[:, cols]).astype(BF16)
            s_old = row_ref[4, h:h + 1, :]
            s_new = row_ref[5, h:h + 1, :]
            c_ref[h] = (jnp.concatenate([s_old, s_old], axis=1) * c_ref[h]
                        + jnp.concatenate([s_new, s_new], axis=1) * upd[h])
        yield MLP_PHASES_AFTER["ml_stage"][3]

    mix = jnp.dot(y_ref[...], w_out_ref[...], preferred_element_type=F32)
    yield MLP_PHASES_AFTER["w_out"]
    return x_ref[0] + _rms(mix, post_g_ref[...])


def _mlp_tile(h_ref, pre_g_ref, w_up_ref, w_down_ref, post_g_ref):
    vb = _rms(h_ref[...], pre_g_ref[...]).astype(BF16)
    d_ff = w_up_ref.shape[1]
    acc = jnp.zeros(h_ref.shape, F32)
    for c in range(d_ff // MLP_FF_CHUNK):
        cols = slice(c * MLP_FF_CHUNK, (c + 1) * MLP_FF_CHUNK)
        f = jnp.maximum(jnp.dot(vb, w_up_ref[:, cols], preferred_element_type=F32), 0.0)
        yield
        acc = acc + jnp.dot((f * f).astype(BF16), w_down_ref[cols, :], preferred_element_type=F32)
        yield
    return h_ref[...] + _rms(acc, post_g_ref[...])


def _run(gen):
    while True:
        try:
            next(gen)
        except StopIteration as stop:
            return stop.value


def _interleave(primary, secondary, lead=0, per=1):
    results = [None, None]

    def advance(idx, gen):
        if results[idx] is None:
            try:
                return next(gen)
            except StopIteration as stop:
                results[idx] = (stop.value,)
        return 0

    for _ in range(lead):
        advance(1, secondary)
    while results[0] is None:
        for _ in range(advance(0, primary) * per):
            advance(1, secondary)
    while results[1] is None:
        advance(1, secondary)
    return results[0][0], results[1][0]


def _weight_copy_jobs(hbm, vmem, src_row_starts=None):
    rows, cols = vmem.shape
    jobs = []
    if src_row_starts is None:
        for cb in range(cols // WEIGHT_BLOCK):
            for rb in range(rows // WEIGHT_ROWS):
                r0, c0 = rb * WEIGHT_ROWS, cb * WEIGHT_BLOCK
                jobs.append((hbm, r0, c0, vmem, r0, c0, False))
    else:
        for cb, start in enumerate(src_row_starts):
            for sub in range(WEIGHT_BLOCK // WEIGHT_ROWS):
                for kb in range(rows // WEIGHT_BLOCK):
                    jobs.append((hbm, start + sub * WEIGHT_ROWS, kb * WEIGHT_BLOCK,
                                 vmem, kb * WEIGHT_BLOCK, cb * WEIGHT_BLOCK + sub * WEIGHT_ROWS, True))
    return jobs


def _load_weights(jobs, stage_ref, sem_ref):
    ahead = WEIGHT_SLOTS - 1

    def copy(i):
        src, r0, c0 = jobs[i][:3]
        slot = i % WEIGHT_SLOTS
        return pltpu.make_async_copy(
            src.at[pl.ds(r0, WEIGHT_ROWS), pl.ds(c0, WEIGHT_BLOCK)], stage_ref.at[slot], sem_ref.at[slot])

    for i in range(min(ahead, len(jobs))):
        copy(i).start(priority=i % 2)
    for i in range(len(jobs)):
        if i + ahead < len(jobs):
            copy(i + ahead).start(priority=(i + ahead) % 2)
        copy(i).wait()
        dst, r0, c0, transposed = jobs[i][3:]
        block = stage_ref[i % WEIGHT_SLOTS]
        if transposed:
            dst[r0:r0 + WEIGHT_BLOCK, c0:c0 + WEIGHT_ROWS] = block.T.astype(BF16)
        else:
            dst[r0:r0 + WEIGHT_ROWS, c0:c0 + WEIGHT_BLOCK] = block.astype(BF16)
        yield


def _layer_kernel(x_ref, *refs, tiles_per_seq, n_tiles, n_small, w_in_row_starts, **dims):
    (pre_g, w_if, b_if, rg_cw, ml_cw, rg_cb, ml_cb, w_gate, b_r, b_i, lam, rg_gain, ml_gain, post_g,
     mlp_pre_g, mlp_post_g) = refs[:n_small]
    w_in_t_hbm, w_out_hbm, w_up_hbm, w_down_hbm = refs[n_small:n_small + 4]
    out_ref = refs[n_small + 4]
    w_main, w_out, w_up, w_down, stage_ref, sem_ref, h1_ref = refs[n_small + 5:n_small + 12]
    state = refs[n_small + 12:]
    s = pl.program_id(0)

    @pl.when(s == 0)
    def _():
        h1_ref[...] = jnp.zeros_like(h1_ref)
        _run(_load_weights(_weight_copy_jobs(w_in_t_hbm, w_main, w_in_row_starts)
                           + _weight_copy_jobs(w_out_hbm, w_out) + _weight_copy_jobs(w_up_hbm, w_up)
                           + _weight_copy_jobs(w_down_hbm, w_down), stage_ref, sem_ref))

    h1_new, out = _interleave(
        _mixer_tile(x_ref, s % tiles_per_seq == 0, pre_g, w_main, w_if, b_if, rg_cw, ml_cw, rg_cb, ml_cb,
                    w_gate, b_r, b_i, lam, rg_gain, ml_gain, w_out, post_g, *state, **dims),
        _mlp_tile(h1_ref, mlp_pre_g, w_up, w_down, mlp_post_g),
        MLP_PHASES_AFTER["start"])
    out_ref[0] = out
    h1_ref[...] = h1_new


def _const_spec(shape):
    zeros = (0,) * len(shape)
    return pl.BlockSpec(shape, lambda *_: zeros, pipeline_mode=pl.Buffered(1))


def _block_diag_gates(w_r, w_i):
    nb, bd, _ = w_r.shape
    per = MXU_WIDTH // bd
    w = jnp.stack([w_r, w_i], axis=1).reshape(nb // per, per, 2, bd, bd)
    eye = jnp.eye(per, dtype=w.dtype)
    return jnp.einsum("gptij,pq->gpitqj", w, eye).reshape(nb // per, MXU_WIDTH, 2 * MXU_WIDTH)


def _layer(x, pre_gain, w_in, rg_conv_w, rg_conv_b, gate_r_w, gate_r_b, gate_i_w, gate_i_b, lam,
           ml_conv_w, ml_conv_b, igate_b, fgate_b, rg_gain, ml_gain, w_out, post_gain,
           pre_mlp_gain, w_up, w_down, post_mlp_gain):
    bsz, seq, d_model = x.shape
    assert w_up.shape[1] % MLP_FF_CHUNK == 0
    d_rg = lam.shape[0]
    d_ml = ml_gain.shape[0]
    head_dim = d_ml // ML_HEADS
    assert head_dim == LANES and ML_CHUNK == LANES and 2 * ML_HEADS == SUBLANES
    assert seq % SEQ_TILE == 0 and SEQ_TILE % ML_CHUNK == 0 and d_rg % MXU_WIDTH == 0
    d_conv = d_rg + 2 * d_ml

    assert d_rg == WEIGHT_BLOCK and d_ml == WEIGHT_BLOCK
    starts = dict(rg_x=0, rg_gate=d_rg, q=2 * d_rg, k=2 * d_rg + d_ml, v=2 * d_rg + 2 * d_ml, o=2 * d_rg + 3 * d_ml)
    w_in_row_starts = tuple(starts[n] for n in ("rg_x", "q", "k", "rg_gate", "v", "o"))
    d_main = 2 * d_rg + 4 * d_ml
    if_w = w_in[:, d_main:]
    w_if = jnp.pad(if_w, ((0, 0), (0, LANES - 2 * ML_HEADS))).astype(BF16)
    b_if = jnp.pad(jnp.concatenate([igate_b, fgate_b]), (0, LANES - 2 * ML_HEADS)).reshape(1, LANES)
    w_gate = _block_diag_gates(gate_r_w, gate_i_w).astype(BF16)

    row = lambda v: v.reshape(1, -1)
    small = (row(pre_gain), w_if, b_if, rg_conv_w, ml_conv_w, row(rg_conv_b), row(ml_conv_b), w_gate,
             row(gate_r_b), row(gate_i_b), row(lam),
             row(rg_gain), row(ml_gain), row(post_gain), row(pre_mlp_gain), row(post_mlp_gain))
    big = (jnp.swapaxes(w_in, 0, 1), w_out, w_up, w_down)
    d_ff = w_up.shape[1]
    for w in (w_out, w_up, w_down):
        assert w.shape[0] % WEIGHT_BLOCK == 0 and w.shape[1] % WEIGHT_BLOCK == 0

    tiles_per_seq = seq // SEQ_TILE
    n_tiles = bsz * tiles_per_seq

    def tile_block(t):
        return (t // tiles_per_seq, t % tiles_per_seq, 0)

    x_spec = pl.BlockSpec((1, SEQ_TILE, d_model), lambda s: tile_block(jnp.minimum(s, n_tiles - 1)))
    out_spec = pl.BlockSpec((1, SEQ_TILE, d_model), lambda s: tile_block(jnp.maximum(s - 1, 0)))
    in_specs = ([x_spec] + [_const_spec(op.shape) for op in small]
                + [pl.BlockSpec(memory_space=pl.ANY)] * len(big))
    kern = functools.partial(_layer_kernel, tiles_per_seq=tiles_per_seq, n_tiles=n_tiles, n_small=len(small),
                             w_in_row_starts=w_in_row_starts, d_rg=d_rg, d_ml=d_ml, head_dim=head_dim)
    return pl.pallas_call(
        kern,
        grid=(n_tiles + 1,),
        in_specs=in_specs,
        out_specs=out_spec,
        out_shape=jax.ShapeDtypeStruct(x.shape, x.dtype),
        scratch_shapes=[
            pltpu.VMEM((d_model, d_main), BF16),
            pltpu.VMEM((d_rg + d_ml, d_model), BF16),
            pltpu.VMEM((d_model, d_ff), BF16),
            pltpu.VMEM((d_ff, d_model), BF16),
            pltpu.VMEM((WEIGHT_SLOTS, WEIGHT_ROWS, WEIGHT_BLOCK), F32),
            pltpu.SemaphoreType.DMA((WEIGHT_SLOTS,)),
            pltpu.VMEM((SEQ_TILE, d_model), F32),
            pltpu.VMEM(((d_conv + d_rg) // LANES, SUBLANES + SEQ_TILE, LANES), F32),
            pltpu.VMEM((2 * d_ml // LANES, SEQ_TILE, LANES), F32),
            pltpu.VMEM((d_rg // LANES, SEQ_TILE, LANES), F32),
            pltpu.VMEM((SUBLANES, d_rg), F32),
            pltpu.VMEM((ML_HEADS, head_dim, 2 * head_dim), F32),
            pltpu.VMEM((SUBLANES, ML_CHUNK), F32),
            pltpu.VMEM((7, SUBLANES, ML_CHUNK), F32),
            pltpu.VMEM((SEQ_TILE, d_rg + d_ml), BF16),
        ],
        compiler_params=pltpu.CompilerParams(
            dimension_semantics=("arbitrary",), vmem_limit_bytes=VMEM_LIMIT_BYTES),
        name="layer",
    )(x, *small, *big)


def kernel(x, pre_mix_gain, w_in, rg_conv_w, rg_conv_b, rg_gate_r_w, rg_gate_r_b, rg_gate_i_w, rg_gate_i_b, rg_lambda, ml_conv_w, ml_conv_b, ml_igate_b, ml_fgate_b, rg_out_gain, ml_out_gain, w_out, post_mix_gain, pre_mlp_gain, mlp_w_up, mlp_w_down, post_mlp_gain):
    h = x
    for l in range(w_in.shape[0]):
        h = _layer(h, pre_mix_gain[l], w_in[l], rg_conv_w[l], rg_conv_b[l], rg_gate_r_w[l], rg_gate_r_b[l],
                   rg_gate_i_w[l], rg_gate_i_b[l], rg_lambda[l], ml_conv_w[l], ml_conv_b[l], ml_igate_b[l],
                   ml_fgate_b[l], rg_out_gain[l], ml_out_gain[l], w_out[l], post_mix_gain[l],
                   pre_mlp_gain[l], mlp_w_up[l], mlp_w_down[l], post_mlp_gain[l])
    return h
```
